```python
import math
import jax, jax.numpy as jnp
from jax import lax
import numpy as np

D_MODEL = 2048
BATCH = 1
SEQ = 8192
DEPTH = 1

SSM_WIDTH = D_MODEL // 2
SSM_GROUP = 16
SSM_GROUPS = SSM_WIDTH // SSM_GROUP
SSM_STATE = 64
DT_MIN = 1e-3
DT_MAX = 1e-1
HEAD_DIM = 64
N_Q_HEADS = (D_MODEL // 2) // HEAD_DIM
N_KV_HEADS = 4
Q_PER_KV = N_Q_HEADS // N_KV_HEADS
WINDOW = 128
ROPE_DIM = HEAD_DIM // 4
ROPE_THETA = 500000.0
Q_WIDTH = N_Q_HEADS * HEAD_DIM
KV_WIDTH = N_KV_HEADS * HEAD_DIM
N_BRANCH = 2
IN_WIDTH = SSM_WIDTH + Q_WIDTH + 2 * KV_WIDTH + N_BRANCH * D_MODEL
N_GROUPS = 8
EXPERTS_PER_GROUP = 8
N_EXPERTS = N_GROUPS * EXPERTS_PER_GROUP
TOP_K = 2
D_EXPERT = D_MODEL // 4
MOE_BLOCK = 128
EPS = 1e-6

kernel_name = "hybrid_s5_swa_sink_hiermoe"


def rmsnorm(x, g):
    xf = x.astype(jnp.float32)
    y = xf * lax.rsqrt(jnp.mean(xf * xf, axis=-1, keepdims=True) + EPS)
    return (y * g.astype(jnp.float32)).astype(x.dtype)


def partial_rope(x, positions):
    half = ROPE_DIM // 2
    inv_freq = ROPE_THETA ** (-jnp.arange(half, dtype=jnp.float32) / half)
    ang = positions.astype(jnp.float32)[..., None] * inv_freq
    cos = jnp.cos(ang)[:, :, None, :]
    sin = jnp.sin(ang)[:, :, None, :]
    xr = x[..., :ROPE_DIM].astype(jnp.float32)
    x1, x2 = xr[..., :half], xr[..., half:]
    rot = jnp.concatenate([x1 * cos - x2 * sin, x2 * cos + x1 * sin], axis=-1)
    return jnp.concatenate([rot.astype(x.dtype), x[..., ROPE_DIM:]], axis=-1)


def s5_ssm(u, a_re, a_im, log_dt, b_re, b_im, c_re, c_im, d_skip):
    f32 = jnp.float32
    u = u.astype(f32)
    lam_re = jnp.minimum(a_re.astype(f32), -1e-4)
    lam_im = a_im.astype(f32)
    dt = jnp.exp(log_dt.astype(f32))[:, None]
    mag = jnp.exp(lam_re * dt)
    abar_re = mag * jnp.cos(lam_im * dt)
    abar_im = mag * jnp.sin(lam_im * dt)
    den = lam_re * lam_re + lam_im * lam_im
    nr = abar_re - 1.0
    ni = abar_im
    coef_re = (nr * lam_re + ni * lam_im) / den
    coef_im = (ni * lam_re - nr * lam_im) / den
    br = b_re.astype(f32)
    bi = b_im.astype(f32)
    bbar_re = coef_re[..., None] * br - coef_im[..., None] * bi
    bbar_im = coef_re[..., None] * bi + coef_im[..., None] * br
    bu_re = jnp.einsum('bsgc,gpc->bsgp', u, bbar_re)
    bu_im = jnp.einsum('bsgc,gpc->bsgp', u, bbar_im)
    ar = jnp.broadcast_to(abar_re, bu_re.shape)
    ai = jnp.broadcast_to(abar_im, bu_re.shape)

    def combine(lhs, rhs):
        a1r, a1i, b1r, b1i = lhs
        a2r, a2i, b2r, b2i = rhs
        return (a2r * a1r - a2i * a1i,
                a2r * a1i + a2i * a1r,
                a2r * b1r - a2i * b1i + b2r,
                a2r * b1i + a2i * b1r + b2i)

    _, _, xr, xi = lax.associative_scan(combine, (ar, ai, bu_re, bu_im), axis=1)
    y = (jnp.einsum('bsgp,gcp->bsgc', xr, c_re.astype(f32))
         - jnp.einsum('bsgp,gcp->bsgc', xi, c_im.astype(f32))
         + d_skip.astype(f32) * u)
    return y


def sliding_window_attention(q, k, v, sinks):
    f32 = jnp.float32
    bsz, seq = q.shape[0], q.shape[1]
    nb = seq // WINDOW
    qb = q.astype(f32).reshape(bsz, nb, WINDOW, N_KV_HEADS, Q_PER_KV, HEAD_DIM)
    pad = jnp.zeros((bsz, WINDOW, N_KV_HEADS, HEAD_DIM), f32)

    def band(t):
        t = t.astype(f32)
        prev = jnp.concatenate([pad, t], axis=1)[:, :seq].reshape(bsz, nb, WINDOW, N_KV_HEADS, HEAD_DIM)
        cur = t.reshape(bsz, nb, WINDOW, N_KV_HEADS, HEAD_DIM)
        return jnp.concatenate([prev, cur], axis=2)

    kb = band(k)
    vb = band(v)
    scores = jnp.einsum('bnqhgd,bnkhd->bnhgqk', qb, kb) * (HEAD_DIM ** -0.5)
    qi = jnp.arange(WINDOW)[:, None]
    kj = jnp.arange(2 * WINDOW)[None, :]
    dist = qi + WINDOW - kj
    in_band = (dist >= 0) & (dist < WINDOW)
    not_pad = (jnp.arange(nb)[:, None, None] > 0) | (kj[None] >= WINDOW)
    mask = in_band[None] & not_pad
    scores = jnp.where(mask[None, :, None, None], scores, -jnp.inf)
    sink = sinks.astype(f32).reshape(N_KV_HEADS, Q_PER_KV)[None, None, :, :, None, None]
    m = jnp.maximum(jnp.max(scores, axis=-1, keepdims=True), sink)
    p = jnp.exp(scores - m)
    probs = p / (jnp.sum(p, axis=-1, keepdims=True) + jnp.exp(sink - m))
    out = jnp.einsum('bnhgqk,bnkhd->bnqhgd', probs, vb)
    return out.reshape(bsz, seq, Q_WIDTH)


def hierarchical_moe(x, w_rg, b_rg, w_re, b_re, w_gate, w_up, w_down):
    bsz, seq, d = x.shape
    n_tok = bsz * seq
    xf = x.reshape(n_tok, d)
    lg = (xf @ w_rg + b_rg).astype(jnp.float32)
    pg = jax.nn.softmax(lg, axis=-1)
    grp = jnp.argmax(lg, axis=-1)
    p_grp = jnp.take_along_axis(pg, grp[:, None], axis=-1)
    le = (xf @ w_re + b_re).astype(jnp.float32).reshape(n_tok, N_GROUPS, EXPERTS_PER_GROUP)
    le_g = jnp.take_along_axis(le, grp[:, None, None], axis=1)[:, 0]
    top_l, top_i = lax.top_k(le_g, TOP_K)
    wts = jax.nn.softmax(top_l, axis=-1) * p_grp
    eid = grp[:, None] * EXPERTS_PER_GROUP + top_i
    n_assign = n_tok * TOP_K
    e_flat = eid.reshape(n_assign)
    w_flat = wts.reshape(n_assign)
    tok_flat = jnp.repeat(jnp.arange(n_tok, dtype=jnp.int32), TOP_K)
    order = jnp.argsort(e_flat)
    e_sorted = e_flat[order]
    counts = jnp.bincount(e_flat, length=N_EXPERTS)
    starts = jnp.cumsum(counts) - counts
    padded = ((counts + MOE_BLOCK - 1) // MOE_BLOCK) * MOE_BLOCK
    pad_end = jnp.cumsum(padded)
    pad_start = pad_end - padded
    dest = pad_start[e_sorted] + (jnp.arange(n_assign) - starts[e_sorted])
    n_blocks = -(-(n_assign + N_EXPERTS * (MOE_BLOCK - 1)) // MOE_BLOCK)
    n_rows = n_blocks * MOE_BLOCK
    row_tok = jnp.zeros((n_rows,), jnp.int32).at[dest].set(tok_flat[order])
    row_w = jnp.zeros((n_rows,), jnp.float32).at[dest].set(w_flat[order])
    blk_exp = jnp.minimum(jnp.searchsorted(pad_end, jnp.arange(n_blocks) * MOE_BLOCK, side='right'),
                          N_EXPERTS - 1)
    xs = xf[row_tok].reshape(n_blocks, MOE_BLOCK, d)

    def expert_block(args):
        xb, e = args
        hdn = jax.nn.silu(xb @ w_gate[e]) * (xb @ w_up[e])
        return hdn @ w_down[e]

    ys = lax.map(expert_block, (xs, blk_exp)).reshape(n_rows, d)
    out = jax.ops.segment_sum(ys * row_w[:, None].astype(ys.dtype), row_tok, num_segments=n_tok)
    return out.reshape(bsz, seq, d)


def setup_inputs(seed: int = 0) -> dict:
    key = jax.random.key(seed)
    ks = jax.random.split(key, 32)
    f32 = jnp.float32
    L = DEPTH

    def nrm(k, shape, scale):
        return jax.random.normal(k, shape, f32) * scale

    x = jax.random.normal(ks[0], (BATCH, SEQ, D_MODEL), f32)
    positions = jnp.broadcast_to(jnp.arange(SEQ, dtype=jnp.int32)[None, :], (BATCH, SEQ))
    n_idx = jnp.arange(SSM_STATE, dtype=f32)
    return {
        "x": x,
        "positions": positions,
        "norm_mix_g": 1.0 + nrm(ks[1], (L, D_MODEL), 0.02),
        "w_in": nrm(ks[2], (L, D_MODEL, IN_WIDTH), D_MODEL ** -0.5),
        "b_in": nrm(ks[3], (L, IN_WIDTH), 0.02),
        "ssm_a_re": -0.5 + nrm(ks[4], (L, SSM_GROUPS, SSM_STATE), 0.01),
        "ssm_a_im": math.pi * n_idx + nrm(ks[5], (L, SSM_GROUPS, SSM_STATE), 0.01),
        "ssm_log_dt": jax.random.uniform(ks[6], (L, SSM_GROUPS), f32, math.log(DT_MIN), math.log(DT_MAX)),
        "ssm_b_re": nrm(ks[7], (L, SSM_GROUPS, SSM_STATE, SSM_GROUP), (2 * SSM_GROUP) ** -0.5),
        "ssm_b_im": nrm(ks[8], (L, SSM_GROUPS, SSM_STATE, SSM_GROUP), (2 * SSM_GROUP) ** -0.5),
        "ssm_c_re": nrm(ks[9], (L, SSM_GROUPS, SSM_GROUP, SSM_STATE), (2 * SSM_STATE) ** -0.5),
        "ssm_c_im": nrm(ks[10], (L, SSM_GROUPS, SSM_GROUP, SSM_STATE), (2 * SSM_STATE) ** -0.5),
        "ssm_d": nrm(ks[11], (L, SSM_GROUPS, SSM_GROUP), 1.0),
        "w_glu": nrm(ks[12], (L, SSM_WIDTH, 2 * SSM_WIDTH), SSM_WIDTH ** -0.5),
        "b_glu": nrm(ks[13], (L, 2 * SSM_WIDTH), 0.02),
        "w_br_ssm": nrm(ks[14], (L, SSM_WIDTH, D_MODEL), SSM_WIDTH ** -0.5),
        "attn_sinks": nrm(ks[15], (L, N_Q_HEADS), 1.0),
        "w_br_attn": nrm(ks[16], (L, Q_WIDTH, D_MODEL), Q_WIDTH ** -0.5),
        "w_o": nrm(ks[17], (L, D_MODEL, D_MODEL), D_MODEL ** -0.5),
        "norm_ffn_g": 1.0 + nrm(ks[18], (L, D_MODEL), 0.02),
        "w_router_group": nrm(ks[19], (L, D_MODEL, N_GROUPS), D_MODEL ** -0.5),
        "b_router_group": nrm(ks[20], (L, N_GROUPS), 0.01),
        "w_router_expert": nrm(ks[21], (L, D_MODEL, N_EXPERTS), D_MODEL ** -0.5),
        "b_router_expert": nrm(ks[22], (L, N_EXPERTS), 0.01),
        "w_exp_gate": nrm(ks[23], (L, N_EXPERTS, D_MODEL, D_EXPERT), D_MODEL ** -0.5),
        "w_exp_up": nrm(ks[24], (L, N_EXPERTS, D_MODEL, D_EXPERT), D_MODEL ** -0.5),
        "w_exp_down": nrm(ks[25], (L, N_EXPERTS, D_EXPERT, D_MODEL), D_EXPERT ** -0.5),
        "norm_final_g": 1.0 + nrm(ks[26], (D_MODEL,), 0.02),
    }


def reference(x, positions, norm_mix_g, w_in, b_in, ssm_a_re, ssm_a_im, ssm_log_dt,
              ssm_b_re, ssm_b_im, ssm_c_re, ssm_c_im, ssm_d, w_glu, b_glu, w_br_ssm,
              attn_sinks, w_br_attn, w_o, norm_ffn_g, w_router_group, b_router_group,
              w_router_expert, b_router_expert, w_exp_gate, w_exp_up, w_exp_down, norm_final_g):
    bsz, seq = x.shape[0], x.shape[1]
    splits = [SSM_WIDTH, SSM_WIDTH + Q_WIDTH, SSM_WIDTH + Q_WIDTH + KV_WIDTH,
              SSM_WIDTH + Q_WIDTH + 2 * KV_WIDTH]
    h = x
    for l in range(DEPTH):
        xn = rmsnorm(h, norm_mix_g[l])
        proj = xn @ w_in[l] + b_in[l]
        u, q, k, v, gl = jnp.split(proj, splits, axis=-1)
        y = s5_ssm(u.reshape(bsz, seq, SSM_GROUPS, SSM_GROUP), ssm_a_re[l], ssm_a_im[l],
                   ssm_log_dt[l], ssm_b_re[l], ssm_b_im[l], ssm_c_re[l], ssm_c_im[l], ssm_d[l])
        z = jax.nn.gelu(y.reshape(bsz, seq, SSM_WIDTH)).astype(x.dtype)
        ga, gb = jnp.split(z @ w_glu[l] + b_glu[l], 2, axis=-1)
        y_ssm = (ga * jax.nn.sigmoid(gb)) @ w_br_ssm[l]
        qh = partial_rope(q.reshape(bsz, seq, N_Q_HEADS, HEAD_DIM), positions)
        kh = partial_rope(k.reshape(bsz, seq, N_KV_HEADS, HEAD_DIM), positions)
        vh = v.reshape(bsz, seq, N_KV_HEADS, HEAD_DIM)
        o = sliding_window_attention(qh, kh, vh, attn_sinks[l]).astype(x.dtype)
        y_attn = o @ w_br_attn[l]
        gates = jax.nn.sigmoid(gl.reshape(bsz, seq, N_BRANCH, D_MODEL))
        mixed = gates[:, :, 0, :] * y_ssm + gates[:, :, 1, :] * y_attn
        h = h + mixed @ w_o[l]
        hn = rmsnorm(h, norm_ffn_g[l])
        h = h + hierarchical_moe(hn, w_router_group[l], b_router_group[l], w_router_expert[l],
                                 b_router_expert[l], w_exp_gate[l], w_exp_up[l], w_exp_down[l])
    return rmsnorm(h, norm_final_g)
```

```python
import functools
import math

import numpy as np
import jax
import jax.numpy as jnp
from jax import lax
from jax.experimental import pallas as pl
from jax.experimental.pallas import tpu as pltpu

F32 = jnp.float32
BF16 = jnp.bfloat16
I32 = jnp.int32

D_MODEL = 2048
SSM_WIDTH = 1024
SSM_GROUP = 16
SSM_GROUPS = 64
SSM_STATE = 64
HEAD_DIM = 64
N_Q_HEADS = 16
N_KV_HEADS = 4
Q_PER_KV = 4
WINDOW = 128
ROPE_DIM = 16
ROPE_THETA = 500000.0
Q_WIDTH = 1024
KV_WIDTH = 256
IN_WIDTH = SSM_WIDTH + Q_WIDTH + 2 * KV_WIDTH + 2 * D_MODEL
N_GROUPS = 8
EXPERTS_PER_GROUP = 8
N_EXPERTS = 64
TOP_K = 2
D_EXPERT = 512
MOE_BLOCK = 128
EPS = 1e-6

CHUNK = 16
CW = CHUNK * SSM_GROUP
GROUP_BLOCK = 8
HALF = D_MODEL // 2
LANES = 128
VMEM_LIMIT = 56 * 1024 * 1024

COL_G0, COL_G1, COL_U, COL_Q, COL_K, COL_V = 0, 2048, 4096, 5120, 6144, 6400

HIGHEST = lax.Precision.HIGHEST


def _dot(a, b, precision=None):
    return jnp.dot(a, b, preferred_element_type=F32, precision=precision)


def _dot_nt(a, b, precision=None):
    return lax.dot_general(a, b, (((1,), (1,)), ((), ())), preferred_element_type=F32,
                           precision=precision)


def _sigmoid(x):
    return 1.0 / (1.0 + jnp.exp(-x))


def _pack_halves(lo, hi):
    return pltpu.pack_elementwise([lo, hi], packed_dtype=BF16)


def _unpack_half(w, index):
    return pltpu.unpack_elementwise(w, index=index, packed_dtype=BF16, unpacked_dtype=F32)


def _params(sem, vmem=VMEM_LIMIT):
    return pltpu.CompilerParams(dimension_semantics=sem, vmem_limit_bytes=vmem)


def _inproj_kernel(x_ref, g_ref, w_ref, b_ref, o_ref, xn_ref):
    @pl.when(pl.program_id(1) == 0)
    def _():
        x = x_ref[...]
        ms = jnp.mean(x * x, axis=-1, keepdims=True)
        xn_ref[...] = (x * lax.rsqrt(ms + EPS) * g_ref[...]).astype(BF16)

    acc = _dot(xn_ref[...], w_ref[...])
    o_ref[...] = (acc + b_ref[...]).astype(o_ref.dtype)


def _inproj(x, g, w_bf16, b, tm=1024, tn=512):
    t, d = x.shape
    n = w_bf16.shape[1]
    return pl.pallas_call(
        _inproj_kernel,
        grid=(t // tm, n // tn),
        in_specs=[
            pl.BlockSpec((tm, d), lambda i, j: (i, 0)),
            pl.BlockSpec((1, d), lambda i, j: (0, 0)),
            pl.BlockSpec((d, tn), lambda i, j: (0, j)),
            pl.BlockSpec((1, tn), lambda i, j: (0, j)),
        ],
        out_specs=pl.BlockSpec((tm, tn), lambda i, j: (i, j)),
        out_shape=jax.ShapeDtypeStruct((t, n), BF16),
        scratch_shapes=[pltpu.VMEM((tm, d), BF16)],
        compiler_params=_params(("arbitrary", "arbitrary")),
        name="inproj",
    )(x, g, w_bf16, b)


def _ssm_prep_kernel(are_ref, aim_ref, ldt_ref, btr_ref, bti_ref, cr_ref, ci_ref,
                     t_ref, wsr_ref, wsi_ref, rxr_ref, rxi_ref, a16r_ref, a16i_ref):
    lam_re = jnp.minimum(are_ref[0], -1e-4)
    lam_im = aim_ref[0]
    dt = jnp.exp(ldt_ref[0])
    lr_dt = lam_re * dt
    th = lam_im * dt
    mag = jnp.exp(lr_dt)
    ab_re = mag * jnp.cos(th)
    ab_im = mag * jnp.sin(th)
    den = lam_re * lam_re + lam_im * lam_im
    nr = ab_re - 1.0
    ni = ab_im
    coef_re = (nr * lam_re + ni * lam_im) / den
    coef_im = (ni * lam_re - nr * lam_im) / den
    btr = btr_ref[0]
    bti = bti_ref[0]
    bb_re = coef_re * btr - coef_im * bti
    bb_im = coef_re * bti + coef_im * btr

    e = lax.broadcasted_iota(I32, (CHUNK, SSM_STATE), 0).astype(F32)
    pmag = jnp.exp(e * lr_dt)
    pos_re = pmag * jnp.cos(e * th)
    pos_im = pmag * jnp.sin(e * th)
    nmag = jnp.exp(-e * lr_dt)
    neg_re = nmag * jnp.cos(e * th)
    neg_im = -nmag * jnp.sin(e * th)

    row = lax.broadcasted_iota(I32, (CW, CHUNK), 0)
    col = lax.broadcasted_iota(I32, (CW, CHUNK), 1)
    rep = ((row // SSM_GROUP) == col).astype(F32)
    tile = ((row % SSM_GROUP) == col).astype(F32)

    def expand(sel, m):
        return _dot(sel, m, precision=HIGHEST)

    pr, pi = expand(rep, pos_re), expand(rep, pos_im)
    ctr, cti = expand(tile, cr_ref[0]), expand(tile, ci_ref[0])
    r_re = ctr * pr - cti * pi
    r_im = ctr * pi + cti * pr
    qr, qi = expand(rep, neg_re), expand(rep, neg_im)
    btr_t, bti_t = expand(tile, bb_re), expand(tile, bb_im)
    l_re = btr_t * qr - bti_t * qi
    l_im = btr_t * qi + bti_t * qr

    tm = _dot_nt(l_re, r_re, precision=HIGHEST) - _dot_nt(l_im, r_im, precision=HIGHEST)
    srow = lax.broadcasted_iota(I32, (CW, CW), 0) // SSM_GROUP
    tcol = lax.broadcasted_iota(I32, (CW, CW), 1) // SSM_GROUP
    t_ref[0] = jnp.where(tcol >= srow, tm, 0.0).astype(BF16)

    a15r = pos_re[CHUNK - 1:CHUNK, :]
    a15i = pos_im[CHUNK - 1:CHUNK, :]
    wsr_ref[0] = (l_re * a15r - l_im * a15i).astype(BF16)
    wsi_ref[0] = (l_re * a15i + l_im * a15r).astype(BF16)
    rxr_ref[0] = (r_re * ab_re - r_im * ab_im).astype(BF16)
    rxi_ref[0] = (-(r_re * ab_im + r_im * ab_re)).astype(BF16)
    m16 = jnp.exp(float(CHUNK) * lr_dt)
    a16r_ref[0] = m16 * jnp.cos(float(CHUNK) * th)
    a16i_ref[0] = m16 * jnp.sin(float(CHUNK) * th)


def _ssm_prep(a_re, a_im, log_dt, bt_re, bt_im, c_re, c_im):
    g = a_re.shape[0]
    vec = pl.BlockSpec((1, 1, SSM_STATE), lambda i: (i, 0, 0))
    mat = pl.BlockSpec((1, SSM_GROUP, SSM_STATE), lambda i: (i, 0, 0))
    wide = pl.BlockSpec((1, CW, SSM_STATE), lambda i: (i, 0, 0))
    return pl.pallas_call(
        _ssm_prep_kernel,
        grid=(g,),
        in_specs=[vec, vec, pl.BlockSpec((1, 1, 1), lambda i: (i, 0, 0)), mat, mat, mat, mat],
        out_specs=[pl.BlockSpec((1, CW, CW), lambda i: (i, 0, 0)), wide, wide, wide, wide, vec, vec],
        out_shape=[
            jax.ShapeDtypeStruct((g, CW, CW), BF16),
            jax.ShapeDtypeStruct((g, CW, SSM_STATE), BF16),
            jax.ShapeDtypeStruct((g, CW, SSM_STATE), BF16),
            jax.ShapeDtypeStruct((g, CW, SSM_STATE), BF16),
            jax.ShapeDtypeStruct((g, CW, SSM_STATE), BF16),
            jax.ShapeDtypeStruct((g, 1, SSM_STATE), F32),
            jax.ShapeDtypeStruct((g, 1, SSM_STATE), F32),
        ],
        compiler_params=_params(("arbitrary",)),
        name="ssm_prep",
    )(a_re.reshape(g, 1, SSM_STATE), a_im.reshape(g, 1, SSM_STATE), log_dt.reshape(g, 1, 1),
      bt_re, bt_im, c_re, c_im)


def _gelu_tanh(x):
    c = math.sqrt(2.0 / math.pi)
    return x * (0.5 * (1.0 + jnp.tanh(c * (x + 0.044715 * (x * x * x)))))


def _ssm_kernel(u_ref, t_ref, wsr_ref, wsi_ref, rxr_ref, rxi_ref, a16r_ref, a16i_ref, d_ref,
                z_ref, sr_ref, si_ref):
    nk = u_ref.shape[1]
    for j in range(GROUP_BLOCK):
        u = u_ref[j]
        sr_ref[j * nk:(j + 1) * nk, :] = _dot(u, wsr_ref[j])
        si_ref[j * nk:(j + 1) * nk, :] = _dot(u, wsi_ref[j])

    ar = a16r_ref[...]
    ai = a16i_ref[...]

    def step(k, carry):
        xr, xi = carry
        rows = pl.ds(k, GROUP_BLOCK, stride=nk)
        sr = sr_ref[rows, :]
        si = si_ref[rows, :]
        sr_ref[rows, :] = xr
        si_ref[rows, :] = xi
        return (ar * xr - ai * xi + sr, ar * xi + ai * xr + si)

    zero = jnp.zeros((GROUP_BLOCK, SSM_STATE), F32)
    lax.fori_loop(0, nk, step, (zero, zero))

    for j in range(GROUP_BLOCK):
        u = u_ref[j]
        xr = sr_ref[j * nk:(j + 1) * nk, :].astype(BF16)
        xi = si_ref[j * nk:(j + 1) * nk, :].astype(BF16)
        y = (_dot(u, t_ref[j]) + _dot_nt(xr, rxr_ref[j]) + _dot_nt(xi, rxi_ref[j])
             + d_ref[j] * u.astype(F32))
        z_ref[j] = _gelu_tanh(y).astype(z_ref.dtype)


def _ssm(u_g, tmat, wsr, wsi, rxr, rxi, a16r, a16i, d_tiled):
    g, nk, _ = u_g.shape
    gb = GROUP_BLOCK
    blk3 = lambda a, b: pl.BlockSpec((gb, a, b), lambda i: (i, 0, 0))
    return pl.pallas_call(
        _ssm_kernel,
        grid=(g // gb,),
        in_specs=[blk3(nk, CW), blk3(CW, CW), blk3(CW, SSM_STATE), blk3(CW, SSM_STATE),
                  blk3(CW, SSM_STATE), blk3(CW, SSM_STATE),
                  pl.BlockSpec((gb, SSM_STATE), lambda i: (i, 0)),
                  pl.BlockSpec((gb, SSM_STATE), lambda i: (i, 0)),
                  blk3(1, CW)],
        out_specs=blk3(nk, CW),
        out_shape=jax.ShapeDtypeStruct((g, nk, CW), BF16),
        scratch_shapes=[pltpu.VMEM((gb * nk, SSM_STATE), F32), pltpu.VMEM((gb * nk, SSM_STATE), F32)],
        compiler_params=_params(("arbitrary",)),
        name="ssm",
    )(u_g, tmat, wsr, wsi, rxr, rxi, a16r, a16i, d_tiled)


def _glu_kernel(z_ref, wg_ref, bg_ref, wb_ref, g0_ref, o_ref):
    h = _dot(z_ref[...], wg_ref[...]) + bg_ref[...]
    ga = h[:, :SSM_WIDTH]
    gb = h[:, SSM_WIDTH:]
    a = (ga * _sigmoid(gb)).astype(BF16)
    y = _dot(a, wb_ref[...])
    o_ref[...] = (_sigmoid(g0_ref[...].astype(F32)) * y).astype(o_ref.dtype)


def _glu(z, w_glu, b_glu, w_br, proj, tm=512):
    t = z.shape[0]
    return pl.pallas_call(
        _glu_kernel,
        grid=(t // tm,),
        in_specs=[
            pl.BlockSpec((tm, SSM_WIDTH), lambda i: (i, 0)),
            pl.BlockSpec((SSM_WIDTH, 2 * SSM_WIDTH), lambda i: (0, 0)),
            pl.BlockSpec((1, 2 * SSM_WIDTH), lambda i: (0, 0)),
            pl.BlockSpec((SSM_WIDTH, D_MODEL), lambda i: (0, 0)),
            pl.BlockSpec((tm, D_MODEL), lambda i: (i, COL_G0 // D_MODEL)),
        ],
        out_specs=pl.BlockSpec((tm, D_MODEL), lambda i: (i, 0)),
        out_shape=jax.ShapeDtypeStruct((t, D_MODEL), BF16),
        compiler_params=_params(("arbitrary",)),
        name="glu",
    )(z, w_glu, b_glu, w_br, proj)


def _rope_pattern():
    half = ROPE_DIM // 2
    inv_freq = (np.float32(ROPE_THETA) ** (-np.arange(half, dtype=np.float32) / np.float32(half))).astype(np.float32)
    d = np.arange(LANES) % HEAD_DIM
    freq = np.where(d < ROPE_DIM, inv_freq[d % half], 0.0).astype(np.float32)
    m_up = np.where(d < half, -1.0, 0.0).astype(np.float32)
    m_dn = np.where((d >= half) & (d < ROPE_DIM), 1.0, 0.0).astype(np.float32)
    return np.stack([freq, m_up, m_dn] + [np.zeros(LANES, np.float32)] * 5)


def _attn_kernel(q_ref, k_ref, v_ref, pos_ref, pat_ref, sink_ref, wbr_ref, ysg_ref, g1_ref,
                 o_ref, qbuf, kbuf, vbuf, obuf):
    i = pl.program_id(0)
    tq = q_ref.shape[0]
    nw = tq // WINDOW
    half = ROPE_DIM // 2

    @pl.when(i == 0)
    def _():
        kbuf[0:WINDOW, :] = jnp.zeros((WINDOW, KV_WIDTH), BF16)
        vbuf[0:WINDOW, :] = jnp.zeros((WINDOW, KV_WIDTH), BF16)

    pos = pos_ref[...].astype(F32)
    ang = pos * pat_ref[0:1, :]
    cs = jnp.cos(ang)
    sn = jnp.sin(ang)
    c_up = sn * pat_ref[1:2, :]
    c_dn = sn * pat_ref[2:3, :]

    def rope(x):
        return (x * cs + pltpu.roll(x, LANES - half, 1) * c_up + pltpu.roll(x, half, 1) * c_dn)

    for cb in range(Q_WIDTH // LANES):
        sl = slice(cb * LANES, (cb + 1) * LANES)
        qbuf[:, sl] = rope(q_ref[:, sl].astype(F32)).astype(BF16)
    for cb in range(KV_WIDTH // LANES):
        sl = slice(cb * LANES, (cb + 1) * LANES)
        kbuf[WINDOW:, sl] = rope(k_ref[:, sl].astype(F32)).astype(BF16)
    vbuf[WINDOW:, :] = v_ref[...]

    qi = lax.broadcasted_iota(I32, (WINDOW, 2 * WINDOW), 0)
    kj = lax.broadcasted_iota(I32, (WINDOW, 2 * WINDOW), 1)
    dist = qi + WINDOW - kj
    in_band = (dist >= 0) & (dist < WINDOW)
    cur_only = kj >= WINDOW
    sinks = sink_ref[...]
    scale = HEAD_DIM ** -0.5

    def window(w, carry):
        r0 = pl.multiple_of(w * WINDOW, WINDOW)
        not_first = (i * nw + w) > 0
        mask = in_band & (cur_only | not_first)
        for h in range(N_Q_HEADS):
            kv = h // Q_PER_KV
            qh = qbuf[pl.ds(r0, WINDOW), h * HEAD_DIM:(h + 1) * HEAD_DIM]
            kb = kbuf[pl.ds(r0, 2 * WINDOW), kv * HEAD_DIM:(kv + 1) * HEAD_DIM]
            vb = vbuf[pl.ds(r0, 2 * WINDOW), kv * HEAD_DIM:(kv + 1) * HEAD_DIM]
            s = _dot_nt(qh, kb) * scale
            s = jnp.where(mask, s, -jnp.inf)
            sink = sinks[:, h:h + 1]
            m = jnp.maximum(jnp.max(s, axis=-1, keepdims=True), sink)
            p = jnp.exp(s - m)
            denom = jnp.sum(p, axis=-1, keepdims=True) + jnp.exp(sink - m)
            o = _dot(p.astype(BF16), vb) / denom
            obuf[pl.ds(r0, WINDOW), h * HEAD_DIM:(h + 1) * HEAD_DIM] = o.astype(BF16)
        return carry

    lax.fori_loop(0, nw, window, 0)

    kbuf[0:WINDOW, :] = kbuf[tq:tq + WINDOW, :]
    vbuf[0:WINDOW, :] = vbuf[tq:tq + WINDOW, :]

    y = _dot(obuf[...], wbr_ref[...])
    o_ref[...] = (ysg_ref[...].astype(F32) + _sigmoid(g1_ref[...].astype(F32)) * y).astype(o_ref.dtype)


def _attn(proj, pos_col, sinks, w_br, ysg, tq=512):
    t = proj.shape[0]
    pat = jnp.asarray(_rope_pattern())
    return pl.pallas_call(
        _attn_kernel,
        grid=(t // tq,),
        in_specs=[
            pl.BlockSpec((tq, Q_WIDTH), lambda i: (i, COL_Q // Q_WIDTH)),
            pl.BlockSpec((tq, KV_WIDTH), lambda i: (i, COL_K // KV_WIDTH)),
            pl.BlockSpec((tq, KV_WIDTH), lambda i: (i, COL_V // KV_WIDTH)),
            pl.BlockSpec((tq, 1), lambda i: (i, 0)),
            pl.BlockSpec((8, LANES), lambda i: (0, 0)),
            pl.BlockSpec((1, N_Q_HEADS), lambda i: (0, 0)),
            pl.BlockSpec((Q_WIDTH, D_MODEL), lambda i: (0, 0)),
            pl.BlockSpec((tq, D_MODEL), lambda i: (i, 0)),
            pl.BlockSpec((tq, D_MODEL), lambda i: (i, COL_G1 // D_MODEL)),
        ],
        out_specs=pl.BlockSpec((tq, D_MODEL), lambda i: (i, 0)),
        out_shape=jax.ShapeDtypeStruct((t, D_MODEL), BF16),
        scratch_shapes=[
            pltpu.VMEM((tq, Q_WIDTH), BF16),
            pltpu.VMEM((tq + WINDOW, KV_WIDTH), BF16),
            pltpu.VMEM((tq + WINDOW, KV_WIDTH), BF16),
            pltpu.VMEM((tq, Q_WIDTH), BF16),
        ],
        compiler_params=_params(("arbitrary",)),
        name="attn",
    )(proj, proj, proj, pos_col, pat, sinks, w_br, ysg, proj)


def _oproj_kernel(x_ref, mix_ref, wo_ref, g_ref, wr_ref, br_ref, h_ref, hp_ref, lt_ref):
    h = x_ref[...] + _dot(mix_ref[...], wo_ref[...])
    h_ref[...] = h
    ms = jnp.mean(h * h, axis=-1, keepdims=True)
    hn = h * lax.rsqrt(ms + EPS) * g_ref[...]
    hp_ref[...] = _pack_halves(hn[:, :HALF], hn[:, HALF:])
    logits = _dot(hn, wr_ref[...], precision=HIGHEST) + br_ref[...]
    lt_ref[...] = logits.T


def _oproj(x, mixed, w_o, g, w_router, b_router, tm=512):
    t = x.shape[0]
    return pl.pallas_call(
        _oproj_kernel,
        grid=(t // tm,),
        in_specs=[
            pl.BlockSpec((tm, D_MODEL), lambda i: (i, 0)),
            pl.BlockSpec((tm, D_MODEL), lambda i: (i, 0)),
            pl.BlockSpec((D_MODEL, D_MODEL), lambda i: (0, 0)),
            pl.BlockSpec((1, D_MODEL), lambda i: (0, 0)),
            pl.BlockSpec((D_MODEL, LANES), lambda i: (0, 0)),
            pl.BlockSpec((1, LANES), lambda i: (0, 0)),
        ],
        out_specs=[
            pl.BlockSpec((tm, D_MODEL), lambda i: (i, 0)),
            pl.BlockSpec((tm, HALF), lambda i: (i, 0)),
            pl.BlockSpec((LANES, tm), lambda i: (0, i)),
        ],
        out_shape=[
            jax.ShapeDtypeStruct((t, D_MODEL), F32),
            jax.ShapeDtypeStruct((t, HALF), jnp.uint32),
            jax.ShapeDtypeStruct((LANES, t), F32),
        ],
        compiler_params=_params(("arbitrary",)),
        name="oproj",
    )(x, mixed, w_o, g, w_router, b_router)


ROUTE_CHUNK = 256


def _route_kernel(lt_ref, dest_ref, wts_ref, bexp_ref, eid_ref, rank_ref):
    t = lt_ref.shape[1]
    nc = t // ROUTE_CHUNK
    r8 = lax.broadcasted_iota(I32, (N_GROUPS, ROUTE_CHUNK), 0)
    r64 = lax.broadcasted_iota(I32, (N_EXPERTS, ROUTE_CHUNK), 0)

    def pick(c, carry):
        cols = pl.ds(pl.multiple_of(c * ROUTE_CHUNK, ROUTE_CHUNK), ROUTE_CHUNK)
        lg = lt_ref[0:N_GROUPS, cols]
        m = jnp.max(lg, axis=0, keepdims=True)
        ssum = jnp.sum(jnp.exp(lg - m), axis=0, keepdims=True)
        p_grp = 1.0 / ssum
        grp = jnp.min(jnp.where(lg == m, r8, N_GROUPS), axis=0, keepdims=True)
        le = lt_ref[N_GROUPS:N_GROUPS + N_EXPERTS, cols]
        leg = jnp.where((r64 // EXPERTS_PER_GROUP) == grp, le, -jnp.inf)
        m1 = jnp.max(leg, axis=0, keepdims=True)
        i1 = jnp.min(jnp.where(leg == m1, r64, N_EXPERTS), axis=0, keepdims=True)
        leg2 = jnp.where(r64 == i1, -jnp.inf, leg)
        m2 = jnp.max(leg2, axis=0, keepdims=True)
        i2 = jnp.min(jnp.where(leg2 == m2, r64, N_EXPERTS), axis=0, keepdims=True)
        ex = jnp.exp(m2 - m1)
        eid_ref[0:1, cols] = i1
        eid_ref[1:2, cols] = i2
        wts_ref[0:1, cols] = p_grp / (1.0 + ex)
        wts_ref[1:2, cols] = p_grp * ex / (1.0 + ex)
        return carry

    lax.fori_loop(0, nc, pick, 0)

    a_row = lax.broadcasted_iota(I32, (ROUTE_CHUNK, ROUTE_CHUNK), 0)
    a_col = lax.broadcasted_iota(I32, (ROUTE_CHUNK, ROUTE_CHUNK), 1)
    before = (a_row < a_col).astype(BF16)

    def count(n, carry):
        j = n // nc
        c = n - j * nc
        cols = pl.ds(pl.multiple_of(c * ROUTE_CHUNK, ROUTE_CHUNK), ROUTE_CHUNK)
        oh = r64 == eid_ref[pl.ds(j, 1), cols]
        ohf = oh.astype(F32)
        pref = _dot(ohf.astype(BF16), before) + carry
        rank_ref[pl.ds(j, 1), cols] = jnp.sum(jnp.where(oh, pref, 0.0), axis=0, keepdims=True)
        return carry + jnp.sum(ohf, axis=1, keepdims=True)

    counts = lax.fori_loop(0, TOP_K * nc, count, jnp.zeros((N_EXPERTS, 1), F32))

    padded = jnp.floor((counts + (MOE_BLOCK - 1)) * (1.0 / MOE_BLOCK)) * MOE_BLOCK
    e_row = lax.broadcasted_iota(I32, (N_EXPERTS, N_EXPERTS), 0)
    e_col = lax.broadcasted_iota(I32, (N_EXPERTS, N_EXPERTS), 1)
    incl = (e_col <= e_row).astype(F32)
    pad_end = _dot(incl, jnp.broadcast_to(padded, (N_EXPERTS, LANES)), precision=HIGHEST)[:, 0:1]
    pad_start = pad_end - padded

    def place(n, carry):
        j = n // nc
        c = n - j * nc
        cols = pl.ds(pl.multiple_of(c * ROUTE_CHUNK, ROUTE_CHUNK), ROUTE_CHUNK)
        oh = r64 == eid_ref[pl.ds(j, 1), cols]
        start = jnp.sum(jnp.where(oh, pad_start, 0.0), axis=0, keepdims=True)
        dest_ref[pl.ds(j, 1), cols] = (start + rank_ref[pl.ds(j, 1), cols]).astype(I32)
        return carry

    lax.fori_loop(0, TOP_K * nc, place, 0)

    b0 = (lax.broadcasted_iota(I32, (N_EXPERTS, bexp_ref.shape[1]), 1) * MOE_BLOCK).astype(F32)
    n_done = jnp.sum((pad_end <= b0).astype(F32), axis=0, keepdims=True)
    bexp_ref[...] = jnp.minimum(n_done, float(N_EXPERTS - 1)).astype(I32)


def _route(logits_t, n_blocks_pad):
    t = logits_t.shape[1]
    return pl.pallas_call(
        _route_kernel,
        out_shape=[
            jax.ShapeDtypeStruct((TOP_K, t), I32),
            jax.ShapeDtypeStruct((TOP_K, t), F32),
            jax.ShapeDtypeStruct((1, n_blocks_pad), I32),
        ],
        scratch_shapes=[pltpu.VMEM((TOP_K, t), I32), pltpu.VMEM((TOP_K, t), F32)],
        compiler_params=pltpu.CompilerParams(vmem_limit_bytes=VMEM_LIMIT),
        name="route",
    )(logits_t)


def _invmap_kernel(dest_ref, rt_ref):
    n_rows = rt_ref.shape[0]
    t = dest_ref.shape[0] // TOP_K

    def clear(r, c):
        rt_ref[r] = 0
        return c

    lax.fori_loop(0, n_rows, clear, 0)

    def put(tok, c):
        rt_ref[dest_ref[tok]] = tok
        rt_ref[dest_ref[t + tok]] = tok
        return c

    lax.fori_loop(0, t, put, 0)


def _invmap(dest_flat, n_rows):
    return pl.pallas_call(
        _invmap_kernel,
        in_specs=[pl.BlockSpec(memory_space=pltpu.SMEM)],
        out_specs=pl.BlockSpec(memory_space=pltpu.SMEM),
        out_shape=jax.ShapeDtypeStruct((n_rows,), I32),
        name="invmap",
    )(dest_flat)


def _row_copy(src_hbm, src_row, dst_buf, slot, dst_row, sem):
    return pltpu.make_async_copy(src_hbm.at[pl.ds(src_row, 1), :],
                                 dst_buf.at[slot, pl.ds(dst_row, 1), :], sem.at[slot])


def _expert_kernel(bexp_ref, rt_ref, hp_hbm, wg_ref, wu_ref, wd_ref, ys_ref, xbuf, sem):
    b = pl.program_id(0)
    nb = pl.num_programs(0)

    def start_gather(blk, slot):
        base = blk * MOE_BLOCK

        def body(r, c):
            _row_copy(hp_hbm, rt_ref[base + r], xbuf, slot, r, sem).start()
            return c

        lax.fori_loop(0, MOE_BLOCK, body, 0)

    def wait_gather(slot):
        def body(r, c):
            _row_copy(hp_hbm, 0, xbuf, slot, r, sem).wait()
            return c

        lax.fori_loop(0, MOE_BLOCK, body, 0)

    @pl.when(b == 0)
    def _():
        start_gather(0, 0)

    @pl.when(b + 1 < nb)
    def _():
        start_gather(b + 1, (b + 1) % 2)

    slot = b % 2
    wait_gather(slot)
    xw = xbuf[slot]
    lo = _unpack_half(xw, 0).astype(BF16)
    hi = _unpack_half(xw, 1).astype(BF16)

    def proj(w_ref):
        return (_dot(lo, w_ref[0, :HALF, :].astype(BF16)) + _dot(hi, w_ref[0, HALF:, :].astype(BF16)))

    g = proj(wg_ref)
    u = proj(wu_ref)
    h = (g * _sigmoid(g) * u).astype(BF16)
    y = _dot(h, wd_ref[0].astype(BF16))
    ys_ref[...] = _pack_halves(y[:, :HALF], y[:, HALF:])


def _experts(bexp, row_tok, hn_packed, w_gate, w_up, w_down, n_blocks):
    grid_spec = pltpu.PrefetchScalarGridSpec(
        num_scalar_prefetch=2,
        grid=(n_blocks,),
        in_specs=[
            pl.BlockSpec(memory_space=pl.ANY),
            pl.BlockSpec((1, D_MODEL, D_EXPERT), lambda b, be, rt: (be[b], 0, 0)),
            pl.BlockSpec((1, D_MODEL, D_EXPERT), lambda b, be, rt: (be[b], 0, 0)),
            pl.BlockSpec((1, D_EXPERT, D_MODEL), lambda b, be, rt: (be[b], 0, 0)),
        ],
        out_specs=pl.BlockSpec((MOE_BLOCK, HALF), lambda b, be, rt: (b, 0)),
        scratch_shapes=[pltpu.VMEM((2, MOE_BLOCK, HALF), jnp.uint32), pltpu.SemaphoreType.DMA((2,))],
    )
    return pl.pallas_call(
        _expert_kernel,
        grid_spec=grid_spec,
        out_shape=jax.ShapeDtypeStruct((n_blocks * MOE_BLOCK, HALF), jnp.uint32),
        compiler_params=_params(("arbitrary",)),
        name="experts",
    )(bexp, row_tok, hn_packed, w_gate, w_up, w_down)


def _combine_kernel(dest_ref, ys_hbm, h_ref, w_ref, g_ref, o_ref, ybuf, sem):
    i = pl.program_id(0)
    n = pl.num_programs(0)
    tq = h_ref.shape[0]
    t = n * tq

    def start_gather(blk, slot):
        base = blk * tq

        def body(r, c):
            _row_copy(ys_hbm, dest_ref[base + r], ybuf, slot, r, sem).start()
            _row_copy(ys_hbm, dest_ref[t + base + r], ybuf, slot, tq + r, sem).start()
            return c

        lax.fori_loop(0, tq, body, 0)

    def wait_gather(slot):
        def body(r, c):
            _row_copy(ys_hbm, 0, ybuf, slot, r, sem).wait()
            return c

        lax.fori_loop(0, TOP_K * tq, body, 0)

    @pl.when(i == 0)
    def _():
        start_gather(0, 0)

    @pl.when(i + 1 < n)
    def _():
        start_gather(i + 1, (i + 1) % 2)

    slot = i % 2
    wait_gather(slot)
    y0 = ybuf[slot, 0:tq, :]
    y1 = ybuf[slot, tq:2 * tq, :]
    w = w_ref[...]
    w0 = w[:, 0:1]
    w1 = w[:, 1:2]
    h = h_ref[...]
    lo = h[:, :HALF] + w0 * _unpack_half(y0, 0) + w1 * _unpack_half(y1, 0)
    hi = h[:, HALF:] + w0 * _unpack_half(y0, 1) + w1 * _unpack_half(y1, 1)
    ms = (jnp.sum(lo * lo, axis=-1, keepdims=True) + jnp.sum(hi * hi, axis=-1, keepdims=True)) * (1.0 / D_MODEL)
    inv = lax.rsqrt(ms + EPS)
    g = g_ref[...]
    o_ref[:, :HALF] = lo * inv * g[:, :HALF]
    o_ref[:, HALF:] = hi * inv * g[:, HALF:]


def _combine(dest_flat, ys_packed, h1, wts_tok, g_final, tq=256):
    t = h1.shape[0]
    grid_spec = pltpu.PrefetchScalarGridSpec(
        num_scalar_prefetch=1,
        grid=(t // tq,),
        in_specs=[
            pl.BlockSpec(memory_space=pl.ANY),
            pl.BlockSpec((tq, D_MODEL), lambda i, d: (i, 0)),
            pl.BlockSpec((tq, TOP_K), lambda i, d: (i, 0)),
            pl.BlockSpec((1, D_MODEL), lambda i, d: (0, 0)),
        ],
        out_specs=pl.BlockSpec((tq, D_MODEL), lambda i, d: (i, 0)),
        scratch_shapes=[pltpu.VMEM((2, TOP_K * tq, HALF), jnp.uint32), pltpu.SemaphoreType.DMA((2,))],
    )
    return pl.pallas_call(
        _combine_kernel,
        grid_spec=grid_spec,
        out_shape=jax.ShapeDtypeStruct((t, D_MODEL), F32),
        compiler_params=_params(("arbitrary",)),
        name="combine",
    )(dest_flat, ys_packed, h1, wts_tok, g_final)


def _mixers(h, positions, norm_g, w_in, b_in, a_re, a_im, log_dt, b_re, b_im, c_re, c_im, d_skip,
            w_glu, b_glu, w_br_ssm, sinks, w_br_attn):
    t = h.shape[0]
    ngate = 2 * D_MODEL
    w_perm = jnp.concatenate([w_in[:, IN_WIDTH - ngate:], w_in[:, :IN_WIDTH - ngate]], axis=1).astype(BF16)
    b_perm = jnp.concatenate([b_in[IN_WIDTH - ngate:], b_in[:IN_WIDTH - ngate]]).reshape(1, IN_WIDTH)
    proj = _inproj(h, norm_g.reshape(1, D_MODEL), w_perm, b_perm)

    nk = t // CHUNK
    u_g = (proj[:, COL_U:COL_U + SSM_WIDTH].reshape(nk, CHUNK, SSM_GROUPS, SSM_GROUP)
           .transpose(2, 0, 1, 3).reshape(SSM_GROUPS, nk, CW))
    tmat, wsr, wsi, rxr, rxi, a16r, a16i = _ssm_prep(
        a_re, a_im, log_dt, b_re.transpose(0, 2, 1), b_im.transpose(0, 2, 1), c_re, c_im)
    d_tiled = jnp.tile(d_skip, (1, CHUNK)).reshape(SSM_GROUPS, 1, CW)
    z_g = _ssm(u_g, tmat, wsr, wsi, rxr, rxi, a16r.reshape(SSM_GROUPS, SSM_STATE),
               a16i.reshape(SSM_GROUPS, SSM_STATE), d_tiled)
    z = (z_g.reshape(SSM_GROUPS, nk, CHUNK, SSM_GROUP).transpose(1, 2, 0, 3).reshape(t, SSM_WIDTH))

    ysg = _glu(z, w_glu.astype(BF16), b_glu.reshape(1, -1), w_br_ssm.astype(BF16), proj)
    return _attn(proj, positions.reshape(t, 1), sinks.reshape(1, N_Q_HEADS), w_br_attn.astype(BF16), ysg)


def _moe_tail(h, mixed, w_o, norm_ffn_g, w_rg, b_rg, w_re, b_re, w_gate, w_up, w_down, norm_final_g):
    t = h.shape[0]
    n_route = N_GROUPS + N_EXPERTS
    w_router = jnp.concatenate([w_rg, w_re, jnp.zeros((D_MODEL, LANES - n_route), F32)], axis=1)
    b_router = jnp.concatenate([b_rg, b_re, jnp.zeros((LANES - n_route,), F32)]).reshape(1, LANES)
    h1, hn_packed, logits_t = _oproj(h, mixed, w_o.astype(BF16), norm_ffn_g.reshape(1, D_MODEL),
                                     w_router, b_router)
    n_assign = t * TOP_K
    n_blocks = -(-(n_assign + N_EXPERTS * (MOE_BLOCK - 1)) // MOE_BLOCK)
    n_blocks_pad = -(-n_blocks // LANES) * LANES
    dest, wts, bexp = _route(logits_t, n_blocks_pad)
    dest_flat = dest.reshape(n_assign)
    row_tok = _invmap(dest_flat, n_blocks * MOE_BLOCK)
    ys_packed = _experts(bexp[0, :n_blocks], row_tok, hn_packed, w_gate, w_up, w_down, n_blocks)
    return _combine(dest_flat, ys_packed, h1, wts.T, norm_final_g.reshape(1, D_MODEL))


def kernel(x, positions, norm_mix_g, w_in, b_in, ssm_a_re, ssm_a_im, ssm_log_dt, ssm_b_re, ssm_b_im, ssm_c_re, ssm_c_im, ssm_d, w_glu, b_glu, w_br_ssm, attn_sinks, w_br_attn, w_o, norm_ffn_g, w_router_group, b_router_group, w_router_expert, b_router_expert, w_exp_gate, w_exp_up, w_exp_down, norm_final_g):
    bsz, seq, d = x.shape
    assert bsz == 1 and d == D_MODEL and norm_mix_g.shape[0] == 1
    h = x.reshape(seq, d)
    mixed = _mixers(h, positions, norm_mix_g[0], w_in[0], b_in[0], ssm_a_re[0], ssm_a_im[0], ssm_log_dt[0],
                    ssm_b_re[0], ssm_b_im[0], ssm_c_re[0], ssm_c_im[0], ssm_d[0], w_glu[0], b_glu[0],
                    w_br_ssm[0], attn_sinks[0], w_br_attn[0])
    out = _moe_tail(h, mixed, w_o[0], norm_ffn_g[0], w_router_group[0], b_router_group[0],
                    w_router_expert[0], b_router_expert[0], w_exp_gate[0], w_exp_up[0], w_exp_down[0],
                    norm_final_g)
    return out.reshape(bsz, seq, d)
```

```python
import functools
import math

import numpy as np
import jax
import jax.numpy as jnp
from jax import lax
from jax.experimental import pallas as pl
from jax.experimental.pallas import tpu as pltpu

F32 = jnp.float32
BF16 = jnp.bfloat16
I32 = jnp.int32

D_MODEL = 2048
SSM_WIDTH = 1024
SSM_GROUP = 16
SSM_GROUPS = 64
SSM_STATE = 64
HEAD_DIM = 64
N_Q_HEADS = 16
N_KV_HEADS = 4
Q_PER_KV = 4
WINDOW = 128
ROPE_DIM = 16
ROPE_THETA = 500000.0
Q_WIDTH = 1024
KV_WIDTH = 256
IN_WIDTH = SSM_WIDTH + Q_WIDTH + 2 * KV_WIDTH + 2 * D_MODEL
N_GROUPS = 8
EXPERTS_PER_GROUP = 8
N_EXPERTS = 64
TOP_K = 2
D_EXPERT = 512
MOE_BLOCK = 128
EPS = 1e-6

CHUNK = 16
CW = CHUNK * SSM_GROUP
GROUP_BLOCK = 8
HALF = D_MODEL // 2
LANES = 128
VMEM_LIMIT = 56 * 1024 * 1024

COL_G0, COL_G1, COL_U, COL_Q, COL_K, COL_V = 0, 2048, 4096, 5120, 6144, 6400

HIGHEST = lax.Precision.HIGHEST


def _dot(a, b, precision=None):
    return jnp.dot(a, b, preferred_element_type=F32, precision=precision)


def _dot_nt(a, b, precision=None):
    return lax.dot_general(a, b, (((1,), (1,)), ((), ())), preferred_element_type=F32,
                           precision=precision)


def _sigmoid(x):
    return 1.0 / (1.0 + jnp.exp(-x))


def _pack_halves(lo, hi):
    return pltpu.pack_elementwise([lo, hi], packed_dtype=BF16)


def _unpack_half(w, index):
    return pltpu.unpack_elementwise(w, index=index, packed_dtype=BF16, unpacked_dtype=F32)


def _params(sem, vmem=VMEM_LIMIT):
    return pltpu.CompilerParams(dimension_semantics=sem, vmem_limit_bytes=vmem)


def _inproj_kernel(x_ref, g_ref, w_ref, b_ref, o_ref, xn_ref):
    @pl.when(pl.program_id(1) == 0)
    def _():
        x = x_ref[...]
        ms = jnp.mean(x * x, axis=-1, keepdims=True)
        xn_ref[...] = (x * lax.rsqrt(ms + EPS) * g_ref[...]).astype(BF16)

    acc = _dot(xn_ref[...], w_ref[...])
    o_ref[...] = (acc + b_ref[...]).astype(o_ref.dtype)


def _inproj(x, g, w_bf16, b, tm=1024, tn=512):
    t, d = x.shape
    n = w_bf16.shape[1]
    return pl.pallas_call(
        _inproj_kernel,
        grid=(t // tm, n // tn),
        in_specs=[
            pl.BlockSpec((tm, d), lambda i, j: (i, 0)),
            pl.BlockSpec((1, d), lambda i, j: (0, 0)),
            pl.BlockSpec((d, tn), lambda i, j: (0, j)),
            pl.BlockSpec((1, tn), lambda i, j: (0, j)),
        ],
        out_specs=pl.BlockSpec((tm, tn), lambda i, j: (i, j)),
        out_shape=jax.ShapeDtypeStruct((t, n), BF16),
        scratch_shapes=[pltpu.VMEM((tm, d), BF16)],
        compiler_params=_params(("arbitrary", "arbitrary")),
        name="inproj",
    )(x, g, w_bf16, b)


def _ssm_prep_kernel(are_ref, aim_ref, ldt_ref, btr_ref, bti_ref, cr_ref, ci_ref,
                     t_ref, wsr_ref, wsi_ref, rxr_ref, rxi_ref, a16r_ref, a16i_ref):
    lam_re = jnp.minimum(are_ref[0], -1e-4)
    lam_im = aim_ref[0]
    dt = jnp.exp(ldt_ref[0])
    lr_dt = lam_re * dt
    th = lam_im * dt
    mag = jnp.exp(lr_dt)
    ab_re = mag * jnp.cos(th)
    ab_im = mag * jnp.sin(th)
    den = lam_re * lam_re + lam_im * lam_im
    nr = ab_re - 1.0
    ni = ab_im
    coef_re = (nr * lam_re + ni * lam_im) / den
    coef_im = (ni * lam_re - nr * lam_im) / den
    btr = btr_ref[0]
    bti = bti_ref[0]
    bb_re = coef_re * btr - coef_im * bti
    bb_im = coef_re * bti + coef_im * btr

    e = lax.broadcasted_iota(I32, (CHUNK, SSM_STATE), 0).astype(F32)
    pmag = jnp.exp(e * lr_dt)
    pos_re = pmag * jnp.cos(e * th)
    pos_im = pmag * jnp.sin(e * th)
    nmag = jnp.exp(-e * lr_dt)
    neg_re = nmag * jnp.cos(e * th)
    neg_im = -nmag * jnp.sin(e * th)

    def rep(tab):
        return jnp.broadcast_to(tab[:, None, :], (CHUNK, SSM_GROUP, SSM_STATE)).reshape(CW, SSM_STATE)

    def tile(mat):
        return jnp.broadcast_to(mat[None, :, :], (CHUNK, SSM_GROUP, SSM_STATE)).reshape(CW, SSM_STATE)

    pr, pi = rep(pos_re), rep(pos_im)
    ctr, cti = tile(cr_ref[0]), tile(ci_ref[0])
    r_re = ctr * pr - cti * pi
    r_im = ctr * pi + cti * pr
    qr, qi = rep(neg_re), rep(neg_im)
    btr_t, bti_t = tile(bb_re), tile(bb_im)
    l_re = btr_t * qr - bti_t * qi
    l_im = btr_t * qi + bti_t * qr

    tm = _dot_nt(l_re, r_re, precision=HIGHEST) - _dot_nt(l_im, r_im, precision=HIGHEST)
    srow = lax.broadcasted_iota(I32, (CW, CW), 0) // SSM_GROUP
    tcol = lax.broadcasted_iota(I32, (CW, CW), 1) // SSM_GROUP
    t_ref[0] = jnp.where(tcol >= srow, tm, 0.0).astype(BF16)

    a15r = pos_re[CHUNK - 1:CHUNK, :]
    a15i = pos_im[CHUNK - 1:CHUNK, :]
    wsr_ref[0] = (l_re * a15r - l_im * a15i).astype(BF16)
    wsi_ref[0] = (l_re * a15i + l_im * a15r).astype(BF16)
    rxr_ref[0] = (r_re * ab_re - r_im * ab_im).astype(BF16)
    rxi_ref[0] = (-(r_re * ab_im + r_im * ab_re)).astype(BF16)
    m16 = jnp.exp(float(CHUNK) * lr_dt)
    a16r_ref[0] = m16 * jnp.cos(float(CHUNK) * th)
    a16i_ref[0] = m16 * jnp.sin(float(CHUNK) * th)


def _ssm_prep(a_re, a_im, log_dt, bt_re, bt_im, c_re, c_im):
    g = a_re.shape[0]
    vec = pl.BlockSpec((1, 1, SSM_STATE), lambda i: (i, 0, 0))
    mat = pl.BlockSpec((1, SSM_GROUP, SSM_STATE), lambda i: (i, 0, 0))
    wide = pl.BlockSpec((1, CW, SSM_STATE), lambda i: (i, 0, 0))
    return pl.pallas_call(
        _ssm_prep_kernel,
        grid=(g,),
        in_specs=[vec, vec, pl.BlockSpec((1, 1, 1), lambda i: (i, 0, 0)), mat, mat, mat, mat],
        out_specs=[pl.BlockSpec((1, CW, CW), lambda i: (i, 0, 0)), wide, wide, wide, wide, vec, vec],
        out_shape=[
            jax.ShapeDtypeStruct((g, CW, CW), BF16),
            jax.ShapeDtypeStruct((g, CW, SSM_STATE), BF16),
            jax.ShapeDtypeStruct((g, CW, SSM_STATE), BF16),
            jax.ShapeDtypeStruct((g, CW, SSM_STATE), BF16),
            jax.ShapeDtypeStruct((g, CW, SSM_STATE), BF16),
            jax.ShapeDtypeStruct((g, 1, SSM_STATE), F32),
            jax.ShapeDtypeStruct((g, 1, SSM_STATE), F32),
        ],
        compiler_params=_params(("arbitrary",)),
        name="ssm_prep",
    )(a_re.reshape(g, 1, SSM_STATE), a_im.reshape(g, 1, SSM_STATE), log_dt.reshape(g, 1, 1),
      bt_re, bt_im, c_re, c_im)


def _gelu_tanh(x):
    c = math.sqrt(2.0 / math.pi)
    return x * (0.5 * (1.0 + jnp.tanh(c * (x + 0.044715 * (x * x * x)))))


def _ssm_kernel(u_ref, t_ref, wsr_ref, wsi_ref, rxr_ref, rxi_ref, a16r_ref, a16i_ref, d_ref,
                z_ref, sr_ref, si_ref):
    nk = u_ref.shape[1]
    for j in range(GROUP_BLOCK):
        u = u_ref[j]
        sr_ref[j * nk:(j + 1) * nk, :] = _dot(u, wsr_ref[j])
        si_ref[j * nk:(j + 1) * nk, :] = _dot(u, wsi_ref[j])

    ar = a16r_ref[...]
    ai = a16i_ref[...]

    def step(k, carry):
        xr, xi = carry
        rows = pl.ds(k, GROUP_BLOCK, stride=nk)
        sr = sr_ref[rows, :]
        si = si_ref[rows, :]
        sr_ref[rows, :] = xr
        si_ref[rows, :] = xi
        return (ar * xr - ai * xi + sr, ar * xi + ai * xr + si)

    zero = jnp.zeros((GROUP_BLOCK, SSM_STATE), F32)
    lax.fori_loop(0, nk, step, (zero, zero))

    for j in range(GROUP_BLOCK):
        u = u_ref[j]
        xr = sr_ref[j * nk:(j + 1) * nk, :].astype(BF16)
        xi = si_ref[j * nk:(j + 1) * nk, :].astype(BF16)
        y = (_dot(u, t_ref[j]) + _dot_nt(xr, rxr_ref[j]) + _dot_nt(xi, rxi_ref[j])
             + d_ref[j] * u.astype(F32))
        z_ref[j] = _gelu_tanh(y).astype(z_ref.dtype)


def _ssm(u_g, tmat, wsr, wsi, rxr, rxi, a16r, a16i, d_tiled):
    g, nk, _ = u_g.shape
    gb = GROUP_BLOCK
    blk3 = lambda a, b: pl.BlockSpec((gb, a, b), lambda i: (i, 0, 0))
    return pl.pallas_call(
        _ssm_kernel,
        grid=(g // gb,),
        in_specs=[blk3(nk, CW), blk3(CW, CW), blk3(CW, SSM_STATE), blk3(CW, SSM_STATE),
                  blk3(CW, SSM_STATE), blk3(CW, SSM_STATE),
                  pl.BlockSpec((gb, SSM_STATE), lambda i: (i, 0)),
                  pl.BlockSpec((gb, SSM_STATE), lambda i: (i, 0)),
                  blk3(1, CW)],
        out_specs=blk3(nk, CW),
        out_shape=jax.ShapeDtypeStruct((g, nk, CW), BF16),
        scratch_shapes=[pltpu.VMEM((gb * nk, SSM_STATE), F32), pltpu.VMEM((gb * nk, SSM_STATE), F32)],
        compiler_params=_params(("arbitrary",)),
        name="ssm",
    )(u_g, tmat, wsr, wsi, rxr, rxi, a16r, a16i, d_tiled)


def _glu_kernel(z_ref, wg_ref, bg_ref, wb_ref, g0_ref, o_ref):
    h = _dot(z_ref[...], wg_ref[...]) + bg_ref[...]
    ga = h[:, :SSM_WIDTH]
    gb = h[:, SSM_WIDTH:]
    a = (ga * _sigmoid(gb)).astype(BF16)
    y = _dot(a, wb_ref[...])
    o_ref[...] = (_sigmoid(g0_ref[...].astype(F32)) * y).astype(o_ref.dtype)


def _glu(z, w_glu, b_glu, w_br, proj, tm=512):
    t = z.shape[0]
    return pl.pallas_call(
        _glu_kernel,
        grid=(t // tm,),
        in_specs=[
            pl.BlockSpec((tm, SSM_WIDTH), lambda i: (i, 0)),
            pl.BlockSpec((SSM_WIDTH, 2 * SSM_WIDTH), lambda i: (0, 0)),
            pl.BlockSpec((1, 2 * SSM_WIDTH), lambda i: (0, 0)),
            pl.BlockSpec((SSM_WIDTH, D_MODEL), lambda i: (0, 0)),
            pl.BlockSpec((tm, D_MODEL), lambda i: (i, COL_G0 // D_MODEL)),
        ],
        out_specs=pl.BlockSpec((tm, D_MODEL), lambda i: (i, 0)),
        out_shape=jax.ShapeDtypeStruct((t, D_MODEL), BF16),
        compiler_params=_params(("arbitrary",)),
        name="glu",
    )(z, w_glu, b_glu, w_br, proj)


def _rope_pattern():
    half = ROPE_DIM // 2
    inv_freq = (np.float32(ROPE_THETA) ** (-np.arange(half, dtype=np.float32) / np.float32(half))).astype(np.float32)
    d = np.arange(LANES) % HEAD_DIM
    freq = np.where(d < ROPE_DIM, inv_freq[d % half], 0.0).astype(np.float32)
    m_up = np.where(d < half, -1.0, 0.0).astype(np.float32)
    m_dn = np.where((d >= half) & (d < ROPE_DIM), 1.0, 0.0).astype(np.float32)
    return np.stack([freq, m_up, m_dn] + [np.zeros(LANES, np.float32)] * 5)


def _attn_kernel(q_ref, k_ref, v_ref, pos_ref, pat_ref, sink_ref, wbr_ref, ysg_ref, g1_ref,
                 o_ref, qbuf, kbuf, vbuf, obuf):
    i = pl.program_id(0)
    tq = q_ref.shape[0]
    nw = tq // WINDOW
    half = ROPE_DIM // 2

    @pl.when(i == 0)
    def _():
        kbuf[0:WINDOW, :] = jnp.zeros((WINDOW, KV_WIDTH), BF16)
        vbuf[0:WINDOW, :] = jnp.zeros((WINDOW, KV_WIDTH), BF16)

    pos = pos_ref[...].astype(F32)
    ang = pos * pat_ref[0:1, :]
    cs = jnp.cos(ang)
    sn = jnp.sin(ang)
    c_up = sn * pat_ref[1:2, :]
    c_dn = sn * pat_ref[2:3, :]

    def rope(x):
        return (x * cs + pltpu.roll(x, LANES - half, 1) * c_up + pltpu.roll(x, half, 1) * c_dn)

    for cb in range(Q_WIDTH // LANES):
        sl = slice(cb * LANES, (cb + 1) * LANES)
        qbuf[:, sl] = rope(q_ref[:, sl].astype(F32)).astype(BF16)
    for cb in range(KV_WIDTH // LANES):
        sl = slice(cb * LANES, (cb + 1) * LANES)
        kbuf[WINDOW:, sl] = rope(k_ref[:, sl].astype(F32)).astype(BF16)
    vbuf[WINDOW:, :] = v_ref[...]

    qi = lax.broadcasted_iota(I32, (WINDOW, 2 * WINDOW), 0)
    kj = lax.broadcasted_iota(I32, (WINDOW, 2 * WINDOW), 1)
    dist = qi + WINDOW - kj
    in_band = (dist >= 0) & (dist < WINDOW)
    cur_only = kj >= WINDOW
    sinks = sink_ref[...]
    scale = HEAD_DIM ** -0.5

    def window(w, carry):
        r0 = pl.multiple_of(w * WINDOW, WINDOW)
        not_first = (i * nw + w) > 0
        mask = in_band & (cur_only | not_first)
        for h in range(N_Q_HEADS):
            kv = h // Q_PER_KV
            qh = qbuf[pl.ds(r0, WINDOW), h * HEAD_DIM:(h + 1) * HEAD_DIM]
            kb = kbuf[pl.ds(r0, 2 * WINDOW), kv * HEAD_DIM:(kv + 1) * HEAD_DIM]
            vb = vbuf[pl.ds(r0, 2 * WINDOW), kv * HEAD_DIM:(kv + 1) * HEAD_DIM]
            s = _dot_nt(qh, kb) * scale
            s = jnp.where(mask, s, -jnp.inf)
            sink = sinks[:, h:h + 1]
            m = jnp.maximum(jnp.max(s, axis=-1, keepdims=True), sink)
            p = jnp.exp(s - m)
            denom = jnp.sum(p, axis=-1, keepdims=True) + jnp.exp(sink - m)
            o = _dot(p.astype(BF16), vb) / denom
            obuf[pl.ds(r0, WINDOW), h * HEAD_DIM:(h + 1) * HEAD_DIM] = o.astype(BF16)
        return carry

    lax.fori_loop(0, nw, window, 0)

    kbuf[0:WINDOW, :] = kbuf[tq:tq + WINDOW, :]
    vbuf[0:WINDOW, :] = vbuf[tq:tq + WINDOW, :]

    y = _dot(obuf[...], wbr_ref[...])
    o_ref[...] = (ysg_ref[...].astype(F32) + _sigmoid(g1_ref[...].astype(F32)) * y).astype(o_ref.dtype)


def _attn(proj, pos_col, sinks, w_br, ysg, tq=512):
    t = proj.shape[0]
    pat = jnp.asarray(_rope_pattern())
    return pl.pallas_call(
        _attn_kernel,
        grid=(t // tq,),
        in_specs=[
            pl.BlockSpec((tq, Q_WIDTH), lambda i: (i, COL_Q // Q_WIDTH)),
            pl.BlockSpec((tq, KV_WIDTH), lambda i: (i, COL_K // KV_WIDTH)),
            pl.BlockSpec((tq, KV_WIDTH), lambda i: (i, COL_V // KV_WIDTH)),
            pl.BlockSpec((tq, 1), lambda i: (i, 0)),
            pl.BlockSpec((8, LANES), lambda i: (0, 0)),
            pl.BlockSpec((1, N_Q_HEADS), lambda i: (0, 0)),
            pl.BlockSpec((Q_WIDTH, D_MODEL), lambda i: (0, 0)),
            pl.BlockSpec((tq, D_MODEL), lambda i: (i, 0)),
            pl.BlockSpec((tq, D_MODEL), lambda i: (i, COL_G1 // D_MODEL)),
        ],
        out_specs=pl.BlockSpec((tq, D_MODEL), lambda i: (i, 0)),
        out_shape=jax.ShapeDtypeStruct((t, D_MODEL), BF16),
        scratch_shapes=[
            pltpu.VMEM((tq, Q_WIDTH), BF16),
            pltpu.VMEM((tq + WINDOW, KV_WIDTH), BF16),
            pltpu.VMEM((tq + WINDOW, KV_WIDTH), BF16),
            pltpu.VMEM((tq, Q_WIDTH), BF16),
        ],
        compiler_params=_params(("arbitrary",)),
        name="attn",
    )(proj, proj, proj, pos_col, pat, sinks, w_br, ysg, proj)


def _oproj_kernel(x_ref, mix_ref, wo_ref, g_ref, wr_ref, br_ref, h_ref, hp_ref, lt_ref):
    h = x_ref[...] + _dot(mix_ref[...], wo_ref[...])
    h_ref[...] = h
    ms = jnp.mean(h * h, axis=-1, keepdims=True)
    hn = h * lax.rsqrt(ms + EPS) * g_ref[...]
    hp_ref[...] = _pack_halves(hn[:, :HALF], hn[:, HALF:])
    logits = _dot(hn, wr_ref[...], precision=HIGHEST) + br_ref[...]
    lt_ref[...] = logits.T


def _oproj(x, mixed, w_o, g, w_router, b_router, tm=512):
    t = x.shape[0]
    return pl.pallas_call(
        _oproj_kernel,
        grid=(t // tm,),
        in_specs=[
            pl.BlockSpec((tm, D_MODEL), lambda i: (i, 0)),
            pl.BlockSpec((tm, D_MODEL), lambda i: (i, 0)),
            pl.BlockSpec((D_MODEL, D_MODEL), lambda i: (0, 0)),
            pl.BlockSpec((1, D_MODEL), lambda i: (0, 0)),
            pl.BlockSpec((D_MODEL, LANES), lambda i: (0, 0)),
            pl.BlockSpec((1, LANES), lambda i: (0, 0)),
        ],
        out_specs=[
            pl.BlockSpec((tm, D_MODEL), lambda i: (i, 0)),
            pl.BlockSpec((tm, HALF), lambda i: (i, 0)),
            pl.BlockSpec((LANES, tm), lambda i: (0, i)),
        ],
        out_shape=[
            jax.ShapeDtypeStruct((t, D_MODEL), F32),
            jax.ShapeDtypeStruct((t, HALF), jnp.uint32),
            jax.ShapeDtypeStruct((LANES, t), F32),
        ],
        compiler_params=_params(("arbitrary",)),
        name="oproj",
    )(x, mixed, w_o, g, w_router, b_router)


ROUTE_CHUNK = 256


def _route_kernel(lt_ref, dest_ref, wts_ref, bexp_ref, eid_ref, rank_ref):
    t = lt_ref.shape[1]
    nc = t // ROUTE_CHUNK
    r8 = lax.broadcasted_iota(I32, (N_GROUPS, ROUTE_CHUNK), 0)
    r64 = lax.broadcasted_iota(I32, (N_EXPERTS, ROUTE_CHUNK), 0)

    def pick(c, carry):
        cols = pl.ds(pl.multiple_of(c * ROUTE_CHUNK, ROUTE_CHUNK), ROUTE_CHUNK)
        lg = lt_ref[0:N_GROUPS, cols]
        m = jnp.max(lg, axis=0, keepdims=True)
        ssum = jnp.sum(jnp.exp(lg - m), axis=0, keepdims=True)
        p_grp = 1.0 / ssum
        grp = jnp.min(jnp.where(lg == m, r8, N_GROUPS), axis=0, keepdims=True)
        le = lt_ref[N_GROUPS:N_GROUPS + N_EXPERTS, cols]
        leg = jnp.where((r64 // EXPERTS_PER_GROUP) == grp, le, -jnp.inf)
        m1 = jnp.max(leg, axis=0, keepdims=True)
        i1 = jnp.min(jnp.where(leg == m1, r64, N_EXPERTS), axis=0, keepdims=True)
        leg2 = jnp.where(r64 == i1, -jnp.inf, leg)
        m2 = jnp.max(leg2, axis=0, keepdims=True)
        i2 = jnp.min(jnp.where(leg2 == m2, r64, N_EXPERTS), axis=0, keepdims=True)
        ex = jnp.exp(m2 - m1)
        eid_ref[0:1, cols] = i1
        eid_ref[1:2, cols] = i2
        wts_ref[0:1, cols] = p_grp / (1.0 + ex)
        wts_ref[1:2, cols] = p_grp * ex / (1.0 + ex)
        return carry

    lax.fori_loop(0, nc, pick, 0)

    a_row = lax.broadcasted_iota(I32, (ROUTE_CHUNK, ROUTE_CHUNK), 0)
    a_col = lax.broadcasted_iota(I32, (ROUTE_CHUNK, ROUTE_CHUNK), 1)
    before = (a_row < a_col).astype(BF16)

    def count(n, carry):
        j = n // nc
        c = n - j * nc
        cols = pl.ds(pl.multiple_of(c * ROUTE_CHUNK, ROUTE_CHUNK), ROUTE_CHUNK)
        oh = r64 == eid_ref[pl.ds(j, 1), cols]
        ohf = oh.astype(F32)
        pref = _dot(ohf.astype(BF16), before) + carry
        rank_ref[pl.ds(j, 1), cols] = jnp.sum(jnp.where(oh, pref, 0.0), axis=0, keepdims=True)
        return carry + jnp.sum(ohf, axis=1, keepdims=True)

    counts = lax.fori_loop(0, TOP_K * nc, count, jnp.zeros((N_EXPERTS, 1), F32))

    padded = jnp.floor((counts + (MOE_BLOCK - 1)) * (1.0 / MOE_BLOCK)) * MOE_BLOCK
    e_row = lax.broadcasted_iota(I32, (N_EXPERTS, N_EXPERTS), 0)
    e_col = lax.broadcasted_iota(I32, (N_EXPERTS, N_EXPERTS), 1)
    incl = (e_col <= e_row).astype(F32)
    pad_end = _dot(incl, jnp.broadcast_to(padded, (N_EXPERTS, LANES)), precision=HIGHEST)[:, 0:1]
    pad_start = pad_end - padded

    def place(n, carry):
        j = n // nc
        c = n - j * nc
        cols = pl.ds(pl.multiple_of(c * ROUTE_CHUNK, ROUTE_CHUNK), ROUTE_CHUNK)
        oh = r64 == eid_ref[pl.ds(j, 1), cols]
        start = jnp.sum(jnp.where(oh, pad_start, 0.0), axis=0, keepdims=True)
        dest_ref[pl.ds(j, 1), cols] = (start + rank_ref[pl.ds(j, 1), cols]).astype(I32)
        return carry

    lax.fori_loop(0, TOP_K * nc, place, 0)

    b0 = (lax.broadcasted_iota(I32, (N_EXPERTS, bexp_ref.shape[1]), 1) * MOE_BLOCK).astype(F32)
    n_done = jnp.sum((pad_end <= b0).astype(F32), axis=0, keepdims=True)
    bexp_ref[...] = jnp.minimum(n_done, float(N_EXPERTS - 1)).astype(I32)


def _route(logits_t, n_blocks_pad):
    t = logits_t.shape[1]
    return pl.pallas_call(
        _route_kernel,
        out_shape=[
            jax.ShapeDtypeStruct((TOP_K, t), I32),
            jax.ShapeDtypeStruct((TOP_K, t), F32),
            jax.ShapeDtypeStruct((1, n_blocks_pad), I32),
        ],
        scratch_shapes=[pltpu.VMEM((TOP_K, t), I32), pltpu.VMEM((TOP_K, t), F32)],
        compiler_params=pltpu.CompilerParams(vmem_limit_bytes=VMEM_LIMIT),
        name="route",
    )(logits_t)


def _invmap_kernel(dest_ref, rt_ref):
    n_rows = rt_ref.shape[0]
    t = dest_ref.shape[0] // TOP_K

    def clear(r, c):
        rt_ref[r] = 0
        return c

    lax.fori_loop(0, n_rows, clear, 0, unroll=16)

    def put(tok, c):
        rt_ref[dest_ref[tok]] = tok
        rt_ref[dest_ref[t + tok]] = tok
        return c

    lax.fori_loop(0, t, put, 0, unroll=8)


def _invmap(dest_flat, n_rows):
    return pl.pallas_call(
        _invmap_kernel,
        in_specs=[pl.BlockSpec(memory_space=pltpu.SMEM)],
        out_specs=pl.BlockSpec(memory_space=pltpu.SMEM),
        out_shape=jax.ShapeDtypeStruct((n_rows,), I32),
        name="invmap",
    )(dest_flat)


SUBLANES = 8


def _row_copy(src_hbm, src_row, dst_buf, slot, tile, sub, sem):
    return pltpu.make_async_copy(src_hbm.at[src_row >> 3, pl.ds(src_row & (SUBLANES - 1), 1), :],
                                 dst_buf.at[slot, tile, pl.ds(sub, 1), :], sem.at[slot])


def _slot_copy(src_hbm, dst_buf, slot, sem):
    return pltpu.make_async_copy(src_hbm.at[pl.ds(0, dst_buf.shape[1])], dst_buf.at[slot], sem.at[slot])


def _expert_kernel(bexp_ref, rt_ref, hp_hbm, wg_hbm, wu_hbm, wd_hbm, ys_ref,
                   xbuf, wf_g, wf_u, wf_d, wb_g, wb_u, wb_d, xsem, wsem):
    b = pl.program_id(0)
    nb = pl.num_programs(0) - 1

    def weight_copies(e):
        return (pltpu.make_async_copy(wg_hbm.at[e], wf_g, wsem.at[0]),
                pltpu.make_async_copy(wu_hbm.at[e], wf_u, wsem.at[1]),
                pltpu.make_async_copy(wd_hbm.at[e], wf_d, wsem.at[2]))

    @pl.when(b == 0)
    def _():
        for c in weight_copies(bexp_ref[0]):
            c.start()

    @pl.when(b < nb)
    def _():
        base = b * MOE_BLOCK
        slot = b % 2

        def body(i, c):
            for s in range(SUBLANES):
                _row_copy(hp_hbm, rt_ref[base + i * SUBLANES + s], xbuf, slot, i, s, xsem).start()
            return c

        lax.fori_loop(0, MOE_BLOCK // SUBLANES, body, 0)

    @pl.when(b > 0)
    def _():
        blk = b - 1
        e = bexp_ref[blk]
        first = (blk == 0) | (e != bexp_ref[jnp.maximum(blk - 1, 0)])

        @pl.when(first)
        def _():
            for c in weight_copies(e):
                c.wait()
            wb_g[...] = wf_g[...].astype(BF16)
            wb_u[...] = wf_u[...].astype(BF16)
            wb_d[...] = wf_d[...].astype(BF16)
            nxt = lax.while_loop(lambda j: (j < nb) & (bexp_ref[jnp.minimum(j, nb - 1)] == e),
                                 lambda j: j + 1, blk + 1)

            @pl.when(nxt < nb)
            def _():
                for c in weight_copies(bexp_ref[jnp.minimum(nxt, nb - 1)]):
                    c.start()

        slot = blk % 2
        _slot_copy(hp_hbm, xbuf, slot, xsem).wait()
        xw = xbuf[slot].reshape(MOE_BLOCK, HALF)
        lo = _unpack_half(xw, 0).astype(BF16)
        hi = _unpack_half(xw, 1).astype(BF16)
        g = _dot(lo, wb_g[:HALF, :]) + _dot(hi, wb_g[HALF:, :])
        u = _dot(lo, wb_u[:HALF, :]) + _dot(hi, wb_u[HALF:, :])
        h = (g * _sigmoid(g) * u).astype(BF16)
        y = _dot(h, wb_d[...])
        ys_ref[...] = _pack_halves(y[:, :HALF], y[:, HALF:]).reshape(ys_ref.shape)


def _experts(bexp, row_tok, hn_packed, w_gate, w_up, w_down, n_blocks):
    grid_spec = pltpu.PrefetchScalarGridSpec(
        num_scalar_prefetch=2,
        grid=(n_blocks + 1,),
        in_specs=[pl.BlockSpec(memory_space=pl.ANY)] * 4,
        out_specs=pl.BlockSpec((MOE_BLOCK // SUBLANES, SUBLANES, HALF),
                               lambda b, be, rt: (jnp.maximum(b - 1, 0), 0, 0)),
        scratch_shapes=[
            pltpu.VMEM((2, MOE_BLOCK // SUBLANES, SUBLANES, HALF), jnp.uint32),
            pltpu.VMEM((D_MODEL, D_EXPERT), F32), pltpu.VMEM((D_MODEL, D_EXPERT), F32),
            pltpu.VMEM((D_EXPERT, D_MODEL), F32),
            pltpu.VMEM((D_MODEL, D_EXPERT), BF16), pltpu.VMEM((D_MODEL, D_EXPERT), BF16),
            pltpu.VMEM((D_EXPERT, D_MODEL), BF16),
            pltpu.SemaphoreType.DMA((2,)), pltpu.SemaphoreType.DMA((3,)),
        ],
    )
    return pl.pallas_call(
        _expert_kernel,
        grid_spec=grid_spec,
        out_shape=jax.ShapeDtypeStruct((n_blocks * MOE_BLOCK // SUBLANES, SUBLANES, HALF), jnp.uint32),
        compiler_params=_params(("arbitrary",)),
        name="experts",
    )(bexp, row_tok, hn_packed.reshape(-1, SUBLANES, HALF), w_gate, w_up, w_down)


def _combine_kernel(dest_ref, ys_hbm, h_ref, w_ref, g_ref, o_ref, ybuf, sem):
    i = pl.program_id(0)
    n = pl.num_programs(0)
    tq = h_ref.shape[0]
    t = n * tq

    tiles = tq // SUBLANES

    def start_gather(blk, slot):
        base = blk * tq

        def body(j, c):
            for s in range(SUBLANES):
                r = base + j * SUBLANES + s
                _row_copy(ys_hbm, dest_ref[r], ybuf, slot, j, s, sem).start()
                _row_copy(ys_hbm, dest_ref[t + r], ybuf, slot, tiles + j, s, sem).start()
            return c

        lax.fori_loop(0, tiles, body, 0)

    @pl.when(i == 0)
    def _():
        start_gather(0, 0)

    @pl.when(i + 1 < n)
    def _():
        start_gather(i + 1, (i + 1) % 2)

    slot = i % 2
    _slot_copy(ys_hbm, ybuf, slot, sem).wait()
    y0 = ybuf[slot, 0:tiles].reshape(tq, HALF)
    y1 = ybuf[slot, tiles:2 * tiles].reshape(tq, HALF)
    w = w_ref[...]
    w0 = w[:, 0:1]
    w1 = w[:, 1:2]
    h = h_ref[...]
    lo = h[:, :HALF] + w0 * _unpack_half(y0, 0) + w1 * _unpack_half(y1, 0)
    hi = h[:, HALF:] + w0 * _unpack_half(y0, 1) + w1 * _unpack_half(y1, 1)
    ms = (jnp.sum(lo * lo, axis=-1, keepdims=True) + jnp.sum(hi * hi, axis=-1, keepdims=True)) * (1.0 / D_MODEL)
    inv = lax.rsqrt(ms + EPS)
    g = g_ref[...]
    o_ref[:, :HALF] = lo * inv * g[:, :HALF]
    o_ref[:, HALF:] = hi * inv * g[:, HALF:]


def _combine(dest_flat, ys_packed, h1, wts_tok, g_final, tq=256):
    t = h1.shape[0]
    grid_spec = pltpu.PrefetchScalarGridSpec(
        num_scalar_prefetch=1,
        grid=(t // tq,),
        in_specs=[
            pl.BlockSpec(memory_space=pl.ANY),
            pl.BlockSpec((tq, D_MODEL), lambda i, d: (i, 0)),
            pl.BlockSpec((tq, TOP_K), lambda i, d: (i, 0)),
            pl.BlockSpec((1, D_MODEL), lambda i, d: (0, 0)),
        ],
        out_specs=pl.BlockSpec((tq, D_MODEL), lambda i, d: (i, 0)),
        scratch_shapes=[pltpu.VMEM((2, TOP_K * tq // SUBLANES, SUBLANES, HALF), jnp.uint32),
                        pltpu.SemaphoreType.DMA((2,))],
    )
    return pl.pallas_call(
        _combine_kernel,
        grid_spec=grid_spec,
        out_shape=jax.ShapeDtypeStruct((t, D_MODEL), F32),
        compiler_params=_params(("arbitrary",)),
        name="combine",
    )(dest_flat, ys_packed, h1, wts_tok, g_final)


def _mixers(h, positions, norm_g, w_in, b_in, a_re, a_im, log_dt, b_re, b_im, c_re, c_im, d_skip,
            w_glu, b_glu, w_br_ssm, sinks, w_br_attn):
    t = h.shape[0]
    ngate = 2 * D_MODEL
    w_perm = jnp.concatenate([w_in[:, IN_WIDTH - ngate:], w_in[:, :IN_WIDTH - ngate]], axis=1).astype(BF16)
    b_perm = jnp.concatenate([b_in[IN_WIDTH - ngate:], b_in[:IN_WIDTH - ngate]]).reshape(1, IN_WIDTH)
    proj = _inproj(h, norm_g.reshape(1, D_MODEL), w_perm, b_perm)

    nk = t // CHUNK
    u_g = (proj[:, COL_U:COL_U + SSM_WIDTH].reshape(nk, CHUNK, SSM_GROUPS, SSM_GROUP)
           .transpose(2, 0, 1, 3).reshape(SSM_GROUPS, nk, CW))
    tmat, wsr, wsi, rxr, rxi, a16r, a16i = _ssm_prep(
        a_re, a_im, log_dt, b_re.transpose(0, 2, 1), b_im.transpose(0, 2, 1), c_re, c_im)
    d_tiled = jnp.tile(d_skip, (1, CHUNK)).reshape(SSM_GROUPS, 1, CW)
    z_g = _ssm(u_g, tmat, wsr, wsi, rxr, rxi, a16r.reshape(SSM_GROUPS, SSM_STATE),
               a16i.reshape(SSM_GROUPS, SSM_STATE), d_tiled)
    z = (z_g.reshape(SSM_GROUPS, nk, CHUNK, SSM_GROUP).transpose(1, 2, 0, 3).reshape(t, SSM_WIDTH))

    ysg = _glu(z, w_glu.astype(BF16), b_glu.reshape(1, -1), w_br_ssm.astype(BF16), proj)
    return _attn(proj, positions.reshape(t, 1), sinks.reshape(1, N_Q_HEADS), w_br_attn.astype(BF16), ysg)


def _moe_tail(h, mixed, w_o, norm_ffn_g, w_rg, b_rg, w_re, b_re, w_gate, w_up, w_down, norm_final_g):
    t = h.shape[0]
    n_route = N_GROUPS + N_EXPERTS
    w_router = jnp.concatenate([w_rg, w_re, jnp.zeros((D_MODEL, LANES - n_route), F32)], axis=1)
    b_router = jnp.concatenate([b_rg, b_re, jnp.zeros((LANES - n_route,), F32)]).reshape(1, LANES)
    h1, hn_packed, logits_t = _oproj(h, mixed, w_o.astype(BF16), norm_ffn_g.reshape(1, D_MODEL),
                                     w_router, b_router)
    n_assign = t * TOP_K
    n_blocks = -(-(n_assign + N_EXPERTS * (MOE_BLOCK - 1)) // MOE_BLOCK)
    n_blocks_pad = -(-n_blocks // LANES) * LANES
    dest, wts, bexp = _route(logits_t, n_blocks_pad)
    dest_flat = dest.reshape(n_assign)
    row_tok = _invmap(dest_flat, n_blocks * MOE_BLOCK)
    ys_packed = _experts(bexp[0, :n_blocks], row_tok, hn_packed, w_gate, w_up, w_down, n_blocks)
    return _combine(dest_flat, ys_packed, h1, wts.T, norm_final_g.reshape(1, D_MODEL))


def kernel(x, positions, norm_mix_g, w_in, b_in, ssm_a_re, ssm_a_im, ssm_log_dt, ssm_b_re, ssm_b_im, ssm_c_re, ssm_c_im, ssm_d, w_glu, b_glu, w_br_ssm, attn_sinks, w_br_attn, w_o, norm_ffn_g, w_router_group, b_router_group, w_router_expert, b_router_expert, w_exp_gate, w_exp_up, w_exp_down, norm_final_g):
    bsz, seq, d = x.shape
    assert bsz == 1 and d == D_MODEL and norm_mix_g.shape[0] == 1
    h = x.reshape(seq, d)
    mixed = _mixers(h, positions, norm_mix_g[0], w_in[0], b_in[0], ssm_a_re[0], ssm_a_im[0], ssm_log_dt[0],
                    ssm_b_re[0], ssm_b_im[0], ssm_c_re[0], ssm_c_im[0], ssm_d[0], w_glu[0], b_glu[0],
                    w_br_ssm[0], attn_sinks[0], w_br_attn[0])
    out = _moe_tail(h, mixed, w_o[0], norm_ffn_g[0], w_router_group[0], b_router_group[0],
                    w_router_expert[0], b_router_expert[0], w_exp_gate[0], w_exp_up[0], w_exp_down[0],
                    norm_final_g)
    return out.reshape(bsz, seq, d)
```

```python
import functools
import math

import numpy as np
import jax
import jax.numpy as jnp
from jax import lax
from jax.experimental import pallas as pl
from jax.experimental.pallas import tpu as pltpu

F32 = jnp.float32
BF16 = jnp.bfloat16
I32 = jnp.int32

D_MODEL = 2048
SSM_WIDTH = 1024
SSM_GROUP = 16
SSM_GROUPS = 64
SSM_STATE = 64
HEAD_DIM = 64
N_Q_HEADS = 16
N_KV_HEADS = 4
Q_PER_KV = 4
WINDOW = 128
ROPE_DIM = 16
ROPE_THETA = 500000.0
Q_WIDTH = 1024
KV_WIDTH = 256
IN_WIDTH = SSM_WIDTH + Q_WIDTH + 2 * KV_WIDTH + 2 * D_MODEL
N_GROUPS = 8
EXPERTS_PER_GROUP = 8
N_EXPERTS = 64
TOP_K = 2
D_EXPERT = 512
MOE_BLOCK = 128
EPS = 1e-6

CHUNK = 16
CW = CHUNK * SSM_GROUP
GROUP_BLOCK = 8
HALF = D_MODEL // 2
LANES = 128
VMEM_LIMIT = 56 * 1024 * 1024

COL_G0, COL_G1, COL_Q, COL_K, COL_V = 0, 2048, 4096, 5120, 5376

HIGHEST = lax.Precision.HIGHEST


def _dot(a, b, precision=None):
    return jnp.dot(a, b, preferred_element_type=F32, precision=precision)


def _dot_nt(a, b, precision=None):
    return lax.dot_general(a, b, (((1,), (1,)), ((), ())), preferred_element_type=F32,
                           precision=precision)


def _sigmoid(x):
    return 1.0 / (1.0 + jnp.exp(-x))


def _pack_halves(lo, hi):
    return pltpu.pack_elementwise([lo, hi], packed_dtype=BF16)


def _unpack_half(w, index):
    return pltpu.unpack_elementwise(w, index=index, packed_dtype=BF16, unpacked_dtype=F32)


def _params(sem, vmem=VMEM_LIMIT):
    return pltpu.CompilerParams(dimension_semantics=sem, vmem_limit_bytes=vmem)


def _inproj_kernel(x_ref, g_ref, w_ref, b_ref, o_ref, xn_ref, *rest, chunk_major):
    @pl.when(pl.program_id(1) == 0)
    def _():
        x = x_ref[...]
        ms = jnp.mean(x * x, axis=-1, keepdims=True)
        xn_ref[...] = (x * lax.rsqrt(ms + EPS) * g_ref[...]).astype(BF16)

    acc = _dot(xn_ref[...], w_ref[...]) + b_ref[...]
    if not chunk_major:
        o_ref[...] = acc.astype(o_ref.dtype)
        return
    (acc_ref,) = rest
    nk = o_ref.shape[1]
    for c in range(acc_ref.shape[0]):
        acc_ref[c] = acc[:, c * LANES:(c + 1) * LANES]
    for s in range(CHUNK):
        for c in range(acc_ref.shape[0]):
            o_ref[s, :, c * LANES:(c + 1) * LANES] = acc_ref[c, pl.ds(s, nk, stride=CHUNK), :].astype(o_ref.dtype)


def _inproj(x, g, w_bf16, b, *, chunk_major, tm=1024, tn=512):
    t, d = x.shape
    n = w_bf16.shape[1]
    scratch = [pltpu.VMEM((tm, d), BF16)]
    if chunk_major:
        out_spec = pl.BlockSpec((CHUNK, tm // CHUNK, tn), lambda i, j: (0, i, j))
        out_shape = jax.ShapeDtypeStruct((CHUNK, t // CHUNK, n), BF16)
        scratch.append(pltpu.VMEM((tn // LANES, tm, LANES), F32))
    else:
        out_spec = pl.BlockSpec((tm, tn), lambda i, j: (i, j))
        out_shape = jax.ShapeDtypeStruct((t, n), BF16)
    return pl.pallas_call(
        functools.partial(_inproj_kernel, chunk_major=chunk_major),
        grid=(t // tm, n // tn),
        in_specs=[
            pl.BlockSpec((tm, d), lambda i, j: (i, 0)),
            pl.BlockSpec((1, d), lambda i, j: (0, 0)),
            pl.BlockSpec((d, tn), lambda i, j: (0, j)),
            pl.BlockSpec((1, tn), lambda i, j: (0, j)),
        ],
        out_specs=out_spec,
        out_shape=out_shape,
        scratch_shapes=scratch,
        compiler_params=_params(("arbitrary", "arbitrary")),
        name="inproj_u" if chunk_major else "inproj",
    )(x, g, w_bf16, b)


def _ssm_prep_kernel(are_ref, aim_ref, ldt_ref, btr_ref, bti_ref, cr_ref, ci_ref,
                     t_ref, wsr_ref, wsi_ref, rxr_ref, rxi_ref, a16r_ref, a16i_ref):
    lam_re = jnp.minimum(are_ref[0], -1e-4)
    lam_im = aim_ref[0]
    dt = jnp.exp(ldt_ref[0])
    lr_dt = lam_re * dt
    th = lam_im * dt
    mag = jnp.exp(lr_dt)
    ab_re = mag * jnp.cos(th)
    ab_im = mag * jnp.sin(th)
    den = lam_re * lam_re + lam_im * lam_im
    nr = ab_re - 1.0
    ni = ab_im
    coef_re = (nr * lam_re + ni * lam_im) / den
    coef_im = (ni * lam_re - nr * lam_im) / den
    btr = btr_ref[0]
    bti = bti_ref[0]
    bb_re = coef_re * btr - coef_im * bti
    bb_im = coef_re * bti + coef_im * btr

    e = lax.broadcasted_iota(I32, (CHUNK, SSM_STATE), 0).astype(F32)
    pmag = jnp.exp(e * lr_dt)
    pos_re = pmag * jnp.cos(e * th)
    pos_im = pmag * jnp.sin(e * th)
    nmag = jnp.exp(-e * lr_dt)
    neg_re = nmag * jnp.cos(e * th)
    neg_im = -nmag * jnp.sin(e * th)

    def rep(tab):
        return jnp.broadcast_to(tab[:, None, :], (CHUNK, SSM_GROUP, SSM_STATE)).reshape(CW, SSM_STATE)

    def tile(mat):
        return jnp.broadcast_to(mat[None, :, :], (CHUNK, SSM_GROUP, SSM_STATE)).reshape(CW, SSM_STATE)

    pr, pi = rep(pos_re), rep(pos_im)
    ctr, cti = tile(cr_ref[0]), tile(ci_ref[0])
    r_re = ctr * pr - cti * pi
    r_im = ctr * pi + cti * pr
    qr, qi = rep(neg_re), rep(neg_im)
    btr_t, bti_t = tile(bb_re), tile(bb_im)
    l_re = btr_t * qr - bti_t * qi
    l_im = btr_t * qi + bti_t * qr

    tm = _dot_nt(l_re, r_re, precision=HIGHEST) - _dot_nt(l_im, r_im, precision=HIGHEST)
    srow = lax.broadcasted_iota(I32, (CW, CW), 0) // SSM_GROUP
    tcol = lax.broadcasted_iota(I32, (CW, CW), 1) // SSM_GROUP
    t_ref[0] = jnp.where(tcol >= srow, tm, 0.0).astype(BF16)

    a15r = pos_re[CHUNK - 1:CHUNK, :]
    a15i = pos_im[CHUNK - 1:CHUNK, :]
    wsr_ref[0] = (l_re * a15r - l_im * a15i).astype(BF16)
    wsi_ref[0] = (l_re * a15i + l_im * a15r).astype(BF16)
    rxr_ref[0] = (r_re * ab_re - r_im * ab_im).astype(BF16)
    rxi_ref[0] = (-(r_re * ab_im + r_im * ab_re)).astype(BF16)
    m16 = jnp.exp(float(CHUNK) * lr_dt)
    a16r_ref[0] = m16 * jnp.cos(float(CHUNK) * th)
    a16i_ref[0] = m16 * jnp.sin(float(CHUNK) * th)


def _ssm_prep(a_re, a_im, log_dt, bt_re, bt_im, c_re, c_im):
    g = a_re.shape[0]
    vec = pl.BlockSpec((1, 1, SSM_STATE), lambda i: (i, 0, 0))
    mat = pl.BlockSpec((1, SSM_GROUP, SSM_STATE), lambda i: (i, 0, 0))
    wide = pl.BlockSpec((1, CW, SSM_STATE), lambda i: (i, 0, 0))
    return pl.pallas_call(
        _ssm_prep_kernel,
        grid=(g,),
        in_specs=[vec, vec, pl.BlockSpec((1, 1, 1), lambda i: (i, 0, 0)), mat, mat, mat, mat],
        out_specs=[pl.BlockSpec((1, CW, CW), lambda i: (i, 0, 0)), wide, wide, wide, wide, vec, vec],
        out_shape=[
            jax.ShapeDtypeStruct((g, CW, CW), BF16),
            jax.ShapeDtypeStruct((g, CW, SSM_STATE), BF16),
            jax.ShapeDtypeStruct((g, CW, SSM_STATE), BF16),
            jax.ShapeDtypeStruct((g, CW, SSM_STATE), BF16),
            jax.ShapeDtypeStruct((g, CW, SSM_STATE), BF16),
            jax.ShapeDtypeStruct((g, 1, SSM_STATE), F32),
            jax.ShapeDtypeStruct((g, 1, SSM_STATE), F32),
        ],
        compiler_params=_params(("arbitrary",)),
        name="ssm_prep",
    )(a_re.reshape(g, 1, SSM_STATE), a_im.reshape(g, 1, SSM_STATE), log_dt.reshape(g, 1, 1),
      bt_re, bt_im, c_re, c_im)


def _gelu_tanh(x):
    c = math.sqrt(2.0 / math.pi)
    return x * (0.5 * (1.0 + jnp.tanh(c * (x + 0.044715 * (x * x * x)))))


def _ssm_kernel(u_ref, t_ref, wsr_ref, wsi_ref, rxr_ref, rxi_ref, a16r_ref, a16i_ref, d_ref,
                z_ref, sr_ref, si_ref, ug_ref, zg_ref):
    nk = u_ref.shape[1]
    for j in range(GROUP_BLOCK):
        ug_ref[j] = jnp.concatenate(
            [u_ref[s, :, j * SSM_GROUP:(j + 1) * SSM_GROUP] for s in range(CHUNK)], axis=1)
    u_ref = ug_ref
    for j in range(GROUP_BLOCK):
        u = u_ref[j]
        sr_ref[j * nk:(j + 1) * nk, :] = _dot(u, wsr_ref[j])
        si_ref[j * nk:(j + 1) * nk, :] = _dot(u, wsi_ref[j])

    ar = a16r_ref[...]
    ai = a16i_ref[...]

    def step(k, carry):
        xr, xi = carry
        rows = pl.ds(k, GROUP_BLOCK, stride=nk)
        sr = sr_ref[rows, :]
        si = si_ref[rows, :]
        sr_ref[rows, :] = xr
        si_ref[rows, :] = xi
        return (ar * xr - ai * xi + sr, ar * xi + ai * xr + si)

    zero = jnp.zeros((GROUP_BLOCK, SSM_STATE), F32)
    lax.fori_loop(0, nk, step, (zero, zero))

    for j in range(GROUP_BLOCK):
        u = u_ref[j]
        xr = sr_ref[j * nk:(j + 1) * nk, :].astype(BF16)
        xi = si_ref[j * nk:(j + 1) * nk, :].astype(BF16)
        y = (_dot(u, t_ref[j]) + _dot_nt(xr, rxr_ref[j]) + _dot_nt(xi, rxi_ref[j])
             + d_ref[j] * u.astype(F32))
        zg_ref[j] = _gelu_tanh(y).astype(zg_ref.dtype)

    for t in range(CHUNK):
        z_ref[t] = jnp.concatenate(
            [zg_ref[j, :, t * SSM_GROUP:(t + 1) * SSM_GROUP] for j in range(GROUP_BLOCK)], axis=1)


def _ssm(u3, tmat, wsr, wsi, rxr, rxi, a16r, a16i, d_tiled):
    _, nk, width = u3.shape
    gb = GROUP_BLOCK
    gl = gb * SSM_GROUP
    blk3 = lambda a, b: pl.BlockSpec((gb, a, b), lambda i: (i, 0, 0))
    io = pl.BlockSpec((CHUNK, nk, gl), lambda i: (0, 0, i))
    return pl.pallas_call(
        _ssm_kernel,
        grid=(width // gl,),
        in_specs=[io, blk3(CW, CW), blk3(CW, SSM_STATE), blk3(CW, SSM_STATE),
                  blk3(CW, SSM_STATE), blk3(CW, SSM_STATE),
                  pl.BlockSpec((gb, SSM_STATE), lambda i: (i, 0)),
                  pl.BlockSpec((gb, SSM_STATE), lambda i: (i, 0)),
                  blk3(1, CW)],
        out_specs=io,
        out_shape=jax.ShapeDtypeStruct(u3.shape, BF16),
        scratch_shapes=[pltpu.VMEM((gb * nk, SSM_STATE), F32), pltpu.VMEM((gb * nk, SSM_STATE), F32),
                        pltpu.VMEM((gb, nk, CW), BF16), pltpu.VMEM((gb, nk, CW), BF16)],
        compiler_params=_params(("arbitrary",)),
        name="ssm",
    )(u3, tmat, wsr, wsi, rxr, rxi, a16r, a16i, d_tiled)


def _glu_kernel(z_ref, perm_ref, wg_ref, bg_ref, wb_ref, g0_ref, o_ref):
    tm = o_ref.shape[0]
    z = _dot(perm_ref[...], z_ref[...].reshape(tm, SSM_WIDTH)).astype(BF16)
    h = _dot(z, wg_ref[...]) + bg_ref[...]
    ga = h[:, :SSM_WIDTH]
    gb = h[:, SSM_WIDTH:]
    a = (ga * _sigmoid(gb)).astype(BF16)
    y = _dot(a, wb_ref[...])
    o_ref[...] = (_sigmoid(g0_ref[...].astype(F32)) * y).astype(o_ref.dtype)


def _glu(z3, w_glu, b_glu, w_br, proj, tm=512):
    t = z3.shape[0] * z3.shape[1]
    nk = tm // CHUNK
    r = np.arange(tm)
    perm = np.zeros((tm, tm), np.float32)
    perm[r, (r % CHUNK) * nk + r // CHUNK] = 1.0
    return pl.pallas_call(
        _glu_kernel,
        grid=(t // tm,),
        in_specs=[
            pl.BlockSpec((CHUNK, nk, SSM_WIDTH), lambda i: (0, i, 0)),
            pl.BlockSpec((tm, tm), lambda i: (0, 0)),
            pl.BlockSpec((SSM_WIDTH, 2 * SSM_WIDTH), lambda i: (0, 0)),
            pl.BlockSpec((1, 2 * SSM_WIDTH), lambda i: (0, 0)),
            pl.BlockSpec((SSM_WIDTH, D_MODEL), lambda i: (0, 0)),
            pl.BlockSpec((tm, D_MODEL), lambda i: (i, COL_G0 // D_MODEL)),
        ],
        out_specs=pl.BlockSpec((tm, D_MODEL), lambda i: (i, 0)),
        out_shape=jax.ShapeDtypeStruct((t, D_MODEL), BF16),
        compiler_params=_params(("arbitrary",)),
        name="glu",
    )(z3, jnp.asarray(perm, BF16), w_glu, b_glu, w_br, proj)


def _rope_pattern():
    half = ROPE_DIM // 2
    inv_freq = (np.float32(ROPE_THETA) ** (-np.arange(half, dtype=np.float32) / np.float32(half))).astype(np.float32)
    d = np.arange(LANES) % HEAD_DIM
    freq = np.where(d < ROPE_DIM, inv_freq[d % half], 0.0).astype(np.float32)
    m_up = np.where(d < half, -1.0, 0.0).astype(np.float32)
    m_dn = np.where((d >= half) & (d < ROPE_DIM), 1.0, 0.0).astype(np.float32)
    return np.stack([freq, m_up, m_dn] + [np.zeros(LANES, np.float32)] * 5)


def _attn_kernel(q_ref, k_ref, v_ref, pos_ref, pat_ref, sink_ref, wbr_ref, ysg_ref, g1_ref,
                 o_ref, qbuf, kbuf, vbuf, obuf):
    i = pl.program_id(0)
    tq = q_ref.shape[0]
    nw = tq // WINDOW
    half = ROPE_DIM // 2

    @pl.when(i == 0)
    def _():
        kbuf[0:WINDOW, :] = jnp.zeros((WINDOW, KV_WIDTH), BF16)
        vbuf[0:WINDOW, :] = jnp.zeros((WINDOW, KV_WIDTH), BF16)

    pos = pos_ref[...].astype(F32)
    ang = pos * pat_ref[0:1, :]
    cs = jnp.cos(ang)
    sn = jnp.sin(ang)
    c_up = sn * pat_ref[1:2, :]
    c_dn = sn * pat_ref[2:3, :]

    def rope(x):
        return (x * cs + pltpu.roll(x, LANES - half, 1) * c_up + pltpu.roll(x, half, 1) * c_dn)

    for cb in range(Q_WIDTH // LANES):
        sl = slice(cb * LANES, (cb + 1) * LANES)
        qbuf[:, sl] = rope(q_ref[:, sl].astype(F32)).astype(BF16)
    for cb in range(KV_WIDTH // LANES):
        sl = slice(cb * LANES, (cb + 1) * LANES)
        kbuf[WINDOW:, sl] = rope(k_ref[:, sl].astype(F32)).astype(BF16)
    vbuf[WINDOW:, :] = v_ref[...]

    qi = lax.broadcasted_iota(I32, (WINDOW, 2 * WINDOW), 0)
    kj = lax.broadcasted_iota(I32, (WINDOW, 2 * WINDOW), 1)
    dist = qi + WINDOW - kj
    in_band = (dist >= 0) & (dist < WINDOW)
    cur_only = kj >= WINDOW
    sinks = sink_ref[...]
    scale = HEAD_DIM ** -0.5

    def window(w, carry):
        r0 = pl.multiple_of(w * WINDOW, WINDOW)
        not_first = (i * nw + w) > 0
        mask = in_band & (cur_only | not_first)
        for h in range(N_Q_HEADS):
            kv = h // Q_PER_KV
            qh = qbuf[pl.ds(r0, WINDOW), h * HEAD_DIM:(h + 1) * HEAD_DIM]
            kb = kbuf[pl.ds(r0, 2 * WINDOW), kv * HEAD_DIM:(kv + 1) * HEAD_DIM]
            vb = vbuf[pl.ds(r0, 2 * WINDOW), kv * HEAD_DIM:(kv + 1) * HEAD_DIM]
            s = _dot_nt(qh, kb) * scale
            s = jnp.where(mask, s, -jnp.inf)
            sink = sinks[:, h:h + 1]
            m = jnp.maximum(jnp.max(s, axis=-1, keepdims=True), sink)
            p = jnp.exp(s - m)
            denom = jnp.sum(p, axis=-1, keepdims=True) + jnp.exp(sink - m)
            o = _dot(p.astype(BF16), vb) / denom
            obuf[pl.ds(r0, WINDOW), h * HEAD_DIM:(h + 1) * HEAD_DIM] = o.astype(BF16)
        return carry

    lax.fori_loop(0, nw, window, 0)

    kbuf[0:WINDOW, :] = kbuf[tq:tq + WINDOW, :]
    vbuf[0:WINDOW, :] = vbuf[tq:tq + WINDOW, :]

    y = _dot(obuf[...], wbr_ref[...])
    o_ref[...] = (ysg_ref[...].astype(F32) + _sigmoid(g1_ref[...].astype(F32)) * y).astype(o_ref.dtype)


def _attn(proj, pos_col, sinks, w_br, ysg, tq=512):
    t = proj.shape[0]
    pat = jnp.asarray(_rope_pattern())
    return pl.pallas_call(
        _attn_kernel,
        grid=(t // tq,),
        in_specs=[
            pl.BlockSpec((tq, Q_WIDTH), lambda i: (i, COL_Q // Q_WIDTH)),
            pl.BlockSpec((tq, KV_WIDTH), lambda i: (i, COL_K // KV_WIDTH)),
            pl.BlockSpec((tq, KV_WIDTH), lambda i: (i, COL_V // KV_WIDTH)),
            pl.BlockSpec((tq, 1), lambda i: (i, 0)),
            pl.BlockSpec((8, LANES), lambda i: (0, 0)),
            pl.BlockSpec((1, N_Q_HEADS), lambda i: (0, 0)),
            pl.BlockSpec((Q_WIDTH, D_MODEL), lambda i: (0, 0)),
            pl.BlockSpec((tq, D_MODEL), lambda i: (i, 0)),
            pl.BlockSpec((tq, D_MODEL), lambda i: (i, COL_G1 // D_MODEL)),
        ],
        out_specs=pl.BlockSpec((tq, D_MODEL), lambda i: (i, 0)),
        out_shape=jax.ShapeDtypeStruct((t, D_MODEL), BF16),
        scratch_shapes=[
            pltpu.VMEM((tq, Q_WIDTH), BF16),
            pltpu.VMEM((tq + WINDOW, KV_WIDTH), BF16),
            pltpu.VMEM((tq + WINDOW, KV_WIDTH), BF16),
            pltpu.VMEM((tq, Q_WIDTH), BF16),
        ],
        compiler_params=_params(("arbitrary",)),
        name="attn",
    )(proj, proj, proj, pos_col, pat, sinks, w_br, ysg, proj)


def _oproj_kernel(x_ref, mix_ref, wo_ref, g_ref, wr_ref, br_ref, h_ref, hp_ref, lt_ref):
    h = x_ref[...] + _dot(mix_ref[...], wo_ref[...])
    h_ref[...] = h
    ms = jnp.mean(h * h, axis=-1, keepdims=True)
    hn = h * lax.rsqrt(ms + EPS) * g_ref[...]
    hp_ref[...] = _pack_halves(hn[:, :HALF], hn[:, HALF:])
    logits = _dot(hn, wr_ref[...], precision=HIGHEST) + br_ref[...]
    lt_ref[...] = logits.T


def _oproj(x, mixed, w_o, g, w_router, b_router, tm=512):
    t = x.shape[0]
    return pl.pallas_call(
        _oproj_kernel,
        grid=(t // tm,),
        in_specs=[
            pl.BlockSpec((tm, D_MODEL), lambda i: (i, 0)),
            pl.BlockSpec((tm, D_MODEL), lambda i: (i, 0)),
            pl.BlockSpec((D_MODEL, D_MODEL), lambda i: (0, 0)),
            pl.BlockSpec((1, D_MODEL), lambda i: (0, 0)),
            pl.BlockSpec((D_MODEL, LANES), lambda i: (0, 0)),
            pl.BlockSpec((1, LANES), lambda i: (0, 0)),
        ],
        out_specs=[
            pl.BlockSpec((tm, D_MODEL), lambda i: (i, 0)),
            pl.BlockSpec((tm, HALF), lambda i: (i, 0)),
            pl.BlockSpec((LANES, tm), lambda i: (0, i)),
        ],
        out_shape=[
            jax.ShapeDtypeStruct((t, D_MODEL), F32),
            jax.ShapeDtypeStruct((t, HALF), jnp.uint32),
            jax.ShapeDtypeStruct((LANES, t), F32),
        ],
        compiler_params=_params(("arbitrary",)),
        name="oproj",
    )(x, mixed, w_o, g, w_router, b_router)


ROUTE_CHUNK = 256


def _route_kernel(lt_ref, dest_ref, wts_ref, bexp_ref, nvalid_ref, eid_ref, rank_ref):
    t = lt_ref.shape[1]
    nc = t // ROUTE_CHUNK
    r8 = lax.broadcasted_iota(I32, (N_GROUPS, ROUTE_CHUNK), 0)
    r64 = lax.broadcasted_iota(I32, (N_EXPERTS, ROUTE_CHUNK), 0)

    def pick(c, carry):
        cols = pl.ds(pl.multiple_of(c * ROUTE_CHUNK, ROUTE_CHUNK), ROUTE_CHUNK)
        lg = lt_ref[0:N_GROUPS, cols]
        m = jnp.max(lg, axis=0, keepdims=True)
        ssum = jnp.sum(jnp.exp(lg - m), axis=0, keepdims=True)
        p_grp = 1.0 / ssum
        grp = jnp.min(jnp.where(lg == m, r8, N_GROUPS), axis=0, keepdims=True)
        le = lt_ref[N_GROUPS:N_GROUPS + N_EXPERTS, cols]
        leg = jnp.where((r64 // EXPERTS_PER_GROUP) == grp, le, -jnp.inf)
        m1 = jnp.max(leg, axis=0, keepdims=True)
        i1 = jnp.min(jnp.where(leg == m1, r64, N_EXPERTS), axis=0, keepdims=True)
        leg2 = jnp.where(r64 == i1, -jnp.inf, leg)
        m2 = jnp.max(leg2, axis=0, keepdims=True)
        i2 = jnp.min(jnp.where(leg2 == m2, r64, N_EXPERTS), axis=0, keepdims=True)
        ex = jnp.exp(m2 - m1)
        eid_ref[0:1, cols] = i1
        eid_ref[1:2, cols] = i2
        wts_ref[0:1, cols] = p_grp / (1.0 + ex)
        wts_ref[1:2, cols] = p_grp * ex / (1.0 + ex)
        return carry

    lax.fori_loop(0, nc, pick, 0)

    a_row = lax.broadcasted_iota(I32, (ROUTE_CHUNK, ROUTE_CHUNK), 0)
    a_col = lax.broadcasted_iota(I32, (ROUTE_CHUNK, ROUTE_CHUNK), 1)
    before = (a_row < a_col).astype(BF16)

    def count(n, carry):
        j = n // nc
        c = n - j * nc
        cols = pl.ds(pl.multiple_of(c * ROUTE_CHUNK, ROUTE_CHUNK), ROUTE_CHUNK)
        oh = r64 == eid_ref[pl.ds(j, 1), cols]
        ohf = oh.astype(F32)
        pref = _dot(ohf.astype(BF16), before) + carry
        rank_ref[pl.ds(j, 1), cols] = jnp.sum(jnp.where(oh, pref, 0.0), axis=0, keepdims=True)
        return carry + jnp.sum(ohf, axis=1, keepdims=True)

    counts = lax.fori_loop(0, TOP_K * nc, count, jnp.zeros((N_EXPERTS, 1), F32))

    padded = jnp.floor((counts + (MOE_BLOCK - 1)) * (1.0 / MOE_BLOCK)) * MOE_BLOCK
    e_row = lax.broadcasted_iota(I32, (N_EXPERTS, N_EXPERTS), 0)
    e_col = lax.broadcasted_iota(I32, (N_EXPERTS, N_EXPERTS), 1)
    incl = (e_col <= e_row).astype(F32)
    pad_end = _dot(incl, jnp.broadcast_to(padded, (N_EXPERTS, LANES)), precision=HIGHEST)[:, 0:1]
    pad_start = pad_end - padded

    def place(n, carry):
        j = n // nc
        c = n - j * nc
        cols = pl.ds(pl.multiple_of(c * ROUTE_CHUNK, ROUTE_CHUNK), ROUTE_CHUNK)
        oh = r64 == eid_ref[pl.ds(j, 1), cols]
        start = jnp.sum(jnp.where(oh, pad_start, 0.0), axis=0, keepdims=True)
        dest_ref[pl.ds(j, 1), cols] = (start + rank_ref[pl.ds(j, 1), cols]).astype(I32)
        return carry

    lax.fori_loop(0, TOP_K * nc, place, 0)

    b0 = (lax.broadcasted_iota(I32, (N_EXPERTS, bexp_ref.shape[1]), 1) * MOE_BLOCK).astype(F32)
    n_done = jnp.sum((pad_end <= b0).astype(F32), axis=0, keepdims=True)
    bexp_ref[...] = jnp.minimum(n_done, float(N_EXPERTS - 1)).astype(I32)
    live = jnp.minimum(pad_start + counts, b0 + MOE_BLOCK) - jnp.maximum(pad_start, b0)
    nvalid_ref[...] = jnp.sum(jnp.maximum(live, 0.0), axis=0, keepdims=True).astype(I32)


def _route(logits_t, n_blocks_pad):
    t = logits_t.shape[1]
    return pl.pallas_call(
        _route_kernel,
        out_shape=[
            jax.ShapeDtypeStruct((TOP_K, t), I32),
            jax.ShapeDtypeStruct((TOP_K, t), F32),
            jax.ShapeDtypeStruct((1, n_blocks_pad), I32),
            jax.ShapeDtypeStruct((1, n_blocks_pad), I32),
        ],
        scratch_shapes=[pltpu.VMEM((TOP_K, t), I32), pltpu.VMEM((TOP_K, t), F32)],
        compiler_params=pltpu.CompilerParams(vmem_limit_bytes=VMEM_LIMIT),
        name="route",
    )(logits_t)


def _invmap_kernel(dest_ref, rt_ref):
    n_rows = rt_ref.shape[0]
    t = dest_ref.shape[0] // TOP_K

    def clear(r, c):
        rt_ref[r] = 0
        return c

    lax.fori_loop(0, n_rows, clear, 0, unroll=16)

    def put(tok, c):
        rt_ref[dest_ref[tok]] = tok
        rt_ref[dest_ref[t + tok]] = tok
        return c

    lax.fori_loop(0, t, put, 0, unroll=8)


def _invmap(dest_flat, n_rows):
    return pl.pallas_call(
        _invmap_kernel,
        in_specs=[pl.BlockSpec(memory_space=pltpu.SMEM)],
        out_specs=pl.BlockSpec(memory_space=pltpu.SMEM),
        out_shape=jax.ShapeDtypeStruct((n_rows,), I32),
        name="invmap",
    )(dest_flat)


SUBLANES = 8


def _row_copy(src_hbm, src_row, dst_buf, slot, tile, sub, sem):
    return pltpu.make_async_copy(src_hbm.at[src_row >> 3, pl.ds(src_row & (SUBLANES - 1), 1), :],
                                 dst_buf.at[slot, tile, pl.ds(sub, 1), :], sem.at[slot])


def _tiles_copy(src_hbm, dst_buf, slot, tiles, sem):
    return pltpu.make_async_copy(src_hbm.at[pl.ds(0, tiles)], dst_buf.at[slot, pl.ds(0, tiles)], sem.at[slot])


def _expert_kernel(bexp_ref, nvalid_ref, rt_ref, hp_hbm, wg_hbm, wu_hbm, wd_hbm, ys_ref,
                   xbuf, wf_g, wf_u, wf_d, wb_g, wb_u, wb_d, xsem, wsem):
    b = pl.program_id(0)
    nb = pl.num_programs(0) - 1

    def live_tiles(blk):
        return jnp.where(nvalid_ref[blk] > 0, MOE_BLOCK // SUBLANES, 0)

    def weight_copies(e):
        return (pltpu.make_async_copy(wg_hbm.at[e], wf_g, wsem.at[0]),
                pltpu.make_async_copy(wu_hbm.at[e], wf_u, wsem.at[1]),
                pltpu.make_async_copy(wd_hbm.at[e], wf_d, wsem.at[2]))

    @pl.when(b == 0)
    def _():
        for c in weight_copies(bexp_ref[0]):
            c.start(priority=1)

    @pl.when(b < nb)
    def _():
        base = b * MOE_BLOCK
        slot = b % 2

        def body(i, c):
            for s in range(SUBLANES):
                _row_copy(hp_hbm, rt_ref[base + i * SUBLANES + s], xbuf, slot, i, s, xsem).start()
            return c

        lax.fori_loop(0, live_tiles(jnp.minimum(b, nb - 1)), body, 0)

    @pl.when(b > 0)
    def _():
        blk = b - 1
        e = bexp_ref[blk]
        first = (blk == 0) | (e != bexp_ref[jnp.maximum(blk - 1, 0)])

        @pl.when(first)
        def _():
            for c in weight_copies(e):
                c.wait()
            wb_g[...] = wf_g[...].astype(BF16)
            wb_u[...] = wf_u[...].astype(BF16)
            wb_d[...] = wf_d[...].astype(BF16)
            nxt = lax.while_loop(lambda j: (j < nb) & (bexp_ref[jnp.minimum(j, nb - 1)] == e),
                                 lambda j: j + 1, blk + 1)

            @pl.when(nxt < nb)
            def _():
                for c in weight_copies(bexp_ref[jnp.minimum(nxt, nb - 1)]):
                    c.start(priority=1)

        slot = blk % 2
        tiles = live_tiles(blk)

        @pl.when(tiles > 0)
        def _():
            _tiles_copy(hp_hbm, xbuf, slot, tiles, xsem).wait()
            xw = xbuf[slot].reshape(MOE_BLOCK, HALF)
            lo = _unpack_half(xw, 0).astype(BF16)
            hi = _unpack_half(xw, 1).astype(BF16)
            g = _dot(lo, wb_g[:HALF, :]) + _dot(hi, wb_g[HALF:, :])
            u = _dot(lo, wb_u[:HALF, :]) + _dot(hi, wb_u[HALF:, :])
            h = (g * _sigmoid(g) * u).astype(BF16)
            y = _dot(h, wb_d[...])
            ys_ref[...] = _pack_halves(y[:, :HALF], y[:, HALF:]).reshape(ys_ref.shape)

        @pl.when(tiles == 0)
        def _():
            zero = jnp.zeros((MOE_BLOCK, HALF), F32)
            ys_ref[...] = _pack_halves(zero, zero).reshape(ys_ref.shape)


def _experts(bexp, nvalid, row_tok, hn_packed, w_gate, w_up, w_down, n_blocks):
    grid_spec = pltpu.PrefetchScalarGridSpec(
        num_scalar_prefetch=3,
        grid=(n_blocks + 1,),
        in_specs=[pl.BlockSpec(memory_space=pl.ANY)] * 4,
        out_specs=pl.BlockSpec((MOE_BLOCK // SUBLANES, SUBLANES, HALF),
                               lambda b, be, nv, rt: (jnp.maximum(b - 1, 0), 0, 0)),
        scratch_shapes=[
            pltpu.VMEM((2, MOE_BLOCK // SUBLANES, SUBLANES, HALF), jnp.uint32),
            pltpu.VMEM((D_MODEL, D_EXPERT), F32), pltpu.VMEM((D_MODEL, D_EXPERT), F32),
            pltpu.VMEM((D_EXPERT, D_MODEL), F32),
            pltpu.VMEM((D_MODEL, D_EXPERT), BF16), pltpu.VMEM((D_MODEL, D_EXPERT), BF16),
            pltpu.VMEM((D_EXPERT, D_MODEL), BF16),
            pltpu.SemaphoreType.DMA((2,)), pltpu.SemaphoreType.DMA((3,)),
        ],
    )
    return pl.pallas_call(
        _expert_kernel,
        grid_spec=grid_spec,
        out_shape=jax.ShapeDtypeStruct((n_blocks * MOE_BLOCK // SUBLANES, SUBLANES, HALF), jnp.uint32),
        compiler_params=_params(("arbitrary",)),
        name="experts",
    )(bexp, nvalid, row_tok, hn_packed.reshape(-1, SUBLANES, HALF), w_gate, w_up, w_down)


def _combine_kernel(dest_ref, ys_hbm, h_ref, w_ref, g_ref, o_ref, ybuf, sem):
    i = pl.program_id(0)
    n = pl.num_programs(0)
    tq = h_ref.shape[0]
    t = n * tq

    tiles = tq // SUBLANES

    def start_gather(blk, slot):
        base = blk * tq

        def body(j, c):
            for s in range(SUBLANES):
                r = base + j * SUBLANES + s
                _row_copy(ys_hbm, dest_ref[r], ybuf, slot, j, s, sem).start()
                _row_copy(ys_hbm, dest_ref[t + r], ybuf, slot, tiles + j, s, sem).start()
            return c

        lax.fori_loop(0, tiles, body, 0)

    @pl.when(i == 0)
    def _():
        start_gather(0, 0)

    @pl.when(i + 1 < n)
    def _():
        start_gather(i + 1, (i + 1) % 2)

    slot = i % 2
    _tiles_copy(ys_hbm, ybuf, slot, TOP_K * tiles, sem).wait()
    y0 = ybuf[slot, 0:tiles].reshape(tq, HALF)
    y1 = ybuf[slot, tiles:2 * tiles].reshape(tq, HALF)
    w = w_ref[...]
    w0 = w[:, 0:1]
    w1 = w[:, 1:2]
    h = h_ref[...]
    lo = h[:, :HALF] + w0 * _unpack_half(y0, 0) + w1 * _unpack_half(y1, 0)
    hi = h[:, HALF:] + w0 * _unpack_half(y0, 1) + w1 * _unpack_half(y1, 1)
    ms = (jnp.sum(lo * lo, axis=-1, keepdims=True) + jnp.sum(hi * hi, axis=-1, keepdims=True)) * (1.0 / D_MODEL)
    inv = lax.rsqrt(ms + EPS)
    g = g_ref[...]
    o_ref[:, :HALF] = lo * inv * g[:, :HALF]
    o_ref[:, HALF:] = hi * inv * g[:, HALF:]


def _combine(dest_flat, ys_packed, h1, wts_tok, g_final, tq=256):
    t = h1.shape[0]
    grid_spec = pltpu.PrefetchScalarGridSpec(
        num_scalar_prefetch=1,
        grid=(t // tq,),
        in_specs=[
            pl.BlockSpec(memory_space=pl.ANY),
            pl.BlockSpec((tq, D_MODEL), lambda i, d: (i, 0)),
            pl.BlockSpec((tq, TOP_K), lambda i, d: (i, 0)),
            pl.BlockSpec((1, D_MODEL), lambda i, d: (0, 0)),
        ],
        out_specs=pl.BlockSpec((tq, D_MODEL), lambda i, d: (i, 0)),
        scratch_shapes=[pltpu.VMEM((2, TOP_K * tq // SUBLANES, SUBLANES, HALF), jnp.uint32),
                        pltpu.SemaphoreType.DMA((2,))],
    )
    return pl.pallas_call(
        _combine_kernel,
        grid_spec=grid_spec,
        out_shape=jax.ShapeDtypeStruct((t, D_MODEL), F32),
        compiler_params=_params(("arbitrary",)),
        name="combine",
    )(dest_flat, ys_packed, h1, wts_tok, g_final)


def _mixers(h, positions, norm_g, w_in, b_in, a_re, a_im, log_dt, b_re, b_im, c_re, c_im, d_skip,
            w_glu, b_glu, w_br_ssm, sinks, w_br_attn):
    t = h.shape[0]
    ngate = 2 * D_MODEL
    g = norm_g.reshape(1, D_MODEL)
    w_main = jnp.concatenate([w_in[:, IN_WIDTH - ngate:], w_in[:, SSM_WIDTH:IN_WIDTH - ngate]], axis=1).astype(BF16)
    b_main = jnp.concatenate([b_in[IN_WIDTH - ngate:], b_in[SSM_WIDTH:IN_WIDTH - ngate]]).reshape(1, -1)
    proj = _inproj(h, g, w_main, b_main, chunk_major=False)
    u3 = _inproj(h, g, w_in[:, :SSM_WIDTH].astype(BF16), b_in[:SSM_WIDTH].reshape(1, -1), chunk_major=True)

    tmat, wsr, wsi, rxr, rxi, a16r, a16i = _ssm_prep(
        a_re, a_im, log_dt, b_re.transpose(0, 2, 1), b_im.transpose(0, 2, 1), c_re, c_im)
    d_tiled = jnp.tile(d_skip, (1, CHUNK)).reshape(SSM_GROUPS, 1, CW)
    z3 = _ssm(u3, tmat, wsr, wsi, rxr, rxi, a16r.reshape(SSM_GROUPS, SSM_STATE),
              a16i.reshape(SSM_GROUPS, SSM_STATE), d_tiled)

    ysg = _glu(z3, w_glu.astype(BF16), b_glu.reshape(1, -1), w_br_ssm.astype(BF16), proj)
    return _attn(proj, positions.reshape(t, 1), sinks.reshape(1, N_Q_HEADS), w_br_attn.astype(BF16), ysg)


def _moe_tail(h, mixed, w_o, norm_ffn_g, w_rg, b_rg, w_re, b_re, w_gate, w_up, w_down, norm_final_g):
    t = h.shape[0]
    n_route = N_GROUPS + N_EXPERTS
    w_router = jnp.concatenate([w_rg, w_re, jnp.zeros((D_MODEL, LANES - n_route), F32)], axis=1)
    b_router = jnp.concatenate([b_rg, b_re, jnp.zeros((LANES - n_route,), F32)]).reshape(1, LANES)
    h1, hn_packed, logits_t = _oproj(h, mixed, w_o.astype(BF16), norm_ffn_g.reshape(1, D_MODEL),
                                     w_router, b_router)
    n_assign = t * TOP_K
    n_blocks = -(-(n_assign + N_EXPERTS * (MOE_BLOCK - 1)) // MOE_BLOCK)
    n_blocks_pad = -(-n_blocks // LANES) * LANES
    dest, wts, bexp, nvalid = _route(logits_t, n_blocks_pad)
    dest_flat = dest.reshape(n_assign)
    row_tok = _invmap(dest_flat, n_blocks * MOE_BLOCK)
    ys_packed = _experts(bexp[0, :n_blocks], nvalid[0, :n_blocks], row_tok, hn_packed, w_gate, w_up, w_down,
                         n_blocks)
    return _combine(dest_flat, ys_packed, h1, wts.T, norm_final_g.reshape(1, D_MODEL))


def kernel(x, positions, norm_mix_g, w_in, b_in, ssm_a_re, ssm_a_im, ssm_log_dt, ssm_b_re, ssm_b_im, ssm_c_re, ssm_c_im, ssm_d, w_glu, b_glu, w_br_ssm, attn_sinks, w_br_attn, w_o, norm_ffn_g, w_router_group, b_router_group, w_router_expert, b_router_expert, w_exp_gate, w_exp_up, w_exp_down, norm_final_g):
    bsz, seq, d = x.shape
    assert bsz == 1 and d == D_MODEL and norm_mix_g.shape[0] == 1
    h = x.reshape(seq, d)
    mixed = _mixers(h, positions, norm_mix_g[0], w_in[0], b_in[0], ssm_a_re[0], ssm_a_im[0], ssm_log_dt[0],
                    ssm_b_re[0], ssm_b_im[0], ssm_c_re[0], ssm_c_im[0], ssm_d[0], w_glu[0], b_glu[0],
                    w_br_ssm[0], attn_sinks[0], w_br_attn[0])
    out = _moe_tail(h, mixed, w_o[0], norm_ffn_g[0], w_router_group[0], b_router_group[0],
                    w_router_expert[0], b_router_expert[0], w_exp_gate[0], w_exp_up[0], w_exp_down[0],
                    norm_final_g)
    return out.reshape(bsz, seq, d)
```

```python
import functools
import math

import numpy as np
import jax
import jax.numpy as jnp
from jax import lax
from jax.experimental import pallas as pl
from jax.experimental.pallas import tpu as pltpu

F32 = jnp.float32
BF16 = jnp.bfloat16
I32 = jnp.int32

D_MODEL = 2048
SSM_WIDTH = 1024
SSM_GROUP = 16
SSM_GROUPS = 64
SSM_STATE = 64
HEAD_DIM = 64
N_Q_HEADS = 16
N_KV_HEADS = 4
Q_PER_KV = 4
WINDOW = 128
ROPE_DIM = 16
ROPE_THETA = 500000.0
Q_WIDTH = 1024
KV_WIDTH = 256
IN_WIDTH = SSM_WIDTH + Q_WIDTH + 2 * KV_WIDTH + 2 * D_MODEL
N_GROUPS = 8
EXPERTS_PER_GROUP = 8
N_EXPERTS = 64
TOP_K = 2
D_EXPERT = 512
MOE_BLOCK = 128
EPS = 1e-6

CHUNK = 16
CW = CHUNK * SSM_GROUP
GROUP_BLOCK = 8
HALF = D_MODEL // 2
LANES = 128
VMEM_LIMIT = 56 * 1024 * 1024

COL_G0, COL_G1, COL_Q, COL_K, COL_V = 0, 2048, 4096, 5120, 5376

HIGHEST = lax.Precision.HIGHEST


def _dot(a, b, precision=None):
    return jnp.dot(a, b, preferred_element_type=F32, precision=precision)


def _dot_nt(a, b, precision=None):
    return lax.dot_general(a, b, (((1,), (1,)), ((), ())), preferred_element_type=F32,
                           precision=precision)


def _sigmoid(x):
    return 1.0 / (1.0 + jnp.exp(-x))


def _pack_halves(lo, hi):
    return pltpu.pack_elementwise([lo, hi], packed_dtype=BF16)


def _unpack_half(w, index):
    return pltpu.unpack_elementwise(w, index=index, packed_dtype=BF16, unpacked_dtype=F32)


def _params(sem, vmem=VMEM_LIMIT):
    return pltpu.CompilerParams(dimension_semantics=sem, vmem_limit_bytes=vmem)


def _inproj_kernel(x_ref, g_ref, w_ref, b_ref, o_ref, xn_ref, *rest, chunk_major):
    @pl.when(pl.program_id(1) == 0)
    def _():
        x = x_ref[...]
        ms = jnp.mean(x * x, axis=-1, keepdims=True)
        xn_ref[...] = (x * lax.rsqrt(ms + EPS) * g_ref[...]).astype(BF16)

    acc = _dot(xn_ref[...], w_ref[...]) + b_ref[...]
    if not chunk_major:
        o_ref[...] = acc.astype(o_ref.dtype)
        return
    (acc_ref,) = rest
    nk = o_ref.shape[1]
    for c in range(acc_ref.shape[0]):
        acc_ref[c] = acc[:, c * LANES:(c + 1) * LANES]
    for s in range(CHUNK):
        for c in range(acc_ref.shape[0]):
            o_ref[s, :, c * LANES:(c + 1) * LANES] = acc_ref[c, pl.ds(s, nk, stride=CHUNK), :].astype(o_ref.dtype)


def _inproj(x, g, w_bf16, b, *, chunk_major, tm=1024, tn=512):
    t, d = x.shape
    n = w_bf16.shape[1]
    scratch = [pltpu.VMEM((tm, d), BF16)]
    if chunk_major:
        out_spec = pl.BlockSpec((CHUNK, tm // CHUNK, tn), lambda i, j: (0, i, j))
        out_shape = jax.ShapeDtypeStruct((CHUNK, t // CHUNK, n), BF16)
        scratch.append(pltpu.VMEM((tn // LANES, tm, LANES), F32))
    else:
        out_spec = pl.BlockSpec((tm, tn), lambda i, j: (i, j))
        out_shape = jax.ShapeDtypeStruct((t, n), BF16)
    return pl.pallas_call(
        functools.partial(_inproj_kernel, chunk_major=chunk_major),
        grid=(t // tm, n // tn),
        in_specs=[
            pl.BlockSpec((tm, d), lambda i, j: (i, 0)),
            pl.BlockSpec((1, d), lambda i, j: (0, 0)),
            pl.BlockSpec((d, tn), lambda i, j: (0, j)),
            pl.BlockSpec((1, tn), lambda i, j: (0, j)),
        ],
        out_specs=out_spec,
        out_shape=out_shape,
        scratch_shapes=scratch,
        compiler_params=_params(("arbitrary", "arbitrary")),
        name="inproj_u" if chunk_major else "inproj",
    )(x, g, w_bf16, b)


def _ssm_prep_kernel(are_ref, aim_ref, ldt_ref, btr_ref, bti_ref, cr_ref, ci_ref,
                     t_ref, wsr_ref, wsi_ref, rxr_ref, rxi_ref, a16r_ref, a16i_ref):
    lam_re = jnp.minimum(are_ref[0], -1e-4)
    lam_im = aim_ref[0]
    dt = jnp.exp(ldt_ref[0])
    lr_dt = lam_re * dt
    th = lam_im * dt
    mag = jnp.exp(lr_dt)
    ab_re = mag * jnp.cos(th)
    ab_im = mag * jnp.sin(th)
    den = lam_re * lam_re + lam_im * lam_im
    nr = ab_re - 1.0
    ni = ab_im
    coef_re = (nr * lam_re + ni * lam_im) / den
    coef_im = (ni * lam_re - nr * lam_im) / den
    btr = btr_ref[0]
    bti = bti_ref[0]
    bb_re = coef_re * btr - coef_im * bti
    bb_im = coef_re * bti + coef_im * btr

    e = lax.broadcasted_iota(I32, (CHUNK, SSM_STATE), 0).astype(F32)
    pmag = jnp.exp(e * lr_dt)
    pos_re = pmag * jnp.cos(e * th)
    pos_im = pmag * jnp.sin(e * th)
    nmag = jnp.exp(-e * lr_dt)
    neg_re = nmag * jnp.cos(e * th)
    neg_im = -nmag * jnp.sin(e * th)

    def rep(tab):
        return jnp.broadcast_to(tab[:, None, :], (CHUNK, SSM_GROUP, SSM_STATE)).reshape(CW, SSM_STATE)

    def tile(mat):
        return jnp.broadcast_to(mat[None, :, :], (CHUNK, SSM_GROUP, SSM_STATE)).reshape(CW, SSM_STATE)

    pr, pi = rep(pos_re), rep(pos_im)
    ctr, cti = tile(cr_ref[0]), tile(ci_ref[0])
    r_re = ctr * pr - cti * pi
    r_im = ctr * pi + cti * pr
    qr, qi = rep(neg_re), rep(neg_im)
    btr_t, bti_t = tile(bb_re), tile(bb_im)
    l_re = btr_t * qr - bti_t * qi
    l_im = btr_t * qi + bti_t * qr

    tm = _dot_nt(l_re, r_re, precision=HIGHEST) - _dot_nt(l_im, r_im, precision=HIGHEST)
    srow = lax.broadcasted_iota(I32, (CW, CW), 0) // SSM_GROUP
    tcol = lax.broadcasted_iota(I32, (CW, CW), 1) // SSM_GROUP
    t_ref[0] = jnp.where(tcol >= srow, tm, 0.0).astype(BF16)

    a15r = pos_re[CHUNK - 1:CHUNK, :]
    a15i = pos_im[CHUNK - 1:CHUNK, :]
    wsr_ref[0] = (l_re * a15r - l_im * a15i).astype(BF16)
    wsi_ref[0] = (l_re * a15i + l_im * a15r).astype(BF16)
    rxr_ref[0] = (r_re * ab_re - r_im * ab_im).astype(BF16)
    rxi_ref[0] = (-(r_re * ab_im + r_im * ab_re)).astype(BF16)
    m16 = jnp.exp(float(CHUNK) * lr_dt)
    a16r_ref[0] = m16 * jnp.cos(float(CHUNK) * th)
    a16i_ref[0] = m16 * jnp.sin(float(CHUNK) * th)


def _ssm_prep(a_re, a_im, log_dt, bt_re, bt_im, c_re, c_im):
    g = a_re.shape[0]
    vec = pl.BlockSpec((1, 1, SSM_STATE), lambda i: (i, 0, 0))
    mat = pl.BlockSpec((1, SSM_GROUP, SSM_STATE), lambda i: (i, 0, 0))
    wide = pl.BlockSpec((1, CW, SSM_STATE), lambda i: (i, 0, 0))
    return pl.pallas_call(
        _ssm_prep_kernel,
        grid=(g,),
        in_specs=[vec, vec, pl.BlockSpec((1, 1, 1), lambda i: (i, 0, 0)), mat, mat, mat, mat],
        out_specs=[pl.BlockSpec((1, CW, CW), lambda i: (i, 0, 0)), wide, wide, wide, wide, vec, vec],
        out_shape=[
            jax.ShapeDtypeStruct((g, CW, CW), BF16),
            jax.ShapeDtypeStruct((g, CW, SSM_STATE), BF16),
            jax.ShapeDtypeStruct((g, CW, SSM_STATE), BF16),
            jax.ShapeDtypeStruct((g, CW, SSM_STATE), BF16),
            jax.ShapeDtypeStruct((g, CW, SSM_STATE), BF16),
            jax.ShapeDtypeStruct((g, 1, SSM_STATE), F32),
            jax.ShapeDtypeStruct((g, 1, SSM_STATE), F32),
        ],
        compiler_params=_params(("arbitrary",)),
        name="ssm_prep",
    )(a_re.reshape(g, 1, SSM_STATE), a_im.reshape(g, 1, SSM_STATE), log_dt.reshape(g, 1, 1),
      bt_re, bt_im, c_re, c_im)


def _gelu_tanh(x):
    c = math.sqrt(2.0 / math.pi)
    return x * (0.5 * (1.0 + jnp.tanh(c * (x + 0.044715 * (x * x * x)))))


def _ssm_kernel(u_ref, t_ref, wsr_ref, wsi_ref, rxr_ref, rxi_ref, a16r_ref, a16i_ref, d_ref,
                z_ref, sr_ref, si_ref, ug_ref, zg_ref):
    nk = u_ref.shape[1]
    for j in range(GROUP_BLOCK):
        ug_ref[j] = jnp.concatenate(
            [u_ref[s, :, j * SSM_GROUP:(j + 1) * SSM_GROUP] for s in range(CHUNK)], axis=1)
    u_ref = ug_ref
    for j in range(GROUP_BLOCK):
        u = u_ref[j]
        sr_ref[j * nk:(j + 1) * nk, :] = _dot(u, wsr_ref[j])
        si_ref[j * nk:(j + 1) * nk, :] = _dot(u, wsi_ref[j])

    ar = a16r_ref[...]
    ai = a16i_ref[...]

    def step(k, carry):
        xr, xi = carry
        rows = pl.ds(k, GROUP_BLOCK, stride=nk)
        sr = sr_ref[rows, :]
        si = si_ref[rows, :]
        sr_ref[rows, :] = xr
        si_ref[rows, :] = xi
        return (ar * xr - ai * xi + sr, ar * xi + ai * xr + si)

    zero = jnp.zeros((GROUP_BLOCK, SSM_STATE), F32)
    lax.fori_loop(0, nk, step, (zero, zero))

    for j in range(GROUP_BLOCK):
        u = u_ref[j]
        xr = sr_ref[j * nk:(j + 1) * nk, :].astype(BF16)
        xi = si_ref[j * nk:(j + 1) * nk, :].astype(BF16)
        y = (_dot(u, t_ref[j]) + _dot_nt(xr, rxr_ref[j]) + _dot_nt(xi, rxi_ref[j])
             + d_ref[j] * u.astype(F32))
        zg_ref[j] = _gelu_tanh(y).astype(zg_ref.dtype)

    for t in range(CHUNK):
        z_ref[t] = jnp.concatenate(
            [zg_ref[j, :, t * SSM_GROUP:(t + 1) * SSM_GROUP] for j in range(GROUP_BLOCK)], axis=1)


def _ssm(u3, tmat, wsr, wsi, rxr, rxi, a16r, a16i, d_tiled):
    _, nk, width = u3.shape
    gb = GROUP_BLOCK
    gl = gb * SSM_GROUP
    blk3 = lambda a, b: pl.BlockSpec((gb, a, b), lambda i: (i, 0, 0))
    io = pl.BlockSpec((CHUNK, nk, gl), lambda i: (0, 0, i))
    return pl.pallas_call(
        _ssm_kernel,
        grid=(width // gl,),
        in_specs=[io, blk3(CW, CW), blk3(CW, SSM_STATE), blk3(CW, SSM_STATE),
                  blk3(CW, SSM_STATE), blk3(CW, SSM_STATE),
                  pl.BlockSpec((gb, SSM_STATE), lambda i: (i, 0)),
                  pl.BlockSpec((gb, SSM_STATE), lambda i: (i, 0)),
                  blk3(1, CW)],
        out_specs=io,
        out_shape=jax.ShapeDtypeStruct(u3.shape, BF16),
        scratch_shapes=[pltpu.VMEM((gb * nk, SSM_STATE), F32), pltpu.VMEM((gb * nk, SSM_STATE), F32),
                        pltpu.VMEM((gb, nk, CW), BF16), pltpu.VMEM((gb, nk, CW), BF16)],
        compiler_params=_params(("arbitrary",)),
        name="ssm",
    )(u3, tmat, wsr, wsi, rxr, rxi, a16r, a16i, d_tiled)


def _glu_kernel(z_ref, perm_ref, wg_ref, bg_ref, wb_ref, g0_ref, o_ref):
    tm = o_ref.shape[0]
    z = _dot(perm_ref[...], z_ref[...].reshape(tm, SSM_WIDTH)).astype(BF16)
    h = _dot(z, wg_ref[...]) + bg_ref[...]
    ga = h[:, :SSM_WIDTH]
    gb = h[:, SSM_WIDTH:]
    a = (ga * _sigmoid(gb)).astype(BF16)
    y = _dot(a, wb_ref[...])
    o_ref[...] = (_sigmoid(g0_ref[...].astype(F32)) * y).astype(o_ref.dtype)


def _glu(z3, w_glu, b_glu, w_br, proj, tm=512):
    t = z3.shape[0] * z3.shape[1]
    nk = tm // CHUNK
    r = np.arange(tm)
    perm = np.zeros((tm, tm), np.float32)
    perm[r, (r % CHUNK) * nk + r // CHUNK] = 1.0
    return pl.pallas_call(
        _glu_kernel,
        grid=(t // tm,),
        in_specs=[
            pl.BlockSpec((CHUNK, nk, SSM_WIDTH), lambda i: (0, i, 0)),
            pl.BlockSpec((tm, tm), lambda i: (0, 0)),
            pl.BlockSpec((SSM_WIDTH, 2 * SSM_WIDTH), lambda i: (0, 0)),
            pl.BlockSpec((1, 2 * SSM_WIDTH), lambda i: (0, 0)),
            pl.BlockSpec((SSM_WIDTH, D_MODEL), lambda i: (0, 0)),
            pl.BlockSpec((tm, D_MODEL), lambda i: (i, COL_G0 // D_MODEL)),
        ],
        out_specs=pl.BlockSpec((tm, D_MODEL), lambda i: (i, 0)),
        out_shape=jax.ShapeDtypeStruct((t, D_MODEL), BF16),
        compiler_params=_params(("arbitrary",)),
        name="glu",
    )(z3, jnp.asarray(perm, BF16), w_glu, b_glu, w_br, proj)


def _rope_pattern():
    half = ROPE_DIM // 2
    inv_freq = (np.float32(ROPE_THETA) ** (-np.arange(half, dtype=np.float32) / np.float32(half))).astype(np.float32)
    d = np.arange(LANES) % HEAD_DIM
    rotated = d < ROPE_DIM
    pat = np.zeros((16, LANES), np.float32)
    pat[:half] = rotated[None, :] & ((d % half)[None, :] == np.arange(half)[:, None])
    pat[8] = ~rotated
    pat[9] = np.where(d < half, -1.0, 0.0)
    pat[10] = np.where((d >= half) & rotated, 1.0, 0.0)
    return inv_freq.reshape(half, 1), pat


def _attn_kernel(q_ref, k_ref, v_ref, pos_ref, freq_ref, pat_ref, sink_ref, wbr_ref, ysg_ref, g1_ref,
                 o_ref, qbuf, kbuf, vbuf, obuf):
    i = pl.program_id(0)
    tq = q_ref.shape[0]
    nw = tq // WINDOW
    half = ROPE_DIM // 2

    @pl.when(i == 0)
    def _():
        kbuf[:, 0:WINDOW, :] = jnp.zeros((2 * N_KV_HEADS, WINDOW, LANES), BF16)
        vbuf[:, 0:WINDOW, :] = jnp.zeros((2 * N_KV_HEADS, WINDOW, LANES), BF16)

    ang = freq_ref[...] * pos_ref[...].astype(F32)
    spread = lambda tab: lax.dot_general(tab, pat_ref[0:8, :], (((0,), (0,)), ((), ())),
                                         preferred_element_type=F32, precision=HIGHEST)
    cs = spread(jnp.cos(ang)) + pat_ref[8:9, :]
    sn = spread(jnp.sin(ang))
    c_up = sn * pat_ref[9:10, :]
    c_dn = sn * pat_ref[10:11, :]

    def rope(x):
        return (x * cs + pltpu.roll(x, LANES - half, 1) * c_up + pltpu.roll(x, half, 1) * c_dn)

    low = lax.broadcasted_iota(I32, (tq, LANES), 1) < HEAD_DIM

    def split_heads(buf, cb, x):
        xs = pltpu.roll(x, HEAD_DIM, 1)
        zero = jnp.zeros_like(x)
        buf[4 * cb + 0, WINDOW:, :] = jnp.where(low, x, zero).astype(BF16)
        buf[4 * cb + 1, WINDOW:, :] = jnp.where(low, zero, xs).astype(BF16)
        buf[4 * cb + 2, WINDOW:, :] = jnp.where(low, xs, zero).astype(BF16)
        buf[4 * cb + 3, WINDOW:, :] = jnp.where(low, zero, x).astype(BF16)

    for cb in range(Q_WIDTH // LANES):
        sl = slice(cb * LANES, (cb + 1) * LANES)
        qbuf[:, sl] = rope(q_ref[:, sl].astype(F32)).astype(BF16)
    for cb in range(KV_WIDTH // LANES):
        sl = slice(cb * LANES, (cb + 1) * LANES)
        split_heads(kbuf, cb, rope(k_ref[:, sl].astype(F32)))
        split_heads(vbuf, cb, v_ref[:, sl].astype(F32))

    qi = lax.broadcasted_iota(I32, (WINDOW, 2 * WINDOW), 0)
    kj = lax.broadcasted_iota(I32, (WINDOW, 2 * WINDOW), 1)
    dist = qi + WINDOW - kj
    in_band = (dist >= 0) & (dist < WINDOW)
    cur_only = kj >= WINDOW
    sinks = sink_ref[...]
    scale = HEAD_DIM ** -0.5

    def window(w, carry):
        r0 = pl.multiple_of(w * WINDOW, WINDOW)
        rows = pl.ds(r0, 2 * WINDOW)
        not_first = (i * nw + w) > 0
        mask = in_band & (cur_only | not_first)

        def probs(qp, part, h):
            s = _dot_nt(qp, kbuf[2 * (h // Q_PER_KV) + part, rows, :]) * scale
            s = jnp.where(mask, s, -jnp.inf)
            sink = sinks[:, h:h + 1]
            m = jnp.maximum(jnp.max(s, axis=-1, keepdims=True), sink)
            p = jnp.exp(s - m)
            denom = jnp.sum(p, axis=-1, keepdims=True) + jnp.exp(sink - m)
            return p.astype(BF16), 1.0 / denom

        for a in range(N_Q_HEADS // 2):
            kv = (2 * a) // Q_PER_KV
            qp = qbuf[pl.ds(r0, WINDOW), a * LANES:(a + 1) * LANES]
            p_lo, r_lo = probs(qp, 0, 2 * a)
            p_hi, r_hi = probs(qp, 1, 2 * a + 1)
            o = _dot(p_lo, vbuf[2 * kv, rows, :]) * r_lo + _dot(p_hi, vbuf[2 * kv + 1, rows, :]) * r_hi
            obuf[pl.ds(r0, WINDOW), a * LANES:(a + 1) * LANES] = o.astype(BF16)
        return carry

    lax.fori_loop(0, nw, window, 0)

    kbuf[:, 0:WINDOW, :] = kbuf[:, tq:tq + WINDOW, :]
    vbuf[:, 0:WINDOW, :] = vbuf[:, tq:tq + WINDOW, :]

    y = _dot(obuf[...], wbr_ref[...])
    o_ref[...] = (ysg_ref[...].astype(F32) + _sigmoid(g1_ref[...].astype(F32)) * y).astype(o_ref.dtype)


def _attn(proj, pos_row, sinks, w_br, ysg, tq=512):
    t = proj.shape[0]
    freq, pat = (jnp.asarray(a) for a in _rope_pattern())
    return pl.pallas_call(
        _attn_kernel,
        grid=(t // tq,),
        in_specs=[
            pl.BlockSpec((tq, Q_WIDTH), lambda i: (i, COL_Q // Q_WIDTH)),
            pl.BlockSpec((tq, KV_WIDTH), lambda i: (i, COL_K // KV_WIDTH)),
            pl.BlockSpec((tq, KV_WIDTH), lambda i: (i, COL_V // KV_WIDTH)),
            pl.BlockSpec((1, tq), lambda i: (0, i)),
            pl.BlockSpec((ROPE_DIM // 2, 1), lambda i: (0, 0)),
            pl.BlockSpec((16, LANES), lambda i: (0, 0)),
            pl.BlockSpec((1, N_Q_HEADS), lambda i: (0, 0)),
            pl.BlockSpec((Q_WIDTH, D_MODEL), lambda i: (0, 0)),
            pl.BlockSpec((tq, D_MODEL), lambda i: (i, 0)),
            pl.BlockSpec((tq, D_MODEL), lambda i: (i, COL_G1 // D_MODEL)),
        ],
        out_specs=pl.BlockSpec((tq, D_MODEL), lambda i: (i, 0)),
        out_shape=jax.ShapeDtypeStruct((t, D_MODEL), BF16),
        scratch_shapes=[
            pltpu.VMEM((tq, Q_WIDTH), BF16),
            pltpu.VMEM((2 * N_KV_HEADS, tq + WINDOW, LANES), BF16),
            pltpu.VMEM((2 * N_KV_HEADS, tq + WINDOW, LANES), BF16),
            pltpu.VMEM((tq, Q_WIDTH), BF16),
        ],
        compiler_params=_params(("arbitrary",)),
        name="attn",
    )(proj, proj, proj, pos_row, freq, pat, sinks, w_br, ysg, proj)


def _oproj_kernel(x_ref, mix_ref, wo_ref, g_ref, wr_ref, br_ref, h_ref, hp_ref, lt_ref):
    h = x_ref[...] + _dot(mix_ref[...], wo_ref[...])
    h_ref[...] = h
    ms = jnp.mean(h * h, axis=-1, keepdims=True)
    hn = h * lax.rsqrt(ms + EPS) * g_ref[...]
    hp_ref[...] = _pack_halves(hn[:, :HALF], hn[:, HALF:])
    logits = _dot(hn, wr_ref[...], precision=HIGHEST) + br_ref[...]
    lt_ref[...] = logits.T


def _oproj(x, mixed, w_o, g, w_router, b_router, tm=512):
    t = x.shape[0]
    return pl.pallas_call(
        _oproj_kernel,
        grid=(t // tm,),
        in_specs=[
            pl.BlockSpec((tm, D_MODEL), lambda i: (i, 0)),
            pl.BlockSpec((tm, D_MODEL), lambda i: (i, 0)),
            pl.BlockSpec((D_MODEL, D_MODEL), lambda i: (0, 0)),
            pl.BlockSpec((1, D_MODEL), lambda i: (0, 0)),
            pl.BlockSpec((D_MODEL, LANES), lambda i: (0, 0)),
            pl.BlockSpec((1, LANES), lambda i: (0, 0)),
        ],
        out_specs=[
            pl.BlockSpec((tm, D_MODEL), lambda i: (i, 0)),
            pl.BlockSpec((tm, HALF), lambda i: (i, 0)),
            pl.BlockSpec((LANES, tm), lambda i: (0, i)),
        ],
        out_shape=[
            jax.ShapeDtypeStruct((t, D_MODEL), F32),
            jax.ShapeDtypeStruct((t, HALF), jnp.uint32),
            jax.ShapeDtypeStruct((LANES, t), F32),
        ],
        compiler_params=_params(("arbitrary",)),
        name="oproj",
    )(x, mixed, w_o, g, w_router, b_router)


ROUTE_CHUNK = 256


def _route_kernel(lt_ref, dest_ref, wts_ref, bexp_ref, nvalid_ref, eid_ref, rank_ref):
    t = lt_ref.shape[1]
    nc = t // ROUTE_CHUNK
    r8 = lax.broadcasted_iota(I32, (N_GROUPS, ROUTE_CHUNK), 0)
    r64 = lax.broadcasted_iota(I32, (N_EXPERTS, ROUTE_CHUNK), 0)

    def pick(c, carry):
        cols = pl.ds(pl.multiple_of(c * ROUTE_CHUNK, ROUTE_CHUNK), ROUTE_CHUNK)
        lg = lt_ref[0:N_GROUPS, cols]
        m = jnp.max(lg, axis=0, keepdims=True)
        ssum = jnp.sum(jnp.exp(lg - m), axis=0, keepdims=True)
        p_grp = 1.0 / ssum
        grp = jnp.min(jnp.where(lg == m, r8, N_GROUPS), axis=0, keepdims=True)
        le = lt_ref[N_GROUPS:N_GROUPS + N_EXPERTS, cols]
        leg = jnp.where((r64 // EXPERTS_PER_GROUP) == grp, le, -jnp.inf)
        m1 = jnp.max(leg, axis=0, keepdims=True)
        i1 = jnp.min(jnp.where(leg == m1, r64, N_EXPERTS), axis=0, keepdims=True)
        leg2 = jnp.where(r64 == i1, -jnp.inf, leg)
        m2 = jnp.max(leg2, axis=0, keepdims=True)
        i2 = jnp.min(jnp.where(leg2 == m2, r64, N_EXPERTS), axis=0, keepdims=True)
        ex = jnp.exp(m2 - m1)
        eid_ref[0:1, cols] = i1
        eid_ref[1:2, cols] = i2
        wts_ref[0:1, cols] = p_grp / (1.0 + ex)
        wts_ref[1:2, cols] = p_grp * ex / (1.0 + ex)
        return carry

    lax.fori_loop(0, nc, pick, 0)

    a_row = lax.broadcasted_iota(I32, (ROUTE_CHUNK, ROUTE_CHUNK), 0)
    a_col = lax.broadcasted_iota(I32, (ROUTE_CHUNK, ROUTE_CHUNK), 1)
    before = (a_row < a_col).astype(BF16)

    def count(n, carry):
        j = n // nc
        c = n - j * nc
        cols = pl.ds(pl.multiple_of(c * ROUTE_CHUNK, ROUTE_CHUNK), ROUTE_CHUNK)
        oh = r64 == eid_ref[pl.ds(j, 1), cols]
        ohf = oh.astype(F32)
        pref = _dot(ohf.astype(BF16), before) + carry
        rank_ref[pl.ds(j, 1), cols] = jnp.sum(jnp.where(oh, pref, 0.0), axis=0, keepdims=True)
        return carry + jnp.sum(ohf, axis=1, keepdims=True)

    counts = lax.fori_loop(0, TOP_K * nc, count, jnp.zeros((N_EXPERTS, 1), F32))

    padded = jnp.floor((counts + (MOE_BLOCK - 1)) * (1.0 / MOE_BLOCK)) * MOE_BLOCK
    e_row = lax.broadcasted_iota(I32, (N_EXPERTS, N_EXPERTS), 0)
    e_col = lax.broadcasted_iota(I32, (N_EXPERTS, N_EXPERTS), 1)
    incl = (e_col <= e_row).astype(F32)
    pad_end = _dot(incl, jnp.broadcast_to(padded, (N_EXPERTS, LANES)), precision=HIGHEST)[:, 0:1]
    pad_start = pad_end - padded

    def place(n, carry):
        j = n // nc
        c = n - j * nc
        cols = pl.ds(pl.multiple_of(c * ROUTE_CHUNK, ROUTE_CHUNK), ROUTE_CHUNK)
        oh = r64 == eid_ref[pl.ds(j, 1), cols]
        start = jnp.sum(jnp.where(oh, pad_start, 0.0), axis=0, keepdims=True)
        dest_ref[pl.ds(j, 1), cols] = (start + rank_ref[pl.ds(j, 1), cols]).astype(I32)
        return carry

    lax.fori_loop(0, TOP_K * nc, place, 0)

    b0 = (lax.broadcasted_iota(I32, (N_EXPERTS, bexp_ref.shape[1]), 1) * MOE_BLOCK).astype(F32)
    n_done = jnp.sum((pad_end <= b0).astype(F32), axis=0, keepdims=True)
    bexp_ref[...] = jnp.minimum(n_done, float(N_EXPERTS - 1)).astype(I32)
    live = jnp.minimum(pad_start + counts, b0 + MOE_BLOCK) - jnp.maximum(pad_start, b0)
    nvalid_ref[...] = jnp.sum(jnp.maximum(live, 0.0), axis=0, keepdims=True).astype(I32)


def _route(logits_t, n_blocks_pad):
    t = logits_t.shape[1]
    return pl.pallas_call(
        _route_kernel,
        out_shape=[
            jax.ShapeDtypeStruct((TOP_K, t), I32),
            jax.ShapeDtypeStruct((TOP_K, t), F32),
            jax.ShapeDtypeStruct((1, n_blocks_pad), I32),
            jax.ShapeDtypeStruct((1, n_blocks_pad), I32),
        ],
        scratch_shapes=[pltpu.VMEM((TOP_K, t), I32), pltpu.VMEM((TOP_K, t), F32)],
        compiler_params=pltpu.CompilerParams(vmem_limit_bytes=VMEM_LIMIT),
        name="route",
    )(logits_t)


def _invmap_kernel(dest_ref, rt_ref):
    n_rows = rt_ref.shape[0]
    t = dest_ref.shape[0] // TOP_K

    def clear(r, c):
        rt_ref[r] = 0
        return c

    lax.fori_loop(0, n_rows, clear, 0, unroll=16)

    def put(tok, c):
        rt_ref[dest_ref[tok]] = tok
        rt_ref[dest_ref[t + tok]] = tok
        return c

    lax.fori_loop(0, t, put, 0, unroll=8)


def _invmap(dest_flat, n_rows):
    return pl.pallas_call(
        _invmap_kernel,
        in_specs=[pl.BlockSpec(memory_space=pltpu.SMEM)],
        out_specs=pl.BlockSpec(memory_space=pltpu.SMEM),
        out_shape=jax.ShapeDtypeStruct((n_rows,), I32),
        name="invmap",
    )(dest_flat)


SUBLANES = 8


def _row_copy(src_hbm, src_row, dst_buf, slot, tile, sub, sem):
    return pltpu.make_async_copy(src_hbm.at[src_row >> 3, pl.ds(src_row & (SUBLANES - 1), 1), :],
                                 dst_buf.at[slot, tile, pl.ds(sub, 1), :], sem.at[slot])


def _tiles_copy(src_hbm, dst_buf, slot, tiles, sem):
    return pltpu.make_async_copy(src_hbm.at[pl.ds(0, tiles)], dst_buf.at[slot, pl.ds(0, tiles)], sem.at[slot])


def _expert_kernel(bexp_ref, nvalid_ref, rt_ref, hp_hbm, wg_hbm, wu_hbm, wd_hbm, ys_ref,
                   xbuf, wf_g, wf_u, wf_d, wb_g, wb_u, wb_d, ord_ref, xsem, wsem):
    b = pl.program_id(0)
    nb = pl.num_programs(0) - 1

    def live_tiles(blk):
        return jnp.where(nvalid_ref[blk] > 0, MOE_BLOCK // SUBLANES, 0)

    def expert_of(blk):
        return bexp_ref[jnp.minimum(blk, nb - 1)]

    def next_owner(blk, e):
        return lax.while_loop(lambda j: (j < nb) & (expert_of(j) == e), lambda j: j + 1, blk)

    def weight_copies(e, slot):
        return (pltpu.make_async_copy(wg_hbm.at[e], wf_g.at[slot], wsem.at[slot, 0]),
                pltpu.make_async_copy(wu_hbm.at[e], wf_u.at[slot], wsem.at[slot, 1]),
                pltpu.make_async_copy(wd_hbm.at[e], wf_d.at[slot], wsem.at[slot, 2]))

    @pl.when(b == 0)
    def _():
        ord_ref[0] = 0
        e0 = bexp_ref[0]
        for c in weight_copies(e0, 0):
            c.start(priority=1)
        n1 = next_owner(1, e0)

        @pl.when(n1 < nb)
        def _():
            for c in weight_copies(expert_of(n1), 1):
                c.start(priority=1)

    @pl.when(b < nb)
    def _():
        base = b * MOE_BLOCK
        slot = b % 2

        def body(i, c):
            for s in range(SUBLANES):
                _row_copy(hp_hbm, rt_ref[base + i * SUBLANES + s], xbuf, slot, i, s, xsem).start()
            return c

        lax.fori_loop(0, live_tiles(jnp.minimum(b, nb - 1)), body, 0)

    @pl.when(b > 0)
    def _():
        blk = b - 1
        e = bexp_ref[blk]
        first = (blk == 0) | (e != bexp_ref[jnp.maximum(blk - 1, 0)])

        @pl.when(first)
        def _():
            n = ord_ref[0]
            wslot = n % 2
            ord_ref[0] = n + 1
            for c in weight_copies(e, wslot):
                c.wait()
            wb_g[...] = wf_g[wslot].astype(BF16)
            wb_u[...] = wf_u[wslot].astype(BF16)
            wb_d[...] = wf_d[wslot].astype(BF16)
            n1 = next_owner(blk + 1, e)
            n2 = next_owner(n1 + 1, expert_of(n1))

            @pl.when((n1 < nb) & (n2 < nb))
            def _():
                for c in weight_copies(expert_of(n2), wslot):
                    c.start(priority=1)

        slot = blk % 2
        tiles = live_tiles(blk)

        @pl.when(tiles > 0)
        def _():
            _tiles_copy(hp_hbm, xbuf, slot, tiles, xsem).wait()
            xw = xbuf[slot].reshape(MOE_BLOCK, HALF)
            lo = _unpack_half(xw, 0).astype(BF16)
            hi = _unpack_half(xw, 1).astype(BF16)
            g = _dot(lo, wb_g[:HALF, :]) + _dot(hi, wb_g[HALF:, :])
            u = _dot(lo, wb_u[:HALF, :]) + _dot(hi, wb_u[HALF:, :])
            h = (g * _sigmoid(g) * u).astype(BF16)
            y = _dot(h, wb_d[...])
            ys_ref[...] = _pack_halves(y[:, :HALF], y[:, HALF:]).reshape(ys_ref.shape)

        @pl.when(tiles == 0)
        def _():
            zero = jnp.zeros((MOE_BLOCK, HALF), F32)
            ys_ref[...] = _pack_halves(zero, zero).reshape(ys_ref.shape)


def _experts(bexp, nvalid, row_tok, hn_packed, w_gate, w_up, w_down, n_blocks):
    grid_spec = pltpu.PrefetchScalarGridSpec(
        num_scalar_prefetch=3,
        grid=(n_blocks + 1,),
        in_specs=[pl.BlockSpec(memory_space=pl.ANY)] * 4,
        out_specs=pl.BlockSpec((MOE_BLOCK // SUBLANES, SUBLANES, HALF),
                               lambda b, be, nv, rt: (jnp.maximum(b - 1, 0), 0, 0)),
        scratch_shapes=[
            pltpu.VMEM((2, MOE_BLOCK // SUBLANES, SUBLANES, HALF), jnp.uint32),
            pltpu.VMEM((2, D_MODEL, D_EXPERT), F32), pltpu.VMEM((2, D_MODEL, D_EXPERT), F32),
            pltpu.VMEM((2, D_EXPERT, D_MODEL), F32),
            pltpu.VMEM((D_MODEL, D_EXPERT), BF16), pltpu.VMEM((D_MODEL, D_EXPERT), BF16),
            pltpu.VMEM((D_EXPERT, D_MODEL), BF16),
            pltpu.SMEM((1,), I32),
            pltpu.SemaphoreType.DMA((2,)), pltpu.SemaphoreType.DMA((2, 3)),
        ],
    )
    return pl.pallas_call(
        _expert_kernel,
        grid_spec=grid_spec,
        out_shape=jax.ShapeDtypeStruct((n_blocks * MOE_BLOCK // SUBLANES, SUBLANES, HALF), jnp.uint32),
        compiler_params=_params(("arbitrary",)),
        name="experts",
    )(bexp, nvalid, row_tok, hn_packed.reshape(-1, SUBLANES, HALF), w_gate, w_up, w_down)


def _combine_kernel(dest_ref, ys_hbm, h_ref, w_ref, g_ref, o_ref, ybuf, sem):
    i = pl.program_id(0)
    n = pl.num_programs(0)
    tq = h_ref.shape[0]
    t = n * tq

    tiles = tq // SUBLANES

    def start_gather(blk, slot):
        base = blk * tq

        def body(j, c):
            for s in range(SUBLANES):
                r = base + j * SUBLANES + s
                _row_copy(ys_hbm, dest_ref[r], ybuf, slot, j, s, sem).start()
                _row_copy(ys_hbm, dest_ref[t + r], ybuf, slot, tiles + j, s, sem).start()
            return c

        lax.fori_loop(0, tiles, body, 0)

    @pl.when(i == 0)
    def _():
        start_gather(0, 0)

    @pl.when(i + 1 < n)
    def _():
        start_gather(i + 1, (i + 1) % 2)

    slot = i % 2
    _tiles_copy(ys_hbm, ybuf, slot, TOP_K * tiles, sem).wait()
    y0 = ybuf[slot, 0:tiles].reshape(tq, HALF)
    y1 = ybuf[slot, tiles:2 * tiles].reshape(tq, HALF)
    w = w_ref[...]
    w0 = w[:, 0:1]
    w1 = w[:, 1:2]
    h = h_ref[...]
    lo = h[:, :HALF] + w0 * _unpack_half(y0, 0) + w1 * _unpack_half(y1, 0)
    hi = h[:, HALF:] + w0 * _unpack_half(y0, 1) + w1 * _unpack_half(y1, 1)
    ms = (jnp.sum(lo * lo, axis=-1, keepdims=True) + jnp.sum(hi * hi, axis=-1, keepdims=True)) * (1.0 / D_MODEL)
    inv = lax.rsqrt(ms + EPS)
    g = g_ref[...]
    o_ref[:, :HALF] = lo * inv * g[:, :HALF]
    o_ref[:, HALF:] = hi * inv * g[:, HALF:]


def _combine(dest_flat, ys_packed, h1, wts_tok, g_final, tq=256):
    t = h1.shape[0]
    grid_spec = pltpu.PrefetchScalarGridSpec(
        num_scalar_prefetch=1,
        grid=(t // tq,),
        in_specs=[
            pl.BlockSpec(memory_space=pl.ANY),
            pl.BlockSpec((tq, D_MODEL), lambda i, d: (i, 0)),
            pl.BlockSpec((tq, TOP_K), lambda i, d: (i, 0)),
            pl.BlockSpec((1, D_MODEL), lambda i, d: (0, 0)),
        ],
        out_specs=pl.BlockSpec((tq, D_MODEL), lambda i, d: (i, 0)),
        scratch_shapes=[pltpu.VMEM((2, TOP_K * tq // SUBLANES, SUBLANES, HALF), jnp.uint32),
                        pltpu.SemaphoreType.DMA((2,))],
    )
    return pl.pallas_call(
        _combine_kernel,
        grid_spec=grid_spec,
        out_shape=jax.ShapeDtypeStruct((t, D_MODEL), F32),
        compiler_params=_params(("arbitrary",)),
        name="combine",
    )(dest_flat, ys_packed, h1, wts_tok, g_final)


def _mixers(h, positions, norm_g, w_in, b_in, a_re, a_im, log_dt, b_re, b_im, c_re, c_im, d_skip,
            w_glu, b_glu, w_br_ssm, sinks, w_br_attn):
    t = h.shape[0]
    ngate = 2 * D_MODEL
    g = norm_g.reshape(1, D_MODEL)
    w_main = jnp.concatenate([w_in[:, IN_WIDTH - ngate:], w_in[:, SSM_WIDTH:IN_WIDTH - ngate]], axis=1).astype(BF16)
    b_main = jnp.concatenate([b_in[IN_WIDTH - ngate:], b_in[SSM_WIDTH:IN_WIDTH - ngate]]).reshape(1, -1)
    proj = _inproj(h, g, w_main, b_main, chunk_major=False)
    u3 = _inproj(h, g, w_in[:, :SSM_WIDTH].astype(BF16), b_in[:SSM_WIDTH].reshape(1, -1), chunk_major=True)

    tmat, wsr, wsi, rxr, rxi, a16r, a16i = _ssm_prep(
        a_re, a_im, log_dt, b_re.transpose(0, 2, 1), b_im.transpose(0, 2, 1), c_re, c_im)
    d_tiled = jnp.tile(d_skip, (1, CHUNK)).reshape(SSM_GROUPS, 1, CW)
    z3 = _ssm(u3, tmat, wsr, wsi, rxr, rxi, a16r.reshape(SSM_GROUPS, SSM_STATE),
              a16i.reshape(SSM_GROUPS, SSM_STATE), d_tiled)

    ysg = _glu(z3, w_glu.astype(BF16), b_glu.reshape(1, -1), w_br_ssm.astype(BF16), proj)
    return _attn(proj, positions.reshape(1, t), sinks.reshape(1, N_Q_HEADS), w_br_attn.astype(BF16), ysg)


def _moe_tail(h, mixed, w_o, norm_ffn_g, w_rg, b_rg, w_re, b_re, w_gate, w_up, w_down, norm_final_g):
    t = h.shape[0]
    n_route = N_GROUPS + N_EXPERTS
    w_router = jnp.concatenate([w_rg, w_re, jnp.zeros((D_MODEL, LANES - n_route), F32)], axis=1)
    b_router = jnp.concatenate([b_rg, b_re, jnp.zeros((LANES - n_route,), F32)]).reshape(1, LANES)
    h1, hn_packed, logits_t = _oproj(h, mixed, w_o.astype(BF16), norm_ffn_g.reshape(1, D_MODEL),
                                     w_router, b_router)
    n_assign = t * TOP_K
    n_blocks = -(-(n_assign + N_EXPERTS * (MOE_BLOCK - 1)) // MOE_BLOCK)
    n_blocks_pad = -(-n_blocks // LANES) * LANES
    dest, wts, bexp, nvalid = _route(logits_t, n_blocks_pad)
    dest_flat = dest.reshape(n_assign)
    row_tok = _invmap(dest_flat, n_blocks * MOE_BLOCK)
    ys_packed = _experts(bexp[0, :n_blocks], nvalid[0, :n_blocks], row_tok, hn_packed, w_gate, w_up, w_down,
                         n_blocks)
    return _combine(dest_flat, ys_packed, h1, wts.T, norm_final_g.reshape(1, D_MODEL))


def kernel(x, positions, norm_mix_g, w_in, b_in, ssm_a_re, ssm_a_im, ssm_log_dt, ssm_b_re, ssm_b_im, ssm_c_re, ssm_c_im, ssm_d, w_glu, b_glu, w_br_ssm, attn_sinks, w_br_attn, w_o, norm_ffn_g, w_router_group, b_router_group, w_router_expert, b_router_expert, w_exp_gate, w_exp_up, w_exp_down, norm_final_g):
    bsz, seq, d = x.shape
    assert bsz == 1 and d == D_MODEL and norm_mix_g.shape[0] == 1
    h = x.reshape(seq, d)
    mixed = _mixers(h, positions, norm_mix_g[0], w_in[0], b_in[0], ssm_a_re[0], ssm_a_im[0], ssm_log_dt[0],
                    ssm_b_re[0], ssm_b_im[0], ssm_c_re[0], ssm_c_im[0], ssm_d[0], w_glu[0], b_glu[0],
                    w_br_ssm[0], attn_sinks[0], w_br_attn[0])
    out = _moe_tail(h, mixed, w_o[0], norm_ffn_g[0], w_router_group[0], b_router_group[0],
                    w_router_expert[0], b_router_expert[0], w_exp_gate[0], w_exp_up[0], w_exp_down[0],
                    norm_final_g)
    return out.reshape(bsz, seq, d)
```

```python
import functools
import math

import numpy as np
import jax
import jax.numpy as jnp
from jax import lax
from jax.experimental import pallas as pl
from jax.experimental.pallas import tpu as pltpu

F32 = jnp.float32
BF16 = jnp.bfloat16
I32 = jnp.int32

D_MODEL = 2048
SSM_WIDTH = 1024
SSM_GROUP = 16
SSM_GROUPS = 64
SSM_STATE = 64
HEAD_DIM = 64
N_Q_HEADS = 16
N_KV_HEADS = 4
Q_PER_KV = 4
WINDOW = 128
ROPE_DIM = 16
ROPE_THETA = 500000.0
Q_WIDTH = 1024
KV_WIDTH = 256
IN_WIDTH = SSM_WIDTH + Q_WIDTH + 2 * KV_WIDTH + 2 * D_MODEL
N_GROUPS = 8
EXPERTS_PER_GROUP = 8
N_EXPERTS = 64
TOP_K = 2
D_EXPERT = 512
MOE_BLOCK = 128
EPS = 1e-6

CHUNK = 16
CW = CHUNK * SSM_GROUP
GROUP_BLOCK = 8
HALF = D_MODEL // 2
LANES = 128
VMEM_LIMIT = 56 * 1024 * 1024

COL_G0, COL_G1, COL_Q, COL_K, COL_V = 0, 2048, 4096, 5120, 5376

HIGHEST = lax.Precision.HIGHEST


def _dot(a, b, precision=None):
    return jnp.dot(a, b, preferred_element_type=F32, precision=precision)


def _dot_nt(a, b, precision=None):
    return lax.dot_general(a, b, (((1,), (1,)), ((), ())), preferred_element_type=F32,
                           precision=precision)


def _sigmoid(x):
    return 1.0 / (1.0 + jnp.exp(-x))


def _pack_halves(lo, hi):
    return pltpu.pack_elementwise([lo, hi], packed_dtype=BF16)


def _unpack_half(w, index):
    return pltpu.unpack_elementwise(w, index=index, packed_dtype=BF16, unpacked_dtype=F32)


SUBLANES = 8
WORD_TILES = HALF // LANES
assert WORD_TILES == SUBLANES


def _store_token_tiles(ref, x):
    rows = x.shape[0]
    for s in range(WORD_TILES):
        ref[pl.ds(s, rows, stride=SUBLANES), :] = _pack_halves(x[:, s * LANES:(s + 1) * LANES],
                                                               x[:, HALF + s * LANES:HALF + (s + 1) * LANES])


def _load_token_tiles(ref, slot, first_row, rows):
    pieces = [ref[slot, pl.ds(first_row * SUBLANES + s, rows, stride=SUBLANES), :] for s in range(WORD_TILES)]
    return (jnp.concatenate([_unpack_half(p, 0) for p in pieces], axis=1),
            jnp.concatenate([_unpack_half(p, 1) for p in pieces], axis=1))


def _params(sem, vmem=VMEM_LIMIT):
    return pltpu.CompilerParams(dimension_semantics=sem, vmem_limit_bytes=vmem)


def _inproj_kernel(x_ref, g_ref, w_ref, b_ref, o_ref, xn_ref, *rest, chunk_major):
    @pl.when(pl.program_id(1) == 0)
    def _():
        x = x_ref[...]
        ms = jnp.mean(x * x, axis=-1, keepdims=True)
        xn_ref[...] = (x * lax.rsqrt(ms + EPS) * g_ref[...]).astype(BF16)

    acc = _dot(xn_ref[...], w_ref[...]) + b_ref[...]
    if not chunk_major:
        o_ref[...] = acc.astype(o_ref.dtype)
        return
    (acc_ref,) = rest
    nk = o_ref.shape[1]
    for c in range(acc_ref.shape[0]):
        acc_ref[c] = acc[:, c * LANES:(c + 1) * LANES]
    for s in range(CHUNK):
        for c in range(acc_ref.shape[0]):
            o_ref[s, :, c * LANES:(c + 1) * LANES] = acc_ref[c, pl.ds(s, nk, stride=CHUNK), :].astype(o_ref.dtype)


def _inproj(x, g, w_bf16, b, *, chunk_major, tm=1024, tn=512):
    t, d = x.shape
    n = w_bf16.shape[1]
    scratch = [pltpu.VMEM((tm, d), BF16)]
    if chunk_major:
        out_spec = pl.BlockSpec((CHUNK, tm // CHUNK, tn), lambda i, j: (0, i, j))
        out_shape = jax.ShapeDtypeStruct((CHUNK, t // CHUNK, n), BF16)
        scratch.append(pltpu.VMEM((tn // LANES, tm, LANES), F32))
    else:
        out_spec = pl.BlockSpec((tm, tn), lambda i, j: (i, j))
        out_shape = jax.ShapeDtypeStruct((t, n), BF16)
    return pl.pallas_call(
        functools.partial(_inproj_kernel, chunk_major=chunk_major),
        grid=(t // tm, n // tn),
        in_specs=[
            pl.BlockSpec((tm, d), lambda i, j: (i, 0)),
            pl.BlockSpec((1, d), lambda i, j: (0, 0)),
            pl.BlockSpec((d, tn), lambda i, j: (0, j)),
            pl.BlockSpec((1, tn), lambda i, j: (0, j)),
        ],
        out_specs=out_spec,
        out_shape=out_shape,
        scratch_shapes=scratch,
        compiler_params=_params(("arbitrary", "arbitrary")),
        name="inproj_u" if chunk_major else "inproj",
    )(x, g, w_bf16, b)


def _ssm_prep_kernel(are_ref, aim_ref, ldt_ref, btr_ref, bti_ref, cr_ref, ci_ref,
                     t_ref, wsr_ref, wsi_ref, rxr_ref, rxi_ref, a16r_ref, a16i_ref):
    lam_re = jnp.minimum(are_ref[0], -1e-4)
    lam_im = aim_ref[0]
    dt = jnp.exp(ldt_ref[0])
    lr_dt = lam_re * dt
    th = lam_im * dt
    mag = jnp.exp(lr_dt)
    ab_re = mag * jnp.cos(th)
    ab_im = mag * jnp.sin(th)
    den = lam_re * lam_re + lam_im * lam_im
    nr = ab_re - 1.0
    ni = ab_im
    coef_re = (nr * lam_re + ni * lam_im) / den
    coef_im = (ni * lam_re - nr * lam_im) / den
    btr = btr_ref[0]
    bti = bti_ref[0]
    bb_re = coef_re * btr - coef_im * bti
    bb_im = coef_re * bti + coef_im * btr

    e = lax.broadcasted_iota(I32, (CHUNK, SSM_STATE), 0).astype(F32)
    pmag = jnp.exp(e * lr_dt)
    pos_re = pmag * jnp.cos(e * th)
    pos_im = pmag * jnp.sin(e * th)
    nmag = jnp.exp(-e * lr_dt)
    neg_re = nmag * jnp.cos(e * th)
    neg_im = -nmag * jnp.sin(e * th)

    def rep(tab):
        return jnp.broadcast_to(tab[:, None, :], (CHUNK, SSM_GROUP, SSM_STATE)).reshape(CW, SSM_STATE)

    def tile(mat):
        return jnp.broadcast_to(mat[None, :, :], (CHUNK, SSM_GROUP, SSM_STATE)).reshape(CW, SSM_STATE)

    pr, pi = rep(pos_re), rep(pos_im)
    ctr, cti = tile(cr_ref[0]), tile(ci_ref[0])
    r_re = ctr * pr - cti * pi
    r_im = ctr * pi + cti * pr
    qr, qi = rep(neg_re), rep(neg_im)
    btr_t, bti_t = tile(bb_re), tile(bb_im)
    l_re = btr_t * qr - bti_t * qi
    l_im = btr_t * qi + bti_t * qr

    tm = _dot_nt(l_re, r_re, precision=HIGHEST) - _dot_nt(l_im, r_im, precision=HIGHEST)
    srow = lax.broadcasted_iota(I32, (CW, CW), 0) // SSM_GROUP
    tcol = lax.broadcasted_iota(I32, (CW, CW), 1) // SSM_GROUP
    t_ref[0] = jnp.where(tcol >= srow, tm, 0.0).astype(BF16)

    a15r = pos_re[CHUNK - 1:CHUNK, :]
    a15i = pos_im[CHUNK - 1:CHUNK, :]
    wsr_ref[0] = (l_re * a15r - l_im * a15i).astype(BF16)
    wsi_ref[0] = (l_re * a15i + l_im * a15r).astype(BF16)
    rxr_ref[0] = (r_re * ab_re - r_im * ab_im).astype(BF16)
    rxi_ref[0] = (-(r_re * ab_im + r_im * ab_re)).astype(BF16)
    m16 = jnp.exp(float(CHUNK) * lr_dt)
    a16r_ref[0] = m16 * jnp.cos(float(CHUNK) * th)
    a16i_ref[0] = m16 * jnp.sin(float(CHUNK) * th)


def _ssm_prep(a_re, a_im, log_dt, bt_re, bt_im, c_re, c_im):
    g = a_re.shape[0]
    vec = pl.BlockSpec((1, 1, SSM_STATE), lambda i: (i, 0, 0))
    mat = pl.BlockSpec((1, SSM_GROUP, SSM_STATE), lambda i: (i, 0, 0))
    wide = pl.BlockSpec((1, CW, SSM_STATE), lambda i: (i, 0, 0))
    return pl.pallas_call(
        _ssm_prep_kernel,
        grid=(g,),
        in_specs=[vec, vec, pl.BlockSpec((1, 1, 1), lambda i: (i, 0, 0)), mat, mat, mat, mat],
        out_specs=[pl.BlockSpec((1, CW, CW), lambda i: (i, 0, 0)), wide, wide, wide, wide, vec, vec],
        out_shape=[
            jax.ShapeDtypeStruct((g, CW, CW), BF16),
            jax.ShapeDtypeStruct((g, CW, SSM_STATE), BF16),
            jax.ShapeDtypeStruct((g, CW, SSM_STATE), BF16),
            jax.ShapeDtypeStruct((g, CW, SSM_STATE), BF16),
            jax.ShapeDtypeStruct((g, CW, SSM_STATE), BF16),
            jax.ShapeDtypeStruct((g, 1, SSM_STATE), F32),
            jax.ShapeDtypeStruct((g, 1, SSM_STATE), F32),
        ],
        compiler_params=_params(("arbitrary",)),
        name="ssm_prep",
    )(a_re.reshape(g, 1, SSM_STATE), a_im.reshape(g, 1, SSM_STATE), log_dt.reshape(g, 1, 1),
      bt_re, bt_im, c_re, c_im)


def _gelu_tanh(x):
    c = math.sqrt(2.0 / math.pi)
    return x * (0.5 * (1.0 + jnp.tanh(c * (x + 0.044715 * (x * x * x)))))


def _ssm_kernel(u_ref, t_ref, wsr_ref, wsi_ref, rxr_ref, rxi_ref, a16r_ref, a16i_ref, d_ref,
                z_ref, sr_ref, si_ref, ug_ref, zg_ref):
    nk = u_ref.shape[1]
    for j in range(GROUP_BLOCK):
        ug_ref[j] = jnp.concatenate(
            [u_ref[s, :, j * SSM_GROUP:(j + 1) * SSM_GROUP] for s in range(CHUNK)], axis=1)
    u_ref = ug_ref
    for j in range(GROUP_BLOCK):
        u = u_ref[j]
        sr_ref[j * nk:(j + 1) * nk, :] = _dot(u, wsr_ref[j])
        si_ref[j * nk:(j + 1) * nk, :] = _dot(u, wsi_ref[j])

    ar = a16r_ref[...]
    ai = a16i_ref[...]

    def step(k, carry):
        xr, xi = carry
        rows = pl.ds(k, GROUP_BLOCK, stride=nk)
        sr = sr_ref[rows, :]
        si = si_ref[rows, :]
        sr_ref[rows, :] = xr
        si_ref[rows, :] = xi
        return (ar * xr - ai * xi + sr, ar * xi + ai * xr + si)

    zero = jnp.zeros((GROUP_BLOCK, SSM_STATE), F32)
    lax.fori_loop(0, nk, step, (zero, zero))

    for j in range(GROUP_BLOCK):
        u = u_ref[j]
        xr = sr_ref[j * nk:(j + 1) * nk, :].astype(BF16)
        xi = si_ref[j * nk:(j + 1) * nk, :].astype(BF16)
        y = (_dot(u, t_ref[j]) + _dot_nt(xr, rxr_ref[j]) + _dot_nt(xi, rxi_ref[j])
             + d_ref[j] * u.astype(F32))
        zg_ref[j] = _gelu_tanh(y).astype(zg_ref.dtype)

    for t in range(CHUNK):
        z_ref[t] = jnp.concatenate(
            [zg_ref[j, :, t * SSM_GROUP:(t + 1) * SSM_GROUP] for j in range(GROUP_BLOCK)], axis=1)


def _ssm(u3, tmat, wsr, wsi, rxr, rxi, a16r, a16i, d_tiled):
    _, nk, width = u3.shape
    gb = GROUP_BLOCK
    gl = gb * SSM_GROUP
    blk3 = lambda a, b: pl.BlockSpec((gb, a, b), lambda i: (i, 0, 0))
    io = pl.BlockSpec((CHUNK, nk, gl), lambda i: (0, 0, i))
    return pl.pallas_call(
        _ssm_kernel,
        grid=(width // gl,),
        in_specs=[io, blk3(CW, CW), blk3(CW, SSM_STATE), blk3(CW, SSM_STATE),
                  blk3(CW, SSM_STATE), blk3(CW, SSM_STATE),
                  pl.BlockSpec((gb, SSM_STATE), lambda i: (i, 0)),
                  pl.BlockSpec((gb, SSM_STATE), lambda i: (i, 0)),
                  blk3(1, CW)],
        out_specs=io,
        out_shape=jax.ShapeDtypeStruct(u3.shape, BF16),
        scratch_shapes=[pltpu.VMEM((gb * nk, SSM_STATE), F32), pltpu.VMEM((gb * nk, SSM_STATE), F32),
                        pltpu.VMEM((gb, nk, CW), BF16), pltpu.VMEM((gb, nk, CW), BF16)],
        compiler_params=_params(("arbitrary",)),
        name="ssm",
    )(u3, tmat, wsr, wsi, rxr, rxi, a16r, a16i, d_tiled)


def _glu_kernel(z_ref, perm_ref, wg_ref, bg_ref, wb_ref, g0_ref, o_ref):
    tm = o_ref.shape[0]
    z = _dot(perm_ref[...], z_ref[...].reshape(tm, SSM_WIDTH)).astype(BF16)
    h = _dot(z, wg_ref[...]) + bg_ref[...]
    ga = h[:, :SSM_WIDTH]
    gb = h[:, SSM_WIDTH:]
    a = (ga * _sigmoid(gb)).astype(BF16)
    y = _dot(a, wb_ref[...])
    o_ref[...] = (_sigmoid(g0_ref[...].astype(F32)) * y).astype(o_ref.dtype)


def _glu(z3, w_glu, b_glu, w_br, proj, tm=512):
    t = z3.shape[0] * z3.shape[1]
    nk = tm // CHUNK
    r = np.arange(tm)
    perm = np.zeros((tm, tm), np.float32)
    perm[r, (r % CHUNK) * nk + r // CHUNK] = 1.0
    return pl.pallas_call(
        _glu_kernel,
        grid=(t // tm,),
        in_specs=[
            pl.BlockSpec((CHUNK, nk, SSM_WIDTH), lambda i: (0, i, 0)),
            pl.BlockSpec((tm, tm), lambda i: (0, 0)),
            pl.BlockSpec((SSM_WIDTH, 2 * SSM_WIDTH), lambda i: (0, 0)),
            pl.BlockSpec((1, 2 * SSM_WIDTH), lambda i: (0, 0)),
            pl.BlockSpec((SSM_WIDTH, D_MODEL), lambda i: (0, 0)),
            pl.BlockSpec((tm, D_MODEL), lambda i: (i, COL_G0 // D_MODEL)),
        ],
        out_specs=pl.BlockSpec((tm, D_MODEL), lambda i: (i, 0)),
        out_shape=jax.ShapeDtypeStruct((t, D_MODEL), BF16),
        compiler_params=_params(("arbitrary",)),
        name="glu",
    )(z3, jnp.asarray(perm, BF16), w_glu, b_glu, w_br, proj)


def _rope_pattern():
    half = ROPE_DIM // 2
    inv_freq = (np.float32(ROPE_THETA) ** (-np.arange(half, dtype=np.float32) / np.float32(half))).astype(np.float32)
    d = np.arange(LANES) % HEAD_DIM
    rotated = d < ROPE_DIM
    pat = np.zeros((16, LANES), np.float32)
    pat[:half] = rotated[None, :] & ((d % half)[None, :] == np.arange(half)[:, None])
    pat[8] = ~rotated
    pat[9] = np.where(d < half, -1.0, 0.0)
    pat[10] = np.where((d >= half) & rotated, 1.0, 0.0)
    return inv_freq.reshape(half, 1), pat


def _attn_kernel(q_ref, k_ref, v_ref, pos_ref, freq_ref, pat_ref, sink_ref, wbr_ref, ysg_ref, g1_ref,
                 o_ref, qbuf, kbuf, vbuf, obuf):
    i = pl.program_id(0)
    tq = q_ref.shape[0]
    nw = tq // WINDOW
    half = ROPE_DIM // 2

    @pl.when(i == 0)
    def _():
        kbuf[:, 0:WINDOW, :] = jnp.zeros((2 * N_KV_HEADS, WINDOW, LANES), BF16)
        vbuf[:, 0:WINDOW, :] = jnp.zeros((2 * N_KV_HEADS, WINDOW, LANES), BF16)

    ang = freq_ref[...] * pos_ref[...].astype(F32)
    spread = lambda tab: lax.dot_general(tab, pat_ref[0:8, :], (((0,), (0,)), ((), ())),
                                         preferred_element_type=F32, precision=HIGHEST)
    cs = spread(jnp.cos(ang)) + pat_ref[8:9, :]
    sn = spread(jnp.sin(ang))
    c_up = sn * pat_ref[9:10, :]
    c_dn = sn * pat_ref[10:11, :]

    def rope(x):
        return (x * cs + pltpu.roll(x, LANES - half, 1) * c_up + pltpu.roll(x, half, 1) * c_dn)

    low = lax.broadcasted_iota(I32, (tq, LANES), 1) < HEAD_DIM

    def split_heads(buf, cb, x):
        xs = pltpu.roll(x, HEAD_DIM, 1)
        zero = jnp.zeros_like(x)
        buf[4 * cb + 0, WINDOW:, :] = jnp.where(low, x, zero).astype(BF16)
        buf[4 * cb + 1, WINDOW:, :] = jnp.where(low, zero, xs).astype(BF16)
        buf[4 * cb + 2, WINDOW:, :] = jnp.where(low, xs, zero).astype(BF16)
        buf[4 * cb + 3, WINDOW:, :] = jnp.where(low, zero, x).astype(BF16)

    for cb in range(Q_WIDTH // LANES):
        sl = slice(cb * LANES, (cb + 1) * LANES)
        qbuf[:, sl] = rope(q_ref[:, sl].astype(F32)).astype(BF16)
    for cb in range(KV_WIDTH // LANES):
        sl = slice(cb * LANES, (cb + 1) * LANES)
        split_heads(kbuf, cb, rope(k_ref[:, sl].astype(F32)))
        split_heads(vbuf, cb, v_ref[:, sl].astype(F32))

    qi = lax.broadcasted_iota(I32, (WINDOW, 2 * WINDOW), 0)
    kj = lax.broadcasted_iota(I32, (WINDOW, 2 * WINDOW), 1)
    dist = qi + WINDOW - kj
    in_band = (dist >= 0) & (dist < WINDOW)
    cur_only = kj >= WINDOW
    sinks = sink_ref[...]
    scale = HEAD_DIM ** -0.5

    def window(w, carry):
        r0 = pl.multiple_of(w * WINDOW, WINDOW)
        rows = pl.ds(r0, 2 * WINDOW)
        not_first = (i * nw + w) > 0
        mask = in_band & (cur_only | not_first)

        def probs(qp, part, h):
            s = _dot_nt(qp, kbuf[2 * (h // Q_PER_KV) + part, rows, :]) * scale
            s = jnp.where(mask, s, -jnp.inf)
            sink = sinks[:, h:h + 1]
            m = jnp.maximum(jnp.max(s, axis=-1, keepdims=True), sink)
            p = jnp.exp(s - m)
            denom = jnp.sum(p, axis=-1, keepdims=True) + jnp.exp(sink - m)
            return p.astype(BF16), 1.0 / denom

        for a in range(N_Q_HEADS // 2):
            kv = (2 * a) // Q_PER_KV
            qp = qbuf[pl.ds(r0, WINDOW), a * LANES:(a + 1) * LANES]
            p_lo, r_lo = probs(qp, 0, 2 * a)
            p_hi, r_hi = probs(qp, 1, 2 * a + 1)
            o = _dot(p_lo, vbuf[2 * kv, rows, :]) * r_lo + _dot(p_hi, vbuf[2 * kv + 1, rows, :]) * r_hi
            obuf[pl.ds(r0, WINDOW), a * LANES:(a + 1) * LANES] = o.astype(BF16)
        return carry

    lax.fori_loop(0, nw, window, 0)

    kbuf[:, 0:WINDOW, :] = kbuf[:, tq:tq + WINDOW, :]
    vbuf[:, 0:WINDOW, :] = vbuf[:, tq:tq + WINDOW, :]

    y = _dot(obuf[...], wbr_ref[...])
    o_ref[...] = (ysg_ref[...].astype(F32) + _sigmoid(g1_ref[...].astype(F32)) * y).astype(o_ref.dtype)


def _attn(proj, pos_row, sinks, w_br, ysg, tq=512):
    t = proj.shape[0]
    freq, pat = (jnp.asarray(a) for a in _rope_pattern())
    return pl.pallas_call(
        _attn_kernel,
        grid=(t // tq,),
        in_specs=[
            pl.BlockSpec((tq, Q_WIDTH), lambda i: (i, COL_Q // Q_WIDTH)),
            pl.BlockSpec((tq, KV_WIDTH), lambda i: (i, COL_K // KV_WIDTH)),
            pl.BlockSpec((tq, KV_WIDTH), lambda i: (i, COL_V // KV_WIDTH)),
            pl.BlockSpec((1, tq), lambda i: (0, i)),
            pl.BlockSpec((ROPE_DIM // 2, 1), lambda i: (0, 0)),
            pl.BlockSpec((16, LANES), lambda i: (0, 0)),
            pl.BlockSpec((1, N_Q_HEADS), lambda i: (0, 0)),
            pl.BlockSpec((Q_WIDTH, D_MODEL), lambda i: (0, 0)),
            pl.BlockSpec((tq, D_MODEL), lambda i: (i, 0)),
            pl.BlockSpec((tq, D_MODEL), lambda i: (i, COL_G1 // D_MODEL)),
        ],
        out_specs=pl.BlockSpec((tq, D_MODEL), lambda i: (i, 0)),
        out_shape=jax.ShapeDtypeStruct((t, D_MODEL), BF16),
        scratch_shapes=[
            pltpu.VMEM((tq, Q_WIDTH), BF16),
            pltpu.VMEM((2 * N_KV_HEADS, tq + WINDOW, LANES), BF16),
            pltpu.VMEM((2 * N_KV_HEADS, tq + WINDOW, LANES), BF16),
            pltpu.VMEM((tq, Q_WIDTH), BF16),
        ],
        compiler_params=_params(("arbitrary",)),
        name="attn",
    )(proj, proj, proj, pos_row, freq, pat, sinks, w_br, ysg, proj)


def _oproj_kernel(x_ref, mix_ref, wo_ref, g_ref, wr_ref, br_ref, h_ref, hp_ref, lt_ref):
    h = x_ref[...] + _dot(mix_ref[...], wo_ref[...])
    h_ref[...] = h
    ms = jnp.mean(h * h, axis=-1, keepdims=True)
    hn = h * lax.rsqrt(ms + EPS) * g_ref[...]
    _store_token_tiles(hp_ref, hn)
    logits = _dot(hn, wr_ref[...], precision=HIGHEST) + br_ref[...]
    lt_ref[...] = logits.T


def _oproj(x, mixed, w_o, g, w_router, b_router, tm=512):
    t = x.shape[0]
    return pl.pallas_call(
        _oproj_kernel,
        grid=(t // tm,),
        in_specs=[
            pl.BlockSpec((tm, D_MODEL), lambda i: (i, 0)),
            pl.BlockSpec((tm, D_MODEL), lambda i: (i, 0)),
            pl.BlockSpec((D_MODEL, D_MODEL), lambda i: (0, 0)),
            pl.BlockSpec((1, D_MODEL), lambda i: (0, 0)),
            pl.BlockSpec((D_MODEL, LANES), lambda i: (0, 0)),
            pl.BlockSpec((1, LANES), lambda i: (0, 0)),
        ],
        out_specs=[
            pl.BlockSpec((tm, D_MODEL), lambda i: (i, 0)),
            pl.BlockSpec((tm * SUBLANES, LANES), lambda i: (i, 0)),
            pl.BlockSpec((LANES, tm), lambda i: (0, i)),
        ],
        out_shape=[
            jax.ShapeDtypeStruct((t, D_MODEL), F32),
            jax.ShapeDtypeStruct((t * SUBLANES, LANES), jnp.uint32),
            jax.ShapeDtypeStruct((LANES, t), F32),
        ],
        compiler_params=_params(("arbitrary",)),
        name="oproj",
    )(x, mixed, w_o, g, w_router, b_router)


ROUTE_CHUNK = 256


def _route_kernel(lt_ref, dest_ref, wts_ref, bexp_ref, nvalid_ref, eid_ref, rank_ref):
    t = lt_ref.shape[1]
    nc = t // ROUTE_CHUNK
    r8 = lax.broadcasted_iota(I32, (N_GROUPS, ROUTE_CHUNK), 0)
    r64 = lax.broadcasted_iota(I32, (N_EXPERTS, ROUTE_CHUNK), 0)

    def pick(c, carry):
        cols = pl.ds(pl.multiple_of(c * ROUTE_CHUNK, ROUTE_CHUNK), ROUTE_CHUNK)
        lg = lt_ref[0:N_GROUPS, cols]
        m = jnp.max(lg, axis=0, keepdims=True)
        ssum = jnp.sum(jnp.exp(lg - m), axis=0, keepdims=True)
        p_grp = 1.0 / ssum
        grp = jnp.min(jnp.where(lg == m, r8, N_GROUPS), axis=0, keepdims=True)
        le = lt_ref[N_GROUPS:N_GROUPS + N_EXPERTS, cols]
        leg = jnp.where((r64 // EXPERTS_PER_GROUP) == grp, le, -jnp.inf)
        m1 = jnp.max(leg, axis=0, keepdims=True)
        i1 = jnp.min(jnp.where(leg == m1, r64, N_EXPERTS), axis=0, keepdims=True)
        leg2 = jnp.where(r64 == i1, -jnp.inf, leg)
        m2 = jnp.max(leg2, axis=0, keepdims=True)
        i2 = jnp.min(jnp.where(leg2 == m2, r64, N_EXPERTS), axis=0, keepdims=True)
        ex = jnp.exp(m2 - m1)
        eid_ref[0:1, cols] = i1
        eid_ref[1:2, cols] = i2
        wts_ref[0:1, cols] = p_grp / (1.0 + ex)
        wts_ref[1:2, cols] = p_grp * ex / (1.0 + ex)
        return carry

    lax.fori_loop(0, nc, pick, 0)

    a_row = lax.broadcasted_iota(I32, (ROUTE_CHUNK, ROUTE_CHUNK), 0)
    a_col = lax.broadcasted_iota(I32, (ROUTE_CHUNK, ROUTE_CHUNK), 1)
    before = (a_row < a_col).astype(BF16)

    def count(n, carry):
        j = n // nc
        c = n - j * nc
        cols = pl.ds(pl.multiple_of(c * ROUTE_CHUNK, ROUTE_CHUNK), ROUTE_CHUNK)
        oh = r64 == eid_ref[pl.ds(j, 1), cols]
        ohf = oh.astype(F32)
        pref = _dot(ohf.astype(BF16), before) + carry
        rank_ref[pl.ds(j, 1), cols] = jnp.sum(jnp.where(oh, pref, 0.0), axis=0, keepdims=True)
        return carry + jnp.sum(ohf, axis=1, keepdims=True)

    counts = lax.fori_loop(0, TOP_K * nc, count, jnp.zeros((N_EXPERTS, 1), F32))

    padded = jnp.floor((counts + (MOE_BLOCK - 1)) * (1.0 / MOE_BLOCK)) * MOE_BLOCK
    e_row = lax.broadcasted_iota(I32, (N_EXPERTS, N_EXPERTS), 0)
    e_col = lax.broadcasted_iota(I32, (N_EXPERTS, N_EXPERTS), 1)
    incl = (e_col <= e_row).astype(F32)
    pad_end = _dot(incl, jnp.broadcast_to(padded, (N_EXPERTS, LANES)), precision=HIGHEST)[:, 0:1]
    pad_start = pad_end - padded

    def place(n, carry):
        j = n // nc
        c = n - j * nc
        cols = pl.ds(pl.multiple_of(c * ROUTE_CHUNK, ROUTE_CHUNK), ROUTE_CHUNK)
        oh = r64 == eid_ref[pl.ds(j, 1), cols]
        start = jnp.sum(jnp.where(oh, pad_start, 0.0), axis=0, keepdims=True)
        dest_ref[pl.ds(j, 1), cols] = (start + rank_ref[pl.ds(j, 1), cols]).astype(I32)
        return carry

    lax.fori_loop(0, TOP_K * nc, place, 0)

    b0 = (lax.broadcasted_iota(I32, (N_EXPERTS, bexp_ref.shape[1]), 1) * MOE_BLOCK).astype(F32)
    n_done = jnp.sum((pad_end <= b0).astype(F32), axis=0, keepdims=True)
    bexp_ref[...] = jnp.minimum(n_done, float(N_EXPERTS - 1)).astype(I32)
    live = jnp.minimum(pad_start + counts, b0 + MOE_BLOCK) - jnp.maximum(pad_start, b0)
    nvalid_ref[...] = jnp.sum(jnp.maximum(live, 0.0), axis=0, keepdims=True).astype(I32)


def _route(logits_t, n_blocks_pad):
    t = logits_t.shape[1]
    return pl.pallas_call(
        _route_kernel,
        out_shape=[
            jax.ShapeDtypeStruct((TOP_K, t), I32),
            jax.ShapeDtypeStruct((TOP_K, t), F32),
            jax.ShapeDtypeStruct((1, n_blocks_pad), I32),
            jax.ShapeDtypeStruct((1, n_blocks_pad), I32),
        ],
        scratch_shapes=[pltpu.VMEM((TOP_K, t), I32), pltpu.VMEM((TOP_K, t), F32)],
        compiler_params=pltpu.CompilerParams(vmem_limit_bytes=VMEM_LIMIT),
        name="route",
    )(logits_t)


def _invmap_kernel(dest_ref, rt_ref):
    n_rows = rt_ref.shape[0]
    t = dest_ref.shape[0] // TOP_K

    def clear(r, c):
        rt_ref[r] = 0
        return c

    lax.fori_loop(0, n_rows, clear, 0, unroll=16)

    def put(tok, c):
        rt_ref[dest_ref[tok]] = tok
        rt_ref[dest_ref[t + tok]] = tok
        return c

    lax.fori_loop(0, t, put, 0, unroll=8)


def _invmap(dest_flat, n_rows):
    return pl.pallas_call(
        _invmap_kernel,
        in_specs=[pl.BlockSpec(memory_space=pltpu.SMEM)],
        out_specs=pl.BlockSpec(memory_space=pltpu.SMEM),
        out_shape=jax.ShapeDtypeStruct((n_rows,), I32),
        name="invmap",
    )(dest_flat)


def _row_copy(src_hbm, src_row, dst_buf, slot, dst_row, sem):
    return pltpu.make_async_copy(src_hbm.at[pl.ds(pl.multiple_of(src_row * SUBLANES, SUBLANES), SUBLANES), :],
                                 dst_buf.at[slot, pl.ds(dst_row * SUBLANES, SUBLANES), :], sem.at[slot])


def _rows_copy(src_hbm, dst_buf, slot, rows, sem):
    n = rows * SUBLANES
    return pltpu.make_async_copy(src_hbm.at[pl.ds(0, n), :], dst_buf.at[slot, pl.ds(0, n), :], sem.at[slot])


def _expert_kernel(bexp_ref, nvalid_ref, rt_ref, hp_hbm, wg_hbm, wu_hbm, wd_hbm, ys_ref,
                   xbuf, wf_g, wf_u, wf_d, wb_g, wb_u, wb_d, ord_ref, xsem, wsem):
    b = pl.program_id(0)
    nb = pl.num_programs(0) - 1

    def live_rows(blk):
        return jnp.where(nvalid_ref[blk] > 0, MOE_BLOCK, 0)

    def expert_of(blk):
        return bexp_ref[jnp.minimum(blk, nb - 1)]

    def next_owner(blk, e):
        return lax.while_loop(lambda j: (j < nb) & (expert_of(j) == e), lambda j: j + 1, blk)

    def weight_copies(e, slot):
        return (pltpu.make_async_copy(wg_hbm.at[e], wf_g.at[slot], wsem.at[slot, 0]),
                pltpu.make_async_copy(wu_hbm.at[e], wf_u.at[slot], wsem.at[slot, 1]),
                pltpu.make_async_copy(wd_hbm.at[e], wf_d.at[slot], wsem.at[slot, 2]))

    @pl.when(b == 0)
    def _():
        ord_ref[0] = 0
        e0 = bexp_ref[0]
        for c in weight_copies(e0, 0):
            c.start(priority=1)
        n1 = next_owner(1, e0)

        @pl.when(n1 < nb)
        def _():
            for c in weight_copies(expert_of(n1), 1):
                c.start(priority=1)

    @pl.when(b < nb)
    def _():
        base = b * MOE_BLOCK
        slot = b % 2

        def body(i, c):
            for s in range(SUBLANES):
                r = i * SUBLANES + s
                _row_copy(hp_hbm, rt_ref[base + r], xbuf, slot, r, xsem).start()
            return c

        lax.fori_loop(0, live_rows(jnp.minimum(b, nb - 1)) // SUBLANES, body, 0)

    @pl.when(b > 0)
    def _():
        blk = b - 1
        e = bexp_ref[blk]
        first = (blk == 0) | (e != bexp_ref[jnp.maximum(blk - 1, 0)])

        @pl.when(first)
        def _():
            n = ord_ref[0]
            wslot = n % 2
            ord_ref[0] = n + 1
            for c in weight_copies(e, wslot):
                c.wait()
            wb_g[...] = wf_g[wslot].astype(BF16)
            wb_u[...] = wf_u[wslot].astype(BF16)
            wb_d[...] = wf_d[wslot].astype(BF16)
            n1 = next_owner(blk + 1, e)
            n2 = next_owner(n1 + 1, expert_of(n1))

            @pl.when((n1 < nb) & (n2 < nb))
            def _():
                for c in weight_copies(expert_of(n2), wslot):
                    c.start(priority=1)

        slot = blk % 2
        rows = live_rows(blk)

        @pl.when(rows > 0)
        def _():
            _rows_copy(hp_hbm, xbuf, slot, rows, xsem).wait()
            lo, hi = (v.astype(BF16) for v in _load_token_tiles(xbuf, slot, 0, MOE_BLOCK))
            g = _dot(lo, wb_g[:HALF, :]) + _dot(hi, wb_g[HALF:, :])
            u = _dot(lo, wb_u[:HALF, :]) + _dot(hi, wb_u[HALF:, :])
            h = (g * _sigmoid(g) * u).astype(BF16)
            y = _dot(h, wb_d[...])
            _store_token_tiles(ys_ref, y)

        @pl.when(rows == 0)
        def _():
            _store_token_tiles(ys_ref, jnp.zeros((MOE_BLOCK, D_MODEL), F32))


def _experts(bexp, nvalid, row_tok, hn_packed, w_gate, w_up, w_down, n_blocks):
    grid_spec = pltpu.PrefetchScalarGridSpec(
        num_scalar_prefetch=3,
        grid=(n_blocks + 1,),
        in_specs=[pl.BlockSpec(memory_space=pl.ANY)] * 4,
        out_specs=pl.BlockSpec((MOE_BLOCK * SUBLANES, LANES), lambda b, be, nv, rt: (jnp.maximum(b - 1, 0), 0)),
        scratch_shapes=[
            pltpu.VMEM((2, MOE_BLOCK * SUBLANES, LANES), jnp.uint32),
            pltpu.VMEM((2, D_MODEL, D_EXPERT), F32), pltpu.VMEM((2, D_MODEL, D_EXPERT), F32),
            pltpu.VMEM((2, D_EXPERT, D_MODEL), F32),
            pltpu.VMEM((D_MODEL, D_EXPERT), BF16), pltpu.VMEM((D_MODEL, D_EXPERT), BF16),
            pltpu.VMEM((D_EXPERT, D_MODEL), BF16),
            pltpu.SMEM((1,), I32),
            pltpu.SemaphoreType.DMA((2,)), pltpu.SemaphoreType.DMA((2, 3)),
        ],
    )
    return pl.pallas_call(
        _expert_kernel,
        grid_spec=grid_spec,
        out_shape=jax.ShapeDtypeStruct((n_blocks * MOE_BLOCK * SUBLANES, LANES), jnp.uint32),
        compiler_params=_params(("arbitrary",)),
        name="experts",
    )(bexp, nvalid, row_tok, hn_packed, w_gate, w_up, w_down)


def _combine_kernel(dest_ref, ys_hbm, h_ref, w_ref, g_ref, o_ref, ybuf, sem):
    i = pl.program_id(0)
    n = pl.num_programs(0)
    tq = h_ref.shape[0]
    t = n * tq

    def start_gather(blk, slot):
        base = blk * tq

        def body(j, c):
            for s in range(SUBLANES):
                r = j * SUBLANES + s
                _row_copy(ys_hbm, dest_ref[base + r], ybuf, slot, r, sem).start()
                _row_copy(ys_hbm, dest_ref[t + base + r], ybuf, slot, tq + r, sem).start()
            return c

        lax.fori_loop(0, tq // SUBLANES, body, 0)

    @pl.when(i == 0)
    def _():
        start_gather(0, 0)

    @pl.when(i + 1 < n)
    def _():
        start_gather(i + 1, (i + 1) % 2)

    slot = i % 2
    _rows_copy(ys_hbm, ybuf, slot, TOP_K * tq, sem).wait()
    y0_lo, y0_hi = _load_token_tiles(ybuf, slot, 0, tq)
    y1_lo, y1_hi = _load_token_tiles(ybuf, slot, tq, tq)
    w = w_ref[...]
    w0 = w[:, 0:1]
    w1 = w[:, 1:2]
    h = h_ref[...]
    lo = h[:, :HALF] + w0 * y0_lo + w1 * y1_lo
    hi = h[:, HALF:] + w0 * y0_hi + w1 * y1_hi
    ms = (jnp.sum(lo * lo, axis=-1, keepdims=True) + jnp.sum(hi * hi, axis=-1, keepdims=True)) * (1.0 / D_MODEL)
    inv = lax.rsqrt(ms + EPS)
    g = g_ref[...]
    o_ref[:, :HALF] = lo * inv * g[:, :HALF]
    o_ref[:, HALF:] = hi * inv * g[:, HALF:]


def _combine(dest_flat, ys_packed, h1, wts_tok, g_final, tq=256):
    t = h1.shape[0]
    grid_spec = pltpu.PrefetchScalarGridSpec(
        num_scalar_prefetch=1,
        grid=(t // tq,),
        in_specs=[
            pl.BlockSpec(memory_space=pl.ANY),
            pl.BlockSpec((tq, D_MODEL), lambda i, d: (i, 0)),
            pl.BlockSpec((tq, TOP_K), lambda i, d: (i, 0)),
            pl.BlockSpec((1, D_MODEL), lambda i, d: (0, 0)),
        ],
        out_specs=pl.BlockSpec((tq, D_MODEL), lambda i, d: (i, 0)),
        scratch_shapes=[pltpu.VMEM((2, TOP_K * tq * SUBLANES, LANES), jnp.uint32),
                        pltpu.SemaphoreType.DMA((2,))],
    )
    return pl.pallas_call(
        _combine_kernel,
        grid_spec=grid_spec,
        out_shape=jax.ShapeDtypeStruct((t, D_MODEL), F32),
        compiler_params=_params(("arbitrary",)),
        name="combine",
    )(dest_flat, ys_packed, h1, wts_tok, g_final)


def _mixers(h, positions, norm_g, w_in, b_in, a_re, a_im, log_dt, b_re, b_im, c_re, c_im, d_skip,
            w_glu, b_glu, w_br_ssm, sinks, w_br_attn):
    t = h.shape[0]
    ngate = 2 * D_MODEL
    g = norm_g.reshape(1, D_MODEL)
    w_main = jnp.concatenate([w_in[:, IN_WIDTH - ngate:], w_in[:, SSM_WIDTH:IN_WIDTH - ngate]], axis=1).astype(BF16)
    b_main = jnp.concatenate([b_in[IN_WIDTH - ngate:], b_in[SSM_WIDTH:IN_WIDTH - ngate]]).reshape(1, -1)
    proj = _inproj(h, g, w_main, b_main, chunk_major=False)
    u3 = _inproj(h, g, w_in[:, :SSM_WIDTH].astype(BF16), b_in[:SSM_WIDTH].reshape(1, -1), chunk_major=True)

    tmat, wsr, wsi, rxr, rxi, a16r, a16i = _ssm_prep(
        a_re, a_im, log_dt, b_re.transpose(0, 2, 1), b_im.transpose(0, 2, 1), c_re, c_im)
    d_tiled = jnp.tile(d_skip, (1, CHUNK)).reshape(SSM_GROUPS, 1, CW)
    z3 = _ssm(u3, tmat, wsr, wsi, rxr, rxi, a16r.reshape(SSM_GROUPS, SSM_STATE),
              a16i.reshape(SSM_GROUPS, SSM_STATE), d_tiled)

    ysg = _glu(z3, w_glu.astype(BF16), b_glu.reshape(1, -1), w_br_ssm.astype(BF16), proj)
    return _attn(proj, positions.reshape(1, t), sinks.reshape(1, N_Q_HEADS), w_br_attn.astype(BF16), ysg)


def _moe_tail(h, mixed, w_o, norm_ffn_g, w_rg, b_rg, w_re, b_re, w_gate, w_up, w_down, norm_final_g):
    t = h.shape[0]
    n_route = N_GROUPS + N_EXPERTS
    w_router = jnp.concatenate([w_rg, w_re, jnp.zeros((D_MODEL, LANES - n_route), F32)], axis=1)
    b_router = jnp.concatenate([b_rg, b_re, jnp.zeros((LANES - n_route,), F32)]).reshape(1, LANES)
    h1, hn_packed, logits_t = _oproj(h, mixed, w_o.astype(BF16), norm_ffn_g.reshape(1, D_MODEL),
                                     w_router, b_router)
    n_assign = t * TOP_K
    n_blocks = -(-(n_assign + N_EXPERTS * (MOE_BLOCK - 1)) // MOE_BLOCK)
    n_blocks_pad = -(-n_blocks // LANES) * LANES
    dest, wts, bexp, nvalid = _route(logits_t, n_blocks_pad)
    dest_flat = dest.reshape(n_assign)
    row_tok = _invmap(dest_flat, n_blocks * MOE_BLOCK)
    ys_packed = _experts(bexp[0, :n_blocks], nvalid[0, :n_blocks], row_tok, hn_packed, w_gate, w_up, w_down,
                         n_blocks)
    return _combine(dest_flat, ys_packed, h1, wts.T, norm_final_g.reshape(1, D_MODEL))


def kernel(x, positions, norm_mix_g, w_in, b_in, ssm_a_re, ssm_a_im, ssm_log_dt, ssm_b_re, ssm_b_im, ssm_c_re, ssm_c_im, ssm_d, w_glu, b_glu, w_br_ssm, attn_sinks, w_br_attn, w_o, norm_ffn_g, w_router_group, b_router_group, w_router_expert, b_router_expert, w_exp_gate, w_exp_up, w_exp_down, norm_final_g):
    bsz, seq, d = x.shape
    assert bsz == 1 and d == D_MODEL and norm_mix_g.shape[0] == 1
    h = x.reshape(seq, d)
    mixed = _mixers(h, positions, norm_mix_g[0], w_in[0], b_in[0], ssm_a_re[0], ssm_a_im[0], ssm_log_dt[0],
                    ssm_b_re[0], ssm_b_im[0], ssm_c_re[0], ssm_c_im[0], ssm_d[0], w_glu[0], b_glu[0],
                    w_br_ssm[0], attn_sinks[0], w_br_attn[0])
    out = _moe_tail(h, mixed, w_o[0], norm_ffn_g[0], w_router_group[0], b_router_group[0],
                    w_router_expert[0], b_router_expert[0], w_exp_gate[0], w_exp_up[0], w_exp_down[0],
                    norm_final_g)
    return out.reshape(bsz, seq, d)
```

```python
import functools
import math

import numpy as np
import jax
import jax.numpy as jnp
from jax import lax
from jax.experimental import pallas as pl
from jax.experimental.pallas import tpu as pltpu

F32 = jnp.float32
BF16 = jnp.bfloat16
I32 = jnp.int32

D_MODEL = 2048
SSM_WIDTH = 1024
SSM_GROUP = 16
SSM_GROUPS = 64
SSM_STATE = 64
HEAD_DIM = 64
N_Q_HEADS = 16
N_KV_HEADS = 4
Q_PER_KV = 4
WINDOW = 128
ROPE_DIM = 16
ROPE_THETA = 500000.0
Q_WIDTH = 1024
KV_WIDTH = 256
IN_WIDTH = SSM_WIDTH + Q_WIDTH + 2 * KV_WIDTH + 2 * D_MODEL
N_GROUPS = 8
EXPERTS_PER_GROUP = 8
N_EXPERTS = 64
TOP_K = 2
D_EXPERT = 512
MOE_BLOCK = 128
EPS = 1e-6

CHUNK = 16
CW = CHUNK * SSM_GROUP
GROUP_BLOCK = 8
HALF = D_MODEL // 2
LANES = 128
VMEM_LIMIT = 56 * 1024 * 1024

COL_G0, COL_G1, COL_Q, COL_K, COL_V = 0, 2048, 4096, 5120, 5376

HIGHEST = lax.Precision.HIGHEST


def _dot(a, b, precision=None):
    return jnp.dot(a, b, preferred_element_type=F32, precision=precision)


def _dot_nt(a, b, precision=None):
    return lax.dot_general(a, b, (((1,), (1,)), ((), ())), preferred_element_type=F32,
                           precision=precision)


def _sigmoid(x):
    return 1.0 / (1.0 + jnp.exp(-x))


def _pack_halves(lo, hi):
    return pltpu.pack_elementwise([lo, hi], packed_dtype=BF16)


def _unpack_half(w, index):
    return pltpu.unpack_elementwise(w, index=index, packed_dtype=BF16, unpacked_dtype=F32)


SUBLANES = 8
WORD_TILES = HALF // LANES
assert WORD_TILES == SUBLANES


def _store_token_tiles(ref, x):
    rows = x.shape[0]
    for s in range(WORD_TILES):
        ref[pl.ds(s, rows, stride=SUBLANES), :] = _pack_halves(x[:, s * LANES:(s + 1) * LANES],
                                                               x[:, HALF + s * LANES:HALF + (s + 1) * LANES])


def _load_token_tiles(ref, slot, first_row, rows):
    pieces = [ref[slot, pl.ds(first_row * SUBLANES + s, rows, stride=SUBLANES), :] for s in range(WORD_TILES)]
    return (jnp.concatenate([_unpack_half(p, 0) for p in pieces], axis=1),
            jnp.concatenate([_unpack_half(p, 1) for p in pieces], axis=1))


def _params(sem, vmem=VMEM_LIMIT):
    return pltpu.CompilerParams(dimension_semantics=sem, vmem_limit_bytes=vmem)


def _inproj_kernel(x_ref, g_ref, w_ref, b_ref, o_ref, xn_ref, *rest, chunk_major):
    @pl.when(pl.program_id(1) == 0)
    def _():
        x = x_ref[...]
        ms = jnp.mean(x * x, axis=-1, keepdims=True)
        xn_ref[...] = (x * lax.rsqrt(ms + EPS) * g_ref[...]).astype(BF16)

    acc = _dot(xn_ref[...], w_ref[...]) + b_ref[...]
    if not chunk_major:
        o_ref[...] = acc.astype(o_ref.dtype)
        return
    (acc_ref,) = rest
    nk = o_ref.shape[1]
    for c in range(acc_ref.shape[0]):
        acc_ref[c] = acc[:, c * LANES:(c + 1) * LANES]
    for s in range(CHUNK):
        for c in range(acc_ref.shape[0]):
            o_ref[s, :, c * LANES:(c + 1) * LANES] = acc_ref[c, pl.ds(s, nk, stride=CHUNK), :].astype(o_ref.dtype)


def _inproj(x, g, w_bf16, b, *, chunk_major, tm=1024, tn=512):
    t, d = x.shape
    n = w_bf16.shape[1]
    scratch = [pltpu.VMEM((tm, d), BF16)]
    if chunk_major:
        out_spec = pl.BlockSpec((CHUNK, tm // CHUNK, tn), lambda i, j: (0, i, j))
        out_shape = jax.ShapeDtypeStruct((CHUNK, t // CHUNK, n), BF16)
        scratch.append(pltpu.VMEM((tn // LANES, tm, LANES), F32))
    else:
        out_spec = pl.BlockSpec((tm, tn), lambda i, j: (i, j))
        out_shape = jax.ShapeDtypeStruct((t, n), BF16)
    return pl.pallas_call(
        functools.partial(_inproj_kernel, chunk_major=chunk_major),
        grid=(t // tm, n // tn),
        in_specs=[
            pl.BlockSpec((tm, d), lambda i, j: (i, 0)),
            pl.BlockSpec((1, d), lambda i, j: (0, 0)),
            pl.BlockSpec((d, tn), lambda i, j: (0, j)),
            pl.BlockSpec((1, tn), lambda i, j: (0, j)),
        ],
        out_specs=out_spec,
        out_shape=out_shape,
        scratch_shapes=scratch,
        compiler_params=_params(("arbitrary", "arbitrary")),
        name="inproj_u" if chunk_major else "inproj",
    )(x, g, w_bf16, b)


def _ssm_prep_kernel(are_ref, aim_ref, ldt_ref, btr_ref, bti_ref, cr_ref, ci_ref,
                     t_ref, wsr_ref, wsi_ref, rxr_ref, rxi_ref, a16r_ref, a16i_ref):
    lam_re = jnp.minimum(are_ref[0], -1e-4)
    lam_im = aim_ref[0]
    dt = jnp.exp(ldt_ref[0])
    lr_dt = lam_re * dt
    th = lam_im * dt
    mag = jnp.exp(lr_dt)
    ab_re = mag * jnp.cos(th)
    ab_im = mag * jnp.sin(th)
    den = lam_re * lam_re + lam_im * lam_im
    nr = ab_re - 1.0
    ni = ab_im
    coef_re = (nr * lam_re + ni * lam_im) / den
    coef_im = (ni * lam_re - nr * lam_im) / den
    btr = btr_ref[0]
    bti = bti_ref[0]
    bb_re = coef_re * btr - coef_im * bti
    bb_im = coef_re * bti + coef_im * btr

    e = lax.broadcasted_iota(I32, (CHUNK, SSM_STATE), 0).astype(F32)
    pmag = jnp.exp(e * lr_dt)
    pos_re = pmag * jnp.cos(e * th)
    pos_im = pmag * jnp.sin(e * th)
    nmag = jnp.exp(-e * lr_dt)
    neg_re = nmag * jnp.cos(e * th)
    neg_im = -nmag * jnp.sin(e * th)

    def rep(tab):
        return jnp.broadcast_to(tab[:, None, :], (CHUNK, SSM_GROUP, SSM_STATE)).reshape(CW, SSM_STATE)

    def tile(mat):
        return jnp.broadcast_to(mat[None, :, :], (CHUNK, SSM_GROUP, SSM_STATE)).reshape(CW, SSM_STATE)

    pr, pi = rep(pos_re), rep(pos_im)
    ctr, cti = tile(cr_ref[0]), tile(ci_ref[0])
    r_re = ctr * pr - cti * pi
    r_im = ctr * pi + cti * pr
    qr, qi = rep(neg_re), rep(neg_im)
    btr_t, bti_t = tile(bb_re), tile(bb_im)
    l_re = btr_t * qr - bti_t * qi
    l_im = btr_t * qi + bti_t * qr

    tm = _dot_nt(l_re, r_re, precision=HIGHEST) - _dot_nt(l_im, r_im, precision=HIGHEST)
    srow = lax.broadcasted_iota(I32, (CW, CW), 0) // SSM_GROUP
    tcol = lax.broadcasted_iota(I32, (CW, CW), 1) // SSM_GROUP
    t_ref[0] = jnp.where(tcol >= srow, tm, 0.0).astype(BF16)

    a15r = pos_re[CHUNK - 1:CHUNK, :]
    a15i = pos_im[CHUNK - 1:CHUNK, :]
    wsr_ref[0] = (l_re * a15r - l_im * a15i).astype(BF16)
    wsi_ref[0] = (l_re * a15i + l_im * a15r).astype(BF16)
    rxr_ref[0] = (r_re * ab_re - r_im * ab_im).astype(BF16)
    rxi_ref[0] = (-(r_re * ab_im + r_im * ab_re)).astype(BF16)
    m16 = jnp.exp(float(CHUNK) * lr_dt)
    a16r_ref[0] = m16 * jnp.cos(float(CHUNK) * th)
    a16i_ref[0] = m16 * jnp.sin(float(CHUNK) * th)


def _ssm_prep(a_re, a_im, log_dt, bt_re, bt_im, c_re, c_im):
    g = a_re.shape[0]
    vec = pl.BlockSpec((1, 1, SSM_STATE), lambda i: (i, 0, 0))
    mat = pl.BlockSpec((1, SSM_GROUP, SSM_STATE), lambda i: (i, 0, 0))
    wide = pl.BlockSpec((1, CW, SSM_STATE), lambda i: (i, 0, 0))
    return pl.pallas_call(
        _ssm_prep_kernel,
        grid=(g,),
        in_specs=[vec, vec, pl.BlockSpec((1, 1, 1), lambda i: (i, 0, 0)), mat, mat, mat, mat],
        out_specs=[pl.BlockSpec((1, CW, CW), lambda i: (i, 0, 0)), wide, wide, wide, wide, vec, vec],
        out_shape=[
            jax.ShapeDtypeStruct((g, CW, CW), BF16),
            jax.ShapeDtypeStruct((g, CW, SSM_STATE), BF16),
            jax.ShapeDtypeStruct((g, CW, SSM_STATE), BF16),
            jax.ShapeDtypeStruct((g, CW, SSM_STATE), BF16),
            jax.ShapeDtypeStruct((g, CW, SSM_STATE), BF16),
            jax.ShapeDtypeStruct((g, 1, SSM_STATE), F32),
            jax.ShapeDtypeStruct((g, 1, SSM_STATE), F32),
        ],
        compiler_params=_params(("arbitrary",)),
        name="ssm_prep",
    )(a_re.reshape(g, 1, SSM_STATE), a_im.reshape(g, 1, SSM_STATE), log_dt.reshape(g, 1, 1),
      bt_re, bt_im, c_re, c_im)


def _gelu_tanh(x):
    c = math.sqrt(2.0 / math.pi)
    return x * (0.5 * (1.0 + jnp.tanh(c * (x + 0.044715 * (x * x * x)))))


def _ssm_kernel(u_ref, t_ref, wsr_ref, wsi_ref, rxr_ref, rxi_ref, a16r_ref, a16i_ref, d_ref,
                z_ref, sr_ref, si_ref, ug_ref, zg_ref):
    nk = u_ref.shape[1]
    for j in range(GROUP_BLOCK):
        ug_ref[j] = jnp.concatenate(
            [u_ref[s, :, j * SSM_GROUP:(j + 1) * SSM_GROUP] for s in range(CHUNK)], axis=1)
    u_ref = ug_ref
    for j in range(GROUP_BLOCK):
        u = u_ref[j]
        sr_ref[j * nk:(j + 1) * nk, :] = _dot(u, wsr_ref[j])
        si_ref[j * nk:(j + 1) * nk, :] = _dot(u, wsi_ref[j])

    ar = a16r_ref[...]
    ai = a16i_ref[...]

    def step(k, carry):
        xr, xi = carry
        rows = pl.ds(k, GROUP_BLOCK, stride=nk)
        sr = sr_ref[rows, :]
        si = si_ref[rows, :]
        sr_ref[rows, :] = xr
        si_ref[rows, :] = xi
        return (ar * xr - ai * xi + sr, ar * xi + ai * xr + si)

    zero = jnp.zeros((GROUP_BLOCK, SSM_STATE), F32)
    lax.fori_loop(0, nk, step, (zero, zero))

    for j in range(GROUP_BLOCK):
        u = u_ref[j]
        xr = sr_ref[j * nk:(j + 1) * nk, :].astype(BF16)
        xi = si_ref[j * nk:(j + 1) * nk, :].astype(BF16)
        y = (_dot(u, t_ref[j]) + _dot_nt(xr, rxr_ref[j]) + _dot_nt(xi, rxi_ref[j])
             + d_ref[j] * u.astype(F32))
        zg_ref[j] = _gelu_tanh(y).astype(zg_ref.dtype)

    for t in range(CHUNK):
        z_ref[t] = jnp.concatenate(
            [zg_ref[j, :, t * SSM_GROUP:(t + 1) * SSM_GROUP] for j in range(GROUP_BLOCK)], axis=1)


def _ssm(u3, tmat, wsr, wsi, rxr, rxi, a16r, a16i, d_tiled):
    _, nk, width = u3.shape
    gb = GROUP_BLOCK
    gl = gb * SSM_GROUP
    blk3 = lambda a, b: pl.BlockSpec((gb, a, b), lambda i: (i, 0, 0))
    io = pl.BlockSpec((CHUNK, nk, gl), lambda i: (0, 0, i))
    return pl.pallas_call(
        _ssm_kernel,
        grid=(width // gl,),
        in_specs=[io, blk3(CW, CW), blk3(CW, SSM_STATE), blk3(CW, SSM_STATE),
                  blk3(CW, SSM_STATE), blk3(CW, SSM_STATE),
                  pl.BlockSpec((gb, SSM_STATE), lambda i: (i, 0)),
                  pl.BlockSpec((gb, SSM_STATE), lambda i: (i, 0)),
                  blk3(1, CW)],
        out_specs=io,
        out_shape=jax.ShapeDtypeStruct(u3.shape, BF16),
        scratch_shapes=[pltpu.VMEM((gb * nk, SSM_STATE), F32), pltpu.VMEM((gb * nk, SSM_STATE), F32),
                        pltpu.VMEM((gb, nk, CW), BF16), pltpu.VMEM((gb, nk, CW), BF16)],
        compiler_params=_params(("arbitrary",)),
        name="ssm",
    )(u3, tmat, wsr, wsi, rxr, rxi, a16r, a16i, d_tiled)


def _glu_kernel(z_ref, perm_ref, wg_ref, bg_ref, wb_ref, g0_ref, o_ref):
    tm = o_ref.shape[0]
    z = _dot(perm_ref[...], z_ref[...].reshape(tm, SSM_WIDTH)).astype(BF16)
    h = _dot(z, wg_ref[...]) + bg_ref[...]
    ga = h[:, :SSM_WIDTH]
    gb = h[:, SSM_WIDTH:]
    a = (ga * _sigmoid(gb)).astype(BF16)
    y = _dot(a, wb_ref[...])
    o_ref[...] = (_sigmoid(g0_ref[...].astype(F32)) * y).astype(o_ref.dtype)


def _glu(z3, w_glu, b_glu, w_br, proj, tm=512):
    t = z3.shape[0] * z3.shape[1]
    nk = tm // CHUNK
    r = np.arange(tm)
    perm = np.zeros((tm, tm), np.float32)
    perm[r, (r % CHUNK) * nk + r // CHUNK] = 1.0
    return pl.pallas_call(
        _glu_kernel,
        grid=(t // tm,),
        in_specs=[
            pl.BlockSpec((CHUNK, nk, SSM_WIDTH), lambda i: (0, i, 0)),
            pl.BlockSpec((tm, tm), lambda i: (0, 0)),
            pl.BlockSpec((SSM_WIDTH, 2 * SSM_WIDTH), lambda i: (0, 0)),
            pl.BlockSpec((1, 2 * SSM_WIDTH), lambda i: (0, 0)),
            pl.BlockSpec((SSM_WIDTH, D_MODEL), lambda i: (0, 0)),
            pl.BlockSpec((tm, D_MODEL), lambda i: (i, COL_G0 // D_MODEL)),
        ],
        out_specs=pl.BlockSpec((tm, D_MODEL), lambda i: (i, 0)),
        out_shape=jax.ShapeDtypeStruct((t, D_MODEL), BF16),
        compiler_params=_params(("arbitrary",)),
        name="glu",
    )(z3, jnp.asarray(perm, BF16), w_glu, b_glu, w_br, proj)


def _rope_pattern():
    half = ROPE_DIM // 2
    inv_freq = (np.float32(ROPE_THETA) ** (-np.arange(half, dtype=np.float32) / np.float32(half))).astype(np.float32)
    d = np.arange(LANES) % HEAD_DIM
    rotated = d < ROPE_DIM
    pat = np.zeros((16, LANES), np.float32)
    pat[:half] = rotated[None, :] & ((d % half)[None, :] == np.arange(half)[:, None])
    pat[8] = ~rotated
    pat[9] = np.where(d < half, -1.0, 0.0)
    pat[10] = np.where((d >= half) & rotated, 1.0, 0.0)
    return inv_freq.reshape(half, 1), pat


def _attn_kernel(q_ref, k_ref, v_ref, pos_ref, freq_ref, pat_ref, sink_ref, wbr_ref, ysg_ref, g1_ref,
                 o_ref, qbuf, kbuf, vbuf, obuf):
    i = pl.program_id(0)
    tq = q_ref.shape[0]
    nw = tq // WINDOW
    half = ROPE_DIM // 2

    @pl.when(i == 0)
    def _():
        kbuf[:, 0:WINDOW, :] = jnp.zeros((2 * N_KV_HEADS, WINDOW, LANES), BF16)
        vbuf[:, 0:WINDOW, :] = jnp.zeros((2 * N_KV_HEADS, WINDOW, LANES), BF16)

    ang = freq_ref[...] * pos_ref[...].astype(F32)
    spread = lambda tab: lax.dot_general(tab, pat_ref[0:8, :], (((0,), (0,)), ((), ())),
                                         preferred_element_type=F32, precision=HIGHEST)
    cs = spread(jnp.cos(ang)) + pat_ref[8:9, :]
    sn = spread(jnp.sin(ang))
    c_up = sn * pat_ref[9:10, :]
    c_dn = sn * pat_ref[10:11, :]

    def rope(x):
        return (x * cs + pltpu.roll(x, LANES - half, 1) * c_up + pltpu.roll(x, half, 1) * c_dn)

    low = lax.broadcasted_iota(I32, (tq, LANES), 1) < HEAD_DIM

    def split_heads(buf, cb, x):
        xs = pltpu.roll(x, HEAD_DIM, 1)
        zero = jnp.zeros_like(x)
        buf[4 * cb + 0, WINDOW:, :] = jnp.where(low, x, zero).astype(BF16)
        buf[4 * cb + 1, WINDOW:, :] = jnp.where(low, zero, xs).astype(BF16)
        buf[4 * cb + 2, WINDOW:, :] = jnp.where(low, xs, zero).astype(BF16)
        buf[4 * cb + 3, WINDOW:, :] = jnp.where(low, zero, x).astype(BF16)

    for cb in range(Q_WIDTH // LANES):
        sl = slice(cb * LANES, (cb + 1) * LANES)
        qbuf[:, sl] = rope(q_ref[:, sl].astype(F32)).astype(BF16)
    for cb in range(KV_WIDTH // LANES):
        sl = slice(cb * LANES, (cb + 1) * LANES)
        split_heads(kbuf, cb, rope(k_ref[:, sl].astype(F32)))
        split_heads(vbuf, cb, v_ref[:, sl].astype(F32))

    qi = lax.broadcasted_iota(I32, (WINDOW, 2 * WINDOW), 0)
    kj = lax.broadcasted_iota(I32, (WINDOW, 2 * WINDOW), 1)
    dist = qi + WINDOW - kj
    in_band = (dist >= 0) & (dist < WINDOW)
    cur_only = kj >= WINDOW
    sinks = sink_ref[...]
    scale = HEAD_DIM ** -0.5

    def window(w, carry):
        r0 = pl.multiple_of(w * WINDOW, WINDOW)
        rows = pl.ds(r0, 2 * WINDOW)
        not_first = (i * nw + w) > 0
        mask = in_band & (cur_only | not_first)

        def probs(qp, part, h):
            s = _dot_nt(qp, kbuf[2 * (h // Q_PER_KV) + part, rows, :]) * scale
            s = jnp.where(mask, s, -jnp.inf)
            sink = sinks[:, h:h + 1]
            m = jnp.maximum(jnp.max(s, axis=-1, keepdims=True), sink)
            p = jnp.exp(s - m)
            denom = jnp.sum(p, axis=-1, keepdims=True) + jnp.exp(sink - m)
            return p.astype(BF16), 1.0 / denom

        for a in range(N_Q_HEADS // 2):
            kv = (2 * a) // Q_PER_KV
            qp = qbuf[pl.ds(r0, WINDOW), a * LANES:(a + 1) * LANES]
            p_lo, r_lo = probs(qp, 0, 2 * a)
            p_hi, r_hi = probs(qp, 1, 2 * a + 1)
            o = _dot(p_lo, vbuf[2 * kv, rows, :]) * r_lo + _dot(p_hi, vbuf[2 * kv + 1, rows, :]) * r_hi
            obuf[pl.ds(r0, WINDOW), a * LANES:(a + 1) * LANES] = o.astype(BF16)
        return carry

    lax.fori_loop(0, nw, window, 0)

    kbuf[:, 0:WINDOW, :] = kbuf[:, tq:tq + WINDOW, :]
    vbuf[:, 0:WINDOW, :] = vbuf[:, tq:tq + WINDOW, :]

    y = _dot(obuf[...], wbr_ref[...])
    o_ref[...] = (ysg_ref[...].astype(F32) + _sigmoid(g1_ref[...].astype(F32)) * y).astype(o_ref.dtype)


def _attn(proj, pos_row, sinks, w_br, ysg, tq=512):
    t = proj.shape[0]
    freq, pat = (jnp.asarray(a) for a in _rope_pattern())
    return pl.pallas_call(
        _attn_kernel,
        grid=(t // tq,),
        in_specs=[
            pl.BlockSpec((tq, Q_WIDTH), lambda i: (i, COL_Q // Q_WIDTH)),
            pl.BlockSpec((tq, KV_WIDTH), lambda i: (i, COL_K // KV_WIDTH)),
            pl.BlockSpec((tq, KV_WIDTH), lambda i: (i, COL_V // KV_WIDTH)),
            pl.BlockSpec((1, tq), lambda i: (0, i)),
            pl.BlockSpec((ROPE_DIM // 2, 1), lambda i: (0, 0)),
            pl.BlockSpec((16, LANES), lambda i: (0, 0)),
            pl.BlockSpec((1, N_Q_HEADS), lambda i: (0, 0)),
            pl.BlockSpec((Q_WIDTH, D_MODEL), lambda i: (0, 0)),
            pl.BlockSpec((tq, D_MODEL), lambda i: (i, 0)),
            pl.BlockSpec((tq, D_MODEL), lambda i: (i, COL_G1 // D_MODEL)),
        ],
        out_specs=pl.BlockSpec((tq, D_MODEL), lambda i: (i, 0)),
        out_shape=jax.ShapeDtypeStruct((t, D_MODEL), BF16),
        scratch_shapes=[
            pltpu.VMEM((tq, Q_WIDTH), BF16),
            pltpu.VMEM((2 * N_KV_HEADS, tq + WINDOW, LANES), BF16),
            pltpu.VMEM((2 * N_KV_HEADS, tq + WINDOW, LANES), BF16),
            pltpu.VMEM((tq, Q_WIDTH), BF16),
        ],
        compiler_params=_params(("arbitrary",)),
        name="attn",
    )(proj, proj, proj, pos_row, freq, pat, sinks, w_br, ysg, proj)


def _oproj_kernel(x_ref, mix_ref, wo_ref, g_ref, wrh_ref, wrl_ref, br_ref, h_ref, hp_ref, lt_ref):
    h = x_ref[...] + _dot(mix_ref[...], wo_ref[...])
    h_ref[...] = h
    ms = jnp.mean(h * h, axis=-1, keepdims=True)
    hn = h * lax.rsqrt(ms + EPS) * g_ref[...]
    _store_token_tiles(hp_ref, hn)
    hn_hi = hn.astype(BF16)
    hn_lo = (hn - hn_hi.astype(F32)).astype(BF16)
    logits = (_dot(hn_hi, wrh_ref[...]) + _dot(hn_lo, wrh_ref[...]) + _dot(hn_hi, wrl_ref[...])
              + br_ref[...])
    lt_ref[...] = logits.T


def _oproj(x, mixed, w_o, g, w_router, b_router, tm=512):
    t = x.shape[0]
    w_router_hi = w_router.astype(BF16)
    w_router_hi_rest = (w_router - w_router_hi.astype(F32)).astype(BF16)
    return pl.pallas_call(
        _oproj_kernel,
        grid=(t // tm,),
        in_specs=[
            pl.BlockSpec((tm, D_MODEL), lambda i: (i, 0)),
            pl.BlockSpec((tm, D_MODEL), lambda i: (i, 0)),
            pl.BlockSpec((D_MODEL, D_MODEL), lambda i: (0, 0)),
            pl.BlockSpec((1, D_MODEL), lambda i: (0, 0)),
            pl.BlockSpec((D_MODEL, LANES), lambda i: (0, 0)),
            pl.BlockSpec((D_MODEL, LANES), lambda i: (0, 0)),
            pl.BlockSpec((1, LANES), lambda i: (0, 0)),
        ],
        out_specs=[
            pl.BlockSpec((tm, D_MODEL), lambda i: (i, 0)),
            pl.BlockSpec((tm * SUBLANES, LANES), lambda i: (i, 0)),
            pl.BlockSpec((LANES, tm), lambda i: (0, i)),
        ],
        out_shape=[
            jax.ShapeDtypeStruct((t, D_MODEL), F32),
            jax.ShapeDtypeStruct((t * SUBLANES, LANES), jnp.uint32),
            jax.ShapeDtypeStruct((LANES, t), F32),
        ],
        compiler_params=_params(("arbitrary",)),
        name="oproj",
    )(x, mixed, w_o, g, w_router_hi, w_router_hi_rest, b_router)


ROUTE_CHUNK = 256


def _route_kernel(lt_ref, dest_ref, wts_ref, bexp_ref, nvalid_ref, eid_ref, rank_ref):
    t = lt_ref.shape[1]
    nc = t // ROUTE_CHUNK
    r8 = lax.broadcasted_iota(I32, (N_GROUPS, ROUTE_CHUNK), 0)
    r64 = lax.broadcasted_iota(I32, (N_EXPERTS, ROUTE_CHUNK), 0)

    def pick(c, carry):
        cols = pl.ds(pl.multiple_of(c * ROUTE_CHUNK, ROUTE_CHUNK), ROUTE_CHUNK)
        lg = lt_ref[0:N_GROUPS, cols]
        m = jnp.max(lg, axis=0, keepdims=True)
        ssum = jnp.sum(jnp.exp(lg - m), axis=0, keepdims=True)
        p_grp = 1.0 / ssum
        grp = jnp.min(jnp.where(lg == m, r8, N_GROUPS), axis=0, keepdims=True)
        le = lt_ref[N_GROUPS:N_GROUPS + N_EXPERTS, cols]
        leg = jnp.where((r64 // EXPERTS_PER_GROUP) == grp, le, -jnp.inf)
        m1 = jnp.max(leg, axis=0, keepdims=True)
        i1 = jnp.min(jnp.where(leg == m1, r64, N_EXPERTS), axis=0, keepdims=True)
        leg2 = jnp.where(r64 == i1, -jnp.inf, leg)
        m2 = jnp.max(leg2, axis=0, keepdims=True)
        i2 = jnp.min(jnp.where(leg2 == m2, r64, N_EXPERTS), axis=0, keepdims=True)
        ex = jnp.exp(m2 - m1)
        eid_ref[0:1, cols] = i1
        eid_ref[1:2, cols] = i2
        wts_ref[0:1, cols] = p_grp / (1.0 + ex)
        wts_ref[1:2, cols] = p_grp * ex / (1.0 + ex)
        return carry

    lax.fori_loop(0, nc, pick, 0)

    a_row = lax.broadcasted_iota(I32, (ROUTE_CHUNK, ROUTE_CHUNK), 0)
    a_col = lax.broadcasted_iota(I32, (ROUTE_CHUNK, ROUTE_CHUNK), 1)
    before = (a_row < a_col).astype(BF16)

    def count(n, carry):
        j = n // nc
        c = n - j * nc
        cols = pl.ds(pl.multiple_of(c * ROUTE_CHUNK, ROUTE_CHUNK), ROUTE_CHUNK)
        oh = r64 == eid_ref[pl.ds(j, 1), cols]
        ohf = oh.astype(F32)
        pref = _dot(ohf.astype(BF16), before) + carry
        rank_ref[pl.ds(j, 1), cols] = jnp.sum(jnp.where(oh, pref, 0.0), axis=0, keepdims=True)
        return carry + jnp.sum(ohf, axis=1, keepdims=True)

    counts = lax.fori_loop(0, TOP_K * nc, count, jnp.zeros((N_EXPERTS, 1), F32))

    padded = jnp.floor((counts + (MOE_BLOCK - 1)) * (1.0 / MOE_BLOCK)) * MOE_BLOCK
    e_row = lax.broadcasted_iota(I32, (N_EXPERTS, N_EXPERTS), 0)
    e_col = lax.broadcasted_iota(I32, (N_EXPERTS, N_EXPERTS), 1)
    incl = (e_col <= e_row).astype(F32)
    pad_end = _dot(incl, jnp.broadcast_to(padded, (N_EXPERTS, LANES)), precision=HIGHEST)[:, 0:1]
    pad_start = pad_end - padded

    def place(n, carry):
        j = n // nc
        c = n - j * nc
        cols = pl.ds(pl.multiple_of(c * ROUTE_CHUNK, ROUTE_CHUNK), ROUTE_CHUNK)
        oh = r64 == eid_ref[pl.ds(j, 1), cols]
        start = jnp.sum(jnp.where(oh, pad_start, 0.0), axis=0, keepdims=True)
        dest_ref[pl.ds(j, 1), cols] = (start + rank_ref[pl.ds(j, 1), cols]).astype(I32)
        return carry

    lax.fori_loop(0, TOP_K * nc, place, 0)

    b0 = (lax.broadcasted_iota(I32, (N_EXPERTS, bexp_ref.shape[1]), 1) * MOE_BLOCK).astype(F32)
    n_done = jnp.sum((pad_end <= b0).astype(F32), axis=0, keepdims=True)
    bexp_ref[...] = jnp.minimum(n_done, float(N_EXPERTS - 1)).astype(I32)
    live = jnp.minimum(pad_start + counts, b0 + MOE_BLOCK) - jnp.maximum(pad_start, b0)
    nvalid_ref[...] = jnp.sum(jnp.maximum(live, 0.0), axis=0, keepdims=True).astype(I32)


def _route(logits_t, n_blocks_pad):
    t = logits_t.shape[1]
    return pl.pallas_call(
        _route_kernel,
        out_shape=[
            jax.ShapeDtypeStruct((TOP_K, t), I32),
            jax.ShapeDtypeStruct((TOP_K, t), F32),
            jax.ShapeDtypeStruct((1, n_blocks_pad), I32),
            jax.ShapeDtypeStruct((1, n_blocks_pad), I32),
        ],
        scratch_shapes=[pltpu.VMEM((TOP_K, t), I32), pltpu.VMEM((TOP_K, t), F32)],
        compiler_params=pltpu.CompilerParams(vmem_limit_bytes=VMEM_LIMIT),
        name="route",
    )(logits_t)


def _invmap_kernel(dest_ref, rt_ref):
    n_rows = rt_ref.shape[0]
    t = dest_ref.shape[0] // TOP_K

    def clear(r, c):
        rt_ref[r] = 0
        return c

    lax.fori_loop(0, n_rows, clear, 0, unroll=16)

    def put(tok, c):
        rt_ref[dest_ref[tok]] = tok
        rt_ref[dest_ref[t + tok]] = tok
        return c

    lax.fori_loop(0, t, put, 0, unroll=8)


def _invmap(dest_flat, n_rows):
    return pl.pallas_call(
        _invmap_kernel,
        in_specs=[pl.BlockSpec(memory_space=pltpu.SMEM)],
        out_specs=pl.BlockSpec(memory_space=pltpu.SMEM),
        out_shape=jax.ShapeDtypeStruct((n_rows,), I32),
        name="invmap",
    )(dest_flat)


GATHER_SLOTS = 3


def _row_copy(src_hbm, src_row, dst_buf, slot, dst_row, sem):
    return pltpu.make_async_copy(src_hbm.at[pl.ds(pl.multiple_of(src_row * SUBLANES, SUBLANES), SUBLANES), :],
                                 dst_buf.at[slot, pl.ds(dst_row * SUBLANES, SUBLANES), :], sem.at[slot])


def _rows_copy(src_hbm, dst_buf, slot, rows, sem):
    n = rows * SUBLANES
    return pltpu.make_async_copy(src_hbm.at[pl.ds(0, n), :], dst_buf.at[slot, pl.ds(0, n), :], sem.at[slot])


def _expert_kernel(bexp_ref, nvalid_ref, rt_ref, hp_hbm, wg_hbm, wu_hbm, wd_hbm, ys_ref,
                   xbuf, wf_g, wf_u, wf_d, wb_g, wb_u, wb_d, ord_ref, xsem, wsem):
    b = pl.program_id(0)
    nb = pl.num_programs(0) - 1

    def live_rows(blk):
        return jnp.where(nvalid_ref[blk] > 0, MOE_BLOCK, 0)

    def expert_of(blk):
        return bexp_ref[jnp.minimum(blk, nb - 1)]

    def next_owner(blk, e):
        return lax.while_loop(lambda j: (j < nb) & (expert_of(j) == e), lambda j: j + 1, blk)

    def weight_copies(e, slot):
        return (pltpu.make_async_copy(wg_hbm.at[e], wf_g.at[slot], wsem.at[slot, 0]),
                pltpu.make_async_copy(wu_hbm.at[e], wf_u.at[slot], wsem.at[slot, 1]),
                pltpu.make_async_copy(wd_hbm.at[e], wf_d.at[slot], wsem.at[slot, 2]))

    @pl.when(b == 0)
    def _():
        ord_ref[0] = 0
        e0 = bexp_ref[0]
        for c in weight_copies(e0, 0):
            c.start(priority=1)
        n1 = next_owner(1, e0)

        @pl.when(n1 < nb)
        def _():
            for c in weight_copies(expert_of(n1), 1):
                c.start(priority=1)

    def gather(blk):
        base = blk * MOE_BLOCK
        slot = blk % GATHER_SLOTS

        def body(i, c):
            for s in range(SUBLANES):
                r = i * SUBLANES + s
                _row_copy(hp_hbm, rt_ref[base + r], xbuf, slot, r, xsem).start()
            return c

        lax.fori_loop(0, jnp.where(blk < nb, live_rows(jnp.minimum(blk, nb - 1)) // SUBLANES, 0), body, 0)

    @pl.when(b == 0)
    def _():
        gather(0)

    gather(b + 1)

    @pl.when(b > 0)
    def _():
        blk = b - 1
        e = bexp_ref[blk]
        first = (blk == 0) | (e != bexp_ref[jnp.maximum(blk - 1, 0)])

        @pl.when(first)
        def _():
            n = ord_ref[0]
            wslot = n % 2
            ord_ref[0] = n + 1
            for c in weight_copies(e, wslot):
                c.wait()
            wb_g[...] = wf_g[wslot].astype(BF16)
            wb_u[...] = wf_u[wslot].astype(BF16)
            wb_d[...] = wf_d[wslot].astype(BF16)
            n1 = next_owner(blk + 1, e)
            n2 = next_owner(n1 + 1, expert_of(n1))

            @pl.when((n1 < nb) & (n2 < nb))
            def _():
                for c in weight_copies(expert_of(n2), wslot):
                    c.start(priority=1)

        slot = blk % GATHER_SLOTS
        rows = live_rows(blk)

        @pl.when(rows > 0)
        def _():
            _rows_copy(hp_hbm, xbuf, slot, rows, xsem).wait()
            lo, hi = (v.astype(BF16) for v in _load_token_tiles(xbuf, slot, 0, MOE_BLOCK))
            g = _dot(lo, wb_g[:HALF, :]) + _dot(hi, wb_g[HALF:, :])
            u = _dot(lo, wb_u[:HALF, :]) + _dot(hi, wb_u[HALF:, :])
            h = (g * _sigmoid(g) * u).astype(BF16)
            y = _dot(h, wb_d[...])
            _store_token_tiles(ys_ref, y)

        @pl.when(rows == 0)
        def _():
            _store_token_tiles(ys_ref, jnp.zeros((MOE_BLOCK, D_MODEL), F32))


def _experts(bexp, nvalid, row_tok, hn_packed, w_gate, w_up, w_down, n_blocks):
    grid_spec = pltpu.PrefetchScalarGridSpec(
        num_scalar_prefetch=3,
        grid=(n_blocks + 1,),
        in_specs=[pl.BlockSpec(memory_space=pl.ANY)] * 4,
        out_specs=pl.BlockSpec((MOE_BLOCK * SUBLANES, LANES), lambda b, be, nv, rt: (jnp.maximum(b - 1, 0), 0)),
        scratch_shapes=[
            pltpu.VMEM((GATHER_SLOTS, MOE_BLOCK * SUBLANES, LANES), jnp.uint32),
            pltpu.VMEM((2, D_MODEL, D_EXPERT), F32), pltpu.VMEM((2, D_MODEL, D_EXPERT), F32),
            pltpu.VMEM((2, D_EXPERT, D_MODEL), F32),
            pltpu.VMEM((D_MODEL, D_EXPERT), BF16), pltpu.VMEM((D_MODEL, D_EXPERT), BF16),
            pltpu.VMEM((D_EXPERT, D_MODEL), BF16),
            pltpu.SMEM((1,), I32),
            pltpu.SemaphoreType.DMA((GATHER_SLOTS,)), pltpu.SemaphoreType.DMA((2, 3)),
        ],
    )
    return pl.pallas_call(
        _expert_kernel,
        grid_spec=grid_spec,
        out_shape=jax.ShapeDtypeStruct((n_blocks * MOE_BLOCK * SUBLANES, LANES), jnp.uint32),
        compiler_params=_params(("arbitrary",)),
        name="experts",
    )(bexp, nvalid, row_tok, hn_packed, w_gate, w_up, w_down)


def _combine_kernel(dest_ref, ys_hbm, h_ref, w_ref, g_ref, o_ref, ybuf, sem):
    i = pl.program_id(0)
    n = pl.num_programs(0)
    tq = h_ref.shape[0]
    t = n * tq

    def start_gather(blk, slot):
        base = blk * tq

        def body(j, c):
            for s in range(SUBLANES):
                r = j * SUBLANES + s
                _row_copy(ys_hbm, dest_ref[base + r], ybuf, slot, r, sem).start()
                _row_copy(ys_hbm, dest_ref[t + base + r], ybuf, slot, tq + r, sem).start()
            return c

        lax.fori_loop(0, tq // SUBLANES, body, 0)

    @pl.when(i == 0)
    def _():
        start_gather(0, 0)

    @pl.when(i + 1 < n)
    def _():
        start_gather(i + 1, (i + 1) % 2)

    slot = i % 2
    _rows_copy(ys_hbm, ybuf, slot, TOP_K * tq, sem).wait()
    y0_lo, y0_hi = _load_token_tiles(ybuf, slot, 0, tq)
    y1_lo, y1_hi = _load_token_tiles(ybuf, slot, tq, tq)
    w = w_ref[...]
    w0 = w[:, 0:1]
    w1 = w[:, 1:2]
    h = h_ref[...]
    lo = h[:, :HALF] + w0 * y0_lo + w1 * y1_lo
    hi = h[:, HALF:] + w0 * y0_hi + w1 * y1_hi
    ms = (jnp.sum(lo * lo, axis=-1, keepdims=True) + jnp.sum(hi * hi, axis=-1, keepdims=True)) * (1.0 / D_MODEL)
    inv = lax.rsqrt(ms + EPS)
    g = g_ref[...]
    o_ref[:, :HALF] = lo * inv * g[:, :HALF]
    o_ref[:, HALF:] = hi * inv * g[:, HALF:]


def _combine(dest_flat, ys_packed, h1, wts_tok, g_final, tq=256):
    t = h1.shape[0]
    grid_spec = pltpu.PrefetchScalarGridSpec(
        num_scalar_prefetch=1,
        grid=(t // tq,),
        in_specs=[
            pl.BlockSpec(memory_space=pl.ANY),
            pl.BlockSpec((tq, D_MODEL), lambda i, d: (i, 0)),
            pl.BlockSpec((tq, TOP_K), lambda i, d: (i, 0)),
            pl.BlockSpec((1, D_MODEL), lambda i, d: (0, 0)),
        ],
        out_specs=pl.BlockSpec((tq, D_MODEL), lambda i, d: (i, 0)),
        scratch_shapes=[pltpu.VMEM((2, TOP_K * tq * SUBLANES, LANES), jnp.uint32),
                        pltpu.SemaphoreType.DMA((2,))],
    )
    return pl.pallas_call(
        _combine_kernel,
        grid_spec=grid_spec,
        out_shape=jax.ShapeDtypeStruct((t, D_MODEL), F32),
        compiler_params=_params(("arbitrary",)),
        name="combine",
    )(dest_flat, ys_packed, h1, wts_tok, g_final)


def _mixers(h, positions, norm_g, w_in, b_in, a_re, a_im, log_dt, b_re, b_im, c_re, c_im, d_skip,
            w_glu, b_glu, w_br_ssm, sinks, w_br_attn):
    t = h.shape[0]
    ngate = 2 * D_MODEL
    g = norm_g.reshape(1, D_MODEL)
    w_main = jnp.concatenate([w_in[:, IN_WIDTH - ngate:], w_in[:, SSM_WIDTH:IN_WIDTH - ngate]], axis=1).astype(BF16)
    b_main = jnp.concatenate([b_in[IN_WIDTH - ngate:], b_in[SSM_WIDTH:IN_WIDTH - ngate]]).reshape(1, -1)
    proj = _inproj(h, g, w_main, b_main, chunk_major=False)
    u3 = _inproj(h, g, w_in[:, :SSM_WIDTH].astype(BF16), b_in[:SSM_WIDTH].reshape(1, -1), chunk_major=True)

    tmat, wsr, wsi, rxr, rxi, a16r, a16i = _ssm_prep(
        a_re, a_im, log_dt, b_re.transpose(0, 2, 1), b_im.transpose(0, 2, 1), c_re, c_im)
    d_tiled = jnp.tile(d_skip, (1, CHUNK)).reshape(SSM_GROUPS, 1, CW)
    z3 = _ssm(u3, tmat, wsr, wsi, rxr, rxi, a16r.reshape(SSM_GROUPS, SSM_STATE),
              a16i.reshape(SSM_GROUPS, SSM_STATE), d_tiled)

    ysg = _glu(z3, w_glu.astype(BF16), b_glu.reshape(1, -1), w_br_ssm.astype(BF16), proj)
    return _attn(proj, positions.reshape(1, t), sinks.reshape(1, N_Q_HEADS), w_br_attn.astype(BF16), ysg)


def _moe_tail(h, mixed, w_o, norm_ffn_g, w_rg, b_rg, w_re, b_re, w_gate, w_up, w_down, norm_final_g):
    t = h.shape[0]
    n_route = N_GROUPS + N_EXPERTS
    w_router = jnp.concatenate([w_rg, w_re, jnp.zeros((D_MODEL, LANES - n_route), F32)], axis=1)
    b_router = jnp.concatenate([b_rg, b_re, jnp.zeros((LANES - n_route,), F32)]).reshape(1, LANES)
    h1, hn_packed, logits_t = _oproj(h, mixed, w_o.astype(BF16), norm_ffn_g.reshape(1, D_MODEL),
                                     w_router, b_router)
    n_assign = t * TOP_K
    n_blocks = -(-(n_assign + N_EXPERTS * (MOE_BLOCK - 1)) // MOE_BLOCK)
    n_blocks_pad = -(-n_blocks // LANES) * LANES
    dest, wts, bexp, nvalid = _route(logits_t, n_blocks_pad)
    dest_flat = dest.reshape(n_assign)
    row_tok = _invmap(dest_flat, n_blocks * MOE_BLOCK)
    ys_packed = _experts(bexp[0, :n_blocks], nvalid[0, :n_blocks], row_tok, hn_packed, w_gate, w_up, w_down,
                         n_blocks)
    return _combine(dest_flat, ys_packed, h1, wts.T, norm_final_g.reshape(1, D_MODEL))


def kernel(x, positions, norm_mix_g, w_in, b_in, ssm_a_re, ssm_a_im, ssm_log_dt, ssm_b_re, ssm_b_im, ssm_c_re, ssm_c_im, ssm_d, w_glu, b_glu, w_br_ssm, attn_sinks, w_br_attn, w_o, norm_ffn_g, w_router_group, b_router_group, w_router_expert, b_router_expert, w_exp_gate, w_exp_up, w_exp_down, norm_final_g):
    bsz, seq, d = x.shape
    assert bsz == 1 and d == D_MODEL and norm_mix_g.shape[0] == 1
    h = x.reshape(seq, d)
    mixed = _mixers(h, positions, norm_mix_g[0], w_in[0], b_in[0], ssm_a_re[0], ssm_a_im[0], ssm_log_dt[0],
                    ssm_b_re[0], ssm_b_im[0], ssm_c_re[0], ssm_c_im[0], ssm_d[0], w_glu[0], b_glu[0],
                    w_br_ssm[0], attn_sinks[0], w_br_attn[0])
    out = _moe_tail(h, mixed, w_o[0], norm_ffn_g[0], w_router_group[0], b_router_group[0],
                    w_router_expert[0], b_router_expert[0], w_exp_gate[0], w_exp_up[0], w_exp_down[0],
                    norm_final_g)
    return out.reshape(bsz, seq, d)
```

```python
import functools
import math

import numpy as np
import jax
import jax.numpy as jnp
from jax import lax
from jax.experimental import pallas as pl
from jax.experimental.pallas import tpu as pltpu

F32 = jnp.float32
BF16 = jnp.bfloat16
I32 = jnp.int32

D_MODEL = 2048
SSM_WIDTH = 1024
SSM_GROUP = 16
SSM_GROUPS = 64
SSM_STATE = 64
HEAD_DIM = 64
N_Q_HEADS = 16
N_KV_HEADS = 4
Q_PER_KV = 4
WINDOW = 128
ROPE_DIM = 16
ROPE_THETA = 500000.0
Q_WIDTH = 1024
KV_WIDTH = 256
IN_WIDTH = SSM_WIDTH + Q_WIDTH + 2 * KV_WIDTH + 2 * D_MODEL
N_GROUPS = 8
EXPERTS_PER_GROUP = 8
N_EXPERTS = 64
TOP_K = 2
D_EXPERT = 512
MOE_BLOCK = 128
EPS = 1e-6

CHUNK = 16
CW = CHUNK * SSM_GROUP
GROUP_BLOCK = 8
HALF = D_MODEL // 2
LANES = 128
VMEM_LIMIT = 56 * 1024 * 1024

COL_G0, COL_G1, COL_Q, COL_K, COL_V = 0, 2048, 4096, 5120, 5376

HIGHEST = lax.Precision.HIGHEST


def _dot(a, b, precision=None):
    return jnp.dot(a, b, preferred_element_type=F32, precision=precision)


def _dot_nt(a, b, precision=None):
    return lax.dot_general(a, b, (((1,), (1,)), ((), ())), preferred_element_type=F32,
                           precision=precision)


def _sigmoid(x):
    return 1.0 / (1.0 + jnp.exp(-x))


def _pack_halves(lo, hi):
    return pltpu.pack_elementwise([lo, hi], packed_dtype=BF16)


def _unpack_half(w, index):
    return pltpu.unpack_elementwise(w, index=index, packed_dtype=BF16, unpacked_dtype=F32)


SUBLANES = 8
WORD_TILES = HALF // LANES
assert WORD_TILES == SUBLANES


def _store_token_tiles(ref, x):
    rows = x.shape[0]
    for s in range(WORD_TILES):
        ref[pl.ds(s, rows, stride=SUBLANES), :] = _pack_halves(x[:, s * LANES:(s + 1) * LANES],
                                                               x[:, HALF + s * LANES:HALF + (s + 1) * LANES])


def _load_token_tiles(ref, slot, first_row, rows):
    pieces = [ref[slot, pl.ds(first_row * SUBLANES + s, rows, stride=SUBLANES), :] for s in range(WORD_TILES)]
    return (jnp.concatenate([_unpack_half(p, 0) for p in pieces], axis=1),
            jnp.concatenate([_unpack_half(p, 1) for p in pieces], axis=1))


def _params(sem, vmem=VMEM_LIMIT):
    return pltpu.CompilerParams(dimension_semantics=sem, vmem_limit_bytes=vmem)


def _inproj_kernel(x_ref, g_ref, w_ref, b_ref, o_ref, xn_ref, *rest, chunk_major):
    @pl.when(pl.program_id(1) == 0)
    def _():
        x = x_ref[...]
        ms = jnp.mean(x * x, axis=-1, keepdims=True)
        xn_ref[...] = (x * lax.rsqrt(ms + EPS) * g_ref[...]).astype(BF16)

    acc = _dot(xn_ref[...], w_ref[...]) + b_ref[...]
    if not chunk_major:
        o_ref[...] = acc.astype(o_ref.dtype)
        return
    (acc_ref,) = rest
    nk = o_ref.shape[1]
    for c in range(acc_ref.shape[0]):
        acc_ref[c] = acc[:, c * LANES:(c + 1) * LANES]
    for s in range(CHUNK):
        for c in range(acc_ref.shape[0]):
            o_ref[s, :, c * LANES:(c + 1) * LANES] = acc_ref[c, pl.ds(s, nk, stride=CHUNK), :].astype(o_ref.dtype)


def _inproj(x, g, w_bf16, b, *, chunk_major, tm=1024, tn=512):
    t, d = x.shape
    n = w_bf16.shape[1]
    scratch = [pltpu.VMEM((tm, d), BF16)]
    if chunk_major:
        out_spec = pl.BlockSpec((CHUNK, tm // CHUNK, tn), lambda i, j: (0, i, j))
        out_shape = jax.ShapeDtypeStruct((CHUNK, t // CHUNK, n), BF16)
        scratch.append(pltpu.VMEM((tn // LANES, tm, LANES), F32))
    else:
        out_spec = pl.BlockSpec((tm, tn), lambda i, j: (i, j))
        out_shape = jax.ShapeDtypeStruct((t, n), BF16)
    return pl.pallas_call(
        functools.partial(_inproj_kernel, chunk_major=chunk_major),
        grid=(t // tm, n // tn),
        in_specs=[
            pl.BlockSpec((tm, d), lambda i, j: (i, 0)),
            pl.BlockSpec((1, d), lambda i, j: (0, 0)),
            pl.BlockSpec((d, tn), lambda i, j: (0, j)),
            pl.BlockSpec((1, tn), lambda i, j: (0, j)),
        ],
        out_specs=out_spec,
        out_shape=out_shape,
        scratch_shapes=scratch,
        compiler_params=_params(("arbitrary", "arbitrary")),
        name="inproj_u" if chunk_major else "inproj",
    )(x, g, w_bf16, b)


def _ssm_prep_kernel(are_ref, aim_ref, ldt_ref, btr_ref, bti_ref, cr_ref, ci_ref,
                     t_ref, wsr_ref, wsi_ref, rxr_ref, rxi_ref, a16r_ref, a16i_ref):
    lam_re = jnp.minimum(are_ref[0], -1e-4)
    lam_im = aim_ref[0]
    dt = jnp.exp(ldt_ref[0])
    lr_dt = lam_re * dt
    th = lam_im * dt
    mag = jnp.exp(lr_dt)
    ab_re = mag * jnp.cos(th)
    ab_im = mag * jnp.sin(th)
    den = lam_re * lam_re + lam_im * lam_im
    nr = ab_re - 1.0
    ni = ab_im
    coef_re = (nr * lam_re + ni * lam_im) / den
    coef_im = (ni * lam_re - nr * lam_im) / den
    btr = btr_ref[0]
    bti = bti_ref[0]
    bb_re = coef_re * btr - coef_im * bti
    bb_im = coef_re * bti + coef_im * btr

    e = lax.broadcasted_iota(I32, (CHUNK, SSM_STATE), 0).astype(F32)
    pmag = jnp.exp(e * lr_dt)
    pos_re = pmag * jnp.cos(e * th)
    pos_im = pmag * jnp.sin(e * th)
    nmag = jnp.exp(-e * lr_dt)
    neg_re = nmag * jnp.cos(e * th)
    neg_im = -nmag * jnp.sin(e * th)

    def rep(tab):
        return jnp.broadcast_to(tab[:, None, :], (CHUNK, SSM_GROUP, SSM_STATE)).reshape(CW, SSM_STATE)

    def tile(mat):
        return jnp.broadcast_to(mat[None, :, :], (CHUNK, SSM_GROUP, SSM_STATE)).reshape(CW, SSM_STATE)

    pr, pi = rep(pos_re), rep(pos_im)
    ctr, cti = tile(cr_ref[0]), tile(ci_ref[0])
    r_re = ctr * pr - cti * pi
    r_im = ctr * pi + cti * pr
    qr, qi = rep(neg_re), rep(neg_im)
    btr_t, bti_t = tile(bb_re), tile(bb_im)
    l_re = btr_t * qr - bti_t * qi
    l_im = btr_t * qi + bti_t * qr

    def split(v):
        head = v.astype(BF16)
        return head, (v - head.astype(F32)).astype(BF16)

    def dot_nt3(a, b):
        (ah, al), (bh, bl) = split(a), split(b)
        return _dot_nt(ah, bh) + _dot_nt(al, bh) + _dot_nt(ah, bl)

    tm = dot_nt3(l_re, r_re) - dot_nt3(l_im, r_im)
    srow = lax.broadcasted_iota(I32, (CW, CW), 0) // SSM_GROUP
    tcol = lax.broadcasted_iota(I32, (CW, CW), 1) // SSM_GROUP
    t_ref[0] = jnp.where(tcol >= srow, tm, 0.0).astype(BF16)

    a15r = pos_re[CHUNK - 1:CHUNK, :]
    a15i = pos_im[CHUNK - 1:CHUNK, :]
    wsr_ref[0] = (l_re * a15r - l_im * a15i).astype(BF16)
    wsi_ref[0] = (l_re * a15i + l_im * a15r).astype(BF16)
    rxr_ref[0] = (r_re * ab_re - r_im * ab_im).astype(BF16)
    rxi_ref[0] = (-(r_re * ab_im + r_im * ab_re)).astype(BF16)
    m16 = jnp.exp(float(CHUNK) * lr_dt)
    a16r_ref[0] = m16 * jnp.cos(float(CHUNK) * th)
    a16i_ref[0] = m16 * jnp.sin(float(CHUNK) * th)


def _ssm_prep(a_re, a_im, log_dt, bt_re, bt_im, c_re, c_im):
    g = a_re.shape[0]
    vec = pl.BlockSpec((1, 1, SSM_STATE), lambda i: (i, 0, 0))
    mat = pl.BlockSpec((1, SSM_GROUP, SSM_STATE), lambda i: (i, 0, 0))
    wide = pl.BlockSpec((1, CW, SSM_STATE), lambda i: (i, 0, 0))
    return pl.pallas_call(
        _ssm_prep_kernel,
        grid=(g,),
        in_specs=[vec, vec, pl.BlockSpec((1, 1, 1), lambda i: (i, 0, 0)), mat, mat, mat, mat],
        out_specs=[pl.BlockSpec((1, CW, CW), lambda i: (i, 0, 0)), wide, wide, wide, wide, vec, vec],
        out_shape=[
            jax.ShapeDtypeStruct((g, CW, CW), BF16),
            jax.ShapeDtypeStruct((g, CW, SSM_STATE), BF16),
            jax.ShapeDtypeStruct((g, CW, SSM_STATE), BF16),
            jax.ShapeDtypeStruct((g, CW, SSM_STATE), BF16),
            jax.ShapeDtypeStruct((g, CW, SSM_STATE), BF16),
            jax.ShapeDtypeStruct((g, 1, SSM_STATE), F32),
            jax.ShapeDtypeStruct((g, 1, SSM_STATE), F32),
        ],
        compiler_params=_params(("arbitrary",)),
        name="ssm_prep",
    )(a_re.reshape(g, 1, SSM_STATE), a_im.reshape(g, 1, SSM_STATE), log_dt.reshape(g, 1, 1),
      bt_re, bt_im, c_re, c_im)


def _gelu_tanh(x):
    c = math.sqrt(2.0 / math.pi)
    return x * (0.5 * (1.0 + jnp.tanh(c * (x + 0.044715 * (x * x * x)))))


def _ssm_kernel(u_ref, t_ref, wsr_ref, wsi_ref, rxr_ref, rxi_ref, a16r_ref, a16i_ref, d_ref,
                z_ref, sr_ref, si_ref, ug_ref, zg_ref):
    nk = u_ref.shape[1]
    for j in range(GROUP_BLOCK):
        ug_ref[j] = jnp.concatenate(
            [u_ref[s, :, j * SSM_GROUP:(j + 1) * SSM_GROUP] for s in range(CHUNK)], axis=1)
    u_ref = ug_ref
    for j in range(GROUP_BLOCK):
        u = u_ref[j]
        sr_ref[j * nk:(j + 1) * nk, :] = _dot(u, wsr_ref[j])
        si_ref[j * nk:(j + 1) * nk, :] = _dot(u, wsi_ref[j])

    ar = a16r_ref[...]
    ai = a16i_ref[...]

    def step(k, carry):
        xr, xi = carry
        rows = pl.ds(k, GROUP_BLOCK, stride=nk)
        sr = sr_ref[rows, :]
        si = si_ref[rows, :]
        sr_ref[rows, :] = xr
        si_ref[rows, :] = xi
        return (ar * xr - ai * xi + sr, ar * xi + ai * xr + si)

    zero = jnp.zeros((GROUP_BLOCK, SSM_STATE), F32)
    lax.fori_loop(0, nk, step, (zero, zero), unroll=4)

    for j in range(GROUP_BLOCK):
        u = u_ref[j]
        xr = sr_ref[j * nk:(j + 1) * nk, :].astype(BF16)
        xi = si_ref[j * nk:(j + 1) * nk, :].astype(BF16)
        y = (_dot(u, t_ref[j]) + _dot_nt(xr, rxr_ref[j]) + _dot_nt(xi, rxi_ref[j])
             + d_ref[j] * u.astype(F32))
        zg_ref[j] = _gelu_tanh(y).astype(zg_ref.dtype)

    for t in range(CHUNK):
        z_ref[t] = jnp.concatenate(
            [zg_ref[j, :, t * SSM_GROUP:(t + 1) * SSM_GROUP] for j in range(GROUP_BLOCK)], axis=1)


def _ssm(u3, tmat, wsr, wsi, rxr, rxi, a16r, a16i, d_tiled):
    _, nk, width = u3.shape
    gb = GROUP_BLOCK
    gl = gb * SSM_GROUP
    blk3 = lambda a, b: pl.BlockSpec((gb, a, b), lambda i: (i, 0, 0))
    io = pl.BlockSpec((CHUNK, nk, gl), lambda i: (0, 0, i))
    return pl.pallas_call(
        _ssm_kernel,
        grid=(width // gl,),
        in_specs=[io, blk3(CW, CW), blk3(CW, SSM_STATE), blk3(CW, SSM_STATE),
                  blk3(CW, SSM_STATE), blk3(CW, SSM_STATE),
                  pl.BlockSpec((gb, SSM_STATE), lambda i: (i, 0)),
                  pl.BlockSpec((gb, SSM_STATE), lambda i: (i, 0)),
                  blk3(1, CW)],
        out_specs=io,
        out_shape=jax.ShapeDtypeStruct(u3.shape, BF16),
        scratch_shapes=[pltpu.VMEM((gb * nk, SSM_STATE), F32), pltpu.VMEM((gb * nk, SSM_STATE), F32),
                        pltpu.VMEM((gb, nk, CW), BF16), pltpu.VMEM((gb, nk, CW), BF16)],
        compiler_params=_params(("arbitrary",)),
        name="ssm",
    )(u3, tmat, wsr, wsi, rxr, rxi, a16r, a16i, d_tiled)


def _glu_kernel(z_ref, perm_ref, wg_ref, bg_ref, wb_ref, g0_ref, o_ref):
    tm = o_ref.shape[0]
    z = _dot(perm_ref[...], z_ref[...].reshape(tm, SSM_WIDTH)).astype(BF16)
    h = _dot(z, wg_ref[...]) + bg_ref[...]
    ga = h[:, :SSM_WIDTH]
    gb = h[:, SSM_WIDTH:]
    a = (ga * _sigmoid(gb)).astype(BF16)
    y = _dot(a, wb_ref[...])
    o_ref[...] = (_sigmoid(g0_ref[...].astype(F32)) * y).astype(o_ref.dtype)


def _glu(z3, w_glu, b_glu, w_br, proj, tm=512):
    t = z3.shape[0] * z3.shape[1]
    nk = tm // CHUNK
    r = np.arange(tm)
    perm = np.zeros((tm, tm), np.float32)
    perm[r, (r % CHUNK) * nk + r // CHUNK] = 1.0
    return pl.pallas_call(
        _glu_kernel,
        grid=(t // tm,),
        in_specs=[
            pl.BlockSpec((CHUNK, nk, SSM_WIDTH), lambda i: (0, i, 0)),
            pl.BlockSpec((tm, tm), lambda i: (0, 0)),
            pl.BlockSpec((SSM_WIDTH, 2 * SSM_WIDTH), lambda i: (0, 0)),
            pl.BlockSpec((1, 2 * SSM_WIDTH), lambda i: (0, 0)),
            pl.BlockSpec((SSM_WIDTH, D_MODEL), lambda i: (0, 0)),
            pl.BlockSpec((tm, D_MODEL), lambda i: (i, COL_G0 // D_MODEL)),
        ],
        out_specs=pl.BlockSpec((tm, D_MODEL), lambda i: (i, 0)),
        out_shape=jax.ShapeDtypeStruct((t, D_MODEL), BF16),
        compiler_params=_params(("arbitrary",)),
        name="glu",
    )(z3, jnp.asarray(perm, BF16), w_glu, b_glu, w_br, proj)


def _rope_pattern():
    half = ROPE_DIM // 2
    inv_freq = (np.float32(ROPE_THETA) ** (-np.arange(half, dtype=np.float32) / np.float32(half))).astype(np.float32)
    d = np.arange(LANES) % HEAD_DIM
    rotated = d < ROPE_DIM
    pat = np.zeros((16, LANES), np.float32)
    pat[:half] = rotated[None, :] & ((d % half)[None, :] == np.arange(half)[:, None])
    pat[8] = ~rotated
    pat[9] = np.where(d < half, -1.0, 0.0)
    pat[10] = np.where((d >= half) & rotated, 1.0, 0.0)
    return inv_freq.reshape(half, 1), pat


def _attn_kernel(q_ref, k_ref, v_ref, pos_ref, freq_ref, pat_ref, sink_ref, wbr_ref, ysg_ref, g1_ref,
                 o_ref, qbuf, kbuf, vbuf, obuf):
    i = pl.program_id(0)
    tq = q_ref.shape[0]
    nw = tq // WINDOW
    half = ROPE_DIM // 2

    @pl.when(i == 0)
    def _():
        kbuf[:, 0:WINDOW, :] = jnp.zeros((2 * N_KV_HEADS, WINDOW, LANES), BF16)
        vbuf[:, 0:WINDOW, :] = jnp.zeros((2 * N_KV_HEADS, WINDOW, LANES), BF16)

    ang = freq_ref[...] * pos_ref[...].astype(F32)
    spread = lambda tab: lax.dot_general(tab, pat_ref[0:8, :], (((0,), (0,)), ((), ())),
                                         preferred_element_type=F32, precision=HIGHEST)
    cs = spread(jnp.cos(ang)) + pat_ref[8:9, :]
    sn = spread(jnp.sin(ang))
    c_up = sn * pat_ref[9:10, :]
    c_dn = sn * pat_ref[10:11, :]

    def rope(x):
        return (x * cs + pltpu.roll(x, LANES - half, 1) * c_up + pltpu.roll(x, half, 1) * c_dn)

    low = lax.broadcasted_iota(I32, (tq, LANES), 1) < HEAD_DIM

    def split_heads(buf, cb, x):
        xs = pltpu.roll(x, HEAD_DIM, 1)
        zero = jnp.zeros_like(x)
        buf[4 * cb + 0, WINDOW:, :] = jnp.where(low, x, zero).astype(BF16)
        buf[4 * cb + 1, WINDOW:, :] = jnp.where(low, zero, xs).astype(BF16)
        buf[4 * cb + 2, WINDOW:, :] = jnp.where(low, xs, zero).astype(BF16)
        buf[4 * cb + 3, WINDOW:, :] = jnp.where(low, zero, x).astype(BF16)

    for cb in range(Q_WIDTH // LANES):
        sl = slice(cb * LANES, (cb + 1) * LANES)
        qbuf[:, sl] = rope(q_ref[:, sl].astype(F32)).astype(BF16)
    for cb in range(KV_WIDTH // LANES):
        sl = slice(cb * LANES, (cb + 1) * LANES)
        split_heads(kbuf, cb, rope(k_ref[:, sl].astype(F32)))
        split_heads(vbuf, cb, v_ref[:, sl].astype(F32))

    qi = lax.broadcasted_iota(I32, (WINDOW, 2 * WINDOW), 0)
    kj = lax.broadcasted_iota(I32, (WINDOW, 2 * WINDOW), 1)
    dist = qi + WINDOW - kj
    in_band = (dist >= 0) & (dist < WINDOW)
    cur_only = kj >= WINDOW
    sinks = sink_ref[...]
    scale = HEAD_DIM ** -0.5

    def window(w, carry):
        r0 = pl.multiple_of(w * WINDOW, WINDOW)
        rows = pl.ds(r0, 2 * WINDOW)
        not_first = (i * nw + w) > 0
        mask = in_band & (cur_only | not_first)

        def probs(qp, part, h):
            s = _dot_nt(qp, kbuf[2 * (h // Q_PER_KV) + part, rows, :]) * scale
            s = jnp.where(mask, s, -jnp.inf)
            sink = sinks[:, h:h + 1]
            m = jnp.maximum(jnp.max(s, axis=-1, keepdims=True), sink)
            p = jnp.exp(s - m)
            denom = jnp.sum(p, axis=-1, keepdims=True) + jnp.exp(sink - m)
            return p.astype(BF16), 1.0 / denom

        for a in range(N_Q_HEADS // 2):
            kv = (2 * a) // Q_PER_KV
            qp = qbuf[pl.ds(r0, WINDOW), a * LANES:(a + 1) * LANES]
            p_lo, r_lo = probs(qp, 0, 2 * a)
            p_hi, r_hi = probs(qp, 1, 2 * a + 1)
            o = _dot(p_lo, vbuf[2 * kv, rows, :]) * r_lo + _dot(p_hi, vbuf[2 * kv + 1, rows, :]) * r_hi
            obuf[pl.ds(r0, WINDOW), a * LANES:(a + 1) * LANES] = o.astype(BF16)
        return carry

    lax.fori_loop(0, nw, window, 0)

    kbuf[:, 0:WINDOW, :] = kbuf[:, tq:tq + WINDOW, :]
    vbuf[:, 0:WINDOW, :] = vbuf[:, tq:tq + WINDOW, :]

    y = _dot(obuf[...], wbr_ref[...])
    o_ref[...] = (ysg_ref[...].astype(F32) + _sigmoid(g1_ref[...].astype(F32)) * y).astype(o_ref.dtype)


def _attn(proj, pos_row, sinks, w_br, ysg, tq=512):
    t = proj.shape[0]
    freq, pat = (jnp.asarray(a) for a in _rope_pattern())
    return pl.pallas_call(
        _attn_kernel,
        grid=(t // tq,),
        in_specs=[
            pl.BlockSpec((tq, Q_WIDTH), lambda i: (i, COL_Q // Q_WIDTH)),
            pl.BlockSpec((tq, KV_WIDTH), lambda i: (i, COL_K // KV_WIDTH)),
            pl.BlockSpec((tq, KV_WIDTH), lambda i: (i, COL_V // KV_WIDTH)),
            pl.BlockSpec((1, tq), lambda i: (0, i)),
            pl.BlockSpec((ROPE_DIM // 2, 1), lambda i: (0, 0)),
            pl.BlockSpec((16, LANES), lambda i: (0, 0)),
            pl.BlockSpec((1, N_Q_HEADS), lambda i: (0, 0)),
            pl.BlockSpec((Q_WIDTH, D_MODEL), lambda i: (0, 0)),
            pl.BlockSpec((tq, D_MODEL), lambda i: (i, 0)),
            pl.BlockSpec((tq, D_MODEL), lambda i: (i, COL_G1 // D_MODEL)),
        ],
        out_specs=pl.BlockSpec((tq, D_MODEL), lambda i: (i, 0)),
        out_shape=jax.ShapeDtypeStruct((t, D_MODEL), BF16),
        scratch_shapes=[
            pltpu.VMEM((tq, Q_WIDTH), BF16),
            pltpu.VMEM((2 * N_KV_HEADS, tq + WINDOW, LANES), BF16),
            pltpu.VMEM((2 * N_KV_HEADS, tq + WINDOW, LANES), BF16),
            pltpu.VMEM((tq, Q_WIDTH), BF16),
        ],
        compiler_params=_params(("arbitrary",)),
        name="attn",
    )(proj, proj, proj, pos_row, freq, pat, sinks, w_br, ysg, proj)


def _oproj_kernel(x_ref, mix_ref, wo_ref, g_ref, wrh_ref, wrl_ref, br_ref, h_ref, hp_ref, lt_ref):
    h = x_ref[...] + _dot(mix_ref[...], wo_ref[...])
    h_ref[...] = h
    ms = jnp.mean(h * h, axis=-1, keepdims=True)
    hn = h * lax.rsqrt(ms + EPS) * g_ref[...]
    _store_token_tiles(hp_ref, hn)
    hn_hi = hn.astype(BF16)
    hn_lo = (hn - hn_hi.astype(F32)).astype(BF16)
    logits = (_dot(hn_hi, wrh_ref[...]) + _dot(hn_lo, wrh_ref[...]) + _dot(hn_hi, wrl_ref[...])
              + br_ref[...])
    lt_ref[...] = logits.T


def _oproj(x, mixed, w_o, g, w_router, b_router, tm=512):
    t = x.shape[0]
    w_router_hi = w_router.astype(BF16)
    w_router_hi_rest = (w_router - w_router_hi.astype(F32)).astype(BF16)
    return pl.pallas_call(
        _oproj_kernel,
        grid=(t // tm,),
        in_specs=[
            pl.BlockSpec((tm, D_MODEL), lambda i: (i, 0)),
            pl.BlockSpec((tm, D_MODEL), lambda i: (i, 0)),
            pl.BlockSpec((D_MODEL, D_MODEL), lambda i: (0, 0)),
            pl.BlockSpec((1, D_MODEL), lambda i: (0, 0)),
            pl.BlockSpec((D_MODEL, LANES), lambda i: (0, 0)),
            pl.BlockSpec((D_MODEL, LANES), lambda i: (0, 0)),
            pl.BlockSpec((1, LANES), lambda i: (0, 0)),
        ],
        out_specs=[
            pl.BlockSpec((tm, D_MODEL), lambda i: (i, 0)),
            pl.BlockSpec((tm * SUBLANES, LANES), lambda i: (i, 0)),
            pl.BlockSpec((LANES, tm), lambda i: (0, i)),
        ],
        out_shape=[
            jax.ShapeDtypeStruct((t, D_MODEL), F32),
            jax.ShapeDtypeStruct((t * SUBLANES, LANES), jnp.uint32),
            jax.ShapeDtypeStruct((LANES, t), F32),
        ],
        compiler_params=_params(("arbitrary",)),
        name="oproj",
    )(x, mixed, w_o, g, w_router_hi, w_router_hi_rest, b_router)


ROUTE_CHUNK = 256


def _route_kernel(lt_ref, dest_ref, wts_ref, bexp_ref, nvalid_ref, eid_ref, rank_ref):
    t = lt_ref.shape[1]
    nc = t // ROUTE_CHUNK
    r8 = lax.broadcasted_iota(I32, (N_GROUPS, ROUTE_CHUNK), 0)
    r64 = lax.broadcasted_iota(I32, (N_EXPERTS, ROUTE_CHUNK), 0)

    def pick(c, carry):
        cols = pl.ds(pl.multiple_of(c * ROUTE_CHUNK, ROUTE_CHUNK), ROUTE_CHUNK)
        lg = lt_ref[0:N_GROUPS, cols]
        m = jnp.max(lg, axis=0, keepdims=True)
        ssum = jnp.sum(jnp.exp(lg - m), axis=0, keepdims=True)
        p_grp = 1.0 / ssum
        grp = jnp.min(jnp.where(lg == m, r8, N_GROUPS), axis=0, keepdims=True)
        le = lt_ref[N_GROUPS:N_GROUPS + N_EXPERTS, cols]
        leg = jnp.where((r64 // EXPERTS_PER_GROUP) == grp, le, -jnp.inf)
        m1 = jnp.max(leg, axis=0, keepdims=True)
        i1 = jnp.min(jnp.where(leg == m1, r64, N_EXPERTS), axis=0, keepdims=True)
        leg2 = jnp.where(r64 == i1, -jnp.inf, leg)
        m2 = jnp.max(leg2, axis=0, keepdims=True)
        i2 = jnp.min(jnp.where(leg2 == m2, r64, N_EXPERTS), axis=0, keepdims=True)
        ex = jnp.exp(m2 - m1)
        eid_ref[0:1, cols] = i1
        eid_ref[1:2, cols] = i2
        wts_ref[0:1, cols] = p_grp / (1.0 + ex)
        wts_ref[1:2, cols] = p_grp * ex / (1.0 + ex)
        return carry

    lax.fori_loop(0, nc, pick, 0)

    a_row = lax.broadcasted_iota(I32, (ROUTE_CHUNK, ROUTE_CHUNK), 0)
    a_col = lax.broadcasted_iota(I32, (ROUTE_CHUNK, ROUTE_CHUNK), 1)
    before = (a_row < a_col).astype(BF16)

    def count(n, carry):
        j = n // nc
        c = n - j * nc
        cols = pl.ds(pl.multiple_of(c * ROUTE_CHUNK, ROUTE_CHUNK), ROUTE_CHUNK)
        oh = r64 == eid_ref[pl.ds(j, 1), cols]
        ohf = oh.astype(F32)
        pref = _dot(ohf.astype(BF16), before) + carry
        rank_ref[pl.ds(j, 1), cols] = jnp.sum(jnp.where(oh, pref, 0.0), axis=0, keepdims=True)
        return carry + jnp.sum(ohf, axis=1, keepdims=True)

    counts = lax.fori_loop(0, TOP_K * nc, count, jnp.zeros((N_EXPERTS, 1), F32))

    padded = jnp.floor((counts + (MOE_BLOCK - 1)) * (1.0 / MOE_BLOCK)) * MOE_BLOCK
    e_row = lax.broadcasted_iota(I32, (N_EXPERTS, N_EXPERTS), 0)
    e_col = lax.broadcasted_iota(I32, (N_EXPERTS, N_EXPERTS), 1)
    incl = (e_col <= e_row).astype(F32)
    pad_end = _dot(incl, jnp.broadcast_to(padded, (N_EXPERTS, LANES)), precision=HIGHEST)[:, 0:1]
    pad_start = pad_end - padded

    def place(n, carry):
        j = n // nc
        c = n - j * nc
        cols = pl.ds(pl.multiple_of(c * ROUTE_CHUNK, ROUTE_CHUNK), ROUTE_CHUNK)
        oh = r64 == eid_ref[pl.ds(j, 1), cols]
        start = jnp.sum(jnp.where(oh, pad_start, 0.0), axis=0, keepdims=True)
        dest_ref[pl.ds(j, 1), cols] = (start + rank_ref[pl.ds(j, 1), cols]).astype(I32)
        return carry

    lax.fori_loop(0, TOP_K * nc, place, 0)

    b0 = (lax.broadcasted_iota(I32, (N_EXPERTS, bexp_ref.shape[1]), 1) * MOE_BLOCK).astype(F32)
    n_done = jnp.sum((pad_end <= b0).astype(F32), axis=0, keepdims=True)
    bexp_ref[...] = jnp.minimum(n_done, float(N_EXPERTS - 1)).astype(I32)
    live = jnp.minimum(pad_start + counts, b0 + MOE_BLOCK) - jnp.maximum(pad_start, b0)
    nvalid_ref[...] = jnp.sum(jnp.maximum(live, 0.0), axis=0, keepdims=True).astype(I32)


def _route(logits_t, n_blocks_pad):
    t = logits_t.shape[1]
    return pl.pallas_call(
        _route_kernel,
        out_shape=[
            jax.ShapeDtypeStruct((TOP_K, t), I32),
            jax.ShapeDtypeStruct((TOP_K, t), F32),
            jax.ShapeDtypeStruct((1, n_blocks_pad), I32),
            jax.ShapeDtypeStruct((1, n_blocks_pad), I32),
        ],
        scratch_shapes=[pltpu.VMEM((TOP_K, t), I32), pltpu.VMEM((TOP_K, t), F32)],
        compiler_params=pltpu.CompilerParams(vmem_limit_bytes=VMEM_LIMIT),
        name="route",
    )(logits_t)


def _invmap_kernel(dest_ref, rt_ref):
    n_rows = rt_ref.shape[0]
    t = dest_ref.shape[0] // TOP_K

    def clear(r, c):
        rt_ref[r] = 0
        return c

    lax.fori_loop(0, n_rows, clear, 0, unroll=16)

    def put(tok, c):
        rt_ref[dest_ref[tok]] = tok
        rt_ref[dest_ref[t + tok]] = tok
        return c

    lax.fori_loop(0, t, put, 0, unroll=8)


def _invmap(dest_flat, n_rows):
    return pl.pallas_call(
        _invmap_kernel,
        in_specs=[pl.BlockSpec(memory_space=pltpu.SMEM)],
        out_specs=pl.BlockSpec(memory_space=pltpu.SMEM),
        out_shape=jax.ShapeDtypeStruct((n_rows,), I32),
        name="invmap",
    )(dest_flat)


GATHER_SLOTS = 4


def _row_copy(src_hbm, src_row, dst_buf, slot, dst_row, sem):
    return pltpu.make_async_copy(src_hbm.at[pl.ds(pl.multiple_of(src_row * SUBLANES, SUBLANES), SUBLANES), :],
                                 dst_buf.at[slot, pl.ds(dst_row * SUBLANES, SUBLANES), :], sem.at[slot])


def _rows_copy(src_hbm, dst_buf, slot, rows, sem):
    n = rows * SUBLANES
    return pltpu.make_async_copy(src_hbm.at[pl.ds(0, n), :], dst_buf.at[slot, pl.ds(0, n), :], sem.at[slot])


def _expert_kernel(bexp_ref, nvalid_ref, rt_ref, hp_hbm, wg_hbm, wu_hbm, wd_hbm, ys_ref,
                   xbuf, wf_g, wf_u, wf_d, wb_g, wb_u, wb_d, ord_ref, xsem, wsem):
    b = pl.program_id(0)
    nb = pl.num_programs(0) - 1

    def live_rows(blk):
        return jnp.where(nvalid_ref[blk] > 0, MOE_BLOCK, 0)

    def expert_of(blk):
        return bexp_ref[jnp.minimum(blk, nb - 1)]

    def next_owner(blk, e):
        return lax.while_loop(lambda j: (j < nb) & (expert_of(j) == e), lambda j: j + 1, blk)

    def weight_copies(e, slot):
        return (pltpu.make_async_copy(wg_hbm.at[e], wf_g.at[slot], wsem.at[slot, 0]),
                pltpu.make_async_copy(wu_hbm.at[e], wf_u.at[slot], wsem.at[slot, 1]),
                pltpu.make_async_copy(wd_hbm.at[e], wf_d.at[slot], wsem.at[slot, 2]))

    @pl.when(b == 0)
    def _():
        ord_ref[0] = 0
        e0 = bexp_ref[0]
        for c in weight_copies(e0, 0):
            c.start(priority=1)
        n1 = next_owner(1, e0)

        @pl.when(n1 < nb)
        def _():
            for c in weight_copies(expert_of(n1), 1):
                c.start(priority=1)

    def gather(blk):
        base = blk * MOE_BLOCK
        slot = blk % GATHER_SLOTS

        def body(i, c):
            for s in range(SUBLANES):
                r = i * SUBLANES + s
                _row_copy(hp_hbm, rt_ref[base + r], xbuf, slot, r, xsem).start()
            return c

        lax.fori_loop(0, jnp.where(blk < nb, live_rows(jnp.minimum(blk, nb - 1)) // SUBLANES, 0), body, 0)

    @pl.when(b == 0)
    def _():
        for blk in range(GATHER_SLOTS - 2):
            gather(blk)

    gather(b + GATHER_SLOTS - 2)

    @pl.when(b > 0)
    def _():
        blk = b - 1
        e = bexp_ref[blk]
        first = (blk == 0) | (e != bexp_ref[jnp.maximum(blk - 1, 0)])

        @pl.when(first)
        def _():
            n = ord_ref[0]
            wslot = n % 2
            ord_ref[0] = n + 1
            for c in weight_copies(e, wslot):
                c.wait()
            wb_g[...] = wf_g[wslot].astype(BF16)
            wb_u[...] = wf_u[wslot].astype(BF16)
            wb_d[...] = wf_d[wslot].astype(BF16)
            n1 = next_owner(blk + 1, e)
            n2 = next_owner(n1 + 1, expert_of(n1))

            @pl.when((n1 < nb) & (n2 < nb))
            def _():
                for c in weight_copies(expert_of(n2), wslot):
                    c.start(priority=1)

        slot = blk % GATHER_SLOTS
        rows = live_rows(blk)

        @pl.when(rows > 0)
        def _():
            _rows_copy(hp_hbm, xbuf, slot, rows, xsem).wait()
            lo, hi = (v.astype(BF16) for v in _load_token_tiles(xbuf, slot, 0, MOE_BLOCK))
            g = _dot(lo, wb_g[:HALF, :]) + _dot(hi, wb_g[HALF:, :])
            u = _dot(lo, wb_u[:HALF, :]) + _dot(hi, wb_u[HALF:, :])
            h = (g * _sigmoid(g) * u).astype(BF16)
            y = _dot(h, wb_d[...])
            _store_token_tiles(ys_ref, y)

        @pl.when(rows == 0)
        def _():
            _store_token_tiles(ys_ref, jnp.zeros((MOE_BLOCK, D_MODEL), F32))


def _experts(bexp, nvalid, row_tok, hn_packed, w_gate, w_up, w_down, n_blocks):
    grid_spec = pltpu.PrefetchScalarGridSpec(
        num_scalar_prefetch=3,
        grid=(n_blocks + 1,),
        in_specs=[pl.BlockSpec(memory_space=pl.ANY)] * 4,
        out_specs=pl.BlockSpec((MOE_BLOCK * SUBLANES, LANES), lambda b, be, nv, rt: (jnp.maximum(b - 1, 0), 0)),
        scratch_shapes=[
            pltpu.VMEM((GATHER_SLOTS, MOE_BLOCK * SUBLANES, LANES), jnp.uint32),
            pltpu.VMEM((2, D_MODEL, D_EXPERT), F32), pltpu.VMEM((2, D_MODEL, D_EXPERT), F32),
            pltpu.VMEM((2, D_EXPERT, D_MODEL), F32),
            pltpu.VMEM((D_MODEL, D_EXPERT), BF16), pltpu.VMEM((D_MODEL, D_EXPERT), BF16),
            pltpu.VMEM((D_EXPERT, D_MODEL), BF16),
            pltpu.SMEM((1,), I32),
            pltpu.SemaphoreType.DMA((GATHER_SLOTS,)), pltpu.SemaphoreType.DMA((2, 3)),
        ],
    )
    return pl.pallas_call(
        _expert_kernel,
        grid_spec=grid_spec,
        out_shape=jax.ShapeDtypeStruct((n_blocks * MOE_BLOCK * SUBLANES, LANES), jnp.uint32),
        compiler_params=_params(("arbitrary",)),
        name="experts",
    )(bexp, nvalid, row_tok, hn_packed, w_gate, w_up, w_down)


def _combine_kernel(dest_ref, ys_hbm, h_ref, w_ref, g_ref, o_ref, ybuf, sem):
    i = pl.program_id(0)
    n = pl.num_programs(0)
    tq = h_ref.shape[0]
    t = n * tq

    def start_gather(blk, slot):
        base = blk * tq

        def body(j, c):
            for s in range(SUBLANES):
                r = j * SUBLANES + s
                _row_copy(ys_hbm, dest_ref[base + r], ybuf, slot, r, sem).start()
                _row_copy(ys_hbm, dest_ref[t + base + r], ybuf, slot, tq + r, sem).start()
            return c

        lax.fori_loop(0, tq // SUBLANES, body, 0)

    @pl.when(i == 0)
    def _():
        start_gather(0, 0)

    @pl.when(i + 1 < n)
    def _():
        start_gather(i + 1, (i + 1) % 2)

    slot = i % 2
    _rows_copy(ys_hbm, ybuf, slot, TOP_K * tq, sem).wait()
    y0_lo, y0_hi = _load_token_tiles(ybuf, slot, 0, tq)
    y1_lo, y1_hi = _load_token_tiles(ybuf, slot, tq, tq)
    w = w_ref[...]
    w0 = w[:, 0:1]
    w1 = w[:, 1:2]
    h = h_ref[...]
    lo = h[:, :HALF] + w0 * y0_lo + w1 * y1_lo
    hi = h[:, HALF:] + w0 * y0_hi + w1 * y1_hi
    ms = (jnp.sum(lo * lo, axis=-1, keepdims=True) + jnp.sum(hi * hi, axis=-1, keepdims=True)) * (1.0 / D_MODEL)
    inv = lax.rsqrt(ms + EPS)
    g = g_ref[...]
    o_ref[:, :HALF] = lo * inv * g[:, :HALF]
    o_ref[:, HALF:] = hi * inv * g[:, HALF:]


def _combine(dest_flat, ys_packed, h1, wts_tok, g_final, tq=256):
    t = h1.shape[0]
    grid_spec = pltpu.PrefetchScalarGridSpec(
        num_scalar_prefetch=1,
        grid=(t // tq,),
        in_specs=[
            pl.BlockSpec(memory_space=pl.ANY),
            pl.BlockSpec((tq, D_MODEL), lambda i, d: (i, 0)),
            pl.BlockSpec((tq, TOP_K), lambda i, d: (i, 0)),
            pl.BlockSpec((1, D_MODEL), lambda i, d: (0, 0)),
        ],
        out_specs=pl.BlockSpec((tq, D_MODEL), lambda i, d: (i, 0)),
        scratch_shapes=[pltpu.VMEM((2, TOP_K * tq * SUBLANES, LANES), jnp.uint32),
                        pltpu.SemaphoreType.DMA((2,))],
    )
    return pl.pallas_call(
        _combine_kernel,
        grid_spec=grid_spec,
        out_shape=jax.ShapeDtypeStruct((t, D_MODEL), F32),
        compiler_params=_params(("arbitrary",)),
        name="combine",
    )(dest_flat, ys_packed, h1, wts_tok, g_final)


def _mixers(h, positions, norm_g, w_in, b_in, a_re, a_im, log_dt, b_re, b_im, c_re, c_im, d_skip,
            w_glu, b_glu, w_br_ssm, sinks, w_br_attn):
    t = h.shape[0]
    ngate = 2 * D_MODEL
    g = norm_g.reshape(1, D_MODEL)
    w_main = jnp.concatenate([w_in[:, IN_WIDTH - ngate:], w_in[:, SSM_WIDTH:IN_WIDTH - ngate]], axis=1).astype(BF16)
    b_main = jnp.concatenate([b_in[IN_WIDTH - ngate:], b_in[SSM_WIDTH:IN_WIDTH - ngate]]).reshape(1, -1)
    proj = _inproj(h, g, w_main, b_main, chunk_major=False)
    u3 = _inproj(h, g, w_in[:, :SSM_WIDTH].astype(BF16), b_in[:SSM_WIDTH].reshape(1, -1), chunk_major=True)

    tmat, wsr, wsi, rxr, rxi, a16r, a16i = _ssm_prep(
        a_re, a_im, log_dt, b_re.transpose(0, 2, 1), b_im.transpose(0, 2, 1), c_re, c_im)
    d_tiled = jnp.tile(d_skip, (1, CHUNK)).reshape(SSM_GROUPS, 1, CW)
    z3 = _ssm(u3, tmat, wsr, wsi, rxr, rxi, a16r.reshape(SSM_GROUPS, SSM_STATE),
              a16i.reshape(SSM_GROUPS, SSM_STATE), d_tiled)

    ysg = _glu(z3, w_glu.astype(BF16), b_glu.reshape(1, -1), w_br_ssm.astype(BF16), proj)
    return _attn(proj, positions.reshape(1, t), sinks.reshape(1, N_Q_HEADS), w_br_attn.astype(BF16), ysg)


def _moe_tail(h, mixed, w_o, norm_ffn_g, w_rg, b_rg, w_re, b_re, w_gate, w_up, w_down, norm_final_g):
    t = h.shape[0]
    n_route = N_GROUPS + N_EXPERTS
    w_router = jnp.concatenate([w_rg, w_re, jnp.zeros((D_MODEL, LANES - n_route), F32)], axis=1)
    b_router = jnp.concatenate([b_rg, b_re, jnp.zeros((LANES - n_route,), F32)]).reshape(1, LANES)
    h1, hn_packed, logits_t = _oproj(h, mixed, w_o.astype(BF16), norm_ffn_g.reshape(1, D_MODEL),
                                     w_router, b_router)
    n_assign = t * TOP_K
    n_blocks = -(-(n_assign + N_EXPERTS * (MOE_BLOCK - 1)) // MOE_BLOCK)
    n_blocks_pad = -(-n_blocks // LANES) * LANES
    dest, wts, bexp, nvalid = _route(logits_t, n_blocks_pad)
    dest_flat = dest.reshape(n_assign)
    row_tok = _invmap(dest_flat, n_blocks * MOE_BLOCK)
    ys_packed = _experts(bexp[0, :n_blocks], nvalid[0, :n_blocks], row_tok, hn_packed, w_gate, w_up, w_down,
                         n_blocks)
    return _combine(dest_flat, ys_packed, h1, wts.T, norm_final_g.reshape(1, D_MODEL))


def kernel(x, positions, norm_mix_g, w_in, b_in, ssm_a_re, ssm_a_im, ssm_log_dt, ssm_b_re, ssm_b_im, ssm_c_re, ssm_c_im, ssm_d, w_glu, b_glu, w_br_ssm, attn_sinks, w_br_attn, w_o, norm_ffn_g, w_router_group, b_router_group, w_router_expert, b_router_expert, w_exp_gate, w_exp_up, w_exp_down, norm_final_g):
    bsz, seq, d = x.shape
    assert bsz == 1 and d == D_MODEL and norm_mix_g.shape[0] == 1
    h = x.reshape(seq, d)
    mixed = _mixers(h, positions, norm_mix_g[0], w_in[0], b_in[0], ssm_a_re[0], ssm_a_im[0], ssm_log_dt[0],
                    ssm_b_re[0], ssm_b_im[0], ssm_c_re[0], ssm_c_im[0], ssm_d[0], w_glu[0], b_glu[0],
                    w_br_ssm[0], attn_sinks[0], w_br_attn[0])
    out = _moe_tail(h, mixed, w_o[0], norm_ffn_g[0], w_router_group[0], b_router_group[0],
                    w_router_expert[0], b_router_expert[0], w_exp_gate[0], w_exp_up[0], w_exp_down[0],
                    norm_final_g)
    return out.reshape(bsz, seq, d)
```

```python
import functools
import math

import numpy as np
import jax
import jax.numpy as jnp
from jax import lax
from jax.experimental import pallas as pl
from jax.experimental.pallas import tpu as pltpu

F32 = jnp.float32
BF16 = jnp.bfloat16
I32 = jnp.int32

D_MODEL = 2048
SSM_WIDTH = 1024
SSM_GROUP = 16
SSM_GROUPS = 64
SSM_STATE = 64
HEAD_DIM = 64
N_Q_HEADS = 16
N_KV_HEADS = 4
Q_PER_KV = 4
WINDOW = 128
ROPE_DIM = 16
ROPE_THETA = 500000.0
Q_WIDTH = 1024
KV_WIDTH = 256
IN_WIDTH = SSM_WIDTH + Q_WIDTH + 2 * KV_WIDTH + 2 * D_MODEL
N_GROUPS = 8
EXPERTS_PER_GROUP = 8
N_EXPERTS = 64
TOP_K = 2
D_EXPERT = 512
MOE_BLOCK = 128
EPS = 1e-6

CHUNK = 16
CW = CHUNK * SSM_GROUP
GROUP_BLOCK = 8
HALF = D_MODEL // 2
LANES = 128
VMEM_LIMIT = 56 * 1024 * 1024

COL_G0, COL_G1, COL_Q, COL_K, COL_V = 0, 2048, 4096, 5120, 5376

HIGHEST = lax.Precision.HIGHEST


def _dot(a, b, precision=None):
    return jnp.dot(a, b, preferred_element_type=F32, precision=precision)


def _dot_nt(a, b, precision=None):
    return lax.dot_general(a, b, (((1,), (1,)), ((), ())), preferred_element_type=F32,
                           precision=precision)


def _dot_tn(a, b):
    return lax.dot_general(a, b, (((0,), (0,)), ((), ())), preferred_element_type=F32)


def _sigmoid(x):
    return 1.0 / (1.0 + jnp.exp(-x))


def _pack_halves(lo, hi):
    return pltpu.pack_elementwise([lo, hi], packed_dtype=BF16)


def _unpack_half(w, index):
    return pltpu.unpack_elementwise(w, index=index, packed_dtype=BF16, unpacked_dtype=F32)


SUBLANES = 8
WORD_TILES = HALF // LANES
assert WORD_TILES == SUBLANES


def _store_token_tiles(ref, x):
    rows = x.shape[0]
    for s in range(WORD_TILES):
        ref[pl.ds(s, rows, stride=SUBLANES), :] = _pack_halves(x[:, s * LANES:(s + 1) * LANES],
                                                               x[:, HALF + s * LANES:HALF + (s + 1) * LANES])


def _load_token_tiles(ref, slot, first_row, rows):
    pieces = [ref[slot, pl.ds(first_row * SUBLANES + s, rows, stride=SUBLANES), :] for s in range(WORD_TILES)]
    return (jnp.concatenate([_unpack_half(p, 0) for p in pieces], axis=1),
            jnp.concatenate([_unpack_half(p, 1) for p in pieces], axis=1))


def _params(sem, vmem=VMEM_LIMIT):
    return pltpu.CompilerParams(dimension_semantics=sem, vmem_limit_bytes=vmem)


def _inproj_kernel(x_ref, g_ref, w_ref, b_ref, o_ref, xn_ref, *rest, chunk_major):
    @pl.when(pl.program_id(1) == 0)
    def _():
        x = x_ref[...]
        ms = jnp.mean(x * x, axis=-1, keepdims=True)
        xn_ref[...] = (x * lax.rsqrt(ms + EPS) * g_ref[...]).astype(BF16)

    acc = _dot(xn_ref[...], w_ref[...].astype(BF16)) + b_ref[...]
    if not chunk_major:
        o_ref[...] = acc.astype(o_ref.dtype)
        return
    (acc_ref,) = rest
    nk = o_ref.shape[1]
    for c in range(acc_ref.shape[0]):
        acc_ref[c] = acc[:, c * LANES:(c + 1) * LANES]
    for s in range(CHUNK):
        for c in range(acc_ref.shape[0]):
            o_ref[s, :, c * LANES:(c + 1) * LANES] = acc_ref[c, pl.ds(s, nk, stride=CHUNK), :].astype(o_ref.dtype)


def _inproj(x, g, w, b, *, chunk_major, segments, tm=1024, tn=512):
    t, d = x.shape
    (c0, n0), (c1, n1) = (tuple(segments) + ((0, 0),))[:2]
    assert all(v % tn == 0 for v in (c0, n0, c1, n1))
    n = n0 + n1
    p0, q0, q1 = n0 // tn, c0 // tn, c1 // tn
    panel = lambda i, j: (0, jnp.where(j < p0, j + q0, j - p0 + q1))
    scratch = [pltpu.VMEM((tm, d), BF16)]
    if chunk_major:
        out_spec = pl.BlockSpec((CHUNK, tm // CHUNK, tn), lambda i, j: (0, i, j))
        out_shape = jax.ShapeDtypeStruct((CHUNK, t // CHUNK, n), BF16)
        scratch.append(pltpu.VMEM((tn // LANES, tm, LANES), F32))
    else:
        out_spec = pl.BlockSpec((tm, tn), lambda i, j: (i, j))
        out_shape = jax.ShapeDtypeStruct((t, n), BF16)
    return pl.pallas_call(
        functools.partial(_inproj_kernel, chunk_major=chunk_major),
        grid=(t // tm, n // tn),
        in_specs=[
            pl.BlockSpec((tm, d), lambda i, j: (i, 0)),
            pl.BlockSpec((1, d), lambda i, j: (0, 0)),
            pl.BlockSpec((d, tn), panel),
            pl.BlockSpec((1, tn), panel),
        ],
        out_specs=out_spec,
        out_shape=out_shape,
        scratch_shapes=scratch,
        compiler_params=_params(("arbitrary", "arbitrary")),
        name="inproj_u" if chunk_major else "inproj",
    )(x, g, w, b)


def _ssm_prep_kernel(are_ref, aim_ref, ldt_ref, btr_ref, bti_ref, cr_ref, ci_ref,
                     t_ref, wsr_ref, wsi_ref, rxr_ref, rxi_ref, a16r_ref, a16i_ref):
    lam_re = jnp.minimum(are_ref[0], -1e-4)
    lam_im = aim_ref[0]
    dt = jnp.exp(ldt_ref[0])
    lr_dt = lam_re * dt
    th = lam_im * dt
    mag = jnp.exp(lr_dt)
    ab_re = mag * jnp.cos(th)
    ab_im = mag * jnp.sin(th)
    den = lam_re * lam_re + lam_im * lam_im
    nr = ab_re - 1.0
    ni = ab_im
    coef_re = (nr * lam_re + ni * lam_im) / den
    coef_im = (ni * lam_re - nr * lam_im) / den
    btr = btr_ref[0]
    bti = bti_ref[0]
    bb_re = coef_re * btr - coef_im * bti
    bb_im = coef_re * bti + coef_im * btr

    e = lax.broadcasted_iota(I32, (CHUNK, SSM_STATE), 0).astype(F32)
    pmag = jnp.exp(e * lr_dt)
    pos_re = pmag * jnp.cos(e * th)
    pos_im = pmag * jnp.sin(e * th)
    nmag = jnp.exp(-e * lr_dt)
    neg_re = nmag * jnp.cos(e * th)
    neg_im = -nmag * jnp.sin(e * th)

    def rep(tab):
        return jnp.broadcast_to(tab[:, None, :], (CHUNK, SSM_GROUP, SSM_STATE)).reshape(CW, SSM_STATE)

    def tile(mat):
        return jnp.broadcast_to(mat[None, :, :], (CHUNK, SSM_GROUP, SSM_STATE)).reshape(CW, SSM_STATE)

    pr, pi = rep(pos_re), rep(pos_im)
    ctr, cti = tile(cr_ref[0]), tile(ci_ref[0])
    r_re = ctr * pr - cti * pi
    r_im = ctr * pi + cti * pr
    qr, qi = rep(neg_re), rep(neg_im)
    btr_t, bti_t = tile(bb_re), tile(bb_im)
    l_re = btr_t * qr - bti_t * qi
    l_im = btr_t * qi + bti_t * qr

    def split(v):
        head = v.astype(BF16)
        return head, (v - head.astype(F32)).astype(BF16)

    def dot_nt3(a, b):
        (ah, al), (bh, bl) = split(a), split(b)
        return _dot_nt(ah, bh) + _dot_nt(al, bh) + _dot_nt(ah, bl)

    tm = dot_nt3(l_re, r_re) - dot_nt3(l_im, r_im)
    srow = lax.broadcasted_iota(I32, (CW, CW), 0) // SSM_GROUP
    tcol = lax.broadcasted_iota(I32, (CW, CW), 1) // SSM_GROUP
    t_ref[0] = jnp.where(tcol >= srow, tm, 0.0).astype(BF16)

    a15r = pos_re[CHUNK - 1:CHUNK, :]
    a15i = pos_im[CHUNK - 1:CHUNK, :]
    wsr_ref[0] = (l_re * a15r - l_im * a15i).astype(BF16)
    wsi_ref[0] = (l_re * a15i + l_im * a15r).astype(BF16)
    rxr_ref[0] = (r_re * ab_re - r_im * ab_im).astype(BF16)
    rxi_ref[0] = (-(r_re * ab_im + r_im * ab_re)).astype(BF16)
    m16 = jnp.exp(float(CHUNK) * lr_dt)
    a16r_ref[0] = m16 * jnp.cos(float(CHUNK) * th)
    a16i_ref[0] = m16 * jnp.sin(float(CHUNK) * th)


def _ssm_prep(a_re, a_im, log_dt, bt_re, bt_im, c_re, c_im):
    g = a_re.shape[0]
    vec = pl.BlockSpec((1, 1, SSM_STATE), lambda i: (i, 0, 0))
    mat = pl.BlockSpec((1, SSM_GROUP, SSM_STATE), lambda i: (i, 0, 0))
    wide = pl.BlockSpec((1, CW, SSM_STATE), lambda i: (i, 0, 0))
    return pl.pallas_call(
        _ssm_prep_kernel,
        grid=(g,),
        in_specs=[vec, vec, pl.BlockSpec((1, 1, 1), lambda i: (i, 0, 0)), mat, mat, mat, mat],
        out_specs=[pl.BlockSpec((1, CW, CW), lambda i: (i, 0, 0)), wide, wide, wide, wide, vec, vec],
        out_shape=[
            jax.ShapeDtypeStruct((g, CW, CW), BF16),
            jax.ShapeDtypeStruct((g, CW, SSM_STATE), BF16),
            jax.ShapeDtypeStruct((g, CW, SSM_STATE), BF16),
            jax.ShapeDtypeStruct((g, CW, SSM_STATE), BF16),
            jax.ShapeDtypeStruct((g, CW, SSM_STATE), BF16),
            jax.ShapeDtypeStruct((g, 1, SSM_STATE), F32),
            jax.ShapeDtypeStruct((g, 1, SSM_STATE), F32),
        ],
        compiler_params=_params(("arbitrary",)),
        name="ssm_prep",
    )(a_re.reshape(g, 1, SSM_STATE), a_im.reshape(g, 1, SSM_STATE), log_dt.reshape(g, 1, 1),
      bt_re, bt_im, c_re, c_im)


def _gelu_tanh(x):
    c = math.sqrt(2.0 / math.pi)
    return x * (0.5 * (1.0 + jnp.tanh(c * (x + 0.044715 * (x * x * x)))))


def _ssm_kernel(u_ref, t_ref, wsr_ref, wsi_ref, rxr_ref, rxi_ref, a16r_ref, a16i_ref, d_ref,
                z_ref, sr_ref, si_ref, ug_ref, zg_ref):
    nk = u_ref.shape[1]
    for j in range(GROUP_BLOCK):
        ug_ref[j] = jnp.concatenate(
            [u_ref[s, :, j * SSM_GROUP:(j + 1) * SSM_GROUP] for s in range(CHUNK)], axis=1)
    u_ref = ug_ref
    for j in range(GROUP_BLOCK):
        u = u_ref[j]
        sr_ref[j * nk:(j + 1) * nk, :] = _dot(u, wsr_ref[j])
        si_ref[j * nk:(j + 1) * nk, :] = _dot(u, wsi_ref[j])

    ar = a16r_ref[...]
    ai = a16i_ref[...]

    def step(k, carry):
        xr, xi = carry
        rows = pl.ds(k, GROUP_BLOCK, stride=nk)
        sr = sr_ref[rows, :]
        si = si_ref[rows, :]
        sr_ref[rows, :] = xr
        si_ref[rows, :] = xi
        return (ar * xr - ai * xi + sr, ar * xi + ai * xr + si)

    zero = jnp.zeros((GROUP_BLOCK, SSM_STATE), F32)
    lax.fori_loop(0, nk, step, (zero, zero), unroll=4)

    for j in range(GROUP_BLOCK):
        u = u_ref[j]
        xr = sr_ref[j * nk:(j + 1) * nk, :].astype(BF16)
        xi = si_ref[j * nk:(j + 1) * nk, :].astype(BF16)
        y = (_dot(u, t_ref[j]) + _dot_nt(xr, rxr_ref[j]) + _dot_nt(xi, rxi_ref[j])
             + d_ref[j] * u.astype(F32))
        zg_ref[j] = _gelu_tanh(y).astype(zg_ref.dtype)

    for t in range(CHUNK):
        z_ref[t] = jnp.concatenate(
            [zg_ref[j, :, t * SSM_GROUP:(t + 1) * SSM_GROUP] for j in range(GROUP_BLOCK)], axis=1)


def _ssm(u3, tmat, wsr, wsi, rxr, rxi, a16r, a16i, d_tiled):
    _, nk, width = u3.shape
    gb = GROUP_BLOCK
    gl = gb * SSM_GROUP
    blk3 = lambda a, b: pl.BlockSpec((gb, a, b), lambda i: (i, 0, 0))
    io = pl.BlockSpec((CHUNK, nk, gl), lambda i: (0, 0, i))
    return pl.pallas_call(
        _ssm_kernel,
        grid=(width // gl,),
        in_specs=[io, blk3(CW, CW), blk3(CW, SSM_STATE), blk3(CW, SSM_STATE),
                  blk3(CW, SSM_STATE), blk3(CW, SSM_STATE),
                  pl.BlockSpec((gb, SSM_STATE), lambda i: (i, 0)),
                  pl.BlockSpec((gb, SSM_STATE), lambda i: (i, 0)),
                  blk3(1, CW)],
        out_specs=io,
        out_shape=jax.ShapeDtypeStruct(u3.shape, BF16),
        scratch_shapes=[pltpu.VMEM((gb * nk, SSM_STATE), F32), pltpu.VMEM((gb * nk, SSM_STATE), F32),
                        pltpu.VMEM((gb, nk, CW), BF16), pltpu.VMEM((gb, nk, CW), BF16)],
        compiler_params=_params(("arbitrary",)),
        name="ssm",
    )(u3, tmat, wsr, wsi, rxr, rxi, a16r, a16i, d_tiled)


def _glu_kernel(z_ref, perm_ref, wg_ref, bg_ref, wb_ref, g0_ref, o_ref):
    tm = o_ref.shape[0]
    z = _dot(perm_ref[...], z_ref[...].reshape(tm, SSM_WIDTH)).astype(BF16)
    h = _dot(z, wg_ref[...]) + bg_ref[...]
    ga = h[:, :SSM_WIDTH]
    gb = h[:, SSM_WIDTH:]
    a = (ga * _sigmoid(gb)).astype(BF16)
    y = _dot(a, wb_ref[...])
    o_ref[...] = (_sigmoid(g0_ref[...].astype(F32)) * y).astype(o_ref.dtype)


def _glu(z3, w_glu, b_glu, w_br, proj, tm=512):
    t = z3.shape[0] * z3.shape[1]
    nk = tm // CHUNK
    r = np.arange(tm)
    perm = np.zeros((tm, tm), np.float32)
    perm[r, (r % CHUNK) * nk + r // CHUNK] = 1.0
    return pl.pallas_call(
        _glu_kernel,
        grid=(t // tm,),
        in_specs=[
            pl.BlockSpec((CHUNK, nk, SSM_WIDTH), lambda i: (0, i, 0)),
            pl.BlockSpec((tm, tm), lambda i: (0, 0)),
            pl.BlockSpec((SSM_WIDTH, 2 * SSM_WIDTH), lambda i: (0, 0)),
            pl.BlockSpec((1, 2 * SSM_WIDTH), lambda i: (0, 0)),
            pl.BlockSpec((SSM_WIDTH, D_MODEL), lambda i: (0, 0)),
            pl.BlockSpec((tm, D_MODEL), lambda i: (i, COL_G0 // D_MODEL)),
        ],
        out_specs=pl.BlockSpec((tm, D_MODEL), lambda i: (i, 0)),
        out_shape=jax.ShapeDtypeStruct((t, D_MODEL), BF16),
        compiler_params=_params(("arbitrary",)),
        name="glu",
    )(z3, jnp.asarray(perm, BF16), w_glu, b_glu, w_br, proj)


def _rope_pattern():
    half = ROPE_DIM // 2
    inv_freq = (np.float32(ROPE_THETA) ** (-np.arange(half, dtype=np.float32) / np.float32(half))).astype(np.float32)
    d = np.arange(LANES) % HEAD_DIM
    rotated = d < ROPE_DIM
    pat = np.zeros((16, LANES), np.float32)
    pat[:half] = rotated[None, :] & ((d % half)[None, :] == np.arange(half)[:, None])
    pat[8] = ~rotated
    pat[9] = np.where(d < half, -1.0, 0.0)
    pat[10] = np.where((d >= half) & rotated, 1.0, 0.0)
    return inv_freq.reshape(half, 1), pat


def _attn_kernel(q_ref, k_ref, v_ref, pos_ref, freq_ref, pat_ref, sink_ref, wbr_ref, ysg_ref, g1_ref,
                 o_ref, qbuf, kbuf, vbuf, obuf, sbuf, pbuf):
    i = pl.program_id(0)
    tq = q_ref.shape[0]
    nw = tq // WINDOW
    half = ROPE_DIM // 2

    @pl.when(i == 0)
    def _():
        kbuf[:, 0:WINDOW, :] = jnp.zeros((2 * N_KV_HEADS, WINDOW, LANES), BF16)
        vbuf[:, 0:WINDOW, :] = jnp.zeros((2 * N_KV_HEADS, WINDOW, LANES), BF16)

    ang = freq_ref[...] * pos_ref[...].astype(F32)
    spread = lambda tab: lax.dot_general(tab, pat_ref[0:8, :], (((0,), (0,)), ((), ())),
                                         preferred_element_type=F32, precision=HIGHEST)
    cs = spread(jnp.cos(ang)) + pat_ref[8:9, :]
    sn = spread(jnp.sin(ang))
    c_up = sn * pat_ref[9:10, :]
    c_dn = sn * pat_ref[10:11, :]

    def rope(x):
        return (x * cs + pltpu.roll(x, LANES - half, 1) * c_up + pltpu.roll(x, half, 1) * c_dn)

    low = lax.broadcasted_iota(I32, (tq, LANES), 1) < HEAD_DIM

    def split_heads(buf, cb, x):
        xs = pltpu.roll(x, HEAD_DIM, 1)
        zero = jnp.zeros_like(x)
        buf[4 * cb + 0, WINDOW:, :] = jnp.where(low, x, zero).astype(BF16)
        buf[4 * cb + 1, WINDOW:, :] = jnp.where(low, zero, xs).astype(BF16)
        buf[4 * cb + 2, WINDOW:, :] = jnp.where(low, xs, zero).astype(BF16)
        buf[4 * cb + 3, WINDOW:, :] = jnp.where(low, zero, x).astype(BF16)

    scale = HEAD_DIM ** -0.5
    for cb in range(Q_WIDTH // LANES):
        sl = slice(cb * LANES, (cb + 1) * LANES)
        qbuf[:, sl] = (rope(q_ref[:, sl].astype(F32)) * scale).astype(BF16)
    for cb in range(KV_WIDTH // LANES):
        sl = slice(cb * LANES, (cb + 1) * LANES)
        split_heads(kbuf, cb, rope(k_ref[:, sl].astype(F32)))
        split_heads(vbuf, cb, v_ref[:, sl].astype(F32))

    kj = lax.broadcasted_iota(I32, (2 * WINDOW, WINDOW), 0)
    qi = lax.broadcasted_iota(I32, (2 * WINDOW, WINDOW), 1)
    dist = qi + WINDOW - kj
    in_band = (dist >= 0) & (dist < WINDOW)
    cur_only = kj >= WINDOW
    sinks = sink_ref[...]

    def window(w, carry):
        r0 = pl.multiple_of(w * WINDOW, WINDOW)
        rows = pl.ds(r0, 2 * WINDOW)
        not_first = (i * nw + w) > 0
        mask = in_band & (cur_only | not_first)

        for h in range(N_Q_HEADS):
            qp = qbuf[pl.ds(r0, WINDOW), (h // 2) * LANES:(h // 2 + 1) * LANES]
            sbuf[h] = _dot_nt(kbuf[2 * (h // Q_PER_KV) + h % 2, rows, :], qp)
        for h in range(N_Q_HEADS):
            s = jnp.where(mask, sbuf[h], -jnp.inf)
            sink = sinks[:, h:h + 1]
            m = jnp.maximum(jnp.max(s, axis=0, keepdims=True), sink)
            p = jnp.exp(s - m)
            denom = jnp.sum(p, axis=0, keepdims=True) + jnp.exp(sink - m)
            pbuf[h] = (p * (1.0 / denom)).astype(BF16)
        for a in range(N_Q_HEADS // 2):
            kv = (2 * a) // Q_PER_KV
            o = _dot_tn(pbuf[2 * a], vbuf[2 * kv, rows, :]) + _dot_tn(pbuf[2 * a + 1], vbuf[2 * kv + 1, rows, :])
            obuf[pl.ds(r0, WINDOW), a * LANES:(a + 1) * LANES] = o.astype(BF16)
        return carry

    lax.fori_loop(0, nw, window, 0)

    kbuf[:, 0:WINDOW, :] = kbuf[:, tq:tq + WINDOW, :]
    vbuf[:, 0:WINDOW, :] = vbuf[:, tq:tq + WINDOW, :]

    y = _dot(obuf[...], wbr_ref[...])
    o_ref[...] = (ysg_ref[...].astype(F32) + _sigmoid(g1_ref[...].astype(F32)) * y).astype(o_ref.dtype)


def _attn(proj, pos_row, sinks, w_br, ysg, tq=512):
    t = proj.shape[0]
    freq, pat = (jnp.asarray(a) for a in _rope_pattern())
    return pl.pallas_call(
        _attn_kernel,
        grid=(t // tq,),
        in_specs=[
            pl.BlockSpec((tq, Q_WIDTH), lambda i: (i, COL_Q // Q_WIDTH)),
            pl.BlockSpec((tq, KV_WIDTH), lambda i: (i, COL_K // KV_WIDTH)),
            pl.BlockSpec((tq, KV_WIDTH), lambda i: (i, COL_V // KV_WIDTH)),
            pl.BlockSpec((1, tq), lambda i: (0, i)),
            pl.BlockSpec((ROPE_DIM // 2, 1), lambda i: (0, 0)),
            pl.BlockSpec((16, LANES), lambda i: (0, 0)),
            pl.BlockSpec((1, N_Q_HEADS), lambda i: (0, 0)),
            pl.BlockSpec((Q_WIDTH, D_MODEL), lambda i: (0, 0)),
            pl.BlockSpec((tq, D_MODEL), lambda i: (i, 0)),
            pl.BlockSpec((tq, D_MODEL), lambda i: (i, COL_G1 // D_MODEL)),
        ],
        out_specs=pl.BlockSpec((tq, D_MODEL), lambda i: (i, 0)),
        out_shape=jax.ShapeDtypeStruct((t, D_MODEL), BF16),
        scratch_shapes=[
            pltpu.VMEM((tq, Q_WIDTH), BF16),
            pltpu.VMEM((2 * N_KV_HEADS, tq + WINDOW, LANES), BF16),
            pltpu.VMEM((2 * N_KV_HEADS, tq + WINDOW, LANES), BF16),
            pltpu.VMEM((tq, Q_WIDTH), BF16),
            pltpu.VMEM((N_Q_HEADS, 2 * WINDOW, WINDOW), F32),
            pltpu.VMEM((N_Q_HEADS, 2 * WINDOW, WINDOW), BF16),
        ],
        compiler_params=_params(("arbitrary",)),
        name="attn",
    )(proj, proj, proj, pos_row, freq, pat, sinks, w_br, ysg, proj)


def _oproj_kernel(x_ref, mix_ref, wo_ref, g_ref, wrh_ref, wrl_ref, br_ref, h_ref, hp_ref, lt_ref):
    h = x_ref[...] + _dot(mix_ref[...], wo_ref[...])
    h_ref[...] = h
    ms = jnp.mean(h * h, axis=-1, keepdims=True)
    hn = h * lax.rsqrt(ms + EPS) * g_ref[...]
    _store_token_tiles(hp_ref, hn)
    hn_hi = hn.astype(BF16)
    hn_lo = (hn - hn_hi.astype(F32)).astype(BF16)
    logits = (_dot(hn_hi, wrh_ref[...]) + _dot(hn_lo, wrh_ref[...]) + _dot(hn_hi, wrl_ref[...])
              + br_ref[...])
    lt_ref[...] = logits.T


def _oproj(x, mixed, w_o, g, w_router, b_router, tm=512):
    t = x.shape[0]
    w_router_hi = w_router.astype(BF16)
    w_router_hi_rest = (w_router - w_router_hi.astype(F32)).astype(BF16)
    return pl.pallas_call(
        _oproj_kernel,
        grid=(t // tm,),
        in_specs=[
            pl.BlockSpec((tm, D_MODEL), lambda i: (i, 0)),
            pl.BlockSpec((tm, D_MODEL), lambda i: (i, 0)),
            pl.BlockSpec((D_MODEL, D_MODEL), lambda i: (0, 0)),
            pl.BlockSpec((1, D_MODEL), lambda i: (0, 0)),
            pl.BlockSpec((D_MODEL, LANES), lambda i: (0, 0)),
            pl.BlockSpec((D_MODEL, LANES), lambda i: (0, 0)),
            pl.BlockSpec((1, LANES), lambda i: (0, 0)),
        ],
        out_specs=[
            pl.BlockSpec((tm, D_MODEL), lambda i: (i, 0)),
            pl.BlockSpec((tm * SUBLANES, LANES), lambda i: (i, 0)),
            pl.BlockSpec((LANES, tm), lambda i: (0, i)),
        ],
        out_shape=[
            jax.ShapeDtypeStruct((t, D_MODEL), F32),
            jax.ShapeDtypeStruct((t * SUBLANES, LANES), jnp.uint32),
            jax.ShapeDtypeStruct((LANES, t), F32),
        ],
        compiler_params=_params(("arbitrary",)),
        name="oproj",
    )(x, mixed, w_o, g, w_router_hi, w_router_hi_rest, b_router)


ROUTE_CHUNK = 256


def _route_kernel(lt_ref, dest_ref, wts_ref, bexp_ref, nvalid_ref, eid_ref, rank_ref):
    t = lt_ref.shape[1]
    nc = t // ROUTE_CHUNK
    r8 = lax.broadcasted_iota(I32, (N_GROUPS, ROUTE_CHUNK), 0)
    r64 = lax.broadcasted_iota(I32, (N_EXPERTS, ROUTE_CHUNK), 0)

    def pick(c, carry):
        cols = pl.ds(pl.multiple_of(c * ROUTE_CHUNK, ROUTE_CHUNK), ROUTE_CHUNK)
        lg = lt_ref[0:N_GROUPS, cols]
        m = jnp.max(lg, axis=0, keepdims=True)
        ssum = jnp.sum(jnp.exp(lg - m), axis=0, keepdims=True)
        p_grp = 1.0 / ssum
        grp = jnp.min(jnp.where(lg == m, r8, N_GROUPS), axis=0, keepdims=True)
        le = lt_ref[N_GROUPS:N_GROUPS + N_EXPERTS, cols]
        leg = jnp.where((r64 // EXPERTS_PER_GROUP) == grp, le, -jnp.inf)
        m1 = jnp.max(leg, axis=0, keepdims=True)
        i1 = jnp.min(jnp.where(leg == m1, r64, N_EXPERTS), axis=0, keepdims=True)
        leg2 = jnp.where(r64 == i1, -jnp.inf, leg)
        m2 = jnp.max(leg2, axis=0, keepdims=True)
        i2 = jnp.min(jnp.where(leg2 == m2, r64, N_EXPERTS), axis=0, keepdims=True)
        ex = jnp.exp(m2 - m1)
        eid_ref[0:1, cols] = i1
        eid_ref[1:2, cols] = i2
        wts_ref[0:1, cols] = p_grp / (1.0 + ex)
        wts_ref[1:2, cols] = p_grp * ex / (1.0 + ex)
        return carry

    lax.fori_loop(0, nc, pick, 0)

    a_row = lax.broadcasted_iota(I32, (ROUTE_CHUNK, ROUTE_CHUNK), 0)
    a_col = lax.broadcasted_iota(I32, (ROUTE_CHUNK, ROUTE_CHUNK), 1)
    before = (a_row < a_col).astype(BF16)

    def count(n, carry):
        j = n // nc
        c = n - j * nc
        cols = pl.ds(pl.multiple_of(c * ROUTE_CHUNK, ROUTE_CHUNK), ROUTE_CHUNK)
        oh = r64 == eid_ref[pl.ds(j, 1), cols]
        ohf = oh.astype(F32)
        pref = _dot(ohf.astype(BF16), before) + carry
        rank_ref[pl.ds(j, 1), cols] = jnp.sum(jnp.where(oh, pref, 0.0), axis=0, keepdims=True)
        return carry + jnp.sum(ohf, axis=1, keepdims=True)

    counts = lax.fori_loop(0, TOP_K * nc, count, jnp.zeros((N_EXPERTS, 1), F32))

    padded = jnp.floor((counts + (MOE_BLOCK - 1)) * (1.0 / MOE_BLOCK)) * MOE_BLOCK
    e_row = lax.broadcasted_iota(I32, (N_EXPERTS, N_EXPERTS), 0)
    e_col = lax.broadcasted_iota(I32, (N_EXPERTS, N_EXPERTS), 1)
    incl = (e_col <= e_row).astype(F32)
    pad_end = _dot(incl, jnp.broadcast_to(padded, (N_EXPERTS, LANES)), precision=HIGHEST)[:, 0:1]
    pad_start = pad_end - padded

    def place(n, carry):
        j = n // nc
        c = n - j * nc
        cols = pl.ds(pl.multiple_of(c * ROUTE_CHUNK, ROUTE_CHUNK), ROUTE_CHUNK)
        oh = r64 == eid_ref[pl.ds(j, 1), cols]
        start = jnp.sum(jnp.where(oh, pad_start, 0.0), axis=0, keepdims=True)
        dest_ref[pl.ds(j, 1), cols] = (start + rank_ref[pl.ds(j, 1), cols]).astype(I32)
        return carry

    lax.fori_loop(0, TOP_K * nc, place, 0)

    b0 = (lax.broadcasted_iota(I32, (N_EXPERTS, bexp_ref.shape[1]), 1) * MOE_BLOCK).astype(F32)
    n_done = jnp.sum((pad_end <= b0).astype(F32), axis=0, keepdims=True)
    bexp_ref[...] = jnp.minimum(n_done, float(N_EXPERTS - 1)).astype(I32)
    live = jnp.minimum(pad_start + counts, b0 + MOE_BLOCK) - jnp.maximum(pad_start, b0)
    nvalid_ref[...] = jnp.sum(jnp.maximum(live, 0.0), axis=0, keepdims=True).astype(I32)


def _route(logits_t, n_blocks_pad):
    t = logits_t.shape[1]
    return pl.pallas_call(
        _route_kernel,
        out_shape=[
            jax.ShapeDtypeStruct((TOP_K, t), I32),
            jax.ShapeDtypeStruct((TOP_K, t), F32),
            jax.ShapeDtypeStruct((1, n_blocks_pad), I32),
            jax.ShapeDtypeStruct((1, n_blocks_pad), I32),
        ],
        scratch_shapes=[pltpu.VMEM((TOP_K, t), I32), pltpu.VMEM((TOP_K, t), F32)],
        compiler_params=pltpu.CompilerParams(vmem_limit_bytes=VMEM_LIMIT),
        name="route",
    )(logits_t)


def _invmap_kernel(dest_ref, rt_ref):
    n_rows = rt_ref.shape[0]
    t = dest_ref.shape[0] // TOP_K

    def clear(r, c):
        rt_ref[r] = 0
        return c

    lax.fori_loop(0, n_rows, clear, 0, unroll=16)

    def put(tok, c):
        rt_ref[dest_ref[tok]] = tok
        rt_ref[dest_ref[t + tok]] = tok
        return c

    lax.fori_loop(0, t, put, 0, unroll=8)


def _invmap(dest_flat, n_rows):
    return pl.pallas_call(
        _invmap_kernel,
        in_specs=[pl.BlockSpec(memory_space=pltpu.SMEM)],
        out_specs=pl.BlockSpec(memory_space=pltpu.SMEM),
        out_shape=jax.ShapeDtypeStruct((n_rows,), I32),
        name="invmap",
    )(dest_flat)


GATHER_SLOTS = 4


def _row_copy(src_hbm, src_row, dst_buf, slot, dst_row, sem):
    return pltpu.make_async_copy(src_hbm.at[pl.ds(pl.multiple_of(src_row * SUBLANES, SUBLANES), SUBLANES), :],
                                 dst_buf.at[slot, pl.ds(dst_row * SUBLANES, SUBLANES), :], sem.at[slot])


def _rows_copy(src_hbm, dst_buf, slot, rows, sem):
    n = rows * SUBLANES
    return pltpu.make_async_copy(src_hbm.at[pl.ds(0, n), :], dst_buf.at[slot, pl.ds(0, n), :], sem.at[slot])


def _expert_kernel(bexp_ref, nvalid_ref, rt_ref, hp_hbm, wg_hbm, wu_hbm, wd_hbm, ys_ref,
                   xbuf, wf_g, wf_u, wf_d, wb_g, wb_u, wb_d, ord_ref, xsem, wsem):
    b = pl.program_id(0)
    nb = pl.num_programs(0) - 1

    def live_rows(blk):
        return jnp.where(nvalid_ref[blk] > 0, MOE_BLOCK, 0)

    def expert_of(blk):
        return bexp_ref[jnp.minimum(blk, nb - 1)]

    def next_owner(blk, e):
        return lax.while_loop(lambda j: (j < nb) & (expert_of(j) == e), lambda j: j + 1, blk)

    def weight_copies(e, slot):
        return (pltpu.make_async_copy(wg_hbm.at[e], wf_g.at[slot], wsem.at[slot, 0]),
                pltpu.make_async_copy(wu_hbm.at[e], wf_u.at[slot], wsem.at[slot, 1]),
                pltpu.make_async_copy(wd_hbm.at[e], wf_d.at[slot], wsem.at[slot, 2]))

    @pl.when(b == 0)
    def _():
        ord_ref[0] = 0
        e0 = bexp_ref[0]
        for c in weight_copies(e0, 0):
            c.start(priority=1)
        n1 = next_owner(1, e0)

        @pl.when(n1 < nb)
        def _():
            for c in weight_copies(expert_of(n1), 1):
                c.start(priority=1)

    def gather(blk):
        base = blk * MOE_BLOCK
        slot = blk % GATHER_SLOTS

        def body(i, c):
            for s in range(SUBLANES):
                r = i * SUBLANES + s
                _row_copy(hp_hbm, rt_ref[base + r], xbuf, slot, r, xsem).start()
            return c

        lax.fori_loop(0, jnp.where(blk < nb, live_rows(jnp.minimum(blk, nb - 1)) // SUBLANES, 0), body, 0)

    @pl.when(b == 0)
    def _():
        for blk in range(GATHER_SLOTS - 2):
            gather(blk)

    gather(b + GATHER_SLOTS - 2)

    @pl.when(b > 0)
    def _():
        blk = b - 1
        e = bexp_ref[blk]
        first = (blk == 0) | (e != bexp_ref[jnp.maximum(blk - 1, 0)])

        @pl.when(first)
        def _():
            n = ord_ref[0]
            wslot = n % 2
            ord_ref[0] = n + 1
            for c in weight_copies(e, wslot):
                c.wait()
            wb_g[...] = wf_g[wslot].astype(BF16)
            wb_u[...] = wf_u[wslot].astype(BF16)
            wb_d[...] = wf_d[wslot].astype(BF16)
            n1 = next_owner(blk + 1, e)
            n2 = next_owner(n1 + 1, expert_of(n1))

            @pl.when((n1 < nb) & (n2 < nb))
            def _():
                for c in weight_copies(expert_of(n2), wslot):
                    c.start(priority=1)

        slot = blk % GATHER_SLOTS
        rows = live_rows(blk)

        @pl.when(rows > 0)
        def _():
            _rows_copy(hp_hbm, xbuf, slot, rows, xsem).wait()
            lo, hi = (v.astype(BF16) for v in _load_token_tiles(xbuf, slot, 0, MOE_BLOCK))
            g = _dot(lo, wb_g[:HALF, :]) + _dot(hi, wb_g[HALF:, :])
            u = _dot(lo, wb_u[:HALF, :]) + _dot(hi, wb_u[HALF:, :])
            h = (g * _sigmoid(g) * u).astype(BF16)
            y = _dot(h, wb_d[...])
            _store_token_tiles(ys_ref, y)

        @pl.when(rows == 0)
        def _():
            _store_token_tiles(ys_ref, jnp.zeros((MOE_BLOCK, D_MODEL), F32))


def _experts(bexp, nvalid, row_tok, hn_packed, w_gate, w_up, w_down, n_blocks):
    grid_spec = pltpu.PrefetchScalarGridSpec(
        num_scalar_prefetch=3,
        grid=(n_blocks + 1,),
        in_specs=[pl.BlockSpec(memory_space=pl.ANY)] * 4,
        out_specs=pl.BlockSpec((MOE_BLOCK * SUBLANES, LANES), lambda b, be, nv, rt: (jnp.maximum(b - 1, 0), 0)),
        scratch_shapes=[
            pltpu.VMEM((GATHER_SLOTS, MOE_BLOCK * SUBLANES, LANES), jnp.uint32),
            pltpu.VMEM((2, D_MODEL, D_EXPERT), F32), pltpu.VMEM((2, D_MODEL, D_EXPERT), F32),
            pltpu.VMEM((2, D_EXPERT, D_MODEL), F32),
            pltpu.VMEM((D_MODEL, D_EXPERT), BF16), pltpu.VMEM((D_MODEL, D_EXPERT), BF16),
            pltpu.VMEM((D_EXPERT, D_MODEL), BF16),
            pltpu.SMEM((1,), I32),
            pltpu.SemaphoreType.DMA((GATHER_SLOTS,)), pltpu.SemaphoreType.DMA((2, 3)),
        ],
    )
    return pl.pallas_call(
        _expert_kernel,
        grid_spec=grid_spec,
        out_shape=jax.ShapeDtypeStruct((n_blocks * MOE_BLOCK * SUBLANES, LANES), jnp.uint32),
        compiler_params=_params(("arbitrary",)),
        name="experts",
    )(bexp, nvalid, row_tok, hn_packed, w_gate, w_up, w_down)


def _combine_kernel(dest_ref, ys_hbm, h_ref, w_ref, g_ref, o_ref, ybuf, sem):
    i = pl.program_id(0)
    n = pl.num_programs(0)
    tq = h_ref.shape[0]
    t = n * tq

    def start_gather(blk, slot):
        base = blk * tq

        def body(j, c):
            for s in range(SUBLANES):
                r = j * SUBLANES + s
                _row_copy(ys_hbm, dest_ref[base + r], ybuf, slot, r, sem).start()
                _row_copy(ys_hbm, dest_ref[t + base + r], ybuf, slot, tq + r, sem).start()
            return c

        lax.fori_loop(0, tq // SUBLANES, body, 0)

    @pl.when(i == 0)
    def _():
        start_gather(0, 0)

    @pl.when(i + 1 < n)
    def _():
        start_gather(i + 1, (i + 1) % 2)

    slot = i % 2
    _rows_copy(ys_hbm, ybuf, slot, TOP_K * tq, sem).wait()
    y0_lo, y0_hi = _load_token_tiles(ybuf, slot, 0, tq)
    y1_lo, y1_hi = _load_token_tiles(ybuf, slot, tq, tq)
    w = w_ref[...]
    w0 = w[:, 0:1]
    w1 = w[:, 1:2]
    h = h_ref[...]
    lo = h[:, :HALF] + w0 * y0_lo + w1 * y1_lo
    hi = h[:, HALF:] + w0 * y0_hi + w1 * y1_hi
    ms = (jnp.sum(lo * lo, axis=-1, keepdims=True) + jnp.sum(hi * hi, axis=-1, keepdims=True)) * (1.0 / D_MODEL)
    inv = lax.rsqrt(ms + EPS)
    g = g_ref[...]
    o_ref[:, :HALF] = lo * inv * g[:, :HALF]
    o_ref[:, HALF:] = hi * inv * g[:, HALF:]


def _combine(dest_flat, ys_packed, h1, wts_tok, g_final, tq=256):
    t = h1.shape[0]
    grid_spec = pltpu.PrefetchScalarGridSpec(
        num_scalar_prefetch=1,
        grid=(t // tq,),
        in_specs=[
            pl.BlockSpec(memory_space=pl.ANY),
            pl.BlockSpec((tq, D_MODEL), lambda i, d: (i, 0)),
            pl.BlockSpec((tq, TOP_K), lambda i, d: (i, 0)),
            pl.BlockSpec((1, D_MODEL), lambda i, d: (0, 0)),
        ],
        out_specs=pl.BlockSpec((tq, D_MODEL), lambda i, d: (i, 0)),
        scratch_shapes=[pltpu.VMEM((2, TOP_K * tq * SUBLANES, LANES), jnp.uint32),
                        pltpu.SemaphoreType.DMA((2,))],
    )
    return pl.pallas_call(
        _combine_kernel,
        grid_spec=grid_spec,
        out_shape=jax.ShapeDtypeStruct((t, D_MODEL), F32),
        compiler_params=_params(("arbitrary",)),
        name="combine",
    )(dest_flat, ys_packed, h1, wts_tok, g_final)


def _mixers(h, positions, norm_g, w_in, b_in, a_re, a_im, log_dt, b_re, b_im, c_re, c_im, d_skip,
            w_glu, b_glu, w_br_ssm, sinks, w_br_attn):
    t = h.shape[0]
    ngate = 2 * D_MODEL
    g = norm_g.reshape(1, D_MODEL)
    b_row = b_in.reshape(1, IN_WIDTH)
    proj = _inproj(h, g, w_in, b_row, chunk_major=False,
                   segments=((IN_WIDTH - ngate, ngate), (SSM_WIDTH, IN_WIDTH - ngate - SSM_WIDTH)))
    u3 = _inproj(h, g, w_in, b_row, chunk_major=True, segments=((0, SSM_WIDTH),))

    tmat, wsr, wsi, rxr, rxi, a16r, a16i = _ssm_prep(
        a_re, a_im, log_dt, b_re.transpose(0, 2, 1), b_im.transpose(0, 2, 1), c_re, c_im)
    d_tiled = jnp.tile(d_skip, (1, CHUNK)).reshape(SSM_GROUPS, 1, CW)
    z3 = _ssm(u3, tmat, wsr, wsi, rxr, rxi, a16r.reshape(SSM_GROUPS, SSM_STATE),
              a16i.reshape(SSM_GROUPS, SSM_STATE), d_tiled)

    ysg = _glu(z3, w_glu.astype(BF16), b_glu.reshape(1, -1), w_br_ssm.astype(BF16), proj)
    return _attn(proj, positions.reshape(1, t), sinks.reshape(1, N_Q_HEADS), w_br_attn.astype(BF16), ysg)


def _moe_tail(h, mixed, w_o, norm_ffn_g, w_rg, b_rg, w_re, b_re, w_gate, w_up, w_down, norm_final_g):
    t = h.shape[0]
    n_route = N_GROUPS + N_EXPERTS
    w_router = jnp.concatenate([w_rg, w_re, jnp.zeros((D_MODEL, LANES - n_route), F32)], axis=1)
    b_router = jnp.concatenate([b_rg, b_re, jnp.zeros((LANES - n_route,), F32)]).reshape(1, LANES)
    h1, hn_packed, logits_t = _oproj(h, mixed, w_o.astype(BF16), norm_ffn_g.reshape(1, D_MODEL),
                                     w_router, b_router)
    n_assign = t * TOP_K
    n_blocks = -(-(n_assign + N_EXPERTS * (MOE_BLOCK - 1)) // MOE_BLOCK)
    n_blocks_pad = -(-n_blocks // LANES) * LANES
    dest, wts, bexp, nvalid = _route(logits_t, n_blocks_pad)
    dest_flat = dest.reshape(n_assign)
    row_tok = _invmap(dest_flat, n_blocks * MOE_BLOCK)
    ys_packed = _experts(bexp[0, :n_blocks], nvalid[0, :n_blocks], row_tok, hn_packed, w_gate, w_up, w_down,
                         n_blocks)
    return _combine(dest_flat, ys_packed, h1, wts.T, norm_final_g.reshape(1, D_MODEL))


def kernel(x, positions, norm_mix_g, w_in, b_in, ssm_a_re, ssm_a_im, ssm_log_dt, ssm_b_re, ssm_b_im, ssm_c_re, ssm_c_im, ssm_d, w_glu, b_glu, w_br_ssm, attn_sinks, w_br_attn, w_o, norm_ffn_g, w_router_group, b_router_group, w_router_expert, b_router_expert, w_exp_gate, w_exp_up, w_exp_down, norm_final_g):
    bsz, seq, d = x.shape
    assert bsz == 1 and d == D_MODEL and norm_mix_g.shape[0] == 1
    h = x.reshape(seq, d)
    mixed = _mixers(h, positions, norm_mix_g[0], w_in[0], b_in[0], ssm_a_re[0], ssm_a_im[0], ssm_log_dt[0],
                    ssm_b_re[0], ssm_b_im[0], ssm_c_re[0], ssm_c_im[0], ssm_d[0], w_glu[0], b_glu[0],
                    w_br_ssm[0], attn_sinks[0], w_br_attn[0])
    out = _moe_tail(h, mixed, w_o[0], norm_ffn_g[0], w_router_group[0], b_router_group[0],
                    w_router_expert[0], b_router_expert[0], w_exp_gate[0], w_exp_up[0], w_exp_down[0],
                    norm_final_g)
    return out.reshape(bsz, seq, d)
```

```python
import functools
import math

import numpy as np
import jax
import jax.numpy as jnp
from jax import lax
from jax.experimental import pallas as pl
from jax.experimental.pallas import tpu as pltpu

F32 = jnp.float32
BF16 = jnp.bfloat16
I32 = jnp.int32

D_MODEL = 2048
SSM_WIDTH = 1024
SSM_GROUP = 16
SSM_GROUPS = 64
SSM_STATE = 64
HEAD_DIM = 64
N_Q_HEADS = 16
N_KV_HEADS = 4
Q_PER_KV = 4
WINDOW = 128
ROPE_DIM = 16
ROPE_THETA = 500000.0
Q_WIDTH = 1024
KV_WIDTH = 256
IN_WIDTH = SSM_WIDTH + Q_WIDTH + 2 * KV_WIDTH + 2 * D_MODEL
N_GROUPS = 8
EXPERTS_PER_GROUP = 8
N_EXPERTS = 64
TOP_K = 2
D_EXPERT = 512
MOE_BLOCK = 128
EPS = 1e-6

CHUNK = 16
CW = CHUNK * SSM_GROUP
GROUP_BLOCK = 8
HALF = D_MODEL // 2
LANES = 128
VMEM_LIMIT = 56 * 1024 * 1024

COL_G0, COL_G1, COL_Q, COL_K, COL_V = 0, 2048, 4096, 5120, 5376

HIGHEST = lax.Precision.HIGHEST


def _dot(a, b, precision=None):
    return jnp.dot(a, b, preferred_element_type=F32, precision=precision)


def _dot_nt(a, b, precision=None):
    return lax.dot_general(a, b, (((1,), (1,)), ((), ())), preferred_element_type=F32,
                           precision=precision)


def _dot_tn(a, b):
    return lax.dot_general(a, b, (((0,), (0,)), ((), ())), preferred_element_type=F32)


def _sigmoid(x):
    return 1.0 / (1.0 + jnp.exp(-x))


def _pack_halves(lo, hi):
    return pltpu.pack_elementwise([lo, hi], packed_dtype=BF16)


def _unpack_half(w, index):
    return pltpu.unpack_elementwise(w, index=index, packed_dtype=BF16, unpacked_dtype=F32)


SUBLANES = 8
WORD_TILES = HALF // LANES
assert WORD_TILES == SUBLANES


def _store_token_tiles(ref, x):
    rows = x.shape[0]
    for s in range(WORD_TILES):
        ref[pl.ds(s, rows, stride=SUBLANES), :] = _pack_halves(x[:, s * LANES:(s + 1) * LANES],
                                                               x[:, HALF + s * LANES:HALF + (s + 1) * LANES])


def _load_token_tiles(ref, slot, first_row, rows):
    pieces = [ref[slot, pl.ds(first_row * SUBLANES + s, rows, stride=SUBLANES), :] for s in range(WORD_TILES)]
    return (jnp.concatenate([_unpack_half(p, 0) for p in pieces], axis=1),
            jnp.concatenate([_unpack_half(p, 1) for p in pieces], axis=1))


def _params(sem, vmem=VMEM_LIMIT):
    return pltpu.CompilerParams(dimension_semantics=sem, vmem_limit_bytes=vmem)


def _inproj_kernel(x_ref, g_ref, w_ref, b_ref, o_ref, xn_ref, *rest, chunk_major):
    @pl.when(pl.program_id(1) == 0)
    def _():
        x = x_ref[...]
        ms = jnp.mean(x * x, axis=-1, keepdims=True)
        xn_ref[...] = (x * lax.rsqrt(ms + EPS) * g_ref[...]).astype(BF16)

    acc = _dot(xn_ref[...], w_ref[...].astype(BF16)) + b_ref[...]
    if not chunk_major:
        o_ref[...] = acc.astype(o_ref.dtype)
        return
    (acc_ref,) = rest
    nk = o_ref.shape[1]
    for c in range(acc_ref.shape[0]):
        acc_ref[c] = acc[:, c * LANES:(c + 1) * LANES]
    for s in range(CHUNK):
        for c in range(acc_ref.shape[0]):
            o_ref[s, :, c * LANES:(c + 1) * LANES] = acc_ref[c, pl.ds(s, nk, stride=CHUNK), :].astype(o_ref.dtype)


def _inproj(x, g, w, b, *, chunk_major, segments, tm=1024, tn=512):
    t, d = x.shape
    (c0, n0), (c1, n1) = (tuple(segments) + ((0, 0),))[:2]
    assert all(v % tn == 0 for v in (c0, n0, c1, n1))
    n = n0 + n1
    p0, q0, q1 = n0 // tn, c0 // tn, c1 // tn
    panel = lambda i, j: (0, jnp.where(j < p0, j + q0, j - p0 + q1))
    scratch = [pltpu.VMEM((tm, d), BF16)]
    if chunk_major:
        out_spec = pl.BlockSpec((CHUNK, tm // CHUNK, tn), lambda i, j: (0, i, j))
        out_shape = jax.ShapeDtypeStruct((CHUNK, t // CHUNK, n), BF16)
        scratch.append(pltpu.VMEM((tn // LANES, tm, LANES), F32))
    else:
        out_spec = pl.BlockSpec((tm, tn), lambda i, j: (i, j))
        out_shape = jax.ShapeDtypeStruct((t, n), BF16)
    return pl.pallas_call(
        functools.partial(_inproj_kernel, chunk_major=chunk_major),
        grid=(t // tm, n // tn),
        in_specs=[
            pl.BlockSpec((tm, d), lambda i, j: (i, 0)),
            pl.BlockSpec((1, d), lambda i, j: (0, 0)),
            pl.BlockSpec((d, tn), panel),
            pl.BlockSpec((1, tn), panel),
        ],
        out_specs=out_spec,
        out_shape=out_shape,
        scratch_shapes=scratch,
        compiler_params=_params(("arbitrary", "arbitrary")),
        name="inproj_u" if chunk_major else "inproj",
    )(x, g, w, b)


def _ssm_prep_kernel(are_ref, aim_ref, ldt_ref, btr_ref, bti_ref, cr_ref, ci_ref,
                     t_ref, wsr_ref, wsi_ref, rxr_ref, rxi_ref, a16r_ref, a16i_ref):
    lam_re = jnp.minimum(are_ref[0], -1e-4)
    lam_im = aim_ref[0]
    dt = jnp.exp(ldt_ref[0])
    lr_dt = lam_re * dt
    th = lam_im * dt
    mag = jnp.exp(lr_dt)
    ab_re = mag * jnp.cos(th)
    ab_im = mag * jnp.sin(th)
    den = lam_re * lam_re + lam_im * lam_im
    nr = ab_re - 1.0
    ni = ab_im
    coef_re = (nr * lam_re + ni * lam_im) / den
    coef_im = (ni * lam_re - nr * lam_im) / den
    btr = btr_ref[0]
    bti = bti_ref[0]
    bb_re = coef_re * btr - coef_im * bti
    bb_im = coef_re * bti + coef_im * btr

    e = lax.broadcasted_iota(I32, (CHUNK, SSM_STATE), 0).astype(F32)
    pmag = jnp.exp(e * lr_dt)
    pos_re = pmag * jnp.cos(e * th)
    pos_im = pmag * jnp.sin(e * th)
    nmag = jnp.exp(-e * lr_dt)
    neg_re = nmag * jnp.cos(e * th)
    neg_im = -nmag * jnp.sin(e * th)

    def rep(tab):
        return jnp.broadcast_to(tab[:, None, :], (CHUNK, SSM_GROUP, SSM_STATE)).reshape(CW, SSM_STATE)

    def tile(mat):
        return jnp.broadcast_to(mat[None, :, :], (CHUNK, SSM_GROUP, SSM_STATE)).reshape(CW, SSM_STATE)

    pr, pi = rep(pos_re), rep(pos_im)
    ctr, cti = tile(cr_ref[0]), tile(ci_ref[0])
    r_re = ctr * pr - cti * pi
    r_im = ctr * pi + cti * pr
    qr, qi = rep(neg_re), rep(neg_im)
    btr_t, bti_t = tile(bb_re), tile(bb_im)
    l_re = btr_t * qr - bti_t * qi
    l_im = btr_t * qi + bti_t * qr

    def split(v):
        head = v.astype(BF16)
        return head, (v - head.astype(F32)).astype(BF16)

    def dot_nt3(a, b):
        (ah, al), (bh, bl) = split(a), split(b)
        return _dot_nt(ah, bh) + _dot_nt(al, bh) + _dot_nt(ah, bl)

    tm = dot_nt3(l_re, r_re) - dot_nt3(l_im, r_im)
    srow = lax.broadcasted_iota(I32, (CW, CW), 0) // SSM_GROUP
    tcol = lax.broadcasted_iota(I32, (CW, CW), 1) // SSM_GROUP
    t_ref[0] = jnp.where(tcol >= srow, tm, 0.0).astype(BF16)

    a15r = pos_re[CHUNK - 1:CHUNK, :]
    a15i = pos_im[CHUNK - 1:CHUNK, :]
    wsr_ref[0] = (l_re * a15r - l_im * a15i).astype(BF16)
    wsi_ref[0] = (l_re * a15i + l_im * a15r).astype(BF16)
    rxr_ref[0] = (r_re * ab_re - r_im * ab_im).astype(BF16)
    rxi_ref[0] = (-(r_re * ab_im + r_im * ab_re)).astype(BF16)
    m16 = jnp.exp(float(CHUNK) * lr_dt)
    a16r_ref[0] = m16 * jnp.cos(float(CHUNK) * th)
    a16i_ref[0] = m16 * jnp.sin(float(CHUNK) * th)


def _ssm_prep(a_re, a_im, log_dt, bt_re, bt_im, c_re, c_im):
    g = a_re.shape[0]
    vec = pl.BlockSpec((1, 1, SSM_STATE), lambda i: (i, 0, 0))
    mat = pl.BlockSpec((1, SSM_GROUP, SSM_STATE), lambda i: (i, 0, 0))
    wide = pl.BlockSpec((1, CW, SSM_STATE), lambda i: (i, 0, 0))
    return pl.pallas_call(
        _ssm_prep_kernel,
        grid=(g,),
        in_specs=[vec, vec, pl.BlockSpec((1, 1, 1), lambda i: (i, 0, 0)), mat, mat, mat, mat],
        out_specs=[pl.BlockSpec((1, CW, CW), lambda i: (i, 0, 0)), wide, wide, wide, wide, vec, vec],
        out_shape=[
            jax.ShapeDtypeStruct((g, CW, CW), BF16),
            jax.ShapeDtypeStruct((g, CW, SSM_STATE), BF16),
            jax.ShapeDtypeStruct((g, CW, SSM_STATE), BF16),
            jax.ShapeDtypeStruct((g, CW, SSM_STATE), BF16),
            jax.ShapeDtypeStruct((g, CW, SSM_STATE), BF16),
            jax.ShapeDtypeStruct((g, 1, SSM_STATE), F32),
            jax.ShapeDtypeStruct((g, 1, SSM_STATE), F32),
        ],
        compiler_params=_params(("arbitrary",)),
        name="ssm_prep",
    )(a_re.reshape(g, 1, SSM_STATE), a_im.reshape(g, 1, SSM_STATE), log_dt.reshape(g, 1, 1),
      bt_re, bt_im, c_re, c_im)


def _gelu_tanh(x):
    c = math.sqrt(2.0 / math.pi)
    return x * (0.5 * (1.0 + jnp.tanh(c * (x + 0.044715 * (x * x * x)))))


def _ssm_kernel(u_ref, t_ref, wsr_ref, wsi_ref, rxr_ref, rxi_ref, a16r_ref, a16i_ref, d_ref,
                z_ref, sr_ref, si_ref, ug_ref, zg_ref):
    nk = u_ref.shape[1]
    for j in range(GROUP_BLOCK):
        ug_ref[j] = jnp.concatenate(
            [u_ref[s, :, j * SSM_GROUP:(j + 1) * SSM_GROUP] for s in range(CHUNK)], axis=1)
    u_ref = ug_ref
    for j in range(GROUP_BLOCK):
        u = u_ref[j]
        sr_ref[j * nk:(j + 1) * nk, :] = _dot(u, wsr_ref[j])
        si_ref[j * nk:(j + 1) * nk, :] = _dot(u, wsi_ref[j])

    ar = a16r_ref[...]
    ai = a16i_ref[...]

    def step(k, carry):
        xr, xi = carry
        rows = pl.ds(k, GROUP_BLOCK, stride=nk)
        sr = sr_ref[rows, :]
        si = si_ref[rows, :]
        sr_ref[rows, :] = xr
        si_ref[rows, :] = xi
        return (ar * xr - ai * xi + sr, ar * xi + ai * xr + si)

    zero = jnp.zeros((GROUP_BLOCK, SSM_STATE), F32)
    lax.fori_loop(0, nk, step, (zero, zero), unroll=4)

    for j in range(GROUP_BLOCK):
        u = u_ref[j]
        xr = sr_ref[j * nk:(j + 1) * nk, :].astype(BF16)
        xi = si_ref[j * nk:(j + 1) * nk, :].astype(BF16)
        y = (_dot(u, t_ref[j]) + _dot_nt(xr, rxr_ref[j]) + _dot_nt(xi, rxi_ref[j])
             + d_ref[j] * u.astype(F32))
        zg_ref[j] = _gelu_tanh(y).astype(zg_ref.dtype)

    for t in range(CHUNK):
        z_ref[t] = jnp.concatenate(
            [zg_ref[j, :, t * SSM_GROUP:(t + 1) * SSM_GROUP] for j in range(GROUP_BLOCK)], axis=1)


def _ssm(u3, tmat, wsr, wsi, rxr, rxi, a16r, a16i, d_tiled):
    _, nk, width = u3.shape
    gb = GROUP_BLOCK
    gl = gb * SSM_GROUP
    blk3 = lambda a, b: pl.BlockSpec((gb, a, b), lambda i: (i, 0, 0))
    io = pl.BlockSpec((CHUNK, nk, gl), lambda i: (0, 0, i))
    return pl.pallas_call(
        _ssm_kernel,
        grid=(width // gl,),
        in_specs=[io, blk3(CW, CW), blk3(CW, SSM_STATE), blk3(CW, SSM_STATE),
                  blk3(CW, SSM_STATE), blk3(CW, SSM_STATE),
                  pl.BlockSpec((gb, SSM_STATE), lambda i: (i, 0)),
                  pl.BlockSpec((gb, SSM_STATE), lambda i: (i, 0)),
                  blk3(1, CW)],
        out_specs=io,
        out_shape=jax.ShapeDtypeStruct(u3.shape, BF16),
        scratch_shapes=[pltpu.VMEM((gb * nk, SSM_STATE), F32), pltpu.VMEM((gb * nk, SSM_STATE), F32),
                        pltpu.VMEM((gb, nk, CW), BF16), pltpu.VMEM((gb, nk, CW), BF16)],
        compiler_params=_params(("arbitrary",)),
        name="ssm",
    )(u3, tmat, wsr, wsi, rxr, rxi, a16r, a16i, d_tiled)


def _glu_kernel(z_ref, perm_ref, wg_ref, bg_ref, wb_ref, g0_ref, o_ref):
    tm = o_ref.shape[0]
    z = _dot(perm_ref[...], z_ref[...].reshape(tm, SSM_WIDTH)).astype(BF16)
    h = _dot(z, wg_ref[...]) + bg_ref[...]
    ga = h[:, :SSM_WIDTH]
    gb = h[:, SSM_WIDTH:]
    a = (ga * _sigmoid(gb)).astype(BF16)
    y = _dot(a, wb_ref[...])
    o_ref[...] = (_sigmoid(g0_ref[...].astype(F32)) * y).astype(o_ref.dtype)


def _glu(z3, w_glu, b_glu, w_br, proj, tm=512):
    t = z3.shape[0] * z3.shape[1]
    nk = tm // CHUNK
    r = np.arange(tm)
    perm = np.zeros((tm, tm), np.float32)
    perm[r, (r % CHUNK) * nk + r // CHUNK] = 1.0
    return pl.pallas_call(
        _glu_kernel,
        grid=(t // tm,),
        in_specs=[
            pl.BlockSpec((CHUNK, nk, SSM_WIDTH), lambda i: (0, i, 0)),
            pl.BlockSpec((tm, tm), lambda i: (0, 0)),
            pl.BlockSpec((SSM_WIDTH, 2 * SSM_WIDTH), lambda i: (0, 0)),
            pl.BlockSpec((1, 2 * SSM_WIDTH), lambda i: (0, 0)),
            pl.BlockSpec((SSM_WIDTH, D_MODEL), lambda i: (0, 0)),
            pl.BlockSpec((tm, D_MODEL), lambda i: (i, COL_G0 // D_MODEL)),
        ],
        out_specs=pl.BlockSpec((tm, D_MODEL), lambda i: (i, 0)),
        out_shape=jax.ShapeDtypeStruct((t, D_MODEL), BF16),
        compiler_params=_params(("arbitrary",)),
        name="glu",
    )(z3, jnp.asarray(perm, BF16), w_glu, b_glu, w_br, proj)


def _rope_pattern():
    half = ROPE_DIM // 2
    inv_freq = (np.float32(ROPE_THETA) ** (-np.arange(half, dtype=np.float32) / np.float32(half))).astype(np.float32)
    d = np.arange(LANES) % HEAD_DIM
    rotated = d < ROPE_DIM
    pat = np.zeros((16, LANES), np.float32)
    pat[:half] = rotated[None, :] & ((d % half)[None, :] == np.arange(half)[:, None])
    pat[8] = ~rotated
    pat[9] = np.where(d < half, -1.0, 0.0)
    pat[10] = np.where((d >= half) & rotated, 1.0, 0.0)
    return inv_freq.reshape(half, 1), pat


def _attn_kernel(q_ref, k_ref, v_ref, pos_ref, freq_ref, pat_ref, sink_ref, wbr_ref, ysg_ref, g1_ref,
                 o_ref, qbuf, kbuf, vbuf, obuf, sbuf, pbuf):
    i = pl.program_id(0)
    tq = q_ref.shape[0]
    nw = tq // WINDOW
    half = ROPE_DIM // 2

    @pl.when(i == 0)
    def _():
        kbuf[:, 0:WINDOW, :] = jnp.zeros((2 * N_KV_HEADS, WINDOW, LANES), BF16)
        vbuf[:, 0:WINDOW, :] = jnp.zeros((2 * N_KV_HEADS, WINDOW, LANES), BF16)

    ang = freq_ref[...] * pos_ref[...].astype(F32)
    spread = lambda tab: lax.dot_general(tab, pat_ref[0:8, :], (((0,), (0,)), ((), ())),
                                         preferred_element_type=F32, precision=HIGHEST)
    cs = spread(jnp.cos(ang)) + pat_ref[8:9, :]
    sn = spread(jnp.sin(ang))
    c_up = sn * pat_ref[9:10, :]
    c_dn = sn * pat_ref[10:11, :]

    def rope(x):
        return (x * cs + pltpu.roll(x, LANES - half, 1) * c_up + pltpu.roll(x, half, 1) * c_dn)

    low = lax.broadcasted_iota(I32, (tq, LANES), 1) < HEAD_DIM

    def split_heads(buf, cb, x):
        xs = pltpu.roll(x, HEAD_DIM, 1)
        zero = jnp.zeros_like(x)
        buf[4 * cb + 0, WINDOW:, :] = jnp.where(low, x, zero).astype(BF16)
        buf[4 * cb + 1, WINDOW:, :] = jnp.where(low, zero, xs).astype(BF16)
        buf[4 * cb + 2, WINDOW:, :] = jnp.where(low, xs, zero).astype(BF16)
        buf[4 * cb + 3, WINDOW:, :] = jnp.where(low, zero, x).astype(BF16)

    scale = HEAD_DIM ** -0.5
    for cb in range(Q_WIDTH // LANES):
        sl = slice(cb * LANES, (cb + 1) * LANES)
        qbuf[:, sl] = (rope(q_ref[:, sl].astype(F32)) * scale).astype(BF16)
    for cb in range(KV_WIDTH // LANES):
        sl = slice(cb * LANES, (cb + 1) * LANES)
        split_heads(kbuf, cb, rope(k_ref[:, sl].astype(F32)))
        split_heads(vbuf, cb, v_ref[:, sl].astype(F32))

    kj = lax.broadcasted_iota(I32, (2 * WINDOW, WINDOW), 0)
    qi = lax.broadcasted_iota(I32, (2 * WINDOW, WINDOW), 1)
    dist = qi + WINDOW - kj
    in_band = (dist >= 0) & (dist < WINDOW)
    cur_only = kj >= WINDOW
    sinks = sink_ref[...]

    def window(w, carry):
        r0 = pl.multiple_of(w * WINDOW, WINDOW)
        rows = pl.ds(r0, 2 * WINDOW)
        not_first = (i * nw + w) > 0
        mask = in_band & (cur_only | not_first)

        for h in range(N_Q_HEADS):
            qp = qbuf[pl.ds(r0, WINDOW), (h // 2) * LANES:(h // 2 + 1) * LANES]
            sbuf[h] = _dot_nt(kbuf[2 * (h // Q_PER_KV) + h % 2, rows, :], qp)
        for h in range(N_Q_HEADS):
            s = jnp.where(mask, sbuf[h], -jnp.inf)
            sink = sinks[:, h:h + 1]
            m = jnp.maximum(jnp.max(s, axis=0, keepdims=True), sink)
            p = jnp.exp(s - m)
            denom = jnp.sum(p, axis=0, keepdims=True) + jnp.exp(sink - m)
            pbuf[h] = (p * (1.0 / denom)).astype(BF16)
        for a in range(N_Q_HEADS // 2):
            kv = (2 * a) // Q_PER_KV
            o = _dot_tn(pbuf[2 * a], vbuf[2 * kv, rows, :]) + _dot_tn(pbuf[2 * a + 1], vbuf[2 * kv + 1, rows, :])
            obuf[pl.ds(r0, WINDOW), a * LANES:(a + 1) * LANES] = o.astype(BF16)
        return carry

    lax.fori_loop(0, nw, window, 0)

    kbuf[:, 0:WINDOW, :] = kbuf[:, tq:tq + WINDOW, :]
    vbuf[:, 0:WINDOW, :] = vbuf[:, tq:tq + WINDOW, :]

    y = _dot(obuf[...], wbr_ref[...])
    o_ref[...] = (ysg_ref[...].astype(F32) + _sigmoid(g1_ref[...].astype(F32)) * y).astype(o_ref.dtype)


def _attn(proj, pos_row, sinks, w_br, ysg, tq=512):
    t = proj.shape[0]
    freq, pat = (jnp.asarray(a) for a in _rope_pattern())
    return pl.pallas_call(
        _attn_kernel,
        grid=(t // tq,),
        in_specs=[
            pl.BlockSpec((tq, Q_WIDTH), lambda i: (i, COL_Q // Q_WIDTH)),
            pl.BlockSpec((tq, KV_WIDTH), lambda i: (i, COL_K // KV_WIDTH)),
            pl.BlockSpec((tq, KV_WIDTH), lambda i: (i, COL_V // KV_WIDTH)),
            pl.BlockSpec((1, tq), lambda i: (0, i)),
            pl.BlockSpec((ROPE_DIM // 2, 1), lambda i: (0, 0)),
            pl.BlockSpec((16, LANES), lambda i: (0, 0)),
            pl.BlockSpec((1, N_Q_HEADS), lambda i: (0, 0)),
            pl.BlockSpec((Q_WIDTH, D_MODEL), lambda i: (0, 0)),
            pl.BlockSpec((tq, D_MODEL), lambda i: (i, 0)),
            pl.BlockSpec((tq, D_MODEL), lambda i: (i, COL_G1 // D_MODEL)),
        ],
        out_specs=pl.BlockSpec((tq, D_MODEL), lambda i: (i, 0)),
        out_shape=jax.ShapeDtypeStruct((t, D_MODEL), BF16),
        scratch_shapes=[
            pltpu.VMEM((tq, Q_WIDTH), BF16),
            pltpu.VMEM((2 * N_KV_HEADS, tq + WINDOW, LANES), BF16),
            pltpu.VMEM((2 * N_KV_HEADS, tq + WINDOW, LANES), BF16),
            pltpu.VMEM((tq, Q_WIDTH), BF16),
            pltpu.VMEM((N_Q_HEADS, 2 * WINDOW, WINDOW), F32),
            pltpu.VMEM((N_Q_HEADS, 2 * WINDOW, WINDOW), BF16),
        ],
        compiler_params=_params(("arbitrary",)),
        name="attn",
    )(proj, proj, proj, pos_row, freq, pat, sinks, w_br, ysg, proj)


def _oproj_kernel(x_ref, mix_ref, wo_ref, g_ref, wrh_ref, wrl_ref, br_ref, h_ref, hp_ref, lt_ref):
    h = x_ref[...] + _dot(mix_ref[...], wo_ref[...])
    h_ref[...] = h
    ms = jnp.mean(h * h, axis=-1, keepdims=True)
    hn = h * lax.rsqrt(ms + EPS) * g_ref[...]
    _store_token_tiles(hp_ref, hn)
    hn_hi = hn.astype(BF16)
    hn_lo = (hn - hn_hi.astype(F32)).astype(BF16)
    logits = (_dot(hn_hi, wrh_ref[...]) + _dot(hn_lo, wrh_ref[...]) + _dot(hn_hi, wrl_ref[...])
              + br_ref[...])
    lt_ref[...] = logits.T


def _oproj(x, mixed, w_o, g, w_router, b_router, tm=512):
    t = x.shape[0]
    w_router_hi = w_router.astype(BF16)
    w_router_hi_rest = (w_router - w_router_hi.astype(F32)).astype(BF16)
    return pl.pallas_call(
        _oproj_kernel,
        grid=(t // tm,),
        in_specs=[
            pl.BlockSpec((tm, D_MODEL), lambda i: (i, 0)),
            pl.BlockSpec((tm, D_MODEL), lambda i: (i, 0)),
            pl.BlockSpec((D_MODEL, D_MODEL), lambda i: (0, 0)),
            pl.BlockSpec((1, D_MODEL), lambda i: (0, 0)),
            pl.BlockSpec((D_MODEL, LANES), lambda i: (0, 0)),
            pl.BlockSpec((D_MODEL, LANES), lambda i: (0, 0)),
            pl.BlockSpec((1, LANES), lambda i: (0, 0)),
        ],
        out_specs=[
            pl.BlockSpec((tm, D_MODEL), lambda i: (i, 0)),
            pl.BlockSpec((tm * SUBLANES, LANES), lambda i: (i, 0)),
            pl.BlockSpec((LANES, tm), lambda i: (0, i)),
        ],
        out_shape=[
            jax.ShapeDtypeStruct((t, D_MODEL), F32),
            jax.ShapeDtypeStruct((t * SUBLANES, LANES), jnp.uint32),
            jax.ShapeDtypeStruct((LANES, t), F32),
        ],
        compiler_params=_params(("arbitrary",)),
        name="oproj",
    )(x, mixed, w_o, g, w_router_hi, w_router_hi_rest, b_router)


ROUTE_CHUNK = 256


def _route_kernel(lt_ref, dest_ref, wts_ref, bexp_ref, nvalid_ref, eid_ref, rank_ref):
    t = lt_ref.shape[1]
    nc = t // ROUTE_CHUNK
    r8 = lax.broadcasted_iota(I32, (N_GROUPS, ROUTE_CHUNK), 0)
    r64 = lax.broadcasted_iota(I32, (N_EXPERTS, ROUTE_CHUNK), 0)

    def pick(c, carry):
        cols = pl.ds(pl.multiple_of(c * ROUTE_CHUNK, ROUTE_CHUNK), ROUTE_CHUNK)
        lg = lt_ref[0:N_GROUPS, cols]
        m = jnp.max(lg, axis=0, keepdims=True)
        ssum = jnp.sum(jnp.exp(lg - m), axis=0, keepdims=True)
        p_grp = 1.0 / ssum
        grp = jnp.min(jnp.where(lg == m, r8, N_GROUPS), axis=0, keepdims=True)
        le = lt_ref[N_GROUPS:N_GROUPS + N_EXPERTS, cols]
        leg = jnp.where((r64 // EXPERTS_PER_GROUP) == grp, le, -jnp.inf)
        m1 = jnp.max(leg, axis=0, keepdims=True)
        i1 = jnp.min(jnp.where(leg == m1, r64, N_EXPERTS), axis=0, keepdims=True)
        leg2 = jnp.where(r64 == i1, -jnp.inf, leg)
        m2 = jnp.max(leg2, axis=0, keepdims=True)
        i2 = jnp.min(jnp.where(leg2 == m2, r64, N_EXPERTS), axis=0, keepdims=True)
        ex = jnp.exp(m2 - m1)
        eid_ref[0:1, cols] = i1
        eid_ref[1:2, cols] = i2
        wts_ref[0:1, cols] = p_grp / (1.0 + ex)
        wts_ref[1:2, cols] = p_grp * ex / (1.0 + ex)
        return carry

    lax.fori_loop(0, nc, pick, 0)

    a_row = lax.broadcasted_iota(I32, (ROUTE_CHUNK, ROUTE_CHUNK), 0)
    a_col = lax.broadcasted_iota(I32, (ROUTE_CHUNK, ROUTE_CHUNK), 1)
    before = (a_row < a_col).astype(BF16)

    def count(n, carry):
        j = n // nc
        c = n - j * nc
        cols = pl.ds(pl.multiple_of(c * ROUTE_CHUNK, ROUTE_CHUNK), ROUTE_CHUNK)
        oh = r64 == eid_ref[pl.ds(j, 1), cols]
        ohf = oh.astype(F32)
        pref = _dot(ohf.astype(BF16), before) + carry
        rank_ref[pl.ds(j, 1), cols] = jnp.sum(jnp.where(oh, pref, 0.0), axis=0, keepdims=True)
        return carry + jnp.sum(ohf, axis=1, keepdims=True)

    counts = lax.fori_loop(0, TOP_K * nc, count, jnp.zeros((N_EXPERTS, 1), F32))

    padded = jnp.floor((counts + (MOE_BLOCK - 1)) * (1.0 / MOE_BLOCK)) * MOE_BLOCK
    e_row = lax.broadcasted_iota(I32, (N_EXPERTS, N_EXPERTS), 0)
    e_col = lax.broadcasted_iota(I32, (N_EXPERTS, N_EXPERTS), 1)
    incl = (e_col <= e_row).astype(F32)
    pad_end = _dot(incl, jnp.broadcast_to(padded, (N_EXPERTS, LANES)), precision=HIGHEST)[:, 0:1]
    pad_start = pad_end - padded

    def place(n, carry):
        j = n // nc
        c = n - j * nc
        cols = pl.ds(pl.multiple_of(c * ROUTE_CHUNK, ROUTE_CHUNK), ROUTE_CHUNK)
        oh = r64 == eid_ref[pl.ds(j, 1), cols]
        start = jnp.sum(jnp.where(oh, pad_start, 0.0), axis=0, keepdims=True)
        dest_ref[pl.ds(j, 1), cols] = (start + rank_ref[pl.ds(j, 1), cols]).astype(I32)
        return carry

    lax.fori_loop(0, TOP_K * nc, place, 0)

    b0 = (lax.broadcasted_iota(I32, (N_EXPERTS, bexp_ref.shape[1]), 1) * MOE_BLOCK).astype(F32)
    n_done = jnp.sum((pad_end <= b0).astype(F32), axis=0, keepdims=True)
    bexp_ref[...] = jnp.minimum(n_done, float(N_EXPERTS - 1)).astype(I32)
    live = jnp.minimum(pad_start + counts, b0 + MOE_BLOCK) - jnp.maximum(pad_start, b0)
    nvalid_ref[...] = jnp.sum(jnp.maximum(live, 0.0), axis=0, keepdims=True).astype(I32)


def _route(logits_t, n_blocks_pad):
    t = logits_t.shape[1]
    return pl.pallas_call(
        _route_kernel,
        out_shape=[
            jax.ShapeDtypeStruct((TOP_K, t), I32),
            jax.ShapeDtypeStruct((TOP_K, t), F32),
            jax.ShapeDtypeStruct((1, n_blocks_pad), I32),
            jax.ShapeDtypeStruct((1, n_blocks_pad), I32),
        ],
        scratch_shapes=[pltpu.VMEM((TOP_K, t), I32), pltpu.VMEM((TOP_K, t), F32)],
        compiler_params=pltpu.CompilerParams(vmem_limit_bytes=VMEM_LIMIT),
        name="route",
    )(logits_t)


def _invmap_kernel(dest_ref, rt_ref):
    n_rows = rt_ref.shape[0]
    t = dest_ref.shape[0] // TOP_K

    def clear(r, c):
        rt_ref[r] = 0
        return c

    lax.fori_loop(0, n_rows, clear, 0, unroll=16)

    def put(tok, c):
        rt_ref[dest_ref[tok]] = tok
        rt_ref[dest_ref[t + tok]] = tok
        return c

    lax.fori_loop(0, t, put, 0, unroll=8)


def _invmap(dest_flat, n_rows):
    return pl.pallas_call(
        _invmap_kernel,
        in_specs=[pl.BlockSpec(memory_space=pltpu.SMEM)],
        out_specs=pl.BlockSpec(memory_space=pltpu.SMEM),
        out_shape=jax.ShapeDtypeStruct((n_rows,), I32),
        name="invmap",
    )(dest_flat)


GATHER_SLOTS = 4


def _row_copy(src_hbm, src_row, dst_buf, slot, dst_row, sem):
    return pltpu.make_async_copy(src_hbm.at[pl.ds(pl.multiple_of(src_row * SUBLANES, SUBLANES), SUBLANES), :],
                                 dst_buf.at[slot, pl.ds(dst_row * SUBLANES, SUBLANES), :], sem.at[slot])


def _rows_copy(src_hbm, dst_buf, slot, rows, sem):
    n = rows * SUBLANES
    return pltpu.make_async_copy(src_hbm.at[pl.ds(0, n), :], dst_buf.at[slot, pl.ds(0, n), :], sem.at[slot])


def _expert_kernel(bexp_ref, nvalid_ref, rt_ref, hp_hbm, wg_hbm, wu_hbm, wd_hbm, ys_ref,
                   xbuf, wf_g, wf_u, wf_d, wb_g, wb_u, wb_d, ord_ref, xsem, wsem):
    b = pl.program_id(0)
    nb = pl.num_programs(0) - 1

    def expert_of(blk):
        return bexp_ref[jnp.minimum(blk, nb - 1)]

    def next_owner(blk, e):
        return lax.while_loop(lambda j: (j < nb) & (expert_of(j) == e), lambda j: j + 1, blk)

    def weight_copies(e, slot):
        return (pltpu.make_async_copy(wg_hbm.at[e], wf_g.at[slot], wsem.at[slot, 0]),
                pltpu.make_async_copy(wu_hbm.at[e], wf_u.at[slot], wsem.at[slot, 1]),
                pltpu.make_async_copy(wd_hbm.at[e], wf_d.at[slot], wsem.at[slot, 2]))

    @pl.when(b == 0)
    def _():
        ord_ref[0] = 0
        e0 = bexp_ref[0]
        for c in weight_copies(e0, 0):
            c.start(priority=1)
        n1 = next_owner(1, e0)

        @pl.when(n1 < nb)
        def _():
            for c in weight_copies(expert_of(n1), 1):
                c.start(priority=1)

    look = GATHER_SLOTS - 1

    def live(blk):
        return nvalid_ref[jnp.clip(blk, 0, nb - 1)] > 0

    def gather_started(blk):
        return (blk < look) | live(blk - look)

    def start_rows(blk, first, last):
        base = blk * MOE_BLOCK
        slot = blk % GATHER_SLOTS
        for r in range(first, last):
            _row_copy(hp_hbm, rt_ref[base + r], xbuf, slot, r, xsem).start()

    def wait_rows(blk):
        _rows_copy(hp_hbm, xbuf, blk % GATHER_SLOTS, MOE_BLOCK, xsem).wait()

    @pl.when(b == 0)
    def _():
        def body(i, c):
            for k in range(look):
                for s in range(SUBLANES):
                    r = i * SUBLANES + s
                    _row_copy(hp_hbm, rt_ref[k * MOE_BLOCK + r], xbuf, k, r, xsem).start()
            return c

        lax.fori_loop(0, MOE_BLOCK // SUBLANES, body, 0)

    @pl.when(b > 0)
    def _():
        blk = b - 1
        e = bexp_ref[blk]
        first = (blk == 0) | (e != bexp_ref[jnp.maximum(blk - 1, 0)])

        @pl.when(first)
        def _():
            n = ord_ref[0]
            wslot = n % 2
            ord_ref[0] = n + 1
            for c in weight_copies(e, wslot):
                c.wait()
            wb_g[...] = wf_g[wslot].astype(BF16)
            wb_u[...] = wf_u[wslot].astype(BF16)
            wb_d[...] = wf_d[wslot].astype(BF16)
            n1 = next_owner(blk + 1, e)
            n2 = next_owner(n1 + 1, expert_of(n1))

            @pl.when((n1 < nb) & (n2 < nb))
            def _():
                for c in weight_copies(expert_of(n2), wslot):
                    c.start(priority=1)

        slot = blk % GATHER_SLOTS
        quarter = MOE_BLOCK // 4

        @pl.when(live(blk))
        def _():
            wait_rows(blk)
            lo, hi = (v.astype(BF16) for v in _load_token_tiles(xbuf, slot, 0, MOE_BLOCK))
            start_rows(blk + look, 0, quarter)
            g = _dot(lo, wb_g[:HALF, :]) + _dot(hi, wb_g[HALF:, :])
            start_rows(blk + look, quarter, 2 * quarter)
            u = _dot(lo, wb_u[:HALF, :]) + _dot(hi, wb_u[HALF:, :])
            start_rows(blk + look, 2 * quarter, 3 * quarter)
            h = (g * _sigmoid(g) * u).astype(BF16)
            y = _dot(h, wb_d[...])
            start_rows(blk + look, 3 * quarter, MOE_BLOCK)
            _store_token_tiles(ys_ref, y)

        @pl.when(jnp.logical_not(live(blk)))
        def _():
            @pl.when(gather_started(blk))
            def _():
                wait_rows(blk)

            _store_token_tiles(ys_ref, jnp.zeros((MOE_BLOCK, D_MODEL), F32))

        @pl.when(b == nb)
        def _():
            for k in range(look):
                @pl.when(gather_started(nb + k))
                def _():
                    wait_rows(nb + k)


def _experts(bexp, nvalid, row_tok, hn_packed, w_gate, w_up, w_down, n_blocks):
    grid_spec = pltpu.PrefetchScalarGridSpec(
        num_scalar_prefetch=3,
        grid=(n_blocks + 1,),
        in_specs=[pl.BlockSpec(memory_space=pl.ANY)] * 4,
        out_specs=pl.BlockSpec((MOE_BLOCK * SUBLANES, LANES), lambda b, be, nv, rt: (jnp.maximum(b - 1, 0), 0)),
        scratch_shapes=[
            pltpu.VMEM((GATHER_SLOTS, MOE_BLOCK * SUBLANES, LANES), jnp.uint32),
            pltpu.VMEM((2, D_MODEL, D_EXPERT), F32), pltpu.VMEM((2, D_MODEL, D_EXPERT), F32),
            pltpu.VMEM((2, D_EXPERT, D_MODEL), F32),
            pltpu.VMEM((D_MODEL, D_EXPERT), BF16), pltpu.VMEM((D_MODEL, D_EXPERT), BF16),
            pltpu.VMEM((D_EXPERT, D_MODEL), BF16),
            pltpu.SMEM((1,), I32),
            pltpu.SemaphoreType.DMA((GATHER_SLOTS,)), pltpu.SemaphoreType.DMA((2, 3)),
        ],
    )
    return pl.pallas_call(
        _expert_kernel,
        grid_spec=grid_spec,
        out_shape=jax.ShapeDtypeStruct((n_blocks * MOE_BLOCK * SUBLANES, LANES), jnp.uint32),
        compiler_params=_params(("arbitrary",)),
        name="experts",
    )(bexp, nvalid, row_tok, hn_packed, w_gate, w_up, w_down)


def _combine_kernel(dest_ref, ys_hbm, h_ref, w_ref, g_ref, o_ref, ybuf, sem):
    i = pl.program_id(0)
    n = pl.num_programs(0)
    tq = h_ref.shape[0]
    t = n * tq

    def start_gather(blk, slot):
        base = blk * tq

        def body(j, c):
            for s in range(SUBLANES):
                r = j * SUBLANES + s
                _row_copy(ys_hbm, dest_ref[base + r], ybuf, slot, r, sem).start(priority=0)
                _row_copy(ys_hbm, dest_ref[t + base + r], ybuf, slot, tq + r, sem).start(priority=1)
            return c

        lax.fori_loop(0, tq // SUBLANES, body, 0)

    @pl.when(i == 0)
    def _():
        start_gather(0, 0)

    @pl.when(i + 1 < n)
    def _():
        start_gather(i + 1, (i + 1) % 2)

    slot = i % 2
    _rows_copy(ys_hbm, ybuf, slot, TOP_K * tq, sem).wait()
    y0_lo, y0_hi = _load_token_tiles(ybuf, slot, 0, tq)
    y1_lo, y1_hi = _load_token_tiles(ybuf, slot, tq, tq)
    w = w_ref[...]
    w0 = w[:, 0:1]
    w1 = w[:, 1:2]
    h = h_ref[...]
    lo = h[:, :HALF] + w0 * y0_lo + w1 * y1_lo
    hi = h[:, HALF:] + w0 * y0_hi + w1 * y1_hi
    ms = (jnp.sum(lo * lo, axis=-1, keepdims=True) + jnp.sum(hi * hi, axis=-1, keepdims=True)) * (1.0 / D_MODEL)
    inv = lax.rsqrt(ms + EPS)
    g = g_ref[...]
    o_ref[:, :HALF] = lo * inv * g[:, :HALF]
    o_ref[:, HALF:] = hi * inv * g[:, HALF:]


def _combine(dest_flat, ys_packed, h1, wts_tok, g_final, tq=256):
    t = h1.shape[0]
    grid_spec = pltpu.PrefetchScalarGridSpec(
        num_scalar_prefetch=1,
        grid=(t // tq,),
        in_specs=[
            pl.BlockSpec(memory_space=pl.ANY),
            pl.BlockSpec((tq, D_MODEL), lambda i, d: (i, 0)),
            pl.BlockSpec((tq, TOP_K), lambda i, d: (i, 0)),
            pl.BlockSpec((1, D_MODEL), lambda i, d: (0, 0)),
        ],
        out_specs=pl.BlockSpec((tq, D_MODEL), lambda i, d: (i, 0)),
        scratch_shapes=[pltpu.VMEM((2, TOP_K * tq * SUBLANES, LANES), jnp.uint32),
                        pltpu.SemaphoreType.DMA((2,))],
    )
    return pl.pallas_call(
        _combine_kernel,
        grid_spec=grid_spec,
        out_shape=jax.ShapeDtypeStruct((t, D_MODEL), F32),
        compiler_params=_params(("arbitrary",)),
        name="combine",
    )(dest_flat, ys_packed, h1, wts_tok, g_final)


def _mixers(h, positions, norm_g, w_in, b_in, a_re, a_im, log_dt, b_re, b_im, c_re, c_im, d_skip,
            w_glu, b_glu, w_br_ssm, sinks, w_br_attn):
    t = h.shape[0]
    ngate = 2 * D_MODEL
    g = norm_g.reshape(1, D_MODEL)
    b_row = b_in.reshape(1, IN_WIDTH)
    proj = _inproj(h, g, w_in, b_row, chunk_major=False,
                   segments=((IN_WIDTH - ngate, ngate), (SSM_WIDTH, IN_WIDTH - ngate - SSM_WIDTH)))
    u3 = _inproj(h, g, w_in, b_row, chunk_major=True, segments=((0, SSM_WIDTH),))

    tmat, wsr, wsi, rxr, rxi, a16r, a16i = _ssm_prep(
        a_re, a_im, log_dt, b_re.transpose(0, 2, 1), b_im.transpose(0, 2, 1), c_re, c_im)
    d_tiled = jnp.tile(d_skip, (1, CHUNK)).reshape(SSM_GROUPS, 1, CW)
    z3 = _ssm(u3, tmat, wsr, wsi, rxr, rxi, a16r.reshape(SSM_GROUPS, SSM_STATE),
              a16i.reshape(SSM_GROUPS, SSM_STATE), d_tiled)

    ysg = _glu(z3, w_glu.astype(BF16), b_glu.reshape(1, -1), w_br_ssm.astype(BF16), proj)
    return _attn(proj, positions.reshape(1, t), sinks.reshape(1, N_Q_HEADS), w_br_attn.astype(BF16), ysg)


def _moe_tail(h, mixed, w_o, norm_ffn_g, w_rg, b_rg, w_re, b_re, w_gate, w_up, w_down, norm_final_g):
    t = h.shape[0]
    n_route = N_GROUPS + N_EXPERTS
    w_router = jnp.concatenate([w_rg, w_re, jnp.zeros((D_MODEL, LANES - n_route), F32)], axis=1)
    b_router = jnp.concatenate([b_rg, b_re, jnp.zeros((LANES - n_route,), F32)]).reshape(1, LANES)
    h1, hn_packed, logits_t = _oproj(h, mixed, w_o.astype(BF16), norm_ffn_g.reshape(1, D_MODEL),
                                     w_router, b_router)
    n_assign = t * TOP_K
    n_blocks = -(-(n_assign + N_EXPERTS * (MOE_BLOCK - 1)) // MOE_BLOCK)
    n_blocks_pad = -(-n_blocks // LANES) * LANES
    dest, wts, bexp, nvalid = _route(logits_t, n_blocks_pad)
    dest_flat = dest.reshape(n_assign)
    row_tok = _invmap(dest_flat, (n_blocks + GATHER_SLOTS - 1) * MOE_BLOCK)
    ys_packed = _experts(bexp[0, :n_blocks], nvalid[0, :n_blocks], row_tok, hn_packed, w_gate, w_up, w_down,
                         n_blocks)
    return _combine(dest_flat, ys_packed, h1, wts.T, norm_final_g.reshape(1, D_MODEL))


def kernel(x, positions, norm_mix_g, w_in, b_in, ssm_a_re, ssm_a_im, ssm_log_dt, ssm_b_re, ssm_b_im, ssm_c_re, ssm_c_im, ssm_d, w_glu, b_glu, w_br_ssm, attn_sinks, w_br_attn, w_o, norm_ffn_g, w_router_group, b_router_group, w_router_expert, b_router_expert, w_exp_gate, w_exp_up, w_exp_down, norm_final_g):
    bsz, seq, d = x.shape
    assert bsz == 1 and d == D_MODEL and norm_mix_g.shape[0] == 1
    h = x.reshape(seq, d)
    mixed = _mixers(h, positions, norm_mix_g[0], w_in[0], b_in[0], ssm_a_re[0], ssm_a_im[0], ssm_log_dt[0],
                    ssm_b_re[0], ssm_b_im[0], ssm_c_re[0], ssm_c_im[0], ssm_d[0], w_glu[0], b_glu[0],
                    w_br_ssm[0], attn_sinks[0], w_br_attn[0])
    out = _moe_tail(h, mixed, w_o[0], norm_ffn_g[0], w_router_group[0], b_router_group[0],
                    w_router_expert[0], b_router_expert[0], w_exp_gate[0], w_exp_up[0], w_exp_down[0],
                    norm_final_g)
    return out.reshape(bsz, seq, d)
```

```python
import functools
import math

import numpy as np
import jax
import jax.numpy as jnp
from jax import lax
from jax.experimental import pallas as pl
from jax.experimental.pallas import tpu as pltpu

F32 = jnp.float32
BF16 = jnp.bfloat16
I32 = jnp.int32

D_MODEL = 2048
SSM_WIDTH = 1024
SSM_GROUP = 16
SSM_GROUPS = 64
SSM_STATE = 64
HEAD_DIM = 64
N_Q_HEADS = 16
N_KV_HEADS = 4
Q_PER_KV = 4
WINDOW = 128
ROPE_DIM = 16
ROPE_THETA = 500000.0
Q_WIDTH = 1024
KV_WIDTH = 256
IN_WIDTH = SSM_WIDTH + Q_WIDTH + 2 * KV_WIDTH + 2 * D_MODEL
N_GROUPS = 8
EXPERTS_PER_GROUP = 8
N_EXPERTS = 64
TOP_K = 2
D_EXPERT = 512
MOE_BLOCK = 128
EPS = 1e-6

CHUNK = 16
CW = CHUNK * SSM_GROUP
GROUP_BLOCK = 8
HALF = D_MODEL // 2
LANES = 128
VMEM_LIMIT = 56 * 1024 * 1024

COL_G0, COL_G1, COL_Q, COL_K, COL_V = 0, 2048, 4096, 5120, 5376

HIGHEST = lax.Precision.HIGHEST


def _dot(a, b, precision=None):
    return jnp.dot(a, b, preferred_element_type=F32, precision=precision)


def _dot_nt(a, b, precision=None):
    return lax.dot_general(a, b, (((1,), (1,)), ((), ())), preferred_element_type=F32,
                           precision=precision)


def _dot_tn(a, b):
    return lax.dot_general(a, b, (((0,), (0,)), ((), ())), preferred_element_type=F32)


def _sigmoid(x):
    return 1.0 / (1.0 + jnp.exp(-x))


def _pack_halves(lo, hi):
    return pltpu.pack_elementwise([lo, hi], packed_dtype=BF16)


def _unpack_half(w, index):
    return pltpu.unpack_elementwise(w, index=index, packed_dtype=BF16, unpacked_dtype=F32)


SUBLANES = 8
WORD_TILES = HALF // LANES
assert WORD_TILES == SUBLANES


def _store_token_tiles(ref, x):
    rows = x.shape[0]
    for s in range(WORD_TILES):
        ref[pl.ds(s, rows, stride=SUBLANES), :] = _pack_halves(x[:, s * LANES:(s + 1) * LANES],
                                                               x[:, HALF + s * LANES:HALF + (s + 1) * LANES])


def _load_token_tiles(ref, slot, first_row, rows):
    pieces = [ref[slot, pl.ds(first_row * SUBLANES + s, rows, stride=SUBLANES), :] for s in range(WORD_TILES)]
    return (jnp.concatenate([_unpack_half(p, 0) for p in pieces], axis=1),
            jnp.concatenate([_unpack_half(p, 1) for p in pieces], axis=1))


def _params(sem, vmem=VMEM_LIMIT):
    return pltpu.CompilerParams(dimension_semantics=sem, vmem_limit_bytes=vmem)


def _inproj_kernel(x_ref, g_ref, w_ref, b_ref, o_ref, xn_ref, *rest, chunk_major):
    @pl.when(pl.program_id(1) == 0)
    def _():
        x = x_ref[...]
        ms = jnp.mean(x * x, axis=-1, keepdims=True)
        xn_ref[...] = (x * lax.rsqrt(ms + EPS) * g_ref[...]).astype(BF16)

    acc = _dot(xn_ref[...], w_ref[...].astype(BF16)) + b_ref[...]
    if not chunk_major:
        o_ref[...] = acc.astype(o_ref.dtype)
        return
    (acc_ref,) = rest
    nk = o_ref.shape[1]
    for c in range(acc_ref.shape[0]):
        acc_ref[c] = acc[:, c * LANES:(c + 1) * LANES]
    for s in range(CHUNK):
        for c in range(acc_ref.shape[0]):
            o_ref[s, :, c * LANES:(c + 1) * LANES] = acc_ref[c, pl.ds(s, nk, stride=CHUNK), :].astype(o_ref.dtype)


def _inproj(x, g, w, b, *, chunk_major, segments, tm=1024, tn=512):
    t, d = x.shape
    (c0, n0), (c1, n1) = (tuple(segments) + ((0, 0),))[:2]
    assert all(v % tn == 0 for v in (c0, n0, c1, n1))
    n = n0 + n1
    p0, q0, q1 = n0 // tn, c0 // tn, c1 // tn
    panel = lambda i, j: (0, jnp.where(j < p0, j + q0, j - p0 + q1))
    scratch = [pltpu.VMEM((tm, d), BF16)]
    if chunk_major:
        out_spec = pl.BlockSpec((CHUNK, tm // CHUNK, tn), lambda i, j: (0, i, j))
        out_shape = jax.ShapeDtypeStruct((CHUNK, t // CHUNK, n), BF16)
        scratch.append(pltpu.VMEM((tn // LANES, tm, LANES), F32))
    else:
        out_spec = pl.BlockSpec((tm, tn), lambda i, j: (i, j))
        out_shape = jax.ShapeDtypeStruct((t, n), BF16)
    return pl.pallas_call(
        functools.partial(_inproj_kernel, chunk_major=chunk_major),
        grid=(t // tm, n // tn),
        in_specs=[
            pl.BlockSpec((tm, d), lambda i, j: (i, 0)),
            pl.BlockSpec((1, d), lambda i, j: (0, 0)),
            pl.BlockSpec((d, tn), panel),
            pl.BlockSpec((1, tn), panel),
        ],
        out_specs=out_spec,
        out_shape=out_shape,
        scratch_shapes=scratch,
        compiler_params=_params(("arbitrary", "arbitrary")),
        name="inproj_u" if chunk_major else "inproj",
    )(x, g, w, b)


def _ssm_prep_kernel(are_ref, aim_ref, ldt_ref, btr_ref, bti_ref, cr_ref, ci_ref,
                     t_ref, wsr_ref, wsi_ref, rxr_ref, rxi_ref, a16r_ref, a16i_ref):
    lam_re = jnp.minimum(are_ref[0], -1e-4)
    lam_im = aim_ref[0]
    dt = jnp.exp(ldt_ref[0])
    lr_dt = lam_re * dt
    th = lam_im * dt
    mag = jnp.exp(lr_dt)
    ab_re = mag * jnp.cos(th)
    ab_im = mag * jnp.sin(th)
    den = lam_re * lam_re + lam_im * lam_im
    nr = ab_re - 1.0
    ni = ab_im
    coef_re = (nr * lam_re + ni * lam_im) / den
    coef_im = (ni * lam_re - nr * lam_im) / den
    btr = btr_ref[0]
    bti = bti_ref[0]
    bb_re = coef_re * btr - coef_im * bti
    bb_im = coef_re * bti + coef_im * btr

    e = lax.broadcasted_iota(I32, (CHUNK, SSM_STATE), 0).astype(F32)
    pmag = jnp.exp(e * lr_dt)
    pos_re = pmag * jnp.cos(e * th)
    pos_im = pmag * jnp.sin(e * th)
    nmag = jnp.exp(-e * lr_dt)
    neg_re = nmag * jnp.cos(e * th)
    neg_im = -nmag * jnp.sin(e * th)

    def rep(tab):
        return jnp.broadcast_to(tab[:, None, :], (CHUNK, SSM_GROUP, SSM_STATE)).reshape(CW, SSM_STATE)

    def tile(mat):
        return jnp.broadcast_to(mat[None, :, :], (CHUNK, SSM_GROUP, SSM_STATE)).reshape(CW, SSM_STATE)

    pr, pi = rep(pos_re), rep(pos_im)
    ctr, cti = tile(cr_ref[0]), tile(ci_ref[0])
    r_re = ctr * pr - cti * pi
    r_im = ctr * pi + cti * pr
    qr, qi = rep(neg_re), rep(neg_im)
    btr_t, bti_t = tile(bb_re), tile(bb_im)
    l_re = btr_t * qr - bti_t * qi
    l_im = btr_t * qi + bti_t * qr

    def split(v):
        head = v.astype(BF16)
        return head, (v - head.astype(F32)).astype(BF16)

    def dot_nt3(a, b):
        (ah, al), (bh, bl) = split(a), split(b)
        return _dot_nt(ah, bh) + _dot_nt(al, bh) + _dot_nt(ah, bl)

    tm = dot_nt3(l_re, r_re) - dot_nt3(l_im, r_im)
    srow = lax.broadcasted_iota(I32, (CW, CW), 0) // SSM_GROUP
    tcol = lax.broadcasted_iota(I32, (CW, CW), 1) // SSM_GROUP
    t_ref[0] = jnp.where(tcol >= srow, tm, 0.0).astype(BF16)

    a15r = pos_re[CHUNK - 1:CHUNK, :]
    a15i = pos_im[CHUNK - 1:CHUNK, :]
    wsr_ref[0] = (l_re * a15r - l_im * a15i).astype(BF16)
    wsi_ref[0] = (l_re * a15i + l_im * a15r).astype(BF16)
    rxr_ref[0] = (r_re * ab_re - r_im * ab_im).astype(BF16)
    rxi_ref[0] = (-(r_re * ab_im + r_im * ab_re)).astype(BF16)
    m16 = jnp.exp(float(CHUNK) * lr_dt)
    a16r_ref[0] = m16 * jnp.cos(float(CHUNK) * th)
    a16i_ref[0] = m16 * jnp.sin(float(CHUNK) * th)


def _ssm_prep_block_kernel(*refs):
    def one(j, carry):
        _ssm_prep_kernel(*[r.at[pl.ds(j, 1)] for r in refs])
        return carry

    lax.fori_loop(0, GROUP_BLOCK, one, 0)


def _ssm_prep(a_re, a_im, log_dt, bt_re, bt_im, c_re, c_im):
    g = a_re.shape[0]
    gb = GROUP_BLOCK
    vec = pl.BlockSpec((gb, 1, SSM_STATE), lambda i: (i, 0, 0))
    mat = pl.BlockSpec((gb, SSM_GROUP, SSM_STATE), lambda i: (i, 0, 0))
    wide = pl.BlockSpec((gb, CW, SSM_STATE), lambda i: (i, 0, 0))
    return pl.pallas_call(
        _ssm_prep_block_kernel,
        grid=(g // gb,),
        in_specs=[vec, vec, pl.BlockSpec((gb, 1, 1), lambda i: (i, 0, 0)), mat, mat, mat, mat],
        out_specs=[pl.BlockSpec((gb, CW, CW), lambda i: (i, 0, 0)), wide, wide, wide, wide, vec, vec],
        out_shape=[
            jax.ShapeDtypeStruct((g, CW, CW), BF16),
            jax.ShapeDtypeStruct((g, CW, SSM_STATE), BF16),
            jax.ShapeDtypeStruct((g, CW, SSM_STATE), BF16),
            jax.ShapeDtypeStruct((g, CW, SSM_STATE), BF16),
            jax.ShapeDtypeStruct((g, CW, SSM_STATE), BF16),
            jax.ShapeDtypeStruct((g, 1, SSM_STATE), F32),
            jax.ShapeDtypeStruct((g, 1, SSM_STATE), F32),
        ],
        compiler_params=_params(("arbitrary",)),
        name="ssm_prep",
    )(a_re.reshape(g, 1, SSM_STATE), a_im.reshape(g, 1, SSM_STATE), log_dt.reshape(g, 1, 1),
      bt_re, bt_im, c_re, c_im)


def _gelu_tanh(x):
    c = math.sqrt(2.0 / math.pi)
    return x * (0.5 * (1.0 + jnp.tanh(c * (x + 0.044715 * (x * x * x)))))


def _ssm_kernel(u_ref, t_ref, wsr_ref, wsi_ref, rxr_ref, rxi_ref, a16r_ref, a16i_ref, d_ref,
                z_ref, sr_ref, si_ref, ug_ref, zg_ref):
    nk = u_ref.shape[1]
    for j in range(GROUP_BLOCK):
        ug_ref[j] = jnp.concatenate(
            [u_ref[s, :, j * SSM_GROUP:(j + 1) * SSM_GROUP] for s in range(CHUNK)], axis=1)
    u_ref = ug_ref
    for j in range(GROUP_BLOCK):
        u = u_ref[j]
        sr_ref[j * nk:(j + 1) * nk, :] = _dot(u, wsr_ref[j])
        si_ref[j * nk:(j + 1) * nk, :] = _dot(u, wsi_ref[j])

    ar = a16r_ref[...]
    ai = a16i_ref[...]

    def step(k, carry):
        xr, xi = carry
        rows = pl.ds(k, GROUP_BLOCK, stride=nk)
        sr = sr_ref[rows, :]
        si = si_ref[rows, :]
        sr_ref[rows, :] = xr
        si_ref[rows, :] = xi
        return (ar * xr - ai * xi + sr, ar * xi + ai * xr + si)

    zero = jnp.zeros((GROUP_BLOCK, SSM_STATE), F32)
    lax.fori_loop(0, nk, step, (zero, zero), unroll=4)

    for j in range(GROUP_BLOCK):
        u = u_ref[j]
        xr = sr_ref[j * nk:(j + 1) * nk, :].astype(BF16)
        xi = si_ref[j * nk:(j + 1) * nk, :].astype(BF16)
        y = (_dot(u, t_ref[j]) + _dot_nt(xr, rxr_ref[j]) + _dot_nt(xi, rxi_ref[j])
             + d_ref[j] * u.astype(F32))
        zg_ref[j] = _gelu_tanh(y).astype(zg_ref.dtype)

    for t in range(CHUNK):
        z_ref[t] = jnp.concatenate(
            [zg_ref[j, :, t * SSM_GROUP:(t + 1) * SSM_GROUP] for j in range(GROUP_BLOCK)], axis=1)


def _ssm(u3, tmat, wsr, wsi, rxr, rxi, a16r, a16i, d_tiled):
    _, nk, width = u3.shape
    gb = GROUP_BLOCK
    gl = gb * SSM_GROUP
    blk3 = lambda a, b: pl.BlockSpec((gb, a, b), lambda i: (i, 0, 0))
    io = pl.BlockSpec((CHUNK, nk, gl), lambda i: (0, 0, i))
    return pl.pallas_call(
        _ssm_kernel,
        grid=(width // gl,),
        in_specs=[io, blk3(CW, CW), blk3(CW, SSM_STATE), blk3(CW, SSM_STATE),
                  blk3(CW, SSM_STATE), blk3(CW, SSM_STATE),
                  pl.BlockSpec((gb, SSM_STATE), lambda i: (i, 0)),
                  pl.BlockSpec((gb, SSM_STATE), lambda i: (i, 0)),
                  blk3(1, CW)],
        out_specs=io,
        out_shape=jax.ShapeDtypeStruct(u3.shape, BF16),
        scratch_shapes=[pltpu.VMEM((gb * nk, SSM_STATE), F32), pltpu.VMEM((gb * nk, SSM_STATE), F32),
                        pltpu.VMEM((gb, nk, CW), BF16), pltpu.VMEM((gb, nk, CW), BF16)],
        compiler_params=_params(("arbitrary",)),
        name="ssm",
    )(u3, tmat, wsr, wsi, rxr, rxi, a16r, a16i, d_tiled)


def _glu_kernel(z_ref, perm_ref, wg_ref, bg_ref, wb_ref, g0_ref, o_ref):
    tm = o_ref.shape[0]
    z = _dot(perm_ref[...], z_ref[...].reshape(tm, SSM_WIDTH)).astype(BF16)
    h = _dot(z, wg_ref[...]) + bg_ref[...]
    ga = h[:, :SSM_WIDTH]
    gb = h[:, SSM_WIDTH:]
    a = (ga * _sigmoid(gb)).astype(BF16)
    y = _dot(a, wb_ref[...])
    o_ref[...] = (_sigmoid(g0_ref[...].astype(F32)) * y).astype(o_ref.dtype)


def _glu(z3, w_glu, b_glu, w_br, proj, tm=512):
    t = z3.shape[0] * z3.shape[1]
    nk = tm // CHUNK
    r = np.arange(tm)
    perm = np.zeros((tm, tm), np.float32)
    perm[r, (r % CHUNK) * nk + r // CHUNK] = 1.0
    return pl.pallas_call(
        _glu_kernel,
        grid=(t // tm,),
        in_specs=[
            pl.BlockSpec((CHUNK, nk, SSM_WIDTH), lambda i: (0, i, 0)),
            pl.BlockSpec((tm, tm), lambda i: (0, 0)),
            pl.BlockSpec((SSM_WIDTH, 2 * SSM_WIDTH), lambda i: (0, 0)),
            pl.BlockSpec((1, 2 * SSM_WIDTH), lambda i: (0, 0)),
            pl.BlockSpec((SSM_WIDTH, D_MODEL), lambda i: (0, 0)),
            pl.BlockSpec((tm, D_MODEL), lambda i: (i, COL_G0 // D_MODEL)),
        ],
        out_specs=pl.BlockSpec((tm, D_MODEL), lambda i: (i, 0)),
        out_shape=jax.ShapeDtypeStruct((t, D_MODEL), BF16),
        compiler_params=_params(("arbitrary",)),
        name="glu",
    )(z3, jnp.asarray(perm, BF16), w_glu, b_glu, w_br, proj)


def _rope_pattern():
    half = ROPE_DIM // 2
    inv_freq = (np.float32(ROPE_THETA) ** (-np.arange(half, dtype=np.float32) / np.float32(half))).astype(np.float32)
    d = np.arange(LANES) % HEAD_DIM
    rotated = d < ROPE_DIM
    pat = np.zeros((16, LANES), np.float32)
    pat[:half] = rotated[None, :] & ((d % half)[None, :] == np.arange(half)[:, None])
    pat[8] = ~rotated
    pat[9] = np.where(d < half, -1.0, 0.0)
    pat[10] = np.where((d >= half) & rotated, 1.0, 0.0)
    return inv_freq.reshape(half, 1), pat


def _attn_kernel(q_ref, k_ref, v_ref, pos_ref, freq_ref, pat_ref, sink_ref, wbr_ref, ysg_ref, g1_ref,
                 o_ref, qbuf, kbuf, vbuf, obuf, sbuf, pbuf):
    i = pl.program_id(0)
    tq = q_ref.shape[0]
    nw = tq // WINDOW
    half = ROPE_DIM // 2

    @pl.when(i == 0)
    def _():
        kbuf[:, 0:WINDOW, :] = jnp.zeros((2 * N_KV_HEADS, WINDOW, LANES), BF16)
        vbuf[:, 0:WINDOW, :] = jnp.zeros((2 * N_KV_HEADS, WINDOW, LANES), BF16)

    ang = freq_ref[...] * pos_ref[...].astype(F32)
    spread = lambda tab: lax.dot_general(tab, pat_ref[0:8, :], (((0,), (0,)), ((), ())),
                                         preferred_element_type=F32, precision=HIGHEST)
    cs = spread(jnp.cos(ang)) + pat_ref[8:9, :]
    sn = spread(jnp.sin(ang))
    c_up = sn * pat_ref[9:10, :]
    c_dn = sn * pat_ref[10:11, :]

    def rope(x):
        return (x * cs + pltpu.roll(x, LANES - half, 1) * c_up + pltpu.roll(x, half, 1) * c_dn)

    low = lax.broadcasted_iota(I32, (tq, LANES), 1) < HEAD_DIM

    def split_heads(buf, cb, x):
        xs = pltpu.roll(x, HEAD_DIM, 1)
        zero = jnp.zeros_like(x)
        buf[4 * cb + 0, WINDOW:, :] = jnp.where(low, x, zero).astype(BF16)
        buf[4 * cb + 1, WINDOW:, :] = jnp.where(low, zero, xs).astype(BF16)
        buf[4 * cb + 2, WINDOW:, :] = jnp.where(low, xs, zero).astype(BF16)
        buf[4 * cb + 3, WINDOW:, :] = jnp.where(low, zero, x).astype(BF16)

    scale = HEAD_DIM ** -0.5
    for cb in range(Q_WIDTH // LANES):
        sl = slice(cb * LANES, (cb + 1) * LANES)
        qbuf[:, sl] = (rope(q_ref[:, sl].astype(F32)) * scale).astype(BF16)
    for cb in range(KV_WIDTH // LANES):
        sl = slice(cb * LANES, (cb + 1) * LANES)
        split_heads(kbuf, cb, rope(k_ref[:, sl].astype(F32)))
        split_heads(vbuf, cb, v_ref[:, sl].astype(F32))

    kj = lax.broadcasted_iota(I32, (2 * WINDOW, WINDOW), 0)
    qi = lax.broadcasted_iota(I32, (2 * WINDOW, WINDOW), 1)
    dist = qi + WINDOW - kj
    in_band = (dist >= 0) & (dist < WINDOW)
    cur_only = kj >= WINDOW
    sinks = sink_ref[...]

    def window(w, carry):
        r0 = pl.multiple_of(w * WINDOW, WINDOW)
        rows = pl.ds(r0, 2 * WINDOW)
        not_first = (i * nw + w) > 0
        mask = in_band & (cur_only | not_first)

        for h in range(N_Q_HEADS):
            qp = qbuf[pl.ds(r0, WINDOW), (h // 2) * LANES:(h // 2 + 1) * LANES]
            sbuf[h] = _dot_nt(kbuf[2 * (h // Q_PER_KV) + h % 2, rows, :], qp)
        for h in range(N_Q_HEADS):
            s = jnp.where(mask, sbuf[h], -jnp.inf)
            sink = sinks[:, h:h + 1]
            m = jnp.maximum(jnp.max(s, axis=0, keepdims=True), sink)
            p = jnp.exp(s - m)
            denom = jnp.sum(p, axis=0, keepdims=True) + jnp.exp(sink - m)
            pbuf[h] = (p * (1.0 / denom)).astype(BF16)
        for a in range(N_Q_HEADS // 2):
            kv = (2 * a) // Q_PER_KV
            o = _dot_tn(pbuf[2 * a], vbuf[2 * kv, rows, :]) + _dot_tn(pbuf[2 * a + 1], vbuf[2 * kv + 1, rows, :])
            obuf[pl.ds(r0, WINDOW), a * LANES:(a + 1) * LANES] = o.astype(BF16)
        return carry

    lax.fori_loop(0, nw, window, 0)

    kbuf[:, 0:WINDOW, :] = kbuf[:, tq:tq + WINDOW, :]
    vbuf[:, 0:WINDOW, :] = vbuf[:, tq:tq + WINDOW, :]

    y = _dot(obuf[...], wbr_ref[...])
    o_ref[...] = (ysg_ref[...].astype(F32) + _sigmoid(g1_ref[...].astype(F32)) * y).astype(o_ref.dtype)


def _attn(proj, pos_row, sinks, w_br, ysg, tq=512):
    t = proj.shape[0]
    freq, pat = (jnp.asarray(a) for a in _rope_pattern())
    return pl.pallas_call(
        _attn_kernel,
        grid=(t // tq,),
        in_specs=[
            pl.BlockSpec((tq, Q_WIDTH), lambda i: (i, COL_Q // Q_WIDTH)),
            pl.BlockSpec((tq, KV_WIDTH), lambda i: (i, COL_K // KV_WIDTH)),
            pl.BlockSpec((tq, KV_WIDTH), lambda i: (i, COL_V // KV_WIDTH)),
            pl.BlockSpec((1, tq), lambda i: (0, i)),
            pl.BlockSpec((ROPE_DIM // 2, 1), lambda i: (0, 0)),
            pl.BlockSpec((16, LANES), lambda i: (0, 0)),
            pl.BlockSpec((1, N_Q_HEADS), lambda i: (0, 0)),
            pl.BlockSpec((Q_WIDTH, D_MODEL), lambda i: (0, 0)),
            pl.BlockSpec((tq, D_MODEL), lambda i: (i, 0)),
            pl.BlockSpec((tq, D_MODEL), lambda i: (i, COL_G1 // D_MODEL)),
        ],
        out_specs=pl.BlockSpec((tq, D_MODEL), lambda i: (i, 0)),
        out_shape=jax.ShapeDtypeStruct((t, D_MODEL), BF16),
        scratch_shapes=[
            pltpu.VMEM((tq, Q_WIDTH), BF16),
            pltpu.VMEM((2 * N_KV_HEADS, tq + WINDOW, LANES), BF16),
            pltpu.VMEM((2 * N_KV_HEADS, tq + WINDOW, LANES), BF16),
            pltpu.VMEM((tq, Q_WIDTH), BF16),
            pltpu.VMEM((N_Q_HEADS, 2 * WINDOW, WINDOW), F32),
            pltpu.VMEM((N_Q_HEADS, 2 * WINDOW, WINDOW), BF16),
        ],
        compiler_params=_params(("arbitrary",)),
        name="attn",
    )(proj, proj, proj, pos_row, freq, pat, sinks, w_br, ysg, proj)


def _oproj_kernel(x_ref, mix_ref, wo_ref, g_ref, wrh_ref, wrl_ref, br_ref, h_ref, hp_ref, lt_ref):
    h = x_ref[...] + _dot(mix_ref[...], wo_ref[...])
    h_ref[...] = h
    ms = jnp.mean(h * h, axis=-1, keepdims=True)
    hn = h * lax.rsqrt(ms + EPS) * g_ref[...]
    _store_token_tiles(hp_ref, hn)
    hn_hi = hn.astype(BF16)
    hn_lo = (hn - hn_hi.astype(F32)).astype(BF16)
    logits = (_dot(hn_hi, wrh_ref[...]) + _dot(hn_lo, wrh_ref[...]) + _dot(hn_hi, wrl_ref[...])
              + br_ref[...])
    lt_ref[...] = logits.T


def _oproj(x, mixed, w_o, g, w_router, b_router, tm=512):
    t = x.shape[0]
    w_router_hi = w_router.astype(BF16)
    w_router_hi_rest = (w_router - w_router_hi.astype(F32)).astype(BF16)
    return pl.pallas_call(
        _oproj_kernel,
        grid=(t // tm,),
        in_specs=[
            pl.BlockSpec((tm, D_MODEL), lambda i: (i, 0)),
            pl.BlockSpec((tm, D_MODEL), lambda i: (i, 0)),
            pl.BlockSpec((D_MODEL, D_MODEL), lambda i: (0, 0)),
            pl.BlockSpec((1, D_MODEL), lambda i: (0, 0)),
            pl.BlockSpec((D_MODEL, LANES), lambda i: (0, 0)),
            pl.BlockSpec((D_MODEL, LANES), lambda i: (0, 0)),
            pl.BlockSpec((1, LANES), lambda i: (0, 0)),
        ],
        out_specs=[
            pl.BlockSpec((tm, D_MODEL), lambda i: (i, 0)),
            pl.BlockSpec((tm * SUBLANES, LANES), lambda i: (i, 0)),
            pl.BlockSpec((LANES, tm), lambda i: (0, i)),
        ],
        out_shape=[
            jax.ShapeDtypeStruct((t, D_MODEL), F32),
            jax.ShapeDtypeStruct((t * SUBLANES, LANES), jnp.uint32),
            jax.ShapeDtypeStruct((LANES, t), F32),
        ],
        compiler_params=_params(("arbitrary",)),
        name="oproj",
    )(x, mixed, w_o, g, w_router_hi, w_router_hi_rest, b_router)


ROUTE_CHUNK = 256


def _route_kernel(lt_ref, dest_ref, wts_ref, bexp_ref, nvalid_ref, eid_ref, rank_ref):
    t = lt_ref.shape[1]
    nc = t // ROUTE_CHUNK
    r8 = lax.broadcasted_iota(I32, (N_GROUPS, ROUTE_CHUNK), 0)
    r64 = lax.broadcasted_iota(I32, (N_EXPERTS, ROUTE_CHUNK), 0)

    def pick(c, carry):
        cols = pl.ds(pl.multiple_of(c * ROUTE_CHUNK, ROUTE_CHUNK), ROUTE_CHUNK)
        lg = lt_ref[0:N_GROUPS, cols]
        m = jnp.max(lg, axis=0, keepdims=True)
        ssum = jnp.sum(jnp.exp(lg - m), axis=0, keepdims=True)
        p_grp = 1.0 / ssum
        grp = jnp.min(jnp.where(lg == m, r8, N_GROUPS), axis=0, keepdims=True)
        le = lt_ref[N_GROUPS:N_GROUPS + N_EXPERTS, cols]
        leg = jnp.where((r64 // EXPERTS_PER_GROUP) == grp, le, -jnp.inf)
        m1 = jnp.max(leg, axis=0, keepdims=True)
        i1 = jnp.min(jnp.where(leg == m1, r64, N_EXPERTS), axis=0, keepdims=True)
        leg2 = jnp.where(r64 == i1, -jnp.inf, leg)
        m2 = jnp.max(leg2, axis=0, keepdims=True)
        i2 = jnp.min(jnp.where(leg2 == m2, r64, N_EXPERTS), axis=0, keepdims=True)
        ex = jnp.exp(m2 - m1)
        eid_ref[0:1, cols] = i1
        eid_ref[1:2, cols] = i2
        wts_ref[0:1, cols] = p_grp / (1.0 + ex)
        wts_ref[1:2, cols] = p_grp * ex / (1.0 + ex)
        return carry

    lax.fori_loop(0, nc, pick, 0)

    a_row = lax.broadcasted_iota(I32, (ROUTE_CHUNK, ROUTE_CHUNK), 0)
    a_col = lax.broadcasted_iota(I32, (ROUTE_CHUNK, ROUTE_CHUNK), 1)
    before = (a_row < a_col).astype(BF16)

    def count(n, carry):
        j = n // nc
        c = n - j * nc
        cols = pl.ds(pl.multiple_of(c * ROUTE_CHUNK, ROUTE_CHUNK), ROUTE_CHUNK)
        oh = r64 == eid_ref[pl.ds(j, 1), cols]
        ohf = oh.astype(F32)
        pref = _dot(ohf.astype(BF16), before) + carry
        rank_ref[pl.ds(j, 1), cols] = jnp.sum(jnp.where(oh, pref, 0.0), axis=0, keepdims=True)
        return carry + jnp.sum(ohf, axis=1, keepdims=True)

    counts = lax.fori_loop(0, TOP_K * nc, count, jnp.zeros((N_EXPERTS, 1), F32))

    padded = jnp.floor((counts + (MOE_BLOCK - 1)) * (1.0 / MOE_BLOCK)) * MOE_BLOCK
    e_row = lax.broadcasted_iota(I32, (N_EXPERTS, N_EXPERTS), 0)
    e_col = lax.broadcasted_iota(I32, (N_EXPERTS, N_EXPERTS), 1)
    incl = (e_col <= e_row).astype(F32)
    pad_end = _dot(incl, jnp.broadcast_to(padded, (N_EXPERTS, LANES)), precision=HIGHEST)[:, 0:1]
    pad_start = pad_end - padded

    def place(n, carry):
        j = n // nc
        c = n - j * nc
        cols = pl.ds(pl.multiple_of(c * ROUTE_CHUNK, ROUTE_CHUNK), ROUTE_CHUNK)
        oh = r64 == eid_ref[pl.ds(j, 1), cols]
        start = jnp.sum(jnp.where(oh, pad_start, 0.0), axis=0, keepdims=True)
        dest_ref[pl.ds(j, 1), cols] = (start + rank_ref[pl.ds(j, 1), cols]).astype(I32)
        return carry

    lax.fori_loop(0, TOP_K * nc, place, 0)

    b0 = (lax.broadcasted_iota(I32, (N_EXPERTS, bexp_ref.shape[1]), 1) * MOE_BLOCK).astype(F32)
    n_done = jnp.sum((pad_end <= b0).astype(F32), axis=0, keepdims=True)
    bexp_ref[...] = jnp.minimum(n_done, float(N_EXPERTS - 1)).astype(I32)
    live = jnp.minimum(pad_start + counts, b0 + MOE_BLOCK) - jnp.maximum(pad_start, b0)
    nvalid_ref[...] = jnp.sum(jnp.maximum(live, 0.0), axis=0, keepdims=True).astype(I32)


def _route(logits_t, n_blocks_pad):
    t = logits_t.shape[1]
    return pl.pallas_call(
        _route_kernel,
        out_shape=[
            jax.ShapeDtypeStruct((TOP_K, t), I32),
            jax.ShapeDtypeStruct((TOP_K, t), F32),
            jax.ShapeDtypeStruct((1, n_blocks_pad), I32),
            jax.ShapeDtypeStruct((1, n_blocks_pad), I32),
        ],
        scratch_shapes=[pltpu.VMEM((TOP_K, t), I32), pltpu.VMEM((TOP_K, t), F32)],
        compiler_params=pltpu.CompilerParams(vmem_limit_bytes=VMEM_LIMIT),
        name="route",
    )(logits_t)


def _invmap_kernel(dest_ref, rt_ref):
    n_rows = rt_ref.shape[0]
    t = dest_ref.shape[0] // TOP_K

    def clear(r, c):
        rt_ref[r] = 0
        return c

    lax.fori_loop(0, n_rows, clear, 0, unroll=16)

    def put(tok, c):
        rt_ref[dest_ref[tok]] = tok
        rt_ref[dest_ref[t + tok]] = tok
        return c

    lax.fori_loop(0, t, put, 0, unroll=8)


def _invmap(dest_flat, n_rows):
    return pl.pallas_call(
        _invmap_kernel,
        in_specs=[pl.BlockSpec(memory_space=pltpu.SMEM)],
        out_specs=pl.BlockSpec(memory_space=pltpu.SMEM),
        out_shape=jax.ShapeDtypeStruct((n_rows,), I32),
        name="invmap",
    )(dest_flat)


GATHER_SLOTS = 4


def _row_copy(src_hbm, src_row, dst_buf, slot, dst_row, sem):
    return pltpu.make_async_copy(src_hbm.at[pl.ds(pl.multiple_of(src_row * SUBLANES, SUBLANES), SUBLANES), :],
                                 dst_buf.at[slot, pl.ds(dst_row * SUBLANES, SUBLANES), :], sem.at[slot])


def _rows_copy(src_hbm, dst_buf, slot, rows, sem):
    n = rows * SUBLANES
    return pltpu.make_async_copy(src_hbm.at[pl.ds(0, n), :], dst_buf.at[slot, pl.ds(0, n), :], sem.at[slot])


def _expert_kernel(bexp_ref, nvalid_ref, rt_ref, hp_hbm, wg_hbm, wu_hbm, wd_hbm, ys_ref,
                   xbuf, wf_g, wf_u, wf_d, wb_g, wb_u, wb_d, ord_ref, xsem, wsem):
    b = pl.program_id(0)
    nb = pl.num_programs(0) - 1

    def expert_of(blk):
        return bexp_ref[jnp.minimum(blk, nb - 1)]

    def next_owner(blk, e):
        return lax.while_loop(lambda j: (j < nb) & (expert_of(j) == e), lambda j: j + 1, blk)

    def weight_copies(e, slot):
        return (pltpu.make_async_copy(wg_hbm.at[e], wf_g.at[slot], wsem.at[slot, 0]),
                pltpu.make_async_copy(wu_hbm.at[e], wf_u.at[slot], wsem.at[slot, 1]),
                pltpu.make_async_copy(wd_hbm.at[e], wf_d.at[slot], wsem.at[slot, 2]))

    @pl.when(b == 0)
    def _():
        ord_ref[0] = 0
        e0 = bexp_ref[0]
        for c in weight_copies(e0, 0):
            c.start(priority=1)
        n1 = next_owner(1, e0)

        @pl.when(n1 < nb)
        def _():
            for c in weight_copies(expert_of(n1), 1):
                c.start(priority=1)

    def gathered_rows(blk):
        n = nvalid_ref[jnp.clip(blk, 0, nb - 1)]
        live_rows = ((n + (SUBLANES - 1)) >> 3) << 3
        return jnp.where(blk >= nb, 0, jnp.where((blk < GATHER_SLOTS) & (n > 0), MOE_BLOCK, live_rows))

    def gather(blk):
        base = blk * MOE_BLOCK
        slot = blk % GATHER_SLOTS

        def body(i, c):
            for s in range(SUBLANES):
                r = i * SUBLANES + s
                _row_copy(hp_hbm, rt_ref[base + r], xbuf, slot, r, xsem).start()
            return c

        lax.fori_loop(0, gathered_rows(blk) // SUBLANES, body, 0)

    @pl.when(b == 0)
    def _():
        for blk in range(GATHER_SLOTS - 2):
            gather(blk)

    gather(b + GATHER_SLOTS - 2)

    @pl.when(b > 0)
    def _():
        blk = b - 1
        e = bexp_ref[blk]
        first = (blk == 0) | (e != bexp_ref[jnp.maximum(blk - 1, 0)])

        @pl.when(first)
        def _():
            n = ord_ref[0]
            wslot = n % 2
            ord_ref[0] = n + 1
            for c in weight_copies(e, wslot):
                c.wait()
            wb_g[...] = wf_g[wslot].astype(BF16)
            wb_u[...] = wf_u[wslot].astype(BF16)
            wb_d[...] = wf_d[wslot].astype(BF16)
            n1 = next_owner(blk + 1, e)
            n2 = next_owner(n1 + 1, expert_of(n1))

            @pl.when((n1 < nb) & (n2 < nb))
            def _():
                for c in weight_copies(expert_of(n2), wslot):
                    c.start(priority=1)

        slot = blk % GATHER_SLOTS
        rows = gathered_rows(blk)

        @pl.when(rows > 0)
        def _():
            _rows_copy(hp_hbm, xbuf, slot, rows, xsem).wait()
            lo, hi = (v.astype(BF16) for v in _load_token_tiles(xbuf, slot, 0, MOE_BLOCK))
            g = _dot(lo, wb_g[:HALF, :]) + _dot(hi, wb_g[HALF:, :])
            u = _dot(lo, wb_u[:HALF, :]) + _dot(hi, wb_u[HALF:, :])
            h = (g * _sigmoid(g) * u).astype(BF16)
            y = _dot(h, wb_d[...])
            _store_token_tiles(ys_ref, y)

        @pl.when(rows == 0)
        def _():
            _store_token_tiles(ys_ref, jnp.zeros((MOE_BLOCK, D_MODEL), F32))


def _experts(bexp, nvalid, row_tok, hn_packed, w_gate, w_up, w_down, n_blocks):
    grid_spec = pltpu.PrefetchScalarGridSpec(
        num_scalar_prefetch=3,
        grid=(n_blocks + 1,),
        in_specs=[pl.BlockSpec(memory_space=pl.ANY)] * 4,
        out_specs=pl.BlockSpec((MOE_BLOCK * SUBLANES, LANES), lambda b, be, nv, rt: (jnp.maximum(b - 1, 0), 0)),
        scratch_shapes=[
            pltpu.VMEM((GATHER_SLOTS, MOE_BLOCK * SUBLANES, LANES), jnp.uint32),
            pltpu.VMEM((2, D_MODEL, D_EXPERT), F32), pltpu.VMEM((2, D_MODEL, D_EXPERT), F32),
            pltpu.VMEM((2, D_EXPERT, D_MODEL), F32),
            pltpu.VMEM((D_MODEL, D_EXPERT), BF16), pltpu.VMEM((D_MODEL, D_EXPERT), BF16),
            pltpu.VMEM((D_EXPERT, D_MODEL), BF16),
            pltpu.SMEM((1,), I32),
            pltpu.SemaphoreType.DMA((GATHER_SLOTS,)), pltpu.SemaphoreType.DMA((2, 3)),
        ],
    )
    return pl.pallas_call(
        _expert_kernel,
        grid_spec=grid_spec,
        out_shape=jax.ShapeDtypeStruct((n_blocks * MOE_BLOCK * SUBLANES, LANES), jnp.uint32),
        compiler_params=_params(("arbitrary",)),
        name="experts",
    )(bexp, nvalid, row_tok, hn_packed, w_gate, w_up, w_down)


def _combine_kernel(dest_ref, ys_hbm, h_ref, w_ref, g_ref, o_ref, ybuf, sem):
    i = pl.program_id(0)
    n = pl.num_programs(0)
    tq = h_ref.shape[0]
    t = n * tq

    def start_gather(blk, slot):
        base = blk * tq

        def body(j, c):
            for s in range(SUBLANES):
                r = j * SUBLANES + s
                _row_copy(ys_hbm, dest_ref[base + r], ybuf, slot, r, sem).start(priority=0)
                _row_copy(ys_hbm, dest_ref[t + base + r], ybuf, slot, tq + r, sem).start(priority=1)
            return c

        lax.fori_loop(0, tq // SUBLANES, body, 0)

    @pl.when(i == 0)
    def _():
        start_gather(0, 0)

    @pl.when(i + 1 < n)
    def _():
        start_gather(i + 1, (i + 1) % 2)

    slot = i % 2
    _rows_copy(ys_hbm, ybuf, slot, TOP_K * tq, sem).wait()
    y0_lo, y0_hi = _load_token_tiles(ybuf, slot, 0, tq)
    y1_lo, y1_hi = _load_token_tiles(ybuf, slot, tq, tq)
    w = w_ref[...]
    w0 = w[:, 0:1]
    w1 = w[:, 1:2]
    h = h_ref[...]
    lo = h[:, :HALF] + w0 * y0_lo + w1 * y1_lo
    hi = h[:, HALF:] + w0 * y0_hi + w1 * y1_hi
    ms = (jnp.sum(lo * lo, axis=-1, keepdims=True) + jnp.sum(hi * hi, axis=-1, keepdims=True)) * (1.0 / D_MODEL)
    inv = lax.rsqrt(ms + EPS)
    g = g_ref[...]
    o_ref[:, :HALF] = lo * inv * g[:, :HALF]
    o_ref[:, HALF:] = hi * inv * g[:, HALF:]


def _combine(dest_flat, ys_packed, h1, wts_tok, g_final, tq=256):
    t = h1.shape[0]
    grid_spec = pltpu.PrefetchScalarGridSpec(
        num_scalar_prefetch=1,
        grid=(t // tq,),
        in_specs=[
            pl.BlockSpec(memory_space=pl.ANY),
            pl.BlockSpec((tq, D_MODEL), lambda i, d: (i, 0)),
            pl.BlockSpec((tq, TOP_K), lambda i, d: (i, 0)),
            pl.BlockSpec((1, D_MODEL), lambda i, d: (0, 0)),
        ],
        out_specs=pl.BlockSpec((tq, D_MODEL), lambda i, d: (i, 0)),
        scratch_shapes=[pltpu.VMEM((2, TOP_K * tq * SUBLANES, LANES), jnp.uint32),
                        pltpu.SemaphoreType.DMA((2,))],
    )
    return pl.pallas_call(
        _combine_kernel,
        grid_spec=grid_spec,
        out_shape=jax.ShapeDtypeStruct((t, D_MODEL), F32),
        compiler_params=_params(("arbitrary",)),
        name="combine",
    )(dest_flat, ys_packed, h1, wts_tok, g_final)


def _mixers(h, positions, norm_g, w_in, b_in, a_re, a_im, log_dt, b_re, b_im, c_re, c_im, d_skip,
            w_glu, b_glu, w_br_ssm, sinks, w_br_attn):
    t = h.shape[0]
    ngate = 2 * D_MODEL
    g = norm_g.reshape(1, D_MODEL)
    b_row = b_in.reshape(1, IN_WIDTH)
    proj = _inproj(h, g, w_in, b_row, chunk_major=False,
                   segments=((IN_WIDTH - ngate, ngate), (SSM_WIDTH, IN_WIDTH - ngate - SSM_WIDTH)))
    u3 = _inproj(h, g, w_in, b_row, chunk_major=True, segments=((0, SSM_WIDTH),))

    tmat, wsr, wsi, rxr, rxi, a16r, a16i = _ssm_prep(
        a_re, a_im, log_dt, b_re.transpose(0, 2, 1), b_im.transpose(0, 2, 1), c_re, c_im)
    d_tiled = jnp.tile(d_skip, (1, CHUNK)).reshape(SSM_GROUPS, 1, CW)
    z3 = _ssm(u3, tmat, wsr, wsi, rxr, rxi, a16r.reshape(SSM_GROUPS, SSM_STATE),
              a16i.reshape(SSM_GROUPS, SSM_STATE), d_tiled)

    ysg = _glu(z3, w_glu.astype(BF16), b_glu.reshape(1, -1), w_br_ssm.astype(BF16), proj)
    return _attn(proj, positions.reshape(1, t), sinks.reshape(1, N_Q_HEADS), w_br_attn.astype(BF16), ysg)


def _moe_tail(h, mixed, w_o, norm_ffn_g, w_rg, b_rg, w_re, b_re, w_gate, w_up, w_down, norm_final_g):
    t = h.shape[0]
    n_route = N_GROUPS + N_EXPERTS
    w_router = jnp.concatenate([w_rg, w_re, jnp.zeros((D_MODEL, LANES - n_route), F32)], axis=1)
    b_router = jnp.concatenate([b_rg, b_re, jnp.zeros((LANES - n_route,), F32)]).reshape(1, LANES)
    h1, hn_packed, logits_t = _oproj(h, mixed, w_o.astype(BF16), norm_ffn_g.reshape(1, D_MODEL),
                                     w_router, b_router)
    n_assign = t * TOP_K
    n_blocks = -(-(n_assign + N_EXPERTS * (MOE_BLOCK - 1)) // MOE_BLOCK)
    n_blocks_pad = -(-n_blocks // LANES) * LANES
    dest, wts, bexp, nvalid = _route(logits_t, n_blocks_pad)
    dest_flat = dest.reshape(n_assign)
    row_tok = _invmap(dest_flat, n_blocks * MOE_BLOCK)
    ys_packed = _experts(bexp[0, :n_blocks], nvalid[0, :n_blocks], row_tok, hn_packed, w_gate, w_up, w_down,
                         n_blocks)
    return _combine(dest_flat, ys_packed, h1, wts.T, norm_final_g.reshape(1, D_MODEL))


def kernel(x, positions, norm_mix_g, w_in, b_in, ssm_a_re, ssm_a_im, ssm_log_dt, ssm_b_re, ssm_b_im, ssm_c_re, ssm_c_im, ssm_d, w_glu, b_glu, w_br_ssm, attn_sinks, w_br_attn, w_o, norm_ffn_g, w_router_group, b_router_group, w_router_expert, b_router_expert, w_exp_gate, w_exp_up, w_exp_down, norm_final_g):
    bsz, seq, d = x.shape
    assert bsz == 1 and d == D_MODEL and norm_mix_g.shape[0] == 1
    h = x.reshape(seq, d)
    mixed = _mixers(h, positions, norm_mix_g[0], w_in[0], b_in[0], ssm_a_re[0], ssm_a_im[0], ssm_log_dt[0],
                    ssm_b_re[0], ssm_b_im[0], ssm_c_re[0], ssm_c_im[0], ssm_d[0], w_glu[0], b_glu[0],
                    w_br_ssm[0], attn_sinks[0], w_br_attn[0])
    out = _moe_tail(h, mixed, w_o[0], norm_ffn_g[0], w_router_group[0], b_router_group[0],
                    w_router_expert[0], b_router_expert[0], w_exp_gate[0], w_exp_up[0], w_exp_down[0],
                    norm_final_g)
    return out.reshape(bsz, seq, d)
```

```python
import functools
import math

import numpy as np
import jax
import jax.numpy as jnp
from jax import lax
from jax.experimental import pallas as pl
from jax.experimental.pallas import tpu as pltpu

F32 = jnp.float32
BF16 = jnp.bfloat16
I32 = jnp.int32

D_MODEL = 2048
SSM_WIDTH = 1024
SSM_GROUP = 16
SSM_GROUPS = 64
SSM_STATE = 64
HEAD_DIM = 64
N_Q_HEADS = 16
N_KV_HEADS = 4
Q_PER_KV = 4
WINDOW = 128
ROPE_DIM = 16
ROPE_THETA = 500000.0
Q_WIDTH = 1024
KV_WIDTH = 256
IN_WIDTH = SSM_WIDTH + Q_WIDTH + 2 * KV_WIDTH + 2 * D_MODEL
N_GROUPS = 8
EXPERTS_PER_GROUP = 8
N_EXPERTS = 64
TOP_K = 2
D_EXPERT = 512
MOE_BLOCK = 128
EPS = 1e-6

CHUNK = 16
CW = CHUNK * SSM_GROUP
GROUP_BLOCK = 8
HALF = D_MODEL // 2
LANES = 128
VMEM_LIMIT = 56 * 1024 * 1024

COL_G0, COL_G1, COL_Q, COL_K, COL_V = 0, 2048, 4096, 5120, 5376

HIGHEST = lax.Precision.HIGHEST


def _dot(a, b, precision=None):
    return jnp.dot(a, b, preferred_element_type=F32, precision=precision)


def _dot_nt(a, b, precision=None):
    return lax.dot_general(a, b, (((1,), (1,)), ((), ())), preferred_element_type=F32,
                           precision=precision)


def _dot_tn(a, b):
    return lax.dot_general(a, b, (((0,), (0,)), ((), ())), preferred_element_type=F32)


def _sigmoid(x):
    return 1.0 / (1.0 + jnp.exp(-x))


def _pack_halves(lo, hi):
    return pltpu.pack_elementwise([lo, hi], packed_dtype=BF16)


def _unpack_half(w, index):
    return pltpu.unpack_elementwise(w, index=index, packed_dtype=BF16, unpacked_dtype=F32)


SUBLANES = 8
WORD_TILES = HALF // LANES
assert WORD_TILES == SUBLANES


def _store_token_tiles(ref, x):
    rows = x.shape[0]
    for s in range(WORD_TILES):
        ref[pl.ds(s, rows, stride=SUBLANES), :] = _pack_halves(x[:, s * LANES:(s + 1) * LANES],
                                                               x[:, HALF + s * LANES:HALF + (s + 1) * LANES])


def _load_token_tiles(ref, slot, first_row, rows):
    pieces = [ref[slot, pl.ds(first_row * SUBLANES + s, rows, stride=SUBLANES), :] for s in range(WORD_TILES)]
    return (jnp.concatenate([_unpack_half(p, 0) for p in pieces], axis=1),
            jnp.concatenate([_unpack_half(p, 1) for p in pieces], axis=1))


def _params(sem, vmem=VMEM_LIMIT):
    return pltpu.CompilerParams(dimension_semantics=sem, vmem_limit_bytes=vmem)


def _inproj_kernel(x_ref, g_ref, w_ref, b_ref, o_ref, u_ref, xn_ref, acc_ref, *, main_panels):
    j = pl.program_id(1)

    @pl.when(j == 0)
    def _():
        x = x_ref[...]
        ms = jnp.mean(x * x, axis=-1, keepdims=True)
        xn_ref[...] = (x * lax.rsqrt(ms + EPS) * g_ref[...]).astype(BF16)

    acc = _dot(xn_ref[...], w_ref[...].astype(BF16)) + b_ref[...]

    @pl.when(j < main_panels)
    def _():
        o_ref[...] = acc.astype(o_ref.dtype)

    @pl.when(j >= main_panels)
    def _():
        nk = u_ref.shape[1]
        for c in range(acc_ref.shape[0]):
            acc_ref[c] = acc[:, c * LANES:(c + 1) * LANES]
        for s in range(CHUNK):
            for c in range(acc_ref.shape[0]):
                u_ref[s, :, c * LANES:(c + 1) * LANES] = (
                    acc_ref[c, pl.ds(s, nk, stride=CHUNK), :].astype(u_ref.dtype))


def _inproj(x, g, w, b, *, main_segments, u_segment, tm=1024, tn=512):
    t, d = x.shape
    (c0, n0), (c1, n1) = main_segments
    cu, nu = u_segment
    assert all(v % tn == 0 for v in (c0, n0, c1, n1, cu, nu))
    p0, p1, pu = n0 // tn, n1 // tn, nu // tn
    main = p0 + p1
    panel = lambda i, j: (0, jnp.where(j < p0, j + c0 // tn,
                                       jnp.where(j < main, j - p0 + c1 // tn, j - main + cu // tn)))
    return pl.pallas_call(
        functools.partial(_inproj_kernel, main_panels=main),
        grid=(t // tm, main + pu),
        in_specs=[
            pl.BlockSpec((tm, d), lambda i, j: (i, 0)),
            pl.BlockSpec((1, d), lambda i, j: (0, 0)),
            pl.BlockSpec((d, tn), panel),
            pl.BlockSpec((1, tn), panel),
        ],
        out_specs=[pl.BlockSpec((tm, tn), lambda i, j: (i, jnp.minimum(j, main - 1))),
                   pl.BlockSpec((CHUNK, tm // CHUNK, tn), lambda i, j: (0, i, jnp.maximum(j - main, 0)))],
        out_shape=[jax.ShapeDtypeStruct((t, n0 + n1), BF16),
                   jax.ShapeDtypeStruct((CHUNK, t // CHUNK, nu), BF16)],
        scratch_shapes=[pltpu.VMEM((tm, d), BF16), pltpu.VMEM((tn // LANES, tm, LANES), F32)],
        compiler_params=_params(("arbitrary", "arbitrary")),
        name="inproj",
    )(x, g, w, b)


def _ssm_prep_kernel(are_ref, aim_ref, ldt_ref, btr_ref, bti_ref, cr_ref, ci_ref,
                     t_ref, wsr_ref, wsi_ref, rxr_ref, rxi_ref, a16r_ref, a16i_ref):
    lam_re = jnp.minimum(are_ref[0], -1e-4)
    lam_im = aim_ref[0]
    dt = jnp.exp(ldt_ref[0])
    lr_dt = lam_re * dt
    th = lam_im * dt
    mag = jnp.exp(lr_dt)
    ab_re = mag * jnp.cos(th)
    ab_im = mag * jnp.sin(th)
    den = lam_re * lam_re + lam_im * lam_im
    nr = ab_re - 1.0
    ni = ab_im
    coef_re = (nr * lam_re + ni * lam_im) / den
    coef_im = (ni * lam_re - nr * lam_im) / den
    btr = btr_ref[0]
    bti = bti_ref[0]
    bb_re = coef_re * btr - coef_im * bti
    bb_im = coef_re * bti + coef_im * btr

    e = lax.broadcasted_iota(I32, (CHUNK, SSM_STATE), 0).astype(F32)
    pmag = jnp.exp(e * lr_dt)
    pos_re = pmag * jnp.cos(e * th)
    pos_im = pmag * jnp.sin(e * th)
    nmag = jnp.exp(-e * lr_dt)
    neg_re = nmag * jnp.cos(e * th)
    neg_im = -nmag * jnp.sin(e * th)

    def rep(tab):
        return jnp.broadcast_to(tab[:, None, :], (CHUNK, SSM_GROUP, SSM_STATE)).reshape(CW, SSM_STATE)

    def tile(mat):
        return jnp.broadcast_to(mat[None, :, :], (CHUNK, SSM_GROUP, SSM_STATE)).reshape(CW, SSM_STATE)

    pr, pi = rep(pos_re), rep(pos_im)
    ctr, cti = tile(cr_ref[0]), tile(ci_ref[0])
    r_re = ctr * pr - cti * pi
    r_im = ctr * pi + cti * pr
    qr, qi = rep(neg_re), rep(neg_im)
    btr_t, bti_t = tile(bb_re), tile(bb_im)
    l_re = btr_t * qr - bti_t * qi
    l_im = btr_t * qi + bti_t * qr

    def split(v):
        head = v.astype(BF16)
        return head, (v - head.astype(F32)).astype(BF16)

    def dot_nt3(a, b):
        (ah, al), (bh, bl) = split(a), split(b)
        return _dot_nt(ah, bh) + _dot_nt(al, bh) + _dot_nt(ah, bl)

    tm = dot_nt3(l_re, r_re) - dot_nt3(l_im, r_im)
    srow = lax.broadcasted_iota(I32, (CW, CW), 0) // SSM_GROUP
    tcol = lax.broadcasted_iota(I32, (CW, CW), 1) // SSM_GROUP
    t_ref[0] = jnp.where(tcol >= srow, tm, 0.0).astype(BF16)

    a15r = pos_re[CHUNK - 1:CHUNK, :]
    a15i = pos_im[CHUNK - 1:CHUNK, :]
    wsr_ref[0] = (l_re * a15r - l_im * a15i).astype(BF16)
    wsi_ref[0] = (l_re * a15i + l_im * a15r).astype(BF16)
    rxr_ref[0] = (r_re * ab_re - r_im * ab_im).astype(BF16)
    rxi_ref[0] = (-(r_re * ab_im + r_im * ab_re)).astype(BF16)
    m16 = jnp.exp(float(CHUNK) * lr_dt)
    a16r_ref[0] = m16 * jnp.cos(float(CHUNK) * th)
    a16i_ref[0] = m16 * jnp.sin(float(CHUNK) * th)


def _ssm_prep_block_kernel(*refs):
    def one(j, carry):
        _ssm_prep_kernel(*[r.at[pl.ds(j, 1)] for r in refs])
        return carry

    lax.fori_loop(0, GROUP_BLOCK, one, 0)


def _ssm_prep(a_re, a_im, log_dt, bt_re, bt_im, c_re, c_im):
    g = a_re.shape[0]
    gb = GROUP_BLOCK
    vec = pl.BlockSpec((gb, 1, SSM_STATE), lambda i: (i, 0, 0))
    mat = pl.BlockSpec((gb, SSM_GROUP, SSM_STATE), lambda i: (i, 0, 0))
    wide = pl.BlockSpec((gb, CW, SSM_STATE), lambda i: (i, 0, 0))
    return pl.pallas_call(
        _ssm_prep_block_kernel,
        grid=(g // gb,),
        in_specs=[vec, vec, pl.BlockSpec((gb, 1, 1), lambda i: (i, 0, 0)), mat, mat, mat, mat],
        out_specs=[pl.BlockSpec((gb, CW, CW), lambda i: (i, 0, 0)), wide, wide, wide, wide, vec, vec],
        out_shape=[
            jax.ShapeDtypeStruct((g, CW, CW), BF16),
            jax.ShapeDtypeStruct((g, CW, SSM_STATE), BF16),
            jax.ShapeDtypeStruct((g, CW, SSM_STATE), BF16),
            jax.ShapeDtypeStruct((g, CW, SSM_STATE), BF16),
            jax.ShapeDtypeStruct((g, CW, SSM_STATE), BF16),
            jax.ShapeDtypeStruct((g, 1, SSM_STATE), F32),
            jax.ShapeDtypeStruct((g, 1, SSM_STATE), F32),
        ],
        compiler_params=_params(("arbitrary",)),
        name="ssm_prep",
    )(a_re.reshape(g, 1, SSM_STATE), a_im.reshape(g, 1, SSM_STATE), log_dt.reshape(g, 1, 1),
      bt_re, bt_im, c_re, c_im)


def _gelu_tanh(x):
    c = math.sqrt(2.0 / math.pi)
    return x * (0.5 * (1.0 + jnp.tanh(c * (x + 0.044715 * (x * x * x)))))


def _ssm_kernel(u_ref, t_ref, wsr_ref, wsi_ref, rxr_ref, rxi_ref, a16r_ref, a16i_ref, d_ref,
                z_ref, sr_ref, si_ref, ug_ref, zg_ref):
    nk = u_ref.shape[1]
    for j in range(GROUP_BLOCK):
        ug_ref[j] = jnp.concatenate(
            [u_ref[s, :, j * SSM_GROUP:(j + 1) * SSM_GROUP] for s in range(CHUNK)], axis=1)
    u_ref = ug_ref
    for j in range(GROUP_BLOCK):
        u = u_ref[j]
        sr_ref[j * nk:(j + 1) * nk, :] = _dot(u, wsr_ref[j])
        si_ref[j * nk:(j + 1) * nk, :] = _dot(u, wsi_ref[j])

    ar = a16r_ref[...]
    ai = a16i_ref[...]

    def step(k, carry):
        xr, xi = carry
        rows = pl.ds(k, GROUP_BLOCK, stride=nk)
        sr = sr_ref[rows, :]
        si = si_ref[rows, :]
        sr_ref[rows, :] = xr
        si_ref[rows, :] = xi
        return (ar * xr - ai * xi + sr, ar * xi + ai * xr + si)

    zero = jnp.zeros((GROUP_BLOCK, SSM_STATE), F32)
    lax.fori_loop(0, nk, step, (zero, zero), unroll=4)

    for j in range(GROUP_BLOCK):
        u = u_ref[j]
        xr = sr_ref[j * nk:(j + 1) * nk, :].astype(BF16)
        xi = si_ref[j * nk:(j + 1) * nk, :].astype(BF16)
        y = (_dot(u, t_ref[j]) + _dot_nt(xr, rxr_ref[j]) + _dot_nt(xi, rxi_ref[j])
             + d_ref[j] * u.astype(F32))
        zg_ref[j] = _gelu_tanh(y).astype(zg_ref.dtype)

    for t in range(CHUNK):
        z_ref[t] = jnp.concatenate(
            [zg_ref[j, :, t * SSM_GROUP:(t + 1) * SSM_GROUP] for j in range(GROUP_BLOCK)], axis=1)


def _ssm(u3, tmat, wsr, wsi, rxr, rxi, a16r, a16i, d_tiled):
    _, nk, width = u3.shape
    gb = GROUP_BLOCK
    gl = gb * SSM_GROUP
    blk3 = lambda a, b: pl.BlockSpec((gb, a, b), lambda i: (i, 0, 0))
    io = pl.BlockSpec((CHUNK, nk, gl), lambda i: (0, 0, i))
    return pl.pallas_call(
        _ssm_kernel,
        grid=(width // gl,),
        in_specs=[io, blk3(CW, CW), blk3(CW, SSM_STATE), blk3(CW, SSM_STATE),
                  blk3(CW, SSM_STATE), blk3(CW, SSM_STATE),
                  pl.BlockSpec((gb, SSM_STATE), lambda i: (i, 0)),
                  pl.BlockSpec((gb, SSM_STATE), lambda i: (i, 0)),
                  blk3(1, CW)],
        out_specs=io,
        out_shape=jax.ShapeDtypeStruct(u3.shape, BF16),
        scratch_shapes=[pltpu.VMEM((gb * nk, SSM_STATE), F32), pltpu.VMEM((gb * nk, SSM_STATE), F32),
                        pltpu.VMEM((gb, nk, CW), BF16), pltpu.VMEM((gb, nk, CW), BF16)],
        compiler_params=_params(("arbitrary",)),
        name="ssm",
    )(u3, tmat, wsr, wsi, rxr, rxi, a16r, a16i, d_tiled)


def _glu_kernel(z_ref, perm_ref, wg_ref, bg_ref, wb_ref, g0_ref, o_ref):
    tm = o_ref.shape[0]
    z = _dot(perm_ref[...], z_ref[...].reshape(tm, SSM_WIDTH)).astype(BF16)
    h = _dot(z, wg_ref[...]) + bg_ref[...]
    ga = h[:, :SSM_WIDTH]
    gb = h[:, SSM_WIDTH:]
    a = (ga * _sigmoid(gb)).astype(BF16)
    y = _dot(a, wb_ref[...])
    o_ref[...] = (_sigmoid(g0_ref[...].astype(F32)) * y).astype(o_ref.dtype)


def _glu(z3, w_glu, b_glu, w_br, proj, tm=512):
    t = z3.shape[0] * z3.shape[1]
    nk = tm // CHUNK
    r = np.arange(tm)
    perm = np.zeros((tm, tm), np.float32)
    perm[r, (r % CHUNK) * nk + r // CHUNK] = 1.0
    return pl.pallas_call(
        _glu_kernel,
        grid=(t // tm,),
        in_specs=[
            pl.BlockSpec((CHUNK, nk, SSM_WIDTH), lambda i: (0, i, 0)),
            pl.BlockSpec((tm, tm), lambda i: (0, 0)),
            pl.BlockSpec((SSM_WIDTH, 2 * SSM_WIDTH), lambda i: (0, 0)),
            pl.BlockSpec((1, 2 * SSM_WIDTH), lambda i: (0, 0)),
            pl.BlockSpec((SSM_WIDTH, D_MODEL), lambda i: (0, 0)),
            pl.BlockSpec((tm, D_MODEL), lambda i: (i, COL_G0 // D_MODEL)),
        ],
        out_specs=pl.BlockSpec((tm, D_MODEL), lambda i: (i, 0)),
        out_shape=jax.ShapeDtypeStruct((t, D_MODEL), BF16),
        compiler_params=_params(("arbitrary",)),
        name="glu",
    )(z3, jnp.asarray(perm, BF16), w_glu, b_glu, w_br, proj)


def _rope_pattern():
    half = ROPE_DIM // 2
    inv_freq = (np.float32(ROPE_THETA) ** (-np.arange(half, dtype=np.float32) / np.float32(half))).astype(np.float32)
    d = np.arange(LANES) % HEAD_DIM
    rotated = d < ROPE_DIM
    pat = np.zeros((16, LANES), np.float32)
    pat[:half] = rotated[None, :] & ((d % half)[None, :] == np.arange(half)[:, None])
    pat[8] = ~rotated
    pat[9] = np.where(d < half, -1.0, 0.0)
    pat[10] = np.where((d >= half) & rotated, 1.0, 0.0)
    return inv_freq.reshape(half, 1), pat


def _attn_kernel(q_ref, k_ref, v_ref, pos_ref, freq_ref, pat_ref, sink_ref, wbr_ref, ysg_ref, g1_ref,
                 o_ref, qbuf, kbuf, vbuf, obuf, sbuf, pbuf):
    i = pl.program_id(0)
    tq = q_ref.shape[0]
    nw = tq // WINDOW
    half = ROPE_DIM // 2

    @pl.when(i == 0)
    def _():
        kbuf[:, 0:WINDOW, :] = jnp.zeros((2 * N_KV_HEADS, WINDOW, LANES), BF16)
        vbuf[:, 0:WINDOW, :] = jnp.zeros((2 * N_KV_HEADS, WINDOW, LANES), BF16)

    ang = freq_ref[...] * pos_ref[...].astype(F32)
    spread = lambda tab: lax.dot_general(tab, pat_ref[0:8, :], (((0,), (0,)), ((), ())),
                                         preferred_element_type=F32, precision=HIGHEST)
    cs = spread(jnp.cos(ang)) + pat_ref[8:9, :]
    sn = spread(jnp.sin(ang))
    c_up = sn * pat_ref[9:10, :]
    c_dn = sn * pat_ref[10:11, :]

    def rope(x):
        return (x * cs + pltpu.roll(x, LANES - half, 1) * c_up + pltpu.roll(x, half, 1) * c_dn)

    low = lax.broadcasted_iota(I32, (tq, LANES), 1) < HEAD_DIM

    def split_heads(buf, cb, x):
        xs = pltpu.roll(x, HEAD_DIM, 1)
        zero = jnp.zeros_like(x)
        buf[4 * cb + 0, WINDOW:, :] = jnp.where(low, x, zero).astype(BF16)
        buf[4 * cb + 1, WINDOW:, :] = jnp.where(low, zero, xs).astype(BF16)
        buf[4 * cb + 2, WINDOW:, :] = jnp.where(low, xs, zero).astype(BF16)
        buf[4 * cb + 3, WINDOW:, :] = jnp.where(low, zero, x).astype(BF16)

    scale = HEAD_DIM ** -0.5
    for cb in range(Q_WIDTH // LANES):
        sl = slice(cb * LANES, (cb + 1) * LANES)
        qbuf[:, sl] = (rope(q_ref[:, sl].astype(F32)) * scale).astype(BF16)
    for cb in range(KV_WIDTH // LANES):
        sl = slice(cb * LANES, (cb + 1) * LANES)
        split_heads(kbuf, cb, rope(k_ref[:, sl].astype(F32)))
        split_heads(vbuf, cb, v_ref[:, sl].astype(F32))

    kj = lax.broadcasted_iota(I32, (2 * WINDOW, WINDOW), 0)
    qi = lax.broadcasted_iota(I32, (2 * WINDOW, WINDOW), 1)
    dist = qi + WINDOW - kj
    in_band = (dist >= 0) & (dist < WINDOW)
    cur_only = kj >= WINDOW
    sinks = sink_ref[...]

    def window(w, carry):
        r0 = pl.multiple_of(w * WINDOW, WINDOW)
        rows = pl.ds(r0, 2 * WINDOW)
        not_first = (i * nw + w) > 0
        mask = in_band & (cur_only | not_first)

        for h in range(N_Q_HEADS):
            qp = qbuf[pl.ds(r0, WINDOW), (h // 2) * LANES:(h // 2 + 1) * LANES]
            sbuf[h] = _dot_nt(kbuf[2 * (h // Q_PER_KV) + h % 2, rows, :], qp)
        for h in range(N_Q_HEADS):
            s = jnp.where(mask, sbuf[h], -jnp.inf)
            sink = sinks[:, h:h + 1]
            m = jnp.maximum(jnp.max(s, axis=0, keepdims=True), sink)
            p = jnp.exp(s - m)
            denom = jnp.sum(p, axis=0, keepdims=True) + jnp.exp(sink - m)
            pbuf[h] = (p * (1.0 / denom)).astype(BF16)
        for a in range(N_Q_HEADS // 2):
            kv = (2 * a) // Q_PER_KV
            o = _dot_tn(pbuf[2 * a], vbuf[2 * kv, rows, :]) + _dot_tn(pbuf[2 * a + 1], vbuf[2 * kv + 1, rows, :])
            obuf[pl.ds(r0, WINDOW), a * LANES:(a + 1) * LANES] = o.astype(BF16)
        return carry

    lax.fori_loop(0, nw, window, 0)

    kbuf[:, 0:WINDOW, :] = kbuf[:, tq:tq + WINDOW, :]
    vbuf[:, 0:WINDOW, :] = vbuf[:, tq:tq + WINDOW, :]

    y = _dot(obuf[...], wbr_ref[...])
    o_ref[...] = (ysg_ref[...].astype(F32) + _sigmoid(g1_ref[...].astype(F32)) * y).astype(o_ref.dtype)


def _attn(proj, pos_row, sinks, w_br, ysg, tq=512):
    t = proj.shape[0]
    freq, pat = (jnp.asarray(a) for a in _rope_pattern())
    return pl.pallas_call(
        _attn_kernel,
        grid=(t // tq,),
        in_specs=[
            pl.BlockSpec((tq, Q_WIDTH), lambda i: (i, COL_Q // Q_WIDTH)),
            pl.BlockSpec((tq, KV_WIDTH), lambda i: (i, COL_K // KV_WIDTH)),
            pl.BlockSpec((tq, KV_WIDTH), lambda i: (i, COL_V // KV_WIDTH)),
            pl.BlockSpec((1, tq), lambda i: (0, i)),
            pl.BlockSpec((ROPE_DIM // 2, 1), lambda i: (0, 0)),
            pl.BlockSpec((16, LANES), lambda i: (0, 0)),
            pl.BlockSpec((1, N_Q_HEADS), lambda i: (0, 0)),
            pl.BlockSpec((Q_WIDTH, D_MODEL), lambda i: (0, 0)),
            pl.BlockSpec((tq, D_MODEL), lambda i: (i, 0)),
            pl.BlockSpec((tq, D_MODEL), lambda i: (i, COL_G1 // D_MODEL)),
        ],
        out_specs=pl.BlockSpec((tq, D_MODEL), lambda i: (i, 0)),
        out_shape=jax.ShapeDtypeStruct((t, D_MODEL), BF16),
        scratch_shapes=[
            pltpu.VMEM((tq, Q_WIDTH), BF16),
            pltpu.VMEM((2 * N_KV_HEADS, tq + WINDOW, LANES), BF16),
            pltpu.VMEM((2 * N_KV_HEADS, tq + WINDOW, LANES), BF16),
            pltpu.VMEM((tq, Q_WIDTH), BF16),
            pltpu.VMEM((N_Q_HEADS, 2 * WINDOW, WINDOW), F32),
            pltpu.VMEM((N_Q_HEADS, 2 * WINDOW, WINDOW), BF16),
        ],
        compiler_params=_params(("arbitrary",)),
        name="attn",
    )(proj, proj, proj, pos_row, freq, pat, sinks, w_br, ysg, proj)


def _oproj_kernel(x_ref, mix_ref, wo_ref, g_ref, wrh_ref, wrl_ref, br_ref, h_ref, hp_ref, lt_ref):
    h = x_ref[...] + _dot(mix_ref[...], wo_ref[...])
    h_ref[...] = h
    ms = jnp.mean(h * h, axis=-1, keepdims=True)
    hn = h * lax.rsqrt(ms + EPS) * g_ref[...]
    _store_token_tiles(hp_ref, hn)
    hn_hi = hn.astype(BF16)
    hn_lo = (hn - hn_hi.astype(F32)).astype(BF16)
    logits = (_dot(hn_hi, wrh_ref[...]) + _dot(hn_lo, wrh_ref[...]) + _dot(hn_hi, wrl_ref[...])
              + br_ref[...])
    lt_ref[...] = logits.T


def _oproj(x, mixed, w_o, g, w_router, b_router, tm=512):
    t = x.shape[0]
    w_router_hi = w_router.astype(BF16)
    w_router_hi_rest = (w_router - w_router_hi.astype(F32)).astype(BF16)
    return pl.pallas_call(
        _oproj_kernel,
        grid=(t // tm,),
        in_specs=[
            pl.BlockSpec((tm, D_MODEL), lambda i: (i, 0)),
            pl.BlockSpec((tm, D_MODEL), lambda i: (i, 0)),
            pl.BlockSpec((D_MODEL, D_MODEL), lambda i: (0, 0)),
            pl.BlockSpec((1, D_MODEL), lambda i: (0, 0)),
            pl.BlockSpec((D_MODEL, LANES), lambda i: (0, 0)),
            pl.BlockSpec((D_MODEL, LANES), lambda i: (0, 0)),
            pl.BlockSpec((1, LANES), lambda i: (0, 0)),
        ],
        out_specs=[
            pl.BlockSpec((tm, D_MODEL), lambda i: (i, 0)),
            pl.BlockSpec((tm * SUBLANES, LANES), lambda i: (i, 0)),
            pl.BlockSpec((LANES, tm), lambda i: (0, i)),
        ],
        out_shape=[
            jax.ShapeDtypeStruct((t, D_MODEL), F32),
            jax.ShapeDtypeStruct((t * SUBLANES, LANES), jnp.uint32),
            jax.ShapeDtypeStruct((LANES, t), F32),
        ],
        compiler_params=_params(("arbitrary",)),
        name="oproj",
    )(x, mixed, w_o, g, w_router_hi, w_router_hi_rest, b_router)


ROUTE_CHUNK = 256


def _route_kernel(lt_ref, dest_ref, wts_ref, bexp_ref, nvalid_ref, eid_ref, rank_ref):
    t = lt_ref.shape[1]
    nc = t // ROUTE_CHUNK
    r8 = lax.broadcasted_iota(I32, (N_GROUPS, ROUTE_CHUNK), 0)
    r64 = lax.broadcasted_iota(I32, (N_EXPERTS, ROUTE_CHUNK), 0)

    def pick(c, carry):
        cols = pl.ds(pl.multiple_of(c * ROUTE_CHUNK, ROUTE_CHUNK), ROUTE_CHUNK)
        lg = lt_ref[0:N_GROUPS, cols]
        m = jnp.max(lg, axis=0, keepdims=True)
        ssum = jnp.sum(jnp.exp(lg - m), axis=0, keepdims=True)
        p_grp = 1.0 / ssum
        grp = jnp.min(jnp.where(lg == m, r8, N_GROUPS), axis=0, keepdims=True)
        le = lt_ref[N_GROUPS:N_GROUPS + N_EXPERTS, cols]
        leg = jnp.where((r64 // EXPERTS_PER_GROUP) == grp, le, -jnp.inf)
        m1 = jnp.max(leg, axis=0, keepdims=True)
        i1 = jnp.min(jnp.where(leg == m1, r64, N_EXPERTS), axis=0, keepdims=True)
        leg2 = jnp.where(r64 == i1, -jnp.inf, leg)
        m2 = jnp.max(leg2, axis=0, keepdims=True)
        i2 = jnp.min(jnp.where(leg2 == m2, r64, N_EXPERTS), axis=0, keepdims=True)
        ex = jnp.exp(m2 - m1)
        eid_ref[0:1, cols] = i1
        eid_ref[1:2, cols] = i2
        wts_ref[0:1, cols] = p_grp / (1.0 + ex)
        wts_ref[1:2, cols] = p_grp * ex / (1.0 + ex)
        return carry

    lax.fori_loop(0, nc, pick, 0)

    a_row = lax.broadcasted_iota(I32, (ROUTE_CHUNK, ROUTE_CHUNK), 0)
    a_col = lax.broadcasted_iota(I32, (ROUTE_CHUNK, ROUTE_CHUNK), 1)
    before = (a_row < a_col).astype(BF16)

    def count(n, carry):
        j = n // nc
        c = n - j * nc
        cols = pl.ds(pl.multiple_of(c * ROUTE_CHUNK, ROUTE_CHUNK), ROUTE_CHUNK)
        oh = r64 == eid_ref[pl.ds(j, 1), cols]
        ohf = oh.astype(F32)
        pref = _dot(ohf.astype(BF16), before) + carry
        rank_ref[pl.ds(j, 1), cols] = jnp.sum(jnp.where(oh, pref, 0.0), axis=0, keepdims=True)
        return carry + jnp.sum(ohf, axis=1, keepdims=True)

    counts = lax.fori_loop(0, TOP_K * nc, count, jnp.zeros((N_EXPERTS, 1), F32))

    padded = jnp.floor((counts + (MOE_BLOCK - 1)) * (1.0 / MOE_BLOCK)) * MOE_BLOCK
    e_row = lax.broadcasted_iota(I32, (N_EXPERTS, N_EXPERTS), 0)
    e_col = lax.broadcasted_iota(I32, (N_EXPERTS, N_EXPERTS), 1)
    incl = (e_col <= e_row).astype(F32)
    pad_end = _dot(incl, jnp.broadcast_to(padded, (N_EXPERTS, LANES)), precision=HIGHEST)[:, 0:1]
    pad_start = pad_end - padded

    def place(n, carry):
        j = n // nc
        c = n - j * nc
        cols = pl.ds(pl.multiple_of(c * ROUTE_CHUNK, ROUTE_CHUNK), ROUTE_CHUNK)
        oh = r64 == eid_ref[pl.ds(j, 1), cols]
        start = jnp.sum(jnp.where(oh, pad_start, 0.0), axis=0, keepdims=True)
        dest_ref[pl.ds(j, 1), cols] = (start + rank_ref[pl.ds(j, 1), cols]).astype(I32)
        return carry

    lax.fori_loop(0, TOP_K * nc, place, 0)

    b0 = (lax.broadcasted_iota(I32, (N_EXPERTS, bexp_ref.shape[1]), 1) * MOE_BLOCK).astype(F32)
    n_done = jnp.sum((pad_end <= b0).astype(F32), axis=0, keepdims=True)
    bexp_ref[...] = jnp.minimum(n_done, float(N_EXPERTS - 1)).astype(I32)
    live = jnp.minimum(pad_start + counts, b0 + MOE_BLOCK) - jnp.maximum(pad_start, b0)
    nvalid_ref[...] = jnp.sum(jnp.maximum(live, 0.0), axis=0, keepdims=True).astype(I32)


def _route(logits_t, n_blocks_pad):
    t = logits_t.shape[1]
    return pl.pallas_call(
        _route_kernel,
        out_shape=[
            jax.ShapeDtypeStruct((TOP_K, t), I32),
            jax.ShapeDtypeStruct((TOP_K, t), F32),
            jax.ShapeDtypeStruct((1, n_blocks_pad), I32),
            jax.ShapeDtypeStruct((1, n_blocks_pad), I32),
        ],
        scratch_shapes=[pltpu.VMEM((TOP_K, t), I32), pltpu.VMEM((TOP_K, t), F32)],
        compiler_params=pltpu.CompilerParams(vmem_limit_bytes=VMEM_LIMIT),
        name="route",
    )(logits_t)


def _invmap_kernel(dest_ref, rt_ref):
    n_rows = rt_ref.shape[0]
    t = dest_ref.shape[0] // TOP_K

    def clear(r, c):
        rt_ref[r] = 0
        return c

    lax.fori_loop(0, n_rows, clear, 0, unroll=16)

    def put(tok, c):
        rt_ref[dest_ref[tok]] = tok
        rt_ref[dest_ref[t + tok]] = tok
        return c

    lax.fori_loop(0, t, put, 0, unroll=8)


def _invmap(dest_flat, n_rows):
    return pl.pallas_call(
        _invmap_kernel,
        in_specs=[pl.BlockSpec(memory_space=pltpu.SMEM)],
        out_specs=pl.BlockSpec(memory_space=pltpu.SMEM),
        out_shape=jax.ShapeDtypeStruct((n_rows,), I32),
        name="invmap",
    )(dest_flat)


GATHER_SLOTS = 4


def _row_copy(src_hbm, src_row, dst_buf, slot, dst_row, sem):
    return pltpu.make_async_copy(src_hbm.at[pl.ds(pl.multiple_of(src_row * SUBLANES, SUBLANES), SUBLANES), :],
                                 dst_buf.at[slot, pl.ds(dst_row * SUBLANES, SUBLANES), :], sem.at[slot])


def _rows_copy(src_hbm, dst_buf, slot, rows, sem):
    n = rows * SUBLANES
    return pltpu.make_async_copy(src_hbm.at[pl.ds(0, n), :], dst_buf.at[slot, pl.ds(0, n), :], sem.at[slot])


def _expert_kernel(bexp_ref, nvalid_ref, rt_ref, hp_hbm, wg_hbm, wu_hbm, wd_hbm, ys_ref,
                   xbuf, wf_g, wf_u, wf_d, wb_g, wb_u, wb_d, ord_ref, xsem, wsem):
    b = pl.program_id(0)
    nb = pl.num_programs(0) - 1

    def expert_of(blk):
        return bexp_ref[jnp.minimum(blk, nb - 1)]

    def next_owner(blk, e):
        return lax.while_loop(lambda j: (j < nb) & (expert_of(j) == e), lambda j: j + 1, blk)

    def weight_copies(e, slot):
        return (pltpu.make_async_copy(wg_hbm.at[e], wf_g.at[slot], wsem.at[slot, 0]),
                pltpu.make_async_copy(wu_hbm.at[e], wf_u.at[slot], wsem.at[slot, 1]),
                pltpu.make_async_copy(wd_hbm.at[e], wf_d.at[slot], wsem.at[slot, 2]))

    def start_weights(e, slot):
        for c, queue in zip(weight_copies(e, slot), (1, 0, 1)):
            c.start(priority=queue)

    @pl.when(b == 0)
    def _():
        ord_ref[0] = 0
        e0 = bexp_ref[0]
        start_weights(e0, 0)
        n1 = next_owner(1, e0)

        @pl.when(n1 < nb)
        def _():
            start_weights(expert_of(n1), 1)

    def gathered_rows(blk):
        n = nvalid_ref[jnp.clip(blk, 0, nb - 1)]
        live_rows = ((n + (SUBLANES - 1)) >> 3) << 3
        return jnp.where(blk >= nb, 0, jnp.where((blk < GATHER_SLOTS) & (n > 0), MOE_BLOCK, live_rows))

    def gather(blk):
        base = blk * MOE_BLOCK
        slot = blk % GATHER_SLOTS

        def body(i, c):
            for s in range(SUBLANES):
                r = i * SUBLANES + s
                _row_copy(hp_hbm, rt_ref[base + r], xbuf, slot, r, xsem).start()
            return c

        lax.fori_loop(0, gathered_rows(blk) // SUBLANES, body, 0)

    @pl.when(b == 0)
    def _():
        for blk in range(GATHER_SLOTS - 2):
            gather(blk)

    gather(b + GATHER_SLOTS - 2)

    @pl.when(b > 0)
    def _():
        blk = b - 1
        e = bexp_ref[blk]
        first = (blk == 0) | (e != bexp_ref[jnp.maximum(blk - 1, 0)])

        @pl.when(first)
        def _():
            n = ord_ref[0]
            wslot = n % 2
            ord_ref[0] = n + 1
            for c in weight_copies(e, wslot):
                c.wait()
            wb_g[...] = wf_g[wslot].astype(BF16)
            wb_u[...] = wf_u[wslot].astype(BF16)
            wb_d[...] = wf_d[wslot].astype(BF16)
            n1 = next_owner(blk + 1, e)
            n2 = next_owner(n1 + 1, expert_of(n1))

            @pl.when((n1 < nb) & (n2 < nb))
            def _():
                start_weights(expert_of(n2), wslot)

        slot = blk % GATHER_SLOTS
        rows = gathered_rows(blk)

        @pl.when(rows > 0)
        def _():
            _rows_copy(hp_hbm, xbuf, slot, rows, xsem).wait()
            lo, hi = (v.astype(BF16) for v in _load_token_tiles(xbuf, slot, 0, MOE_BLOCK))
            g = _dot(lo, wb_g[:HALF, :]) + _dot(hi, wb_g[HALF:, :])
            u = _dot(lo, wb_u[:HALF, :]) + _dot(hi, wb_u[HALF:, :])
            h = (g * _sigmoid(g) * u).astype(BF16)
            y = _dot(h, wb_d[...])
            _store_token_tiles(ys_ref, y)

        @pl.when(rows == 0)
        def _():
            _store_token_tiles(ys_ref, jnp.zeros((MOE_BLOCK, D_MODEL), F32))


def _experts(bexp, nvalid, row_tok, hn_packed, w_gate, w_up, w_down, n_blocks):
    grid_spec = pltpu.PrefetchScalarGridSpec(
        num_scalar_prefetch=3,
        grid=(n_blocks + 1,),
        in_specs=[pl.BlockSpec(memory_space=pl.ANY)] * 4,
        out_specs=pl.BlockSpec((MOE_BLOCK * SUBLANES, LANES), lambda b, be, nv, rt: (jnp.maximum(b - 1, 0), 0)),
        scratch_shapes=[
            pltpu.VMEM((GATHER_SLOTS, MOE_BLOCK * SUBLANES, LANES), jnp.uint32),
            pltpu.VMEM((2, D_MODEL, D_EXPERT), F32), pltpu.VMEM((2, D_MODEL, D_EXPERT), F32),
            pltpu.VMEM((2, D_EXPERT, D_MODEL), F32),
            pltpu.VMEM((D_MODEL, D_EXPERT), BF16), pltpu.VMEM((D_MODEL, D_EXPERT), BF16),
            pltpu.VMEM((D_EXPERT, D_MODEL), BF16),
            pltpu.SMEM((1,), I32),
            pltpu.SemaphoreType.DMA((GATHER_SLOTS,)), pltpu.SemaphoreType.DMA((2, 3)),
        ],
    )
    return pl.pallas_call(
        _expert_kernel,
        grid_spec=grid_spec,
        out_shape=jax.ShapeDtypeStruct((n_blocks * MOE_BLOCK * SUBLANES, LANES), jnp.uint32),
        compiler_params=_params(("arbitrary",)),
        name="experts",
    )(bexp, nvalid, row_tok, hn_packed, w_gate, w_up, w_down)


def _combine_kernel(dest_ref, ys_hbm, h_ref, w_ref, g_ref, o_ref, ybuf, sem):
    i = pl.program_id(0)
    n = pl.num_programs(0)
    tq = h_ref.shape[0]
    t = n * tq

    def start_gather(blk, slot):
        base = blk * tq

        def body(j, c):
            for s in range(SUBLANES):
                r = j * SUBLANES + s
                _row_copy(ys_hbm, dest_ref[base + r], ybuf, slot, r, sem).start(priority=0)
                _row_copy(ys_hbm, dest_ref[t + base + r], ybuf, slot, tq + r, sem).start(priority=1)
            return c

        lax.fori_loop(0, tq // SUBLANES, body, 0)

    @pl.when(i == 0)
    def _():
        start_gather(0, 0)

    @pl.when(i + 1 < n)
    def _():
        start_gather(i + 1, (i + 1) % 2)

    slot = i % 2
    _rows_copy(ys_hbm, ybuf, slot, TOP_K * tq, sem).wait()
    y0_lo, y0_hi = _load_token_tiles(ybuf, slot, 0, tq)
    y1_lo, y1_hi = _load_token_tiles(ybuf, slot, tq, tq)
    w = w_ref[...]
    w0 = w[:, 0:1]
    w1 = w[:, 1:2]
    h = h_ref[...]
    lo = h[:, :HALF] + w0 * y0_lo + w1 * y1_lo
    hi = h[:, HALF:] + w0 * y0_hi + w1 * y1_hi
    ms = (jnp.sum(lo * lo, axis=-1, keepdims=True) + jnp.sum(hi * hi, axis=-1, keepdims=True)) * (1.0 / D_MODEL)
    inv = lax.rsqrt(ms + EPS)
    g = g_ref[...]
    o_ref[:, :HALF] = lo * inv * g[:, :HALF]
    o_ref[:, HALF:] = hi * inv * g[:, HALF:]


def _combine(dest_flat, ys_packed, h1, wts_tok, g_final, tq=256):
    t = h1.shape[0]
    grid_spec = pltpu.PrefetchScalarGridSpec(
        num_scalar_prefetch=1,
        grid=(t // tq,),
        in_specs=[
            pl.BlockSpec(memory_space=pl.ANY),
            pl.BlockSpec((tq, D_MODEL), lambda i, d: (i, 0)),
            pl.BlockSpec((tq, TOP_K), lambda i, d: (i, 0)),
            pl.BlockSpec((1, D_MODEL), lambda i, d: (0, 0)),
        ],
        out_specs=pl.BlockSpec((tq, D_MODEL), lambda i, d: (i, 0)),
        scratch_shapes=[pltpu.VMEM((2, TOP_K * tq * SUBLANES, LANES), jnp.uint32),
                        pltpu.SemaphoreType.DMA((2,))],
    )
    return pl.pallas_call(
        _combine_kernel,
        grid_spec=grid_spec,
        out_shape=jax.ShapeDtypeStruct((t, D_MODEL), F32),
        compiler_params=_params(("arbitrary",)),
        name="combine",
    )(dest_flat, ys_packed, h1, wts_tok, g_final)


def _mixers(h, positions, norm_g, w_in, b_in, a_re, a_im, log_dt, b_re, b_im, c_re, c_im, d_skip,
            w_glu, b_glu, w_br_ssm, sinks, w_br_attn):
    t = h.shape[0]
    ngate = 2 * D_MODEL
    g = norm_g.reshape(1, D_MODEL)
    b_row = b_in.reshape(1, IN_WIDTH)
    proj, u3 = _inproj(h, g, w_in, b_row, u_segment=(0, SSM_WIDTH),
                       main_segments=((IN_WIDTH - ngate, ngate), (SSM_WIDTH, IN_WIDTH - ngate - SSM_WIDTH)))

    tmat, wsr, wsi, rxr, rxi, a16r, a16i = _ssm_prep(
        a_re, a_im, log_dt, b_re.transpose(0, 2, 1), b_im.transpose(0, 2, 1), c_re, c_im)
    d_tiled = jnp.tile(d_skip, (1, CHUNK)).reshape(SSM_GROUPS, 1, CW)
    z3 = _ssm(u3, tmat, wsr, wsi, rxr, rxi, a16r.reshape(SSM_GROUPS, SSM_STATE),
              a16i.reshape(SSM_GROUPS, SSM_STATE), d_tiled)

    ysg = _glu(z3, w_glu.astype(BF16), b_glu.reshape(1, -1), w_br_ssm.astype(BF16), proj)
    return _attn(proj, positions.reshape(1, t), sinks.reshape(1, N_Q_HEADS), w_br_attn.astype(BF16), ysg)


def _moe_tail(h, mixed, w_o, norm_ffn_g, w_rg, b_rg, w_re, b_re, w_gate, w_up, w_down, norm_final_g):
    t = h.shape[0]
    n_route = N_GROUPS + N_EXPERTS
    w_router = jnp.concatenate([w_rg, w_re, jnp.zeros((D_MODEL, LANES - n_route), F32)], axis=1)
    b_router = jnp.concatenate([b_rg, b_re, jnp.zeros((LANES - n_route,), F32)]).reshape(1, LANES)
    h1, hn_packed, logits_t = _oproj(h, mixed, w_o.astype(BF16), norm_ffn_g.reshape(1, D_MODEL),
                                     w_router, b_router)
    n_assign = t * TOP_K
    n_blocks = -(-(n_assign + N_EXPERTS * (MOE_BLOCK - 1)) // MOE_BLOCK)
    n_blocks_pad = -(-n_blocks // LANES) * LANES
    dest, wts, bexp, nvalid = _route(logits_t, n_blocks_pad)
    dest_flat = dest.reshape(n_assign)
    row_tok = _invmap(dest_flat, n_blocks * MOE_BLOCK)
    ys_packed = _experts(bexp[0, :n_blocks], nvalid[0, :n_blocks], row_tok, hn_packed, w_gate, w_up, w_down,
                         n_blocks)
    return _combine(dest_flat, ys_packed, h1, wts.T, norm_final_g.reshape(1, D_MODEL))


def kernel(x, positions, norm_mix_g, w_in, b_in, ssm_a_re, ssm_a_im, ssm_log_dt, ssm_b_re, ssm_b_im, ssm_c_re, ssm_c_im, ssm_d, w_glu, b_glu, w_br_ssm, attn_sinks, w_br_attn, w_o, norm_ffn_g, w_router_group, b_router_group, w_router_expert, b_router_expert, w_exp_gate, w_exp_up, w_exp_down, norm_final_g):
    bsz, seq, d = x.shape
    assert bsz == 1 and d == D_MODEL and norm_mix_g.shape[0] == 1
    h = x.reshape(seq, d)
    mixed = _mixers(h, positions, norm_mix_g[0], w_in[0], b_in[0], ssm_a_re[0], ssm_a_im[0], ssm_log_dt[0],
                    ssm_b_re[0], ssm_b_im[0], ssm_c_re[0], ssm_c_im[0], ssm_d[0], w_glu[0], b_glu[0],
                    w_br_ssm[0], attn_sinks[0], w_br_attn[0])
    out = _moe_tail(h, mixed, w_o[0], norm_ffn_g[0], w_router_group[0], b_router_group[0],
                    w_router_expert[0], b_router_expert[0], w_exp_gate[0], w_exp_up[0], w_exp_down[0],
                    norm_final_g)
    return out.reshape(bsz, seq, d)
```

```python
import functools
import math

import numpy as np
import jax
import jax.numpy as jnp
from jax import lax
from jax.experimental import pallas as pl
from jax.experimental.pallas import tpu as pltpu

F32 = jnp.float32
BF16 = jnp.bfloat16
I32 = jnp.int32

D_MODEL = 2048
SSM_WIDTH = 1024
SSM_GROUP = 16
SSM_GROUPS = 64
SSM_STATE = 64
HEAD_DIM = 64
N_Q_HEADS = 16
N_KV_HEADS = 4
Q_PER_KV = 4
WINDOW = 128
ROPE_DIM = 16
ROPE_THETA = 500000.0
Q_WIDTH = 1024
KV_WIDTH = 256
IN_WIDTH = SSM_WIDTH + Q_WIDTH + 2 * KV_WIDTH + 2 * D_MODEL
N_GROUPS = 8
EXPERTS_PER_GROUP = 8
N_EXPERTS = 64
TOP_K = 2
D_EXPERT = 512
MOE_BLOCK = 128
EPS = 1e-6

CHUNK = 16
CW = CHUNK * SSM_GROUP
GROUP_BLOCK = 8
HALF = D_MODEL // 2
LANES = 128
VMEM_LIMIT = 56 * 1024 * 1024

COL_G0, COL_G1, COL_Q, COL_K, COL_V = 0, 2048, 4096, 5120, 5376

HIGHEST = lax.Precision.HIGHEST


def _dot(a, b, precision=None):
    return jnp.dot(a, b, preferred_element_type=F32, precision=precision)


def _dot_nt(a, b, precision=None):
    return lax.dot_general(a, b, (((1,), (1,)), ((), ())), preferred_element_type=F32,
                           precision=precision)


def _dot_tn(a, b):
    return lax.dot_general(a, b, (((0,), (0,)), ((), ())), preferred_element_type=F32)


def _sigmoid(x):
    return 1.0 / (1.0 + jnp.exp(-x))


def _pack_halves(lo, hi):
    return pltpu.pack_elementwise([lo, hi], packed_dtype=BF16)


def _unpack_half(w, index):
    return pltpu.unpack_elementwise(w, index=index, packed_dtype=BF16, unpacked_dtype=F32)


SUBLANES = 8
WORD_TILES = HALF // LANES
assert WORD_TILES == SUBLANES


def _store_token_tiles(ref, x):
    rows = x.shape[0]
    for s in range(WORD_TILES):
        ref[pl.ds(s, rows, stride=SUBLANES), :] = _pack_halves(x[:, s * LANES:(s + 1) * LANES],
                                                               x[:, HALF + s * LANES:HALF + (s + 1) * LANES])


def _load_token_tiles(ref, slot, first_row, rows):
    pieces = [ref[slot, pl.ds(first_row * SUBLANES + s, rows, stride=SUBLANES), :] for s in range(WORD_TILES)]
    return (jnp.concatenate([_unpack_half(p, 0) for p in pieces], axis=1),
            jnp.concatenate([_unpack_half(p, 1) for p in pieces], axis=1))


def _params(sem, vmem=VMEM_LIMIT):
    return pltpu.CompilerParams(dimension_semantics=sem, vmem_limit_bytes=vmem)


def _inproj_kernel(x_ref, g_ref, w_ref, b_ref, o_ref, u_ref, xn_ref, acc_ref, *, main_panels):
    j = pl.program_id(1)

    @pl.when(j == 0)
    def _():
        x = x_ref[...]
        ms = jnp.mean(x * x, axis=-1, keepdims=True)
        xn_ref[...] = (x * lax.rsqrt(ms + EPS) * g_ref[...]).astype(BF16)

    acc = _dot(xn_ref[...], w_ref[...].astype(BF16)) + b_ref[...]

    @pl.when(j < main_panels)
    def _():
        o_ref[...] = acc.astype(o_ref.dtype)

    @pl.when(j >= main_panels)
    def _():
        nk = u_ref.shape[1]
        for c in range(acc_ref.shape[0]):
            acc_ref[c] = acc[:, c * LANES:(c + 1) * LANES]
        for s in range(CHUNK):
            for c in range(acc_ref.shape[0]):
                u_ref[s, :, c * LANES:(c + 1) * LANES] = (
                    acc_ref[c, pl.ds(s, nk, stride=CHUNK), :].astype(u_ref.dtype))


def _inproj(x, g, w, b, *, main_segments, u_segment, tm=1024, tn=512):
    t, d = x.shape
    (c0, n0), (c1, n1) = main_segments
    cu, nu = u_segment
    assert all(v % tn == 0 for v in (c0, n0, c1, n1, cu, nu))
    p0, p1, pu = n0 // tn, n1 // tn, nu // tn
    main = p0 + p1
    panel = lambda i, j: (0, jnp.where(j < p0, j + c0 // tn,
                                       jnp.where(j < main, j - p0 + c1 // tn, j - main + cu // tn)))
    return pl.pallas_call(
        functools.partial(_inproj_kernel, main_panels=main),
        grid=(t // tm, main + pu),
        in_specs=[
            pl.BlockSpec((tm, d), lambda i, j: (i, 0)),
            pl.BlockSpec((1, d), lambda i, j: (0, 0)),
            pl.BlockSpec((d, tn), panel),
            pl.BlockSpec((1, tn), panel),
        ],
        out_specs=[pl.BlockSpec((tm, tn), lambda i, j: (i, jnp.minimum(j, main - 1))),
                   pl.BlockSpec((CHUNK, tm // CHUNK, tn), lambda i, j: (0, i, jnp.maximum(j - main, 0)))],
        out_shape=[jax.ShapeDtypeStruct((t, n0 + n1), BF16),
                   jax.ShapeDtypeStruct((CHUNK, t // CHUNK, nu), BF16)],
        scratch_shapes=[pltpu.VMEM((tm, d), BF16), pltpu.VMEM((tn // LANES, tm, LANES), F32)],
        compiler_params=_params(("arbitrary", "arbitrary")),
        name="inproj",
    )(x, g, w, b)


def _ssm_prep_kernel(are_ref, aim_ref, ldt_ref, btr_ref, bti_ref, cr_ref, ci_ref,
                     t_ref, wsr_ref, wsi_ref, rxr_ref, rxi_ref, a16r_ref, a16i_ref):
    lam_re = jnp.minimum(are_ref[0], -1e-4)
    lam_im = aim_ref[0]
    dt = jnp.exp(ldt_ref[0])
    lr_dt = lam_re * dt
    th = lam_im * dt
    mag = jnp.exp(lr_dt)
    ab_re = mag * jnp.cos(th)
    ab_im = mag * jnp.sin(th)
    den = lam_re * lam_re + lam_im * lam_im
    nr = ab_re - 1.0
    ni = ab_im
    coef_re = (nr * lam_re + ni * lam_im) / den
    coef_im = (ni * lam_re - nr * lam_im) / den
    btr = btr_ref[0]
    bti = bti_ref[0]
    bb_re = coef_re * btr - coef_im * bti
    bb_im = coef_re * bti + coef_im * btr

    e = lax.broadcasted_iota(I32, (CHUNK, SSM_STATE), 0).astype(F32)
    pmag = jnp.exp(e * lr_dt)
    pos_re = pmag * jnp.cos(e * th)
    pos_im = pmag * jnp.sin(e * th)
    nmag = jnp.exp(-e * lr_dt)
    neg_re = nmag * jnp.cos(e * th)
    neg_im = -nmag * jnp.sin(e * th)

    def rep(tab):
        return jnp.broadcast_to(tab[:, None, :], (CHUNK, SSM_GROUP, SSM_STATE)).reshape(CW, SSM_STATE)

    def tile(mat):
        return jnp.broadcast_to(mat[None, :, :], (CHUNK, SSM_GROUP, SSM_STATE)).reshape(CW, SSM_STATE)

    pr, pi = rep(pos_re), rep(pos_im)
    ctr, cti = tile(cr_ref[0]), tile(ci_ref[0])
    r_re = ctr * pr - cti * pi
    r_im = ctr * pi + cti * pr
    qr, qi = rep(neg_re), rep(neg_im)
    btr_t, bti_t = tile(bb_re), tile(bb_im)
    l_re = btr_t * qr - bti_t * qi
    l_im = btr_t * qi + bti_t * qr

    def split(v):
        head = v.astype(BF16)
        return head, (v - head.astype(F32)).astype(BF16)

    def dot_nt3(a, b):
        (ah, al), (bh, bl) = split(a), split(b)
        return _dot_nt(ah, bh) + _dot_nt(al, bh) + _dot_nt(ah, bl)

    tm = dot_nt3(l_re, r_re) - dot_nt3(l_im, r_im)
    srow = lax.broadcasted_iota(I32, (CW, CW), 0) // SSM_GROUP
    tcol = lax.broadcasted_iota(I32, (CW, CW), 1) // SSM_GROUP
    t_ref[0] = jnp.where(tcol >= srow, tm, 0.0).astype(BF16)

    a15r = pos_re[CHUNK - 1:CHUNK, :]
    a15i = pos_im[CHUNK - 1:CHUNK, :]
    wsr_ref[0] = (l_re * a15r - l_im * a15i).astype(BF16)
    wsi_ref[0] = (l_re * a15i + l_im * a15r).astype(BF16)
    rxr_ref[0] = (r_re * ab_re - r_im * ab_im).astype(BF16)
    rxi_ref[0] = (-(r_re * ab_im + r_im * ab_re)).astype(BF16)
    m16 = jnp.exp(float(CHUNK) * lr_dt)
    a16r_ref[0] = m16 * jnp.cos(float(CHUNK) * th)
    a16i_ref[0] = m16 * jnp.sin(float(CHUNK) * th)


def _ssm_prep_block_kernel(*refs):
    def one(j, carry):
        _ssm_prep_kernel(*[r.at[pl.ds(j, 1)] for r in refs])
        return carry

    lax.fori_loop(0, GROUP_BLOCK, one, 0)


def _ssm_prep(a_re, a_im, log_dt, bt_re, bt_im, c_re, c_im):
    g = a_re.shape[0]
    gb = GROUP_BLOCK
    vec = pl.BlockSpec((gb, 1, SSM_STATE), lambda i: (i, 0, 0))
    mat = pl.BlockSpec((gb, SSM_GROUP, SSM_STATE), lambda i: (i, 0, 0))
    wide = pl.BlockSpec((gb, CW, SSM_STATE), lambda i: (i, 0, 0))
    return pl.pallas_call(
        _ssm_prep_block_kernel,
        grid=(g // gb,),
        in_specs=[vec, vec, pl.BlockSpec((gb, 1, 1), lambda i: (i, 0, 0)), mat, mat, mat, mat],
        out_specs=[pl.BlockSpec((gb, CW, CW), lambda i: (i, 0, 0)), wide, wide, wide, wide, vec, vec],
        out_shape=[
            jax.ShapeDtypeStruct((g, CW, CW), BF16),
            jax.ShapeDtypeStruct((g, CW, SSM_STATE), BF16),
            jax.ShapeDtypeStruct((g, CW, SSM_STATE), BF16),
            jax.ShapeDtypeStruct((g, CW, SSM_STATE), BF16),
            jax.ShapeDtypeStruct((g, CW, SSM_STATE), BF16),
            jax.ShapeDtypeStruct((g, 1, SSM_STATE), F32),
            jax.ShapeDtypeStruct((g, 1, SSM_STATE), F32),
        ],
        compiler_params=_params(("arbitrary",)),
        name="ssm_prep",
    )(a_re.reshape(g, 1, SSM_STATE), a_im.reshape(g, 1, SSM_STATE), log_dt.reshape(g, 1, 1),
      bt_re, bt_im, c_re, c_im)


def _gelu_tanh(x):
    c = math.sqrt(2.0 / math.pi)
    return x * (0.5 * (1.0 + jnp.tanh(c * (x + 0.044715 * (x * x * x)))))


def _ssm_kernel(u_ref, t_ref, wsr_ref, wsi_ref, rxr_ref, rxi_ref, a16r_ref, a16i_ref, d_ref,
                z_ref, sr_ref, si_ref, ug_ref, zg_ref):
    nk = u_ref.shape[1]
    for j in range(GROUP_BLOCK):
        ug_ref[j] = jnp.concatenate(
            [u_ref[s, :, j * SSM_GROUP:(j + 1) * SSM_GROUP] for s in range(CHUNK)], axis=1)
    u_ref = ug_ref
    for j in range(GROUP_BLOCK):
        u = u_ref[j]
        sr_ref[j * nk:(j + 1) * nk, :] = _dot(u, wsr_ref[j])
        si_ref[j * nk:(j + 1) * nk, :] = _dot(u, wsi_ref[j])

    ar = a16r_ref[...]
    ai = a16i_ref[...]

    def step(k, carry):
        xr, xi = carry
        rows = pl.ds(k, GROUP_BLOCK, stride=nk)
        sr = sr_ref[rows, :]
        si = si_ref[rows, :]
        sr_ref[rows, :] = xr
        si_ref[rows, :] = xi
        return (ar * xr - ai * xi + sr, ar * xi + ai * xr + si)

    zero = jnp.zeros((GROUP_BLOCK, SSM_STATE), F32)
    lax.fori_loop(0, nk, step, (zero, zero), unroll=8)

    for j in range(GROUP_BLOCK):
        u = u_ref[j]
        xr = sr_ref[j * nk:(j + 1) * nk, :].astype(BF16)
        xi = si_ref[j * nk:(j + 1) * nk, :].astype(BF16)
        y = (_dot(u, t_ref[j]) + _dot_nt(xr, rxr_ref[j]) + _dot_nt(xi, rxi_ref[j])
             + d_ref[j] * u.astype(F32))
        zg_ref[j] = _gelu_tanh(y).astype(zg_ref.dtype)

    for t in range(CHUNK):
        z_ref[t] = jnp.concatenate(
            [zg_ref[j, :, t * SSM_GROUP:(t + 1) * SSM_GROUP] for j in range(GROUP_BLOCK)], axis=1)


def _ssm(u3, tmat, wsr, wsi, rxr, rxi, a16r, a16i, d_tiled):
    _, nk, width = u3.shape
    gb = GROUP_BLOCK
    gl = gb * SSM_GROUP
    blk3 = lambda a, b: pl.BlockSpec((gb, a, b), lambda i: (i, 0, 0))
    io = pl.BlockSpec((CHUNK, nk, gl), lambda i: (0, 0, i))
    return pl.pallas_call(
        _ssm_kernel,
        grid=(width // gl,),
        in_specs=[io, blk3(CW, CW), blk3(CW, SSM_STATE), blk3(CW, SSM_STATE),
                  blk3(CW, SSM_STATE), blk3(CW, SSM_STATE),
                  pl.BlockSpec((gb, SSM_STATE), lambda i: (i, 0)),
                  pl.BlockSpec((gb, SSM_STATE), lambda i: (i, 0)),
                  blk3(1, CW)],
        out_specs=io,
        out_shape=jax.ShapeDtypeStruct(u3.shape, BF16),
        scratch_shapes=[pltpu.VMEM((gb * nk, SSM_STATE), F32), pltpu.VMEM((gb * nk, SSM_STATE), F32),
                        pltpu.VMEM((gb, nk, CW), BF16), pltpu.VMEM((gb, nk, CW), BF16)],
        compiler_params=_params(("arbitrary",)),
        name="ssm",
    )(u3, tmat, wsr, wsi, rxr, rxi, a16r, a16i, d_tiled)


def _glu_kernel(z_ref, perm_ref, wg_ref, bg_ref, wb_ref, g0_ref, o_ref):
    tm = o_ref.shape[0]
    z = _dot(perm_ref[...], z_ref[...].reshape(tm, SSM_WIDTH)).astype(BF16)
    h = _dot(z, wg_ref[...]) + bg_ref[...]
    ga = h[:, :SSM_WIDTH]
    gb = h[:, SSM_WIDTH:]
    a = (ga * _sigmoid(gb)).astype(BF16)
    y = _dot(a, wb_ref[...])
    o_ref[...] = (_sigmoid(g0_ref[...].astype(F32)) * y).astype(o_ref.dtype)


def _glu(z3, w_glu, b_glu, w_br, proj, tm=512):
    t = z3.shape[0] * z3.shape[1]
    nk = tm // CHUNK
    r = np.arange(tm)
    perm = np.zeros((tm, tm), np.float32)
    perm[r, (r % CHUNK) * nk + r // CHUNK] = 1.0
    return pl.pallas_call(
        _glu_kernel,
        grid=(t // tm,),
        in_specs=[
            pl.BlockSpec((CHUNK, nk, SSM_WIDTH), lambda i: (0, i, 0)),
            pl.BlockSpec((tm, tm), lambda i: (0, 0)),
            pl.BlockSpec((SSM_WIDTH, 2 * SSM_WIDTH), lambda i: (0, 0)),
            pl.BlockSpec((1, 2 * SSM_WIDTH), lambda i: (0, 0)),
            pl.BlockSpec((SSM_WIDTH, D_MODEL), lambda i: (0, 0)),
            pl.BlockSpec((tm, D_MODEL), lambda i: (i, COL_G0 // D_MODEL)),
        ],
        out_specs=pl.BlockSpec((tm, D_MODEL), lambda i: (i, 0)),
        out_shape=jax.ShapeDtypeStruct((t, D_MODEL), BF16),
        compiler_params=_params(("arbitrary",)),
        name="glu",
    )(z3, jnp.asarray(perm, BF16), w_glu, b_glu, w_br, proj)


def _rope_pattern():
    half = ROPE_DIM // 2
    inv_freq = (np.float32(ROPE_THETA) ** (-np.arange(half, dtype=np.float32) / np.float32(half))).astype(np.float32)
    d = np.arange(LANES) % HEAD_DIM
    rotated = d < ROPE_DIM
    pat = np.zeros((16, LANES), np.float32)
    pat[:half] = rotated[None, :] & ((d % half)[None, :] == np.arange(half)[:, None])
    pat[8] = ~rotated
    pat[9] = np.where(d < half, -1.0, 0.0)
    pat[10] = np.where((d >= half) & rotated, 1.0, 0.0)
    return inv_freq.reshape(half, 1), pat


def _attn_kernel(q_ref, k_ref, v_ref, pos_ref, freq_ref, pat_ref, sink_ref, wbr_ref, ysg_ref, g1_ref,
                 o_ref, qbuf, kbuf, vbuf, obuf, sbuf, pbuf):
    i = pl.program_id(0)
    tq = q_ref.shape[0]
    nw = tq // WINDOW
    half = ROPE_DIM // 2

    @pl.when(i == 0)
    def _():
        kbuf[:, 0:WINDOW, :] = jnp.zeros((2 * N_KV_HEADS, WINDOW, LANES), BF16)
        vbuf[:, 0:WINDOW, :] = jnp.zeros((2 * N_KV_HEADS, WINDOW, LANES), BF16)

    ang = freq_ref[...] * pos_ref[...].astype(F32)
    spread = lambda tab: lax.dot_general(tab, pat_ref[0:8, :], (((0,), (0,)), ((), ())),
                                         preferred_element_type=F32, precision=HIGHEST)
    cs = spread(jnp.cos(ang)) + pat_ref[8:9, :]
    sn = spread(jnp.sin(ang))
    c_up = sn * pat_ref[9:10, :]
    c_dn = sn * pat_ref[10:11, :]

    def rope(x):
        return (x * cs + pltpu.roll(x, LANES - half, 1) * c_up + pltpu.roll(x, half, 1) * c_dn)

    low = lax.broadcasted_iota(I32, (tq, LANES), 1) < HEAD_DIM

    def split_heads(buf, cb, x):
        xs = pltpu.roll(x, HEAD_DIM, 1)
        zero = jnp.zeros_like(x)
        buf[4 * cb + 0, WINDOW:, :] = jnp.where(low, x, zero).astype(BF16)
        buf[4 * cb + 1, WINDOW:, :] = jnp.where(low, zero, xs).astype(BF16)
        buf[4 * cb + 2, WINDOW:, :] = jnp.where(low, xs, zero).astype(BF16)
        buf[4 * cb + 3, WINDOW:, :] = jnp.where(low, zero, x).astype(BF16)

    scale = HEAD_DIM ** -0.5
    for cb in range(Q_WIDTH // LANES):
        sl = slice(cb * LANES, (cb + 1) * LANES)
        qbuf[:, sl] = (rope(q_ref[:, sl].astype(F32)) * scale).astype(BF16)
    for cb in range(KV_WIDTH // LANES):
        sl = slice(cb * LANES, (cb + 1) * LANES)
        split_heads(kbuf, cb, rope(k_ref[:, sl].astype(F32)))
        split_heads(vbuf, cb, v_ref[:, sl].astype(F32))

    kj = lax.broadcasted_iota(I32, (2 * WINDOW, WINDOW), 0)
    qi = lax.broadcasted_iota(I32, (2 * WINDOW, WINDOW), 1)
    dist = qi + WINDOW - kj
    in_band = (dist >= 0) & (dist < WINDOW)
    cur_only = kj >= WINDOW
    sinks = sink_ref[...]

    def window(w, carry):
        r0 = pl.multiple_of(w * WINDOW, WINDOW)
        rows = pl.ds(r0, 2 * WINDOW)
        not_first = (i * nw + w) > 0
        mask = in_band & (cur_only | not_first)

        for h in range(N_Q_HEADS):
            qp = qbuf[pl.ds(r0, WINDOW), (h // 2) * LANES:(h // 2 + 1) * LANES]
            sbuf[h] = _dot_nt(kbuf[2 * (h // Q_PER_KV) + h % 2, rows, :], qp)
        for h in range(N_Q_HEADS):
            s = jnp.where(mask, sbuf[h], -jnp.inf)
            sink = sinks[:, h:h + 1]
            m = jnp.maximum(jnp.max(s, axis=0, keepdims=True), sink)
            p = jnp.exp(s - m)
            denom = jnp.sum(p, axis=0, keepdims=True) + jnp.exp(sink - m)
            pbuf[h] = (p * (1.0 / denom)).astype(BF16)
        for a in range(N_Q_HEADS // 2):
            kv = (2 * a) // Q_PER_KV
            o = _dot_tn(pbuf[2 * a], vbuf[2 * kv, rows, :]) + _dot_tn(pbuf[2 * a + 1], vbuf[2 * kv + 1, rows, :])
            obuf[pl.ds(r0, WINDOW), a * LANES:(a + 1) * LANES] = o.astype(BF16)
        return carry

    lax.fori_loop(0, nw, window, 0)

    kbuf[:, 0:WINDOW, :] = kbuf[:, tq:tq + WINDOW, :]
    vbuf[:, 0:WINDOW, :] = vbuf[:, tq:tq + WINDOW, :]

    y = _dot(obuf[...], wbr_ref[...])
    o_ref[...] = (ysg_ref[...].astype(F32) + _sigmoid(g1_ref[...].astype(F32)) * y).astype(o_ref.dtype)


def _attn(proj, pos_row, sinks, w_br, ysg, tq=512):
    t = proj.shape[0]
    freq, pat = (jnp.asarray(a) for a in _rope_pattern())
    return pl.pallas_call(
        _attn_kernel,
        grid=(t // tq,),
        in_specs=[
            pl.BlockSpec((tq, Q_WIDTH), lambda i: (i, COL_Q // Q_WIDTH)),
            pl.BlockSpec((tq, KV_WIDTH), lambda i: (i, COL_K // KV_WIDTH)),
            pl.BlockSpec((tq, KV_WIDTH), lambda i: (i, COL_V // KV_WIDTH)),
            pl.BlockSpec((1, tq), lambda i: (0, i)),
            pl.BlockSpec((ROPE_DIM // 2, 1), lambda i: (0, 0)),
            pl.BlockSpec((16, LANES), lambda i: (0, 0)),
            pl.BlockSpec((1, N_Q_HEADS), lambda i: (0, 0)),
            pl.BlockSpec((Q_WIDTH, D_MODEL), lambda i: (0, 0)),
            pl.BlockSpec((tq, D_MODEL), lambda i: (i, 0)),
            pl.BlockSpec((tq, D_MODEL), lambda i: (i, COL_G1 // D_MODEL)),
        ],
        out_specs=pl.BlockSpec((tq, D_MODEL), lambda i: (i, 0)),
        out_shape=jax.ShapeDtypeStruct((t, D_MODEL), BF16),
        scratch_shapes=[
            pltpu.VMEM((tq, Q_WIDTH), BF16),
            pltpu.VMEM((2 * N_KV_HEADS, tq + WINDOW, LANES), BF16),
            pltpu.VMEM((2 * N_KV_HEADS, tq + WINDOW, LANES), BF16),
            pltpu.VMEM((tq, Q_WIDTH), BF16),
            pltpu.VMEM((N_Q_HEADS, 2 * WINDOW, WINDOW), F32),
            pltpu.VMEM((N_Q_HEADS, 2 * WINDOW, WINDOW), BF16),
        ],
        compiler_params=_params(("arbitrary",)),
        name="attn",
    )(proj, proj, proj, pos_row, freq, pat, sinks, w_br, ysg, proj)


def _oproj_kernel(x_ref, mix_ref, wo_ref, g_ref, wrh_ref, wrl_ref, br_ref, h_ref, hp_ref, lt_ref):
    h = x_ref[...] + _dot(mix_ref[...], wo_ref[...])
    h_ref[...] = h
    ms = jnp.mean(h * h, axis=-1, keepdims=True)
    hn = h * lax.rsqrt(ms + EPS) * g_ref[...]
    _store_token_tiles(hp_ref, hn)
    hn_hi = hn.astype(BF16)
    hn_lo = (hn - hn_hi.astype(F32)).astype(BF16)
    logits = (_dot(hn_hi, wrh_ref[...]) + _dot(hn_lo, wrh_ref[...]) + _dot(hn_hi, wrl_ref[...])
              + br_ref[...])
    lt_ref[...] = logits.T


def _oproj(x, mixed, w_o, g, w_router, b_router, tm=512):
    t = x.shape[0]
    w_router_hi = w_router.astype(BF16)
    w_router_hi_rest = (w_router - w_router_hi.astype(F32)).astype(BF16)
    return pl.pallas_call(
        _oproj_kernel,
        grid=(t // tm,),
        in_specs=[
            pl.BlockSpec((tm, D_MODEL), lambda i: (i, 0)),
            pl.BlockSpec((tm, D_MODEL), lambda i: (i, 0)),
            pl.BlockSpec((D_MODEL, D_MODEL), lambda i: (0, 0)),
            pl.BlockSpec((1, D_MODEL), lambda i: (0, 0)),
            pl.BlockSpec((D_MODEL, LANES), lambda i: (0, 0)),
            pl.BlockSpec((D_MODEL, LANES), lambda i: (0, 0)),
            pl.BlockSpec((1, LANES), lambda i: (0, 0)),
        ],
        out_specs=[
            pl.BlockSpec((tm, D_MODEL), lambda i: (i, 0)),
            pl.BlockSpec((tm * SUBLANES, LANES), lambda i: (i, 0)),
            pl.BlockSpec((LANES, tm), lambda i: (0, i)),
        ],
        out_shape=[
            jax.ShapeDtypeStruct((t, D_MODEL), F32),
            jax.ShapeDtypeStruct((t * SUBLANES, LANES), jnp.uint32),
            jax.ShapeDtypeStruct((LANES, t), F32),
        ],
        compiler_params=_params(("arbitrary",)),
        name="oproj",
    )(x, mixed, w_o, g, w_router_hi, w_router_hi_rest, b_router)


ROUTE_CHUNK = 256


def _route_kernel(lt_ref, dest_ref, wts_ref, bexp_ref, nvalid_ref, eid_ref, rank_ref):
    t = lt_ref.shape[1]
    nc = t // ROUTE_CHUNK
    r8 = lax.broadcasted_iota(I32, (N_GROUPS, ROUTE_CHUNK), 0)
    r64 = lax.broadcasted_iota(I32, (N_EXPERTS, ROUTE_CHUNK), 0)

    def pick(c, carry):
        cols = pl.ds(pl.multiple_of(c * ROUTE_CHUNK, ROUTE_CHUNK), ROUTE_CHUNK)
        lg = lt_ref[0:N_GROUPS, cols]
        m = jnp.max(lg, axis=0, keepdims=True)
        ssum = jnp.sum(jnp.exp(lg - m), axis=0, keepdims=True)
        p_grp = 1.0 / ssum
        grp = jnp.min(jnp.where(lg == m, r8, N_GROUPS), axis=0, keepdims=True)
        le = lt_ref[N_GROUPS:N_GROUPS + N_EXPERTS, cols]
        leg = jnp.where((r64 // EXPERTS_PER_GROUP) == grp, le, -jnp.inf)
        m1 = jnp.max(leg, axis=0, keepdims=True)
        i1 = jnp.min(jnp.where(leg == m1, r64, N_EXPERTS), axis=0, keepdims=True)
        leg2 = jnp.where(r64 == i1, -jnp.inf, leg)
        m2 = jnp.max(leg2, axis=0, keepdims=True)
        i2 = jnp.min(jnp.where(leg2 == m2, r64, N_EXPERTS), axis=0, keepdims=True)
        ex = jnp.exp(m2 - m1)
        eid_ref[0:1, cols] = i1
        eid_ref[1:2, cols] = i2
        wts_ref[0:1, cols] = p_grp / (1.0 + ex)
        wts_ref[1:2, cols] = p_grp * ex / (1.0 + ex)
        return carry

    lax.fori_loop(0, nc, pick, 0)

    a_row = lax.broadcasted_iota(I32, (ROUTE_CHUNK, ROUTE_CHUNK), 0)
    a_col = lax.broadcasted_iota(I32, (ROUTE_CHUNK, ROUTE_CHUNK), 1)
    before = (a_row < a_col).astype(BF16)

    def count(n, carry):
        j = n // nc
        c = n - j * nc
        cols = pl.ds(pl.multiple_of(c * ROUTE_CHUNK, ROUTE_CHUNK), ROUTE_CHUNK)
        oh = r64 == eid_ref[pl.ds(j, 1), cols]
        ohf = oh.astype(F32)
        pref = _dot(ohf.astype(BF16), before) + carry
        rank_ref[pl.ds(j, 1), cols] = jnp.sum(jnp.where(oh, pref, 0.0), axis=0, keepdims=True)
        return carry + jnp.sum(ohf, axis=1, keepdims=True)

    counts = lax.fori_loop(0, TOP_K * nc, count, jnp.zeros((N_EXPERTS, 1), F32))

    padded = jnp.floor((counts + (MOE_BLOCK - 1)) * (1.0 / MOE_BLOCK)) * MOE_BLOCK
    e_row = lax.broadcasted_iota(I32, (N_EXPERTS, N_EXPERTS), 0)
    e_col = lax.broadcasted_iota(I32, (N_EXPERTS, N_EXPERTS), 1)
    incl = (e_col <= e_row).astype(F32)
    pad_end = _dot(incl, jnp.broadcast_to(padded, (N_EXPERTS, LANES)), precision=HIGHEST)[:, 0:1]
    pad_start = pad_end - padded

    def place(n, carry):
        j = n // nc
        c = n - j * nc
        cols = pl.ds(pl.multiple_of(c * ROUTE_CHUNK, ROUTE_CHUNK), ROUTE_CHUNK)
        oh = r64 == eid_ref[pl.ds(j, 1), cols]
        start = jnp.sum(jnp.where(oh, pad_start, 0.0), axis=0, keepdims=True)
        dest_ref[pl.ds(j, 1), cols] = (start + rank_ref[pl.ds(j, 1), cols]).astype(I32)
        return carry

    lax.fori_loop(0, TOP_K * nc, place, 0)

    b0 = (lax.broadcasted_iota(I32, (N_EXPERTS, bexp_ref.shape[1]), 1) * MOE_BLOCK).astype(F32)
    n_done = jnp.sum((pad_end <= b0).astype(F32), axis=0, keepdims=True)
    bexp_ref[...] = jnp.minimum(n_done, float(N_EXPERTS - 1)).astype(I32)
    live = jnp.minimum(pad_start + counts, b0 + MOE_BLOCK) - jnp.maximum(pad_start, b0)
    nvalid_ref[...] = jnp.sum(jnp.maximum(live, 0.0), axis=0, keepdims=True).astype(I32)


def _route(logits_t, n_blocks_pad):
    t = logits_t.shape[1]
    return pl.pallas_call(
        _route_kernel,
        out_shape=[
            jax.ShapeDtypeStruct((TOP_K, t), I32),
            jax.ShapeDtypeStruct((TOP_K, t), F32),
            jax.ShapeDtypeStruct((1, n_blocks_pad), I32),
            jax.ShapeDtypeStruct((1, n_blocks_pad), I32),
        ],
        scratch_shapes=[pltpu.VMEM((TOP_K, t), I32), pltpu.VMEM((TOP_K, t), F32)],
        compiler_params=pltpu.CompilerParams(vmem_limit_bytes=VMEM_LIMIT),
        name="route",
    )(logits_t)


def _invmap_kernel(dest_ref, rt_ref, zero_ref, sem):
    t = dest_ref.shape[0] // TOP_K
    zero_ref[...] = jnp.zeros(zero_ref.shape, zero_ref.dtype)
    clear = pltpu.make_async_copy(zero_ref, rt_ref, sem)
    clear.start()
    clear.wait()

    def put(tok, c):
        rt_ref[dest_ref[tok]] = tok
        rt_ref[dest_ref[t + tok]] = tok
        return c

    lax.fori_loop(0, t, put, 0, unroll=8)


def _invmap(dest_flat, n_rows):
    return pl.pallas_call(
        _invmap_kernel,
        in_specs=[pl.BlockSpec(memory_space=pltpu.SMEM)],
        out_specs=pl.BlockSpec(memory_space=pltpu.SMEM),
        out_shape=jax.ShapeDtypeStruct((n_rows,), I32),
        scratch_shapes=[pltpu.VMEM((n_rows,), I32), pltpu.SemaphoreType.DMA(())],
        name="invmap",
    )(dest_flat)


GATHER_SLOTS = 4
WEIGHT_SLOTS = 3


def _row_copy(src_hbm, src_row, dst_buf, slot, dst_row, sem):
    return pltpu.make_async_copy(src_hbm.at[pl.ds(pl.multiple_of(src_row * SUBLANES, SUBLANES), SUBLANES), :],
                                 dst_buf.at[slot, pl.ds(dst_row * SUBLANES, SUBLANES), :], sem.at[slot])


def _rows_copy(src_hbm, dst_buf, slot, rows, sem):
    n = rows * SUBLANES
    return pltpu.make_async_copy(src_hbm.at[pl.ds(0, n), :], dst_buf.at[slot, pl.ds(0, n), :], sem.at[slot])


def _expert_kernel(bexp_ref, nvalid_ref, rt_ref, hp_hbm, wg_hbm, wu_hbm, wd_hbm, ys_ref,
                   xbuf, wf_g, wf_u, wf_d, ord_ref, xsem, wsem):
    b = pl.program_id(0)
    nb = pl.num_programs(0) - 1

    def expert_of(blk):
        return bexp_ref[jnp.minimum(blk, nb - 1)]

    def next_owner(blk, e):
        return lax.while_loop(lambda j: (j < nb) & (expert_of(j) == e), lambda j: j + 1, blk)

    def weight_copies(e, slot):
        return (pltpu.make_async_copy(wg_hbm.at[e], wf_g.at[slot], wsem.at[slot, 0]),
                pltpu.make_async_copy(wu_hbm.at[e], wf_u.at[slot], wsem.at[slot, 1]),
                pltpu.make_async_copy(wd_hbm.at[e], wf_d.at[slot], wsem.at[slot, 2]))

    def start_weights(e, slot):
        for c in weight_copies(e, slot):
            c.start(priority=1)

    @pl.when(b == 0)
    def _():
        ord_ref[0] = 0
        e0 = bexp_ref[0]
        start_weights(e0, 0)
        n1 = next_owner(1, e0)

        @pl.when(n1 < nb)
        def _():
            start_weights(expert_of(n1), 1)

    def gathered_rows(blk):
        n = nvalid_ref[jnp.clip(blk, 0, nb - 1)]
        live_rows = ((n + (SUBLANES - 1)) >> 3) << 3
        return jnp.where(blk >= nb, 0, jnp.where((blk < GATHER_SLOTS) & (n > 0), MOE_BLOCK, live_rows))

    def gather(blk):
        base = blk * MOE_BLOCK
        slot = blk % GATHER_SLOTS

        def body(i, c):
            for s in range(SUBLANES):
                r = i * SUBLANES + s
                _row_copy(hp_hbm, rt_ref[base + r], xbuf, slot, r, xsem).start()
            return c

        lax.fori_loop(0, gathered_rows(blk) // SUBLANES, body, 0)

    @pl.when(b == 0)
    def _():
        for blk in range(GATHER_SLOTS - 2):
            gather(blk)

    gather(b + GATHER_SLOTS - 2)

    @pl.when(b > 0)
    def _():
        blk = b - 1
        e = bexp_ref[blk]
        first = (blk == 0) | (e != bexp_ref[jnp.maximum(blk - 1, 0)])

        @pl.when(first)
        def _():
            n = ord_ref[0]
            ord_ref[0] = n + 1
            for c in weight_copies(e, n % WEIGHT_SLOTS):
                c.wait()
            n1 = next_owner(blk + 1, e)
            n2 = next_owner(n1 + 1, expert_of(n1))

            @pl.when((n1 < nb) & (n2 < nb))
            def _():
                start_weights(expert_of(n2), (n + 2) % WEIGHT_SLOTS)

        slot = blk % GATHER_SLOTS
        rows = gathered_rows(blk)

        @pl.when(rows > 0)
        def _():
            _rows_copy(hp_hbm, xbuf, slot, rows, xsem).wait()
            lo, hi = (v.astype(BF16) for v in _load_token_tiles(xbuf, slot, 0, MOE_BLOCK))
            ws = (ord_ref[0] - 1) % WEIGHT_SLOTS
            g = _dot(lo, wf_g[ws, :HALF, :].astype(BF16)) + _dot(hi, wf_g[ws, HALF:, :].astype(BF16))
            u = _dot(lo, wf_u[ws, :HALF, :].astype(BF16)) + _dot(hi, wf_u[ws, HALF:, :].astype(BF16))
            h = (g * _sigmoid(g) * u).astype(BF16)
            y = _dot(h, wf_d[ws].astype(BF16))
            _store_token_tiles(ys_ref, y)

        @pl.when(rows == 0)
        def _():
            _store_token_tiles(ys_ref, jnp.zeros((MOE_BLOCK, D_MODEL), F32))


def _experts(bexp, nvalid, row_tok, hn_packed, w_gate, w_up, w_down, n_blocks):
    grid_spec = pltpu.PrefetchScalarGridSpec(
        num_scalar_prefetch=3,
        grid=(n_blocks + 1,),
        in_specs=[pl.BlockSpec(memory_space=pl.ANY)] * 4,
        out_specs=pl.BlockSpec((MOE_BLOCK * SUBLANES, LANES), lambda b, be, nv, rt: (jnp.maximum(b - 1, 0), 0)),
        scratch_shapes=[
            pltpu.VMEM((GATHER_SLOTS, MOE_BLOCK * SUBLANES, LANES), jnp.uint32),
            pltpu.VMEM((WEIGHT_SLOTS, D_MODEL, D_EXPERT), F32), pltpu.VMEM((WEIGHT_SLOTS, D_MODEL, D_EXPERT), F32),
            pltpu.VMEM((WEIGHT_SLOTS, D_EXPERT, D_MODEL), F32),
            pltpu.SMEM((1,), I32),
            pltpu.SemaphoreType.DMA((GATHER_SLOTS,)), pltpu.SemaphoreType.DMA((WEIGHT_SLOTS, 3)),
        ],
    )
    return pl.pallas_call(
        _expert_kernel,
        grid_spec=grid_spec,
        out_shape=jax.ShapeDtypeStruct((n_blocks * MOE_BLOCK * SUBLANES, LANES), jnp.uint32),
        compiler_params=_params(("arbitrary",)),
        name="experts",
    )(bexp, nvalid, row_tok, hn_packed, w_gate, w_up, w_down)


def _combine_kernel(dest_ref, ys_hbm, h_ref, w_ref, g_ref, o_ref, ybuf, sem):
    i = pl.program_id(0)
    n = pl.num_programs(0)
    tq = h_ref.shape[0]
    t = n * tq

    def start_gather(blk, slot):
        base = blk * tq

        def body(j, c):
            for s in range(SUBLANES):
                r = j * SUBLANES + s
                _row_copy(ys_hbm, dest_ref[base + r], ybuf, slot, r, sem).start(priority=0)
                _row_copy(ys_hbm, dest_ref[t + base + r], ybuf, slot, tq + r, sem).start(priority=1)
            return c

        lax.fori_loop(0, tq // SUBLANES, body, 0)

    @pl.when(i == 0)
    def _():
        start_gather(0, 0)

    @pl.when(i + 1 < n)
    def _():
        start_gather(i + 1, (i + 1) % 2)

    slot = i % 2
    _rows_copy(ys_hbm, ybuf, slot, TOP_K * tq, sem).wait()
    y0_lo, y0_hi = _load_token_tiles(ybuf, slot, 0, tq)
    y1_lo, y1_hi = _load_token_tiles(ybuf, slot, tq, tq)
    w = w_ref[...]
    w0 = w[:, 0:1]
    w1 = w[:, 1:2]
    h = h_ref[...]
    lo = h[:, :HALF] + w0 * y0_lo + w1 * y1_lo
    hi = h[:, HALF:] + w0 * y0_hi + w1 * y1_hi
    ms = (jnp.sum(lo * lo, axis=-1, keepdims=True) + jnp.sum(hi * hi, axis=-1, keepdims=True)) * (1.0 / D_MODEL)
    inv = lax.rsqrt(ms + EPS)
    g = g_ref[...]
    o_ref[:, :HALF] = lo * inv * g[:, :HALF]
    o_ref[:, HALF:] = hi * inv * g[:, HALF:]


def _combine(dest_flat, ys_packed, h1, wts_tok, g_final, tq=256):
    t = h1.shape[0]
    grid_spec = pltpu.PrefetchScalarGridSpec(
        num_scalar_prefetch=1,
        grid=(t // tq,),
        in_specs=[
            pl.BlockSpec(memory_space=pl.ANY),
            pl.BlockSpec((tq, D_MODEL), lambda i, d: (i, 0)),
            pl.BlockSpec((tq, TOP_K), lambda i, d: (i, 0)),
            pl.BlockSpec((1, D_MODEL), lambda i, d: (0, 0)),
        ],
        out_specs=pl.BlockSpec((tq, D_MODEL), lambda i, d: (i, 0)),
        scratch_shapes=[pltpu.VMEM((2, TOP_K * tq * SUBLANES, LANES), jnp.uint32),
                        pltpu.SemaphoreType.DMA((2,))],
    )
    return pl.pallas_call(
        _combine_kernel,
        grid_spec=grid_spec,
        out_shape=jax.ShapeDtypeStruct((t, D_MODEL), F32),
        compiler_params=_params(("arbitrary",)),
        name="combine",
    )(dest_flat, ys_packed, h1, wts_tok, g_final)


def _mixers(h, positions, norm_g, w_in, b_in, a_re, a_im, log_dt, b_re, b_im, c_re, c_im, d_skip,
            w_glu, b_glu, w_br_ssm, sinks, w_br_attn):
    t = h.shape[0]
    ngate = 2 * D_MODEL
    g = norm_g.reshape(1, D_MODEL)
    b_row = b_in.reshape(1, IN_WIDTH)
    proj, u3 = _inproj(h, g, w_in, b_row, u_segment=(0, SSM_WIDTH),
                       main_segments=((IN_WIDTH - ngate, ngate), (SSM_WIDTH, IN_WIDTH - ngate - SSM_WIDTH)))

    tmat, wsr, wsi, rxr, rxi, a16r, a16i = _ssm_prep(
        a_re, a_im, log_dt, b_re.transpose(0, 2, 1), b_im.transpose(0, 2, 1), c_re, c_im)
    d_tiled = jnp.tile(d_skip, (1, CHUNK)).reshape(SSM_GROUPS, 1, CW)
    z3 = _ssm(u3, tmat, wsr, wsi, rxr, rxi, a16r.reshape(SSM_GROUPS, SSM_STATE),
              a16i.reshape(SSM_GROUPS, SSM_STATE), d_tiled)

    ysg = _glu(z3, w_glu.astype(BF16), b_glu.reshape(1, -1), w_br_ssm.astype(BF16), proj)
    return _attn(proj, positions.reshape(1, t), sinks.reshape(1, N_Q_HEADS), w_br_attn.astype(BF16), ysg)


def _moe_tail(h, mixed, w_o, norm_ffn_g, w_rg, b_rg, w_re, b_re, w_gate, w_up, w_down, norm_final_g):
    t = h.shape[0]
    n_route = N_GROUPS + N_EXPERTS
    w_router = jnp.concatenate([w_rg, w_re, jnp.zeros((D_MODEL, LANES - n_route), F32)], axis=1)
    b_router = jnp.concatenate([b_rg, b_re, jnp.zeros((LANES - n_route,), F32)]).reshape(1, LANES)
    h1, hn_packed, logits_t = _oproj(h, mixed, w_o.astype(BF16), norm_ffn_g.reshape(1, D_MODEL),
                                     w_router, b_router)
    n_assign = t * TOP_K
    n_blocks = -(-(n_assign + N_EXPERTS * (MOE_BLOCK - 1)) // MOE_BLOCK)
    n_blocks_pad = -(-n_blocks // LANES) * LANES
    dest, wts, bexp, nvalid = _route(logits_t, n_blocks_pad)
    dest_flat = dest.reshape(n_assign)
    row_tok = _invmap(dest_flat, n_blocks * MOE_BLOCK)
    ys_packed = _experts(bexp[0, :n_blocks], nvalid[0, :n_blocks], row_tok, hn_packed, w_gate, w_up, w_down,
                         n_blocks)
    return _combine(dest_flat, ys_packed, h1, wts.T, norm_final_g.reshape(1, D_MODEL))


def kernel(x, positions, norm_mix_g, w_in, b_in, ssm_a_re, ssm_a_im, ssm_log_dt, ssm_b_re, ssm_b_im, ssm_c_re, ssm_c_im, ssm_d, w_glu, b_glu, w_br_ssm, attn_sinks, w_br_attn, w_o, norm_ffn_g, w_router_group, b_router_group, w_router_expert, b_router_expert, w_exp_gate, w_exp_up, w_exp_down, norm_final_g):
    bsz, seq, d = x.shape
    assert bsz == 1 and d == D_MODEL and norm_mix_g.shape[0] == 1
    h = x.reshape(seq, d)
    mixed = _mixers(h, positions, norm_mix_g[0], w_in[0], b_in[0], ssm_a_re[0], ssm_a_im[0], ssm_log_dt[0],
                    ssm_b_re[0], ssm_b_im[0], ssm_c_re[0], ssm_c_im[0], ssm_d[0], w_glu[0], b_glu[0],
                    w_br_ssm[0], attn_sinks[0], w_br_attn[0])
    out = _moe_tail(h, mixed, w_o[0], norm_ffn_g[0], w_router_group[0], b_router_group[0],
                    w_router_expert[0], b_router_expert[0], w_exp_gate[0], w_exp_up[0], w_exp_down[0],
                    norm_final_g)
    return out.reshape(bsz, seq, d)
```

```python
import functools
import math

import numpy as np
import jax
import jax.numpy as jnp
from jax import lax
from jax.experimental import pallas as pl
from jax.experimental.pallas import tpu as pltpu

F32 = jnp.float32
BF16 = jnp.bfloat16
I32 = jnp.int32

D_MODEL = 2048
SSM_WIDTH = 1024
SSM_GROUP = 16
SSM_GROUPS = 64
SSM_STATE = 64
HEAD_DIM = 64
N_Q_HEADS = 16
N_KV_HEADS = 4
Q_PER_KV = 4
WINDOW = 128
ROPE_DIM = 16
ROPE_THETA = 500000.0
Q_WIDTH = 1024
KV_WIDTH = 256
IN_WIDTH = SSM_WIDTH + Q_WIDTH + 2 * KV_WIDTH + 2 * D_MODEL
N_GROUPS = 8
EXPERTS_PER_GROUP = 8
N_EXPERTS = 64
TOP_K = 2
D_EXPERT = 512
MOE_BLOCK = 128
EPS = 1e-6

CHUNK = 16
CW = CHUNK * SSM_GROUP
GROUP_BLOCK = 8
HALF = D_MODEL // 2
LANES = 128
VMEM_LIMIT = 56 * 1024 * 1024

COL_G0, COL_G1, COL_Q, COL_K, COL_V = 0, 2048, 4096, 5120, 5376

HIGHEST = lax.Precision.HIGHEST


def _dot(a, b, precision=None):
    return jnp.dot(a, b, preferred_element_type=F32, precision=precision)


def _dot_nt(a, b, precision=None):
    return lax.dot_general(a, b, (((1,), (1,)), ((), ())), preferred_element_type=F32,
                           precision=precision)


def _dot_tn(a, b):
    return lax.dot_general(a, b, (((0,), (0,)), ((), ())), preferred_element_type=F32)


def _sigmoid(x):
    return 1.0 / (1.0 + jnp.exp(-x))


def _pack_halves(lo, hi):
    return pltpu.pack_elementwise([lo, hi], packed_dtype=BF16)


def _unpack_half(w, index):
    return pltpu.unpack_elementwise(w, index=index, packed_dtype=BF16, unpacked_dtype=F32)


SUBLANES = 8
WORD_TILES = HALF // LANES
assert WORD_TILES == SUBLANES


def _store_token_tiles(ref, x):
    rows = x.shape[0]
    for s in range(WORD_TILES):
        ref[pl.ds(s, rows, stride=SUBLANES), :] = _pack_halves(x[:, s * LANES:(s + 1) * LANES],
                                                               x[:, HALF + s * LANES:HALF + (s + 1) * LANES])


def _load_token_tiles(ref, slot, first_row, rows):
    pieces = [ref[slot, pl.ds(first_row * SUBLANES + s, rows, stride=SUBLANES), :] for s in range(WORD_TILES)]
    return (jnp.concatenate([_unpack_half(p, 0) for p in pieces], axis=1),
            jnp.concatenate([_unpack_half(p, 1) for p in pieces], axis=1))


def _params(sem, vmem=VMEM_LIMIT):
    return pltpu.CompilerParams(dimension_semantics=sem, vmem_limit_bytes=vmem)


def _inproj_kernel(x_ref, g_ref, w_ref, b_ref, o_ref, u_ref, xn_ref, acc_ref, *, main_panels):
    j = pl.program_id(1)

    @pl.when(j == 0)
    def _():
        x = x_ref[...]
        ms = jnp.mean(x * x, axis=-1, keepdims=True)
        xn_ref[...] = (x * lax.rsqrt(ms + EPS) * g_ref[...]).astype(BF16)

    acc = _dot(xn_ref[...], w_ref[...].astype(BF16)) + b_ref[...]

    @pl.when(j < main_panels)
    def _():
        o_ref[...] = acc.astype(o_ref.dtype)

    @pl.when(j >= main_panels)
    def _():
        nk = u_ref.shape[1]
        for c in range(acc_ref.shape[0]):
            acc_ref[c] = acc[:, c * LANES:(c + 1) * LANES]
        for s in range(CHUNK):
            for c in range(acc_ref.shape[0]):
                u_ref[s, :, c * LANES:(c + 1) * LANES] = (
                    acc_ref[c, pl.ds(s, nk, stride=CHUNK), :].astype(u_ref.dtype))


def _inproj(x, g, w, b, *, main_segments, u_segment, tm=1024, tn=512):
    t, d = x.shape
    (c0, n0), (c1, n1) = main_segments
    cu, nu = u_segment
    assert all(v % tn == 0 for v in (c0, n0, c1, n1, cu, nu))
    p0, p1, pu = n0 // tn, n1 // tn, nu // tn
    main = p0 + p1
    panel = lambda i, j: (0, jnp.where(j < p0, j + c0 // tn,
                                       jnp.where(j < main, j - p0 + c1 // tn, j - main + cu // tn)))
    return pl.pallas_call(
        functools.partial(_inproj_kernel, main_panels=main),
        grid=(t // tm, main + pu),
        in_specs=[
            pl.BlockSpec((tm, d), lambda i, j: (i, 0)),
            pl.BlockSpec((1, d), lambda i, j: (0, 0)),
            pl.BlockSpec((d, tn), panel),
            pl.BlockSpec((1, tn), panel),
        ],
        out_specs=[pl.BlockSpec((tm, tn), lambda i, j: (i, jnp.minimum(j, main - 1))),
                   pl.BlockSpec((CHUNK, tm // CHUNK, tn), lambda i, j: (0, i, jnp.maximum(j - main, 0)))],
        out_shape=[jax.ShapeDtypeStruct((t, n0 + n1), BF16),
                   jax.ShapeDtypeStruct((CHUNK, t // CHUNK, nu), BF16)],
        scratch_shapes=[pltpu.VMEM((tm, d), BF16), pltpu.VMEM((tn // LANES, tm, LANES), F32)],
        compiler_params=_params(("arbitrary", "arbitrary")),
        name="inproj",
    )(x, g, w, b)


def _ssm_prep_kernel(are_ref, aim_ref, ldt_ref, btr_ref, bti_ref, cr_ref, ci_ref,
                     t_ref, wsr_ref, wsi_ref, rxr_ref, rxi_ref, a16r_ref, a16i_ref):
    lam_re = jnp.minimum(are_ref[0], -1e-4)
    lam_im = aim_ref[0]
    dt = jnp.exp(ldt_ref[0])
    lr_dt = lam_re * dt
    th = lam_im * dt
    mag = jnp.exp(lr_dt)
    ab_re = mag * jnp.cos(th)
    ab_im = mag * jnp.sin(th)
    den = lam_re * lam_re + lam_im * lam_im
    nr = ab_re - 1.0
    ni = ab_im
    coef_re = (nr * lam_re + ni * lam_im) / den
    coef_im = (ni * lam_re - nr * lam_im) / den
    btr = btr_ref[0]
    bti = bti_ref[0]
    bb_re = coef_re * btr - coef_im * bti
    bb_im = coef_re * bti + coef_im * btr

    e = lax.broadcasted_iota(I32, (CHUNK, SSM_STATE), 0).astype(F32)

    def powers(expo):
        pmag = jnp.exp(expo * lr_dt)
        return pmag * jnp.cos(expo * th), pmag * jnp.sin(expo * th)

    def rep(tab):
        return jnp.broadcast_to(tab[:, None, :], (CHUNK, SSM_GROUP, SSM_STATE)).reshape(CW, SSM_STATE)

    def tile(mat):
        return jnp.broadcast_to(mat[None, :, :], (CHUNK, SSM_GROUP, SSM_STATE)).reshape(CW, SSM_STATE)

    def scaled(mr, mi, expo):
        pr, pi = (rep(v) for v in powers(expo))
        return mr * pr - mi * pi, mr * pi + mi * pr

    ctr, cti = tile(cr_ref[0]), tile(ci_ref[0])
    btr_t, bti_t = tile(bb_re), tile(bb_im)
    mid = float(CHUNK // 2)
    r_re, r_im = scaled(ctr, cti, e - mid)
    l_re, l_im = scaled(btr_t, bti_t, mid - e)

    def split(v):
        head = v.astype(BF16)
        return head, (v - head.astype(F32)).astype(BF16)

    def dot_nt3(a, b):
        (ah, al), (bh, bl) = split(a), split(b)
        return _dot_nt(ah, bh) + _dot_nt(al, bh) + _dot_nt(ah, bl)

    tm = dot_nt3(l_re, r_re) - dot_nt3(l_im, r_im)
    srow = lax.broadcasted_iota(I32, (CW, CW), 0) // SSM_GROUP
    tcol = lax.broadcasted_iota(I32, (CW, CW), 1) // SSM_GROUP
    t_ref[0] = jnp.where(tcol >= srow, tm, 0.0).astype(BF16)

    def times_power(vr, vi, n):
        pm = jnp.exp(n * lr_dt)
        pr, pi = pm * jnp.cos(n * th), pm * jnp.sin(n * th)
        return vr * pr - vi * pi, vr * pi + vi * pr

    ws_re, ws_im = times_power(l_re, l_im, float(CHUNK - 1) - mid)
    wsr_ref[0] = ws_re.astype(BF16)
    wsi_ref[0] = ws_im.astype(BF16)
    rx_re, rx_im = times_power(r_re, r_im, mid + 1.0)
    rxr_ref[0] = rx_re.astype(BF16)
    rxi_ref[0] = (-rx_im).astype(BF16)
    m16 = jnp.exp(float(CHUNK) * lr_dt)
    a16r_ref[0] = m16 * jnp.cos(float(CHUNK) * th)
    a16i_ref[0] = m16 * jnp.sin(float(CHUNK) * th)


def _ssm_prep_block_kernel(*refs):
    def one(j, carry):
        _ssm_prep_kernel(*[r.at[pl.ds(j, 1)] for r in refs])
        return carry

    lax.fori_loop(0, GROUP_BLOCK, one, 0)


def _ssm_prep(a_re, a_im, log_dt, bt_re, bt_im, c_re, c_im):
    g = a_re.shape[0]
    gb = GROUP_BLOCK
    vec = pl.BlockSpec((gb, 1, SSM_STATE), lambda i: (i, 0, 0))
    mat = pl.BlockSpec((gb, SSM_GROUP, SSM_STATE), lambda i: (i, 0, 0))
    wide = pl.BlockSpec((gb, CW, SSM_STATE), lambda i: (i, 0, 0))
    return pl.pallas_call(
        _ssm_prep_block_kernel,
        grid=(g // gb,),
        in_specs=[vec, vec, pl.BlockSpec((gb, 1, 1), lambda i: (i, 0, 0)), mat, mat, mat, mat],
        out_specs=[pl.BlockSpec((gb, CW, CW), lambda i: (i, 0, 0)), wide, wide, wide, wide, vec, vec],
        out_shape=[
            jax.ShapeDtypeStruct((g, CW, CW), BF16),
            jax.ShapeDtypeStruct((g, CW, SSM_STATE), BF16),
            jax.ShapeDtypeStruct((g, CW, SSM_STATE), BF16),
            jax.ShapeDtypeStruct((g, CW, SSM_STATE), BF16),
            jax.ShapeDtypeStruct((g, CW, SSM_STATE), BF16),
            jax.ShapeDtypeStruct((g, 1, SSM_STATE), F32),
            jax.ShapeDtypeStruct((g, 1, SSM_STATE), F32),
        ],
        compiler_params=_params(("arbitrary",)),
        name="ssm_prep",
    )(a_re.reshape(g, 1, SSM_STATE), a_im.reshape(g, 1, SSM_STATE), log_dt.reshape(g, 1, 1),
      bt_re, bt_im, c_re, c_im)


def _gelu_tanh(x):
    c = math.sqrt(2.0 / math.pi)
    return x * (0.5 * (1.0 + jnp.tanh(c * (x + 0.044715 * (x * x * x)))))


def _ssm_kernel(u_ref, t_ref, wsr_ref, wsi_ref, rxr_ref, rxi_ref, a16r_ref, a16i_ref, d_ref,
                z_ref, sr_ref, si_ref, ug_ref, zg_ref):
    nk = u_ref.shape[1]
    for j in range(GROUP_BLOCK):
        ug_ref[j] = jnp.concatenate(
            [u_ref[s, :, j * SSM_GROUP:(j + 1) * SSM_GROUP] for s in range(CHUNK)], axis=1)
    u_ref = ug_ref
    for j in range(GROUP_BLOCK):
        u = u_ref[j]
        sr_ref[j * nk:(j + 1) * nk, :] = _dot(u, wsr_ref[j])
        si_ref[j * nk:(j + 1) * nk, :] = _dot(u, wsi_ref[j])

    ar = a16r_ref[...]
    ai = a16i_ref[...]

    def step(k, carry):
        xr, xi = carry
        rows = pl.ds(k, GROUP_BLOCK, stride=nk)
        sr = sr_ref[rows, :]
        si = si_ref[rows, :]
        sr_ref[rows, :] = xr
        si_ref[rows, :] = xi
        return (ar * xr - ai * xi + sr, ar * xi + ai * xr + si)

    zero = jnp.zeros((GROUP_BLOCK, SSM_STATE), F32)
    lax.fori_loop(0, nk, step, (zero, zero), unroll=8)

    for j in range(GROUP_BLOCK):
        u = u_ref[j]
        xr = sr_ref[j * nk:(j + 1) * nk, :].astype(BF16)
        xi = si_ref[j * nk:(j + 1) * nk, :].astype(BF16)
        y = (_dot(u, t_ref[j]) + _dot_nt(xr, rxr_ref[j]) + _dot_nt(xi, rxi_ref[j])
             + d_ref[j] * u.astype(F32))
        zg_ref[j] = _gelu_tanh(y).astype(zg_ref.dtype)

    for t in range(CHUNK):
        z_ref[t] = jnp.concatenate(
            [zg_ref[j, :, t * SSM_GROUP:(t + 1) * SSM_GROUP] for j in range(GROUP_BLOCK)], axis=1)


def _ssm(u3, tmat, wsr, wsi, rxr, rxi, a16r, a16i, d_tiled):
    _, nk, width = u3.shape
    gb = GROUP_BLOCK
    gl = gb * SSM_GROUP
    blk3 = lambda a, b: pl.BlockSpec((gb, a, b), lambda i: (i, 0, 0))
    io = pl.BlockSpec((CHUNK, nk, gl), lambda i: (0, 0, i))
    return pl.pallas_call(
        _ssm_kernel,
        grid=(width // gl,),
        in_specs=[io, blk3(CW, CW), blk3(CW, SSM_STATE), blk3(CW, SSM_STATE),
                  blk3(CW, SSM_STATE), blk3(CW, SSM_STATE),
                  pl.BlockSpec((gb, SSM_STATE), lambda i: (i, 0)),
                  pl.BlockSpec((gb, SSM_STATE), lambda i: (i, 0)),
                  blk3(1, CW)],
        out_specs=io,
        out_shape=jax.ShapeDtypeStruct(u3.shape, BF16),
        scratch_shapes=[pltpu.VMEM((gb * nk, SSM_STATE), F32), pltpu.VMEM((gb * nk, SSM_STATE), F32),
                        pltpu.VMEM((gb, nk, CW), BF16), pltpu.VMEM((gb, nk, CW), BF16)],
        compiler_params=_params(("arbitrary",)),
        name="ssm",
    )(u3, tmat, wsr, wsi, rxr, rxi, a16r, a16i, d_tiled)


def _glu_kernel(z_ref, perm_ref, wg_ref, bg_ref, wb_ref, g0_ref, o_ref):
    tm = o_ref.shape[0]
    z = _dot(perm_ref[...], z_ref[...].reshape(tm, SSM_WIDTH)).astype(BF16)
    h = _dot(z, wg_ref[...]) + bg_ref[...]
    ga = h[:, :SSM_WIDTH]
    gb = h[:, SSM_WIDTH:]
    a = (ga * _sigmoid(gb)).astype(BF16)
    y = _dot(a, wb_ref[...])
    o_ref[...] = (_sigmoid(g0_ref[...].astype(F32)) * y).astype(o_ref.dtype)


def _glu(z3, w_glu, b_glu, w_br, proj, tm=512):
    t = z3.shape[0] * z3.shape[1]
    nk = tm // CHUNK
    r = np.arange(tm)
    perm = np.zeros((tm, tm), np.float32)
    perm[r, (r % CHUNK) * nk + r // CHUNK] = 1.0
    return pl.pallas_call(
        _glu_kernel,
        grid=(t // tm,),
        in_specs=[
            pl.BlockSpec((CHUNK, nk, SSM_WIDTH), lambda i: (0, i, 0)),
            pl.BlockSpec((tm, tm), lambda i: (0, 0)),
            pl.BlockSpec((SSM_WIDTH, 2 * SSM_WIDTH), lambda i: (0, 0)),
            pl.BlockSpec((1, 2 * SSM_WIDTH), lambda i: (0, 0)),
            pl.BlockSpec((SSM_WIDTH, D_MODEL), lambda i: (0, 0)),
            pl.BlockSpec((tm, D_MODEL), lambda i: (i, COL_G0 // D_MODEL)),
        ],
        out_specs=pl.BlockSpec((tm, D_MODEL), lambda i: (i, 0)),
        out_shape=jax.ShapeDtypeStruct((t, D_MODEL), BF16),
        compiler_params=_params(("arbitrary",)),
        name="glu",
    )(z3, jnp.asarray(perm, BF16), w_glu, b_glu, w_br, proj)


def _rope_pattern():
    half = ROPE_DIM // 2
    inv_freq = (np.float32(ROPE_THETA) ** (-np.arange(half, dtype=np.float32) / np.float32(half))).astype(np.float32)
    d = np.arange(LANES) % HEAD_DIM
    rotated = d < ROPE_DIM
    pat = np.zeros((16, LANES), np.float32)
    pat[:half] = rotated[None, :] & ((d % half)[None, :] == np.arange(half)[:, None])
    pat[8] = ~rotated
    pat[9] = np.where(d < half, -1.0, 0.0)
    pat[10] = np.where((d >= half) & rotated, 1.0, 0.0)
    return inv_freq.reshape(half, 1), pat


def _attn_kernel(q_ref, k_ref, v_ref, pos_ref, freq_ref, pat_ref, sink_ref, wbr_ref, ysg_ref, g1_ref,
                 o_ref, qbuf, kbuf, vbuf, obuf, sbuf, pbuf):
    i = pl.program_id(0)
    tq = q_ref.shape[0]
    nw = tq // WINDOW
    half = ROPE_DIM // 2

    @pl.when(i == 0)
    def _():
        kbuf[:, 0:WINDOW, :] = jnp.zeros((2 * N_KV_HEADS, WINDOW, LANES), BF16)
        vbuf[:, 0:WINDOW, :] = jnp.zeros((2 * N_KV_HEADS, WINDOW, LANES), BF16)

    ang = freq_ref[...] * pos_ref[...].astype(F32)
    spread = lambda tab: lax.dot_general(tab, pat_ref[0:8, :], (((0,), (0,)), ((), ())),
                                         preferred_element_type=F32, precision=HIGHEST)
    cs = spread(jnp.cos(ang)) + pat_ref[8:9, :]
    sn = spread(jnp.sin(ang))
    c_up = sn * pat_ref[9:10, :]
    c_dn = sn * pat_ref[10:11, :]

    def rope(x):
        return (x * cs + pltpu.roll(x, LANES - half, 1) * c_up + pltpu.roll(x, half, 1) * c_dn)

    low = lax.broadcasted_iota(I32, (tq, LANES), 1) < HEAD_DIM

    def split_heads(buf, cb, x):
        xs = pltpu.roll(x, HEAD_DIM, 1)
        zero = jnp.zeros_like(x)
        buf[4 * cb + 0, WINDOW:, :] = jnp.where(low, x, zero).astype(BF16)
        buf[4 * cb + 1, WINDOW:, :] = jnp.where(low, zero, xs).astype(BF16)
        buf[4 * cb + 2, WINDOW:, :] = jnp.where(low, xs, zero).astype(BF16)
        buf[4 * cb + 3, WINDOW:, :] = jnp.where(low, zero, x).astype(BF16)

    scale = HEAD_DIM ** -0.5
    for cb in range(Q_WIDTH // LANES):
        sl = slice(cb * LANES, (cb + 1) * LANES)
        qbuf[:, sl] = (rope(q_ref[:, sl].astype(F32)) * scale).astype(BF16)
    for cb in range(KV_WIDTH // LANES):
        sl = slice(cb * LANES, (cb + 1) * LANES)
        split_heads(kbuf, cb, rope(k_ref[:, sl].astype(F32)))
        split_heads(vbuf, cb, v_ref[:, sl].astype(F32))

    kj = lax.broadcasted_iota(I32, (2 * WINDOW, WINDOW), 0)
    qi = lax.broadcasted_iota(I32, (2 * WINDOW, WINDOW), 1)
    dist = qi + WINDOW - kj
    in_band = (dist >= 0) & (dist < WINDOW)
    cur_only = kj >= WINDOW
    sinks = sink_ref[...]

    def window(w, carry):
        r0 = pl.multiple_of(w * WINDOW, WINDOW)
        rows = pl.ds(r0, 2 * WINDOW)
        not_first = (i * nw + w) > 0
        mask = in_band & (cur_only | not_first)

        for h in range(N_Q_HEADS):
            qp = qbuf[pl.ds(r0, WINDOW), (h // 2) * LANES:(h // 2 + 1) * LANES]
            sbuf[h] = _dot_nt(kbuf[2 * (h // Q_PER_KV) + h % 2, rows, :], qp)
        for h in range(N_Q_HEADS):
            s = jnp.where(mask, sbuf[h], -jnp.inf)
            sink = sinks[:, h:h + 1]
            m = jnp.maximum(jnp.max(s, axis=0, keepdims=True), sink)
            p = jnp.exp(s - m)
            denom = jnp.sum(p, axis=0, keepdims=True) + jnp.exp(sink - m)
            pbuf[h] = (p * (1.0 / denom)).astype(BF16)
        for a in range(N_Q_HEADS // 2):
            kv = (2 * a) // Q_PER_KV
            o = _dot_tn(pbuf[2 * a], vbuf[2 * kv, rows, :]) + _dot_tn(pbuf[2 * a + 1], vbuf[2 * kv + 1, rows, :])
            obuf[pl.ds(r0, WINDOW), a * LANES:(a + 1) * LANES] = o.astype(BF16)
        return carry

    lax.fori_loop(0, nw, window, 0)

    kbuf[:, 0:WINDOW, :] = kbuf[:, tq:tq + WINDOW, :]
    vbuf[:, 0:WINDOW, :] = vbuf[:, tq:tq + WINDOW, :]

    y = _dot(obuf[...], wbr_ref[...])
    o_ref[...] = (ysg_ref[...].astype(F32) + _sigmoid(g1_ref[...].astype(F32)) * y).astype(o_ref.dtype)


def _attn(proj, pos_row, sinks, w_br, ysg, tq=512):
    t = proj.shape[0]
    freq, pat = (jnp.asarray(a) for a in _rope_pattern())
    return pl.pallas_call(
        _attn_kernel,
        grid=(t // tq,),
        in_specs=[
            pl.BlockSpec((tq, Q_WIDTH), lambda i: (i, COL_Q // Q_WIDTH)),
            pl.BlockSpec((tq, KV_WIDTH), lambda i: (i, COL_K // KV_WIDTH)),
            pl.BlockSpec((tq, KV_WIDTH), lambda i: (i, COL_V // KV_WIDTH)),
            pl.BlockSpec((1, tq), lambda i: (0, i)),
            pl.BlockSpec((ROPE_DIM // 2, 1), lambda i: (0, 0)),
            pl.BlockSpec((16, LANES), lambda i: (0, 0)),
            pl.BlockSpec((1, N_Q_HEADS), lambda i: (0, 0)),
            pl.BlockSpec((Q_WIDTH, D_MODEL), lambda i: (0, 0)),
            pl.BlockSpec((tq, D_MODEL), lambda i: (i, 0)),
            pl.BlockSpec((tq, D_MODEL), lambda i: (i, COL_G1 // D_MODEL)),
        ],
        out_specs=pl.BlockSpec((tq, D_MODEL), lambda i: (i, 0)),
        out_shape=jax.ShapeDtypeStruct((t, D_MODEL), BF16),
        scratch_shapes=[
            pltpu.VMEM((tq, Q_WIDTH), BF16),
            pltpu.VMEM((2 * N_KV_HEADS, tq + WINDOW, LANES), BF16),
            pltpu.VMEM((2 * N_KV_HEADS, tq + WINDOW, LANES), BF16),
            pltpu.VMEM((tq, Q_WIDTH), BF16),
            pltpu.VMEM((N_Q_HEADS, 2 * WINDOW, WINDOW), F32),
            pltpu.VMEM((N_Q_HEADS, 2 * WINDOW, WINDOW), BF16),
        ],
        compiler_params=_params(("arbitrary",)),
        name="attn",
    )(proj, proj, proj, pos_row, freq, pat, sinks, w_br, ysg, proj)


def _oproj_kernel(x_ref, mix_ref, wo_ref, g_ref, wrh_ref, wrl_ref, br_ref, h_ref, hp_ref, lt_ref):
    h = x_ref[...] + _dot(mix_ref[...], wo_ref[...])
    h_ref[...] = h
    ms = jnp.mean(h * h, axis=-1, keepdims=True)
    hn = h * lax.rsqrt(ms + EPS) * g_ref[...]
    _store_token_tiles(hp_ref, hn)
    hn_hi = hn.astype(BF16)
    hn_lo = (hn - hn_hi.astype(F32)).astype(BF16)
    logits = (_dot(hn_hi, wrh_ref[...]) + _dot(hn_lo, wrh_ref[...]) + _dot(hn_hi, wrl_ref[...])
              + br_ref[...])
    lt_ref[...] = logits.T


def _oproj(x, mixed, w_o, g, w_router, b_router, tm=512):
    t = x.shape[0]
    w_router_hi = w_router.astype(BF16)
    w_router_hi_rest = (w_router - w_router_hi.astype(F32)).astype(BF16)
    return pl.pallas_call(
        _oproj_kernel,
        grid=(t // tm,),
        in_specs=[
            pl.BlockSpec((tm, D_MODEL), lambda i: (i, 0)),
            pl.BlockSpec((tm, D_MODEL), lambda i: (i, 0)),
            pl.BlockSpec((D_MODEL, D_MODEL), lambda i: (0, 0)),
            pl.BlockSpec((1, D_MODEL), lambda i: (0, 0)),
            pl.BlockSpec((D_MODEL, LANES), lambda i: (0, 0)),
            pl.BlockSpec((D_MODEL, LANES), lambda i: (0, 0)),
            pl.BlockSpec((1, LANES), lambda i: (0, 0)),
        ],
        out_specs=[
            pl.BlockSpec((tm, D_MODEL), lambda i: (i, 0)),
            pl.BlockSpec((tm * SUBLANES, LANES), lambda i: (i, 0)),
            pl.BlockSpec((LANES, tm), lambda i: (0, i)),
        ],
        out_shape=[
            jax.ShapeDtypeStruct((t, D_MODEL), F32),
            jax.ShapeDtypeStruct((t * SUBLANES, LANES), jnp.uint32),
            jax.ShapeDtypeStruct((LANES, t), F32),
        ],
        compiler_params=_params(("arbitrary",)),
        name="oproj",
    )(x, mixed, w_o, g, w_router_hi, w_router_hi_rest, b_router)


ROUTE_CHUNK = 256


def _route_kernel(lt_ref, dest_ref, wts_ref, bexp_ref, nvalid_ref, eid_ref, rank_ref):
    t = lt_ref.shape[1]
    nc = t // ROUTE_CHUNK
    r8 = lax.broadcasted_iota(I32, (N_GROUPS, ROUTE_CHUNK), 0)
    r64 = lax.broadcasted_iota(I32, (N_EXPERTS, ROUTE_CHUNK), 0)

    def pick(c, carry):
        cols = pl.ds(pl.multiple_of(c * ROUTE_CHUNK, ROUTE_CHUNK), ROUTE_CHUNK)
        lg = lt_ref[0:N_GROUPS, cols]
        m = jnp.max(lg, axis=0, keepdims=True)
        ssum = jnp.sum(jnp.exp(lg - m), axis=0, keepdims=True)
        p_grp = 1.0 / ssum
        grp = jnp.min(jnp.where(lg == m, r8, N_GROUPS), axis=0, keepdims=True)
        le = lt_ref[N_GROUPS:N_GROUPS + N_EXPERTS, cols]
        leg = jnp.where((r64 // EXPERTS_PER_GROUP) == grp, le, -jnp.inf)
        m1 = jnp.max(leg, axis=0, keepdims=True)
        i1 = jnp.min(jnp.where(leg == m1, r64, N_EXPERTS), axis=0, keepdims=True)
        leg2 = jnp.where(r64 == i1, -jnp.inf, leg)
        m2 = jnp.max(leg2, axis=0, keepdims=True)
        i2 = jnp.min(jnp.where(leg2 == m2, r64, N_EXPERTS), axis=0, keepdims=True)
        ex = jnp.exp(m2 - m1)
        eid_ref[0:1, cols] = i1
        eid_ref[1:2, cols] = i2
        wts_ref[0:1, cols] = p_grp / (1.0 + ex)
        wts_ref[1:2, cols] = p_grp * ex / (1.0 + ex)
        return carry

    lax.fori_loop(0, nc, pick, 0)

    a_row = lax.broadcasted_iota(I32, (ROUTE_CHUNK, ROUTE_CHUNK), 0)
    a_col = lax.broadcasted_iota(I32, (ROUTE_CHUNK, ROUTE_CHUNK), 1)
    before = (a_row < a_col).astype(BF16)

    def count(n, carry):
        j = n // nc
        c = n - j * nc
        cols = pl.ds(pl.multiple_of(c * ROUTE_CHUNK, ROUTE_CHUNK), ROUTE_CHUNK)
        oh = r64 == eid_ref[pl.ds(j, 1), cols]
        ohf = oh.astype(F32)
        pref = _dot(ohf.astype(BF16), before) + carry
        rank_ref[pl.ds(j, 1), cols] = jnp.sum(jnp.where(oh, pref, 0.0), axis=0, keepdims=True)
        return carry + jnp.sum(ohf, axis=1, keepdims=True)

    counts = lax.fori_loop(0, TOP_K * nc, count, jnp.zeros((N_EXPERTS, 1), F32))

    padded = jnp.floor((counts + (MOE_BLOCK - 1)) * (1.0 / MOE_BLOCK)) * MOE_BLOCK
    e_row = lax.broadcasted_iota(I32, (N_EXPERTS, N_EXPERTS), 0)
    e_col = lax.broadcasted_iota(I32, (N_EXPERTS, N_EXPERTS), 1)
    incl = (e_col <= e_row).astype(F32)
    pad_end = _dot(incl, jnp.broadcast_to(padded, (N_EXPERTS, LANES)), precision=HIGHEST)[:, 0:1]
    pad_start = pad_end - padded

    def place(n, carry):
        j = n // nc
        c = n - j * nc
        cols = pl.ds(pl.multiple_of(c * ROUTE_CHUNK, ROUTE_CHUNK), ROUTE_CHUNK)
        oh = r64 == eid_ref[pl.ds(j, 1), cols]
        start = jnp.sum(jnp.where(oh, pad_start, 0.0), axis=0, keepdims=True)
        dest_ref[pl.ds(j, 1), cols] = (start + rank_ref[pl.ds(j, 1), cols]).astype(I32)
        return carry

    lax.fori_loop(0, TOP_K * nc, place, 0)

    b0 = (lax.broadcasted_iota(I32, (N_EXPERTS, bexp_ref.shape[1]), 1) * MOE_BLOCK).astype(F32)
    n_done = jnp.sum((pad_end <= b0).astype(F32), axis=0, keepdims=True)
    bexp_ref[...] = jnp.minimum(n_done, float(N_EXPERTS - 1)).astype(I32)
    live = jnp.minimum(pad_start + counts, b0 + MOE_BLOCK) - jnp.maximum(pad_start, b0)
    nvalid_ref[...] = jnp.sum(jnp.maximum(live, 0.0), axis=0, keepdims=True).astype(I32)


def _route(logits_t, n_blocks_pad):
    t = logits_t.shape[1]
    return pl.pallas_call(
        _route_kernel,
        out_shape=[
            jax.ShapeDtypeStruct((TOP_K, t), I32),
            jax.ShapeDtypeStruct((TOP_K, t), F32),
            jax.ShapeDtypeStruct((1, n_blocks_pad), I32),
            jax.ShapeDtypeStruct((1, n_blocks_pad), I32),
        ],
        scratch_shapes=[pltpu.VMEM((TOP_K, t), I32), pltpu.VMEM((TOP_K, t), F32)],
        compiler_params=pltpu.CompilerParams(vmem_limit_bytes=VMEM_LIMIT),
        name="route",
    )(logits_t)


def _invmap_kernel(dest_ref, rt_ref, zero_ref, sem):
    t = dest_ref.shape[0] // TOP_K
    zero_ref[...] = jnp.zeros(zero_ref.shape, zero_ref.dtype)
    clear = pltpu.make_async_copy(zero_ref, rt_ref, sem)
    clear.start()
    clear.wait()

    def put(tok, c):
        rt_ref[dest_ref[tok]] = tok
        rt_ref[dest_ref[t + tok]] = tok
        return c

    lax.fori_loop(0, t, put, 0, unroll=8)


def _invmap(dest_flat, n_rows):
    return pl.pallas_call(
        _invmap_kernel,
        in_specs=[pl.BlockSpec(memory_space=pltpu.SMEM)],
        out_specs=pl.BlockSpec(memory_space=pltpu.SMEM),
        out_shape=jax.ShapeDtypeStruct((n_rows,), I32),
        scratch_shapes=[pltpu.VMEM((n_rows,), I32), pltpu.SemaphoreType.DMA(())],
        name="invmap",
    )(dest_flat)


GATHER_SLOTS = 4
WEIGHT_SLOTS = 3


def _row_copy(src_hbm, src_row, dst_buf, slot, dst_row, sem):
    return pltpu.make_async_copy(src_hbm.at[pl.ds(pl.multiple_of(src_row * SUBLANES, SUBLANES), SUBLANES), :],
                                 dst_buf.at[slot, pl.ds(dst_row * SUBLANES, SUBLANES), :], sem.at[slot])


def _rows_copy(src_hbm, dst_buf, slot, rows, sem):
    n = rows * SUBLANES
    return pltpu.make_async_copy(src_hbm.at[pl.ds(0, n), :], dst_buf.at[slot, pl.ds(0, n), :], sem.at[slot])


def _expert_kernel(bexp_ref, nvalid_ref, rt_ref, hp_hbm, wg_hbm, wu_hbm, wd_hbm, ys_ref,
                   xbuf, wf_g, wf_u, wf_d, ord_ref, xsem, wsem):
    b = pl.program_id(0)
    nb = pl.num_programs(0) - 1

    def expert_of(blk):
        return bexp_ref[jnp.minimum(blk, nb - 1)]

    def next_owner(blk, e):
        return lax.while_loop(lambda j: (j < nb) & (expert_of(j) == e), lambda j: j + 1, blk)

    def weight_copies(e, slot):
        return (pltpu.make_async_copy(wg_hbm.at[e], wf_g.at[slot], wsem.at[slot, 0]),
                pltpu.make_async_copy(wu_hbm.at[e], wf_u.at[slot], wsem.at[slot, 1]),
                pltpu.make_async_copy(wd_hbm.at[e], wf_d.at[slot], wsem.at[slot, 2]))

    def start_weights(e, slot):
        for c in weight_copies(e, slot):
            c.start(priority=1)

    @pl.when(b == 0)
    def _():
        ord_ref[0] = 0
        e0 = bexp_ref[0]
        start_weights(e0, 0)
        n1 = next_owner(1, e0)

        @pl.when(n1 < nb)
        def _():
            start_weights(expert_of(n1), 1)

    def gathered_rows(blk):
        n = nvalid_ref[jnp.clip(blk, 0, nb - 1)]
        live_rows = ((n + (SUBLANES - 1)) >> 3) << 3
        return jnp.where(blk >= nb, 0, jnp.where((blk < GATHER_SLOTS) & (n > 0), MOE_BLOCK, live_rows))

    def gather(blk):
        base = blk * MOE_BLOCK
        slot = blk % GATHER_SLOTS

        def body(i, c):
            for s in range(SUBLANES):
                r = i * SUBLANES + s
                _row_copy(hp_hbm, rt_ref[base + r], xbuf, slot, r, xsem).start()
            return c

        lax.fori_loop(0, gathered_rows(blk) // SUBLANES, body, 0)

    @pl.when(b == 0)
    def _():
        for blk in range(GATHER_SLOTS - 2):
            gather(blk)

    gather(b + GATHER_SLOTS - 2)

    @pl.when(b > 0)
    def _():
        blk = b - 1
        e = bexp_ref[blk]
        first = (blk == 0) | (e != bexp_ref[jnp.maximum(blk - 1, 0)])

        @pl.when(first)
        def _():
            n = ord_ref[0]
            ord_ref[0] = n + 1
            for c in weight_copies(e, n % WEIGHT_SLOTS):
                c.wait()
            n1 = next_owner(blk + 1, e)
            n2 = next_owner(n1 + 1, expert_of(n1))

            @pl.when((n1 < nb) & (n2 < nb))
            def _():
                start_weights(expert_of(n2), (n + 2) % WEIGHT_SLOTS)

        slot = blk % GATHER_SLOTS
        rows = gathered_rows(blk)

        @pl.when(rows > 0)
        def _():
            _rows_copy(hp_hbm, xbuf, slot, rows, xsem).wait()
            lo, hi = (v.astype(BF16) for v in _load_token_tiles(xbuf, slot, 0, MOE_BLOCK))
            ws = (ord_ref[0] - 1) % WEIGHT_SLOTS
            g = _dot(lo, wf_g[ws, :HALF, :].astype(BF16)) + _dot(hi, wf_g[ws, HALF:, :].astype(BF16))
            u = _dot(lo, wf_u[ws, :HALF, :].astype(BF16)) + _dot(hi, wf_u[ws, HALF:, :].astype(BF16))
            h = (g * _sigmoid(g) * u).astype(BF16)
            y = _dot(h, wf_d[ws].astype(BF16))
            _store_token_tiles(ys_ref, y)

        @pl.when(rows == 0)
        def _():
            _store_token_tiles(ys_ref, jnp.zeros((MOE_BLOCK, D_MODEL), F32))


def _experts(bexp, nvalid, row_tok, hn_packed, w_gate, w_up, w_down, n_blocks):
    grid_spec = pltpu.PrefetchScalarGridSpec(
        num_scalar_prefetch=3,
        grid=(n_blocks + 1,),
        in_specs=[pl.BlockSpec(memory_space=pl.ANY)] * 4,
        out_specs=pl.BlockSpec((MOE_BLOCK * SUBLANES, LANES), lambda b, be, nv, rt: (jnp.maximum(b - 1, 0), 0)),
        scratch_shapes=[
            pltpu.VMEM((GATHER_SLOTS, MOE_BLOCK * SUBLANES, LANES), jnp.uint32),
            pltpu.VMEM((WEIGHT_SLOTS, D_MODEL, D_EXPERT), F32), pltpu.VMEM((WEIGHT_SLOTS, D_MODEL, D_EXPERT), F32),
            pltpu.VMEM((WEIGHT_SLOTS, D_EXPERT, D_MODEL), F32),
            pltpu.SMEM((1,), I32),
            pltpu.SemaphoreType.DMA((GATHER_SLOTS,)), pltpu.SemaphoreType.DMA((WEIGHT_SLOTS, 3)),
        ],
    )
    return pl.pallas_call(
        _expert_kernel,
        grid_spec=grid_spec,
        out_shape=jax.ShapeDtypeStruct((n_blocks * MOE_BLOCK * SUBLANES, LANES), jnp.uint32),
        compiler_params=_params(("arbitrary",)),
        name="experts",
    )(bexp, nvalid, row_tok, hn_packed, w_gate, w_up, w_down)


def _combine_kernel(dest_ref, ys_hbm, h_ref, w_ref, g_ref, o_ref, ybuf, sem):
    i = pl.program_id(0)
    n = pl.num_programs(0)
    tq = h_ref.shape[0]
    t = n * tq

    def start_gather(blk, slot):
        base = blk * tq

        def body(j, c):
            for s in range(SUBLANES):
                r = j * SUBLANES + s
                _row_copy(ys_hbm, dest_ref[base + r], ybuf, slot, r, sem).start(priority=0)
                _row_copy(ys_hbm, dest_ref[t + base + r], ybuf, slot, tq + r, sem).start(priority=1)
            return c

        lax.fori_loop(0, tq // SUBLANES, body, 0)

    @pl.when(i == 0)
    def _():
        start_gather(0, 0)

    @pl.when(i + 1 < n)
    def _():
        start_gather(i + 1, (i + 1) % 2)

    slot = i % 2
    _rows_copy(ys_hbm, ybuf, slot, TOP_K * tq, sem).wait()
    y0_lo, y0_hi = _load_token_tiles(ybuf, slot, 0, tq)
    y1_lo, y1_hi = _load_token_tiles(ybuf, slot, tq, tq)
    w = w_ref[...]
    w0 = w[:, 0:1]
    w1 = w[:, 1:2]
    h = h_ref[...]
    lo = h[:, :HALF] + w0 * y0_lo + w1 * y1_lo
    hi = h[:, HALF:] + w0 * y0_hi + w1 * y1_hi
    ms = (jnp.sum(lo * lo, axis=-1, keepdims=True) + jnp.sum(hi * hi, axis=-1, keepdims=True)) * (1.0 / D_MODEL)
    inv = lax.rsqrt(ms + EPS)
    g = g_ref[...]
    o_ref[:, :HALF] = lo * inv * g[:, :HALF]
    o_ref[:, HALF:] = hi * inv * g[:, HALF:]


def _combine(dest_flat, ys_packed, h1, wts_tok, g_final, tq=256):
    t = h1.shape[0]
    grid_spec = pltpu.PrefetchScalarGridSpec(
        num_scalar_prefetch=1,
        grid=(t // tq,),
        in_specs=[
            pl.BlockSpec(memory_space=pl.ANY),
            pl.BlockSpec((tq, D_MODEL), lambda i, d: (i, 0)),
            pl.BlockSpec((tq, TOP_K), lambda i, d: (i, 0)),
            pl.BlockSpec((1, D_MODEL), lambda i, d: (0, 0)),
        ],
        out_specs=pl.BlockSpec((tq, D_MODEL), lambda i, d: (i, 0)),
        scratch_shapes=[pltpu.VMEM((2, TOP_K * tq * SUBLANES, LANES), jnp.uint32),
                        pltpu.SemaphoreType.DMA((2,))],
    )
    return pl.pallas_call(
        _combine_kernel,
        grid_spec=grid_spec,
        out_shape=jax.ShapeDtypeStruct((t, D_MODEL), F32),
        compiler_params=_params(("arbitrary",)),
        name="combine",
    )(dest_flat, ys_packed, h1, wts_tok, g_final)


def _mixers(h, positions, norm_g, w_in, b_in, a_re, a_im, log_dt, b_re, b_im, c_re, c_im, d_skip,
            w_glu, b_glu, w_br_ssm, sinks, w_br_attn):
    t = h.shape[0]
    ngate = 2 * D_MODEL
    g = norm_g.reshape(1, D_MODEL)
    b_row = b_in.reshape(1, IN_WIDTH)
    proj, u3 = _inproj(h, g, w_in, b_row, u_segment=(0, SSM_WIDTH),
                       main_segments=((IN_WIDTH - ngate, ngate), (SSM_WIDTH, IN_WIDTH - ngate - SSM_WIDTH)))

    tmat, wsr, wsi, rxr, rxi, a16r, a16i = _ssm_prep(
        a_re, a_im, log_dt, b_re.transpose(0, 2, 1), b_im.transpose(0, 2, 1), c_re, c_im)
    d_tiled = jnp.tile(d_skip, (1, CHUNK)).reshape(SSM_GROUPS, 1, CW)
    z3 = _ssm(u3, tmat, wsr, wsi, rxr, rxi, a16r.reshape(SSM_GROUPS, SSM_STATE),
              a16i.reshape(SSM_GROUPS, SSM_STATE), d_tiled)

    ysg = _glu(z3, w_glu.astype(BF16), b_glu.reshape(1, -1), w_br_ssm.astype(BF16), proj)
    return _attn(proj, positions.reshape(1, t), sinks.reshape(1, N_Q_HEADS), w_br_attn.astype(BF16), ysg)


def _moe_tail(h, mixed, w_o, norm_ffn_g, w_rg, b_rg, w_re, b_re, w_gate, w_up, w_down, norm_final_g):
    t = h.shape[0]
    n_route = N_GROUPS + N_EXPERTS
    w_router = jnp.concatenate([w_rg, w_re, jnp.zeros((D_MODEL, LANES - n_route), F32)], axis=1)
    b_router = jnp.concatenate([b_rg, b_re, jnp.zeros((LANES - n_route,), F32)]).reshape(1, LANES)
    h1, hn_packed, logits_t = _oproj(h, mixed, w_o.astype(BF16), norm_ffn_g.reshape(1, D_MODEL),
                                     w_router, b_router)
    n_assign = t * TOP_K
    n_blocks = -(-(n_assign + N_EXPERTS * (MOE_BLOCK - 1)) // MOE_BLOCK)
    n_blocks_pad = -(-n_blocks // LANES) * LANES
    dest, wts, bexp, nvalid = _route(logits_t, n_blocks_pad)
    dest_flat = dest.reshape(n_assign)
    row_tok = _invmap(dest_flat, n_blocks * MOE_BLOCK)
    ys_packed = _experts(bexp[0, :n_blocks], nvalid[0, :n_blocks], row_tok, hn_packed, w_gate, w_up, w_down,
                         n_blocks)
    return _combine(dest_flat, ys_packed, h1, wts.T, norm_final_g.reshape(1, D_MODEL))


def kernel(x, positions, norm_mix_g, w_in, b_in, ssm_a_re, ssm_a_im, ssm_log_dt, ssm_b_re, ssm_b_im, ssm_c_re, ssm_c_im, ssm_d, w_glu, b_glu, w_br_ssm, attn_sinks, w_br_attn, w_o, norm_ffn_g, w_router_group, b_router_group, w_router_expert, b_router_expert, w_exp_gate, w_exp_up, w_exp_down, norm_final_g):
    bsz, seq, d = x.shape
    assert bsz == 1 and d == D_MODEL and norm_mix_g.shape[0] == 1
    h = x.reshape(seq, d)
    mixed = _mixers(h, positions, norm_mix_g[0], w_in[0], b_in[0], ssm_a_re[0], ssm_a_im[0], ssm_log_dt[0],
                    ssm_b_re[0], ssm_b_im[0], ssm_c_re[0], ssm_c_im[0], ssm_d[0], w_glu[0], b_glu[0],
                    w_br_ssm[0], attn_sinks[0], w_br_attn[0])
    out = _moe_tail(h, mixed, w_o[0], norm_ffn_g[0], w_router_group[0], b_router_group[0],
                    w_router_expert[0], b_router_expert[0], w_exp_gate[0], w_exp_up[0], w_exp_down[0],
                    norm_final_g)
    return out.reshape(bsz, seq, d)
```

```python
import functools
import math

import numpy as np
import jax
import jax.numpy as jnp
from jax import lax
from jax.experimental import pallas as pl
from jax.experimental.pallas import tpu as pltpu

F32 = jnp.float32
BF16 = jnp.bfloat16
I32 = jnp.int32

D_MODEL = 2048
SSM_WIDTH = 1024
SSM_GROUP = 16
SSM_GROUPS = 64
SSM_STATE = 64
HEAD_DIM = 64
N_Q_HEADS = 16
N_KV_HEADS = 4
Q_PER_KV = 4
WINDOW = 128
ROPE_DIM = 16
ROPE_THETA = 500000.0
Q_WIDTH = 1024
KV_WIDTH = 256
IN_WIDTH = SSM_WIDTH + Q_WIDTH + 2 * KV_WIDTH + 2 * D_MODEL
N_GROUPS = 8
EXPERTS_PER_GROUP = 8
N_EXPERTS = 64
TOP_K = 2
D_EXPERT = 512
MOE_BLOCK = 128
EPS = 1e-6

CHUNK = 16
CW = CHUNK * SSM_GROUP
GROUP_BLOCK = 8
HALF = D_MODEL // 2
LANES = 128
VMEM_LIMIT = 56 * 1024 * 1024

COL_G0, COL_G1, COL_Q, COL_K, COL_V = 0, 2048, 4096, 5120, 5376

HIGHEST = lax.Precision.HIGHEST


def _dot(a, b, precision=None):
    return jnp.dot(a, b, preferred_element_type=F32, precision=precision)


def _dot_nt(a, b, precision=None):
    return lax.dot_general(a, b, (((1,), (1,)), ((), ())), preferred_element_type=F32,
                           precision=precision)


def _dot_tn(a, b):
    return lax.dot_general(a, b, (((0,), (0,)), ((), ())), preferred_element_type=F32)


def _sigmoid(x):
    return 1.0 / (1.0 + jnp.exp(-x))


def _pack_halves(lo, hi):
    return pltpu.pack_elementwise([lo, hi], packed_dtype=BF16)


def _unpack_half(w, index):
    return pltpu.unpack_elementwise(w, index=index, packed_dtype=BF16, unpacked_dtype=F32)


SUBLANES = 8
WORD_TILES = HALF // LANES
assert WORD_TILES == SUBLANES


def _store_token_tiles(ref, x):
    rows = x.shape[0]
    for s in range(WORD_TILES):
        ref[pl.ds(s, rows, stride=SUBLANES), :] = _pack_halves(x[:, s * LANES:(s + 1) * LANES],
                                                               x[:, HALF + s * LANES:HALF + (s + 1) * LANES])


def _load_token_tiles(ref, slot, first_row, rows):
    pieces = [ref[slot, pl.ds(first_row * SUBLANES + s, rows, stride=SUBLANES), :] for s in range(WORD_TILES)]
    return (jnp.concatenate([_unpack_half(p, 0) for p in pieces], axis=1),
            jnp.concatenate([_unpack_half(p, 1) for p in pieces], axis=1))


def _params(sem, vmem=VMEM_LIMIT):
    return pltpu.CompilerParams(dimension_semantics=sem, vmem_limit_bytes=vmem)


def _inproj_kernel(x_ref, g_ref, w_ref, b_ref, o_ref, u_ref, xn_ref, acc_ref, *, main_panels):
    j = pl.program_id(1)

    @pl.when(j == 0)
    def _():
        x = x_ref[...]
        ms = jnp.mean(x * x, axis=-1, keepdims=True)
        xn_ref[...] = (x * lax.rsqrt(ms + EPS) * g_ref[...]).astype(BF16)

    acc = _dot(xn_ref[...], w_ref[...].astype(BF16)) + b_ref[...]

    @pl.when(j < main_panels)
    def _():
        o_ref[...] = acc.astype(o_ref.dtype)

    @pl.when(j >= main_panels)
    def _():
        nk = u_ref.shape[1]
        for c in range(acc_ref.shape[0]):
            acc_ref[c] = acc[:, c * LANES:(c + 1) * LANES]
        for s in range(CHUNK):
            for c in range(acc_ref.shape[0]):
                u_ref[s, :, c * LANES:(c + 1) * LANES] = (
                    acc_ref[c, pl.ds(s, nk, stride=CHUNK), :].astype(u_ref.dtype))


def _inproj(x, g, w, b, *, main_segments, u_segment, tm=1024, tn=512):
    t, d = x.shape
    (c0, n0), (c1, n1) = main_segments
    cu, nu = u_segment
    assert all(v % tn == 0 for v in (c0, n0, c1, n1, cu, nu))
    p0, p1, pu = n0 // tn, n1 // tn, nu // tn
    main = p0 + p1
    panel = lambda i, j: (0, jnp.where(j < p0, j + c0 // tn,
                                       jnp.where(j < main, j - p0 + c1 // tn, j - main + cu // tn)))
    return pl.pallas_call(
        functools.partial(_inproj_kernel, main_panels=main),
        grid=(t // tm, main + pu),
        in_specs=[
            pl.BlockSpec((tm, d), lambda i, j: (i, 0)),
            pl.BlockSpec((1, d), lambda i, j: (0, 0)),
            pl.BlockSpec((d, tn), panel),
            pl.BlockSpec((1, tn), panel),
        ],
        out_specs=[pl.BlockSpec((tm, tn), lambda i, j: (i, jnp.minimum(j, main - 1))),
                   pl.BlockSpec((CHUNK, tm // CHUNK, tn), lambda i, j: (0, i, jnp.maximum(j - main, 0)))],
        out_shape=[jax.ShapeDtypeStruct((t, n0 + n1), BF16),
                   jax.ShapeDtypeStruct((CHUNK, t // CHUNK, nu), BF16)],
        scratch_shapes=[pltpu.VMEM((tm, d), BF16), pltpu.VMEM((tn // LANES, tm, LANES), F32)],
        compiler_params=_params(("arbitrary", "arbitrary")),
        name="inproj",
    )(x, g, w, b)


def _ssm_prep_kernel(are_ref, aim_ref, ldt_ref, btr_ref, bti_ref, cr_ref, ci_ref,
                     t_ref, wsr_ref, wsi_ref, rxr_ref, rxi_ref, a16r_ref, a16i_ref):
    lam_re = jnp.minimum(are_ref[0], -1e-4)
    lam_im = aim_ref[0]
    dt = jnp.exp(ldt_ref[0])
    lr_dt = lam_re * dt
    th = lam_im * dt
    mag = jnp.exp(lr_dt)
    ab_re = mag * jnp.cos(th)
    ab_im = mag * jnp.sin(th)
    den = lam_re * lam_re + lam_im * lam_im
    nr = ab_re - 1.0
    ni = ab_im
    coef_re = (nr * lam_re + ni * lam_im) / den
    coef_im = (ni * lam_re - nr * lam_im) / den
    btr = btr_ref[0]
    bti = bti_ref[0]
    bb_re = coef_re * btr - coef_im * bti
    bb_im = coef_re * bti + coef_im * btr

    e = lax.broadcasted_iota(I32, (CHUNK, SSM_STATE), 0).astype(F32)

    def powers(expo):
        pmag = jnp.exp(expo * lr_dt)
        return pmag * jnp.cos(expo * th), pmag * jnp.sin(expo * th)

    def rep(tab):
        return jnp.broadcast_to(tab[:, None, :], (CHUNK, SSM_GROUP, SSM_STATE)).reshape(CW, SSM_STATE)

    def tile(mat):
        return jnp.broadcast_to(mat[None, :, :], (CHUNK, SSM_GROUP, SSM_STATE)).reshape(CW, SSM_STATE)

    def scaled(mr, mi, expo):
        pr, pi = (rep(v) for v in powers(expo))
        return mr * pr - mi * pi, mr * pi + mi * pr

    ctr, cti = tile(cr_ref[0]), tile(ci_ref[0])
    btr_t, bti_t = tile(bb_re), tile(bb_im)
    mid = float(CHUNK // 2)
    r_re, r_im = scaled(ctr, cti, e - mid)
    l_re, l_im = scaled(btr_t, bti_t, mid - e)

    def split(v):
        head = v.astype(BF16)
        return head, (v - head.astype(F32)).astype(BF16)

    def dot_nt3(a, b):
        (ah, al), (bh, bl) = split(a), split(b)
        return _dot_nt(ah, bh) + _dot_nt(al, bh) + _dot_nt(ah, bl)

    tm = dot_nt3(l_re, r_re) - dot_nt3(l_im, r_im)
    srow = lax.broadcasted_iota(I32, (CW, CW), 0) // SSM_GROUP
    tcol = lax.broadcasted_iota(I32, (CW, CW), 1) // SSM_GROUP
    t_ref[0] = jnp.where(tcol >= srow, tm, 0.0).astype(BF16)

    def times_power(vr, vi, n):
        pm = jnp.exp(n * lr_dt)
        pr, pi = pm * jnp.cos(n * th), pm * jnp.sin(n * th)
        return vr * pr - vi * pi, vr * pi + vi * pr

    ws_re, ws_im = times_power(l_re, l_im, float(CHUNK - 1) - mid)
    wsr_ref[0] = ws_re.astype(BF16)
    wsi_ref[0] = ws_im.astype(BF16)
    rx_re, rx_im = times_power(r_re, r_im, mid + 1.0)
    rxr_ref[0] = rx_re.astype(BF16)
    rxi_ref[0] = (-rx_im).astype(BF16)
    m16 = jnp.exp(float(CHUNK) * lr_dt)
    a16r_ref[0] = m16 * jnp.cos(float(CHUNK) * th)
    a16i_ref[0] = m16 * jnp.sin(float(CHUNK) * th)


def _ssm_prep_block_kernel(*refs):
    def one(j, carry):
        _ssm_prep_kernel(*[r.at[pl.ds(j, 1)] for r in refs])
        return carry

    lax.fori_loop(0, GROUP_BLOCK, one, 0)


def _ssm_prep(a_re, a_im, log_dt, bt_re, bt_im, c_re, c_im):
    g = a_re.shape[0]
    gb = GROUP_BLOCK
    vec = pl.BlockSpec((gb, 1, SSM_STATE), lambda i: (i, 0, 0))
    mat = pl.BlockSpec((gb, SSM_GROUP, SSM_STATE), lambda i: (i, 0, 0))
    wide = pl.BlockSpec((gb, CW, SSM_STATE), lambda i: (i, 0, 0))
    return pl.pallas_call(
        _ssm_prep_block_kernel,
        grid=(g // gb,),
        in_specs=[vec, vec, pl.BlockSpec((gb, 1, 1), lambda i: (i, 0, 0)), mat, mat, mat, mat],
        out_specs=[pl.BlockSpec((gb, CW, CW), lambda i: (i, 0, 0)), wide, wide, wide, wide, vec, vec],
        out_shape=[
            jax.ShapeDtypeStruct((g, CW, CW), BF16),
            jax.ShapeDtypeStruct((g, CW, SSM_STATE), BF16),
            jax.ShapeDtypeStruct((g, CW, SSM_STATE), BF16),
            jax.ShapeDtypeStruct((g, CW, SSM_STATE), BF16),
            jax.ShapeDtypeStruct((g, CW, SSM_STATE), BF16),
            jax.ShapeDtypeStruct((g, 1, SSM_STATE), F32),
            jax.ShapeDtypeStruct((g, 1, SSM_STATE), F32),
        ],
        compiler_params=_params(("arbitrary",)),
        name="ssm_prep",
    )(a_re.reshape(g, 1, SSM_STATE), a_im.reshape(g, 1, SSM_STATE), log_dt.reshape(g, 1, 1),
      bt_re, bt_im, c_re, c_im)


def _gelu_tanh(x):
    c = math.sqrt(2.0 / math.pi)
    return x * (0.5 * (1.0 + jnp.tanh(c * (x + 0.044715 * (x * x * x)))))


def _ssm_kernel(u_ref, t_ref, wsr_ref, wsi_ref, rxr_ref, rxi_ref, a16r_ref, a16i_ref, d_ref,
                z_ref, sr_ref, si_ref, ug_ref, zg_ref):
    nk = u_ref.shape[1]
    for j in range(GROUP_BLOCK):
        ug_ref[j] = jnp.concatenate(
            [u_ref[s, :, j * SSM_GROUP:(j + 1) * SSM_GROUP] for s in range(CHUNK)], axis=1)
    u_ref = ug_ref
    for j in range(GROUP_BLOCK):
        u = u_ref[j]
        sr_ref[j * nk:(j + 1) * nk, :] = _dot(u, wsr_ref[j])
        si_ref[j * nk:(j + 1) * nk, :] = _dot(u, wsi_ref[j])

    ar = a16r_ref[...]
    ai = a16i_ref[...]

    def step(k, carry):
        xr, xi = carry
        rows = pl.ds(k, GROUP_BLOCK, stride=nk)
        sr = sr_ref[rows, :]
        si = si_ref[rows, :]
        sr_ref[rows, :] = xr
        si_ref[rows, :] = xi
        return (ar * xr - ai * xi + sr, ar * xi + ai * xr + si)

    zero = jnp.zeros((GROUP_BLOCK, SSM_STATE), F32)
    lax.fori_loop(0, nk, step, (zero, zero), unroll=8)

    for j in range(GROUP_BLOCK):
        u = u_ref[j]
        xr = sr_ref[j * nk:(j + 1) * nk, :].astype(BF16)
        xi = si_ref[j * nk:(j + 1) * nk, :].astype(BF16)
        y = (_dot(u, t_ref[j]) + _dot_nt(xr, rxr_ref[j]) + _dot_nt(xi, rxi_ref[j])
             + d_ref[j] * u.astype(F32))
        zg_ref[j] = _gelu_tanh(y).astype(zg_ref.dtype)

    for t in range(CHUNK):
        z_ref[t] = jnp.concatenate(
            [zg_ref[j, :, t * SSM_GROUP:(t + 1) * SSM_GROUP] for j in range(GROUP_BLOCK)], axis=1)


def _ssm(u3, tmat, wsr, wsi, rxr, rxi, a16r, a16i, d_tiled):
    _, nk, width = u3.shape
    gb = GROUP_BLOCK
    gl = gb * SSM_GROUP
    blk3 = lambda a, b: pl.BlockSpec((gb, a, b), lambda i: (i, 0, 0))
    io = pl.BlockSpec((CHUNK, nk, gl), lambda i: (0, 0, i))
    return pl.pallas_call(
        _ssm_kernel,
        grid=(width // gl,),
        in_specs=[io, blk3(CW, CW), blk3(CW, SSM_STATE), blk3(CW, SSM_STATE),
                  blk3(CW, SSM_STATE), blk3(CW, SSM_STATE),
                  pl.BlockSpec((gb, SSM_STATE), lambda i: (i, 0)),
                  pl.BlockSpec((gb, SSM_STATE), lambda i: (i, 0)),
                  blk3(1, CW)],
        out_specs=io,
        out_shape=jax.ShapeDtypeStruct(u3.shape, BF16),
        scratch_shapes=[pltpu.VMEM((gb * nk, SSM_STATE), F32), pltpu.VMEM((gb * nk, SSM_STATE), F32),
                        pltpu.VMEM((gb, nk, CW), BF16), pltpu.VMEM((gb, nk, CW), BF16)],
        compiler_params=_params(("arbitrary",)),
        name="ssm",
    )(u3, tmat, wsr, wsi, rxr, rxi, a16r, a16i, d_tiled)


def _glu_kernel(z_ref, perm_ref, wg_ref, bg_ref, wb_ref, g0_ref, o_ref):
    tm = o_ref.shape[0]
    z = _dot(perm_ref[...], z_ref[...].reshape(tm, SSM_WIDTH)).astype(BF16)
    h = _dot(z, wg_ref[...]) + bg_ref[...]
    ga = h[:, :SSM_WIDTH]
    gb = h[:, SSM_WIDTH:]
    a = (ga * _sigmoid(gb)).astype(BF16)
    y = _dot(a, wb_ref[...])
    o_ref[...] = (_sigmoid(g0_ref[...].astype(F32)) * y).astype(o_ref.dtype)


def _glu(z3, w_glu, b_glu, w_br, proj, tm=512):
    t = z3.shape[0] * z3.shape[1]
    nk = tm // CHUNK
    r = np.arange(tm)
    perm = np.zeros((tm, tm), np.float32)
    perm[r, (r % CHUNK) * nk + r // CHUNK] = 1.0
    return pl.pallas_call(
        _glu_kernel,
        grid=(t // tm,),
        in_specs=[
            pl.BlockSpec((CHUNK, nk, SSM_WIDTH), lambda i: (0, i, 0)),
            pl.BlockSpec((tm, tm), lambda i: (0, 0)),
            pl.BlockSpec((SSM_WIDTH, 2 * SSM_WIDTH), lambda i: (0, 0)),
            pl.BlockSpec((1, 2 * SSM_WIDTH), lambda i: (0, 0)),
            pl.BlockSpec((SSM_WIDTH, D_MODEL), lambda i: (0, 0)),
            pl.BlockSpec((tm, D_MODEL), lambda i: (i, COL_G0 // D_MODEL)),
        ],
        out_specs=pl.BlockSpec((tm, D_MODEL), lambda i: (i, 0)),
        out_shape=jax.ShapeDtypeStruct((t, D_MODEL), BF16),
        compiler_params=_params(("arbitrary",)),
        name="glu",
    )(z3, jnp.asarray(perm, BF16), w_glu, b_glu, w_br, proj)


def _rope_pattern():
    half = ROPE_DIM // 2
    inv_freq = (np.float32(ROPE_THETA) ** (-np.arange(half, dtype=np.float32) / np.float32(half))).astype(np.float32)
    d = np.arange(LANES) % HEAD_DIM
    rotated = d < ROPE_DIM
    pat = np.zeros((16, LANES), np.float32)
    pat[:half] = rotated[None, :] & ((d % half)[None, :] == np.arange(half)[:, None])
    pat[8] = ~rotated
    pat[9] = np.where(d < half, -1.0, 0.0)
    pat[10] = np.where((d >= half) & rotated, 1.0, 0.0)
    return inv_freq.reshape(half, 1), pat


def _attn_kernel(q_ref, k_ref, v_ref, pos_ref, freq_ref, pat_ref, sink_ref, wbr_ref, ysg_ref, g1_ref,
                 o_ref, qbuf, kbuf, vbuf, obuf, sbuf, pbuf):
    i = pl.program_id(0)
    tq = q_ref.shape[0]
    nw = tq // WINDOW
    half = ROPE_DIM // 2

    @pl.when(i == 0)
    def _():
        kbuf[:, 0:WINDOW, :] = jnp.zeros((2 * N_KV_HEADS, WINDOW, LANES), BF16)
        vbuf[:, 0:WINDOW, :] = jnp.zeros((2 * N_KV_HEADS, WINDOW, LANES), BF16)

    ang = freq_ref[...] * pos_ref[...].astype(F32)
    spread = lambda tab: lax.dot_general(tab, pat_ref[0:8, :], (((0,), (0,)), ((), ())),
                                         preferred_element_type=F32, precision=HIGHEST)
    cs = spread(jnp.cos(ang)) + pat_ref[8:9, :]
    sn = spread(jnp.sin(ang))
    c_up = sn * pat_ref[9:10, :]
    c_dn = sn * pat_ref[10:11, :]

    def rope(x):
        return (x * cs + pltpu.roll(x, LANES - half, 1) * c_up + pltpu.roll(x, half, 1) * c_dn)

    low = lax.broadcasted_iota(I32, (tq, LANES), 1) < HEAD_DIM

    def split_heads(buf, cb, x):
        xs = pltpu.roll(x, HEAD_DIM, 1)
        zero = jnp.zeros_like(x)
        buf[4 * cb + 0, WINDOW:, :] = jnp.where(low, x, zero).astype(BF16)
        buf[4 * cb + 1, WINDOW:, :] = jnp.where(low, zero, xs).astype(BF16)
        buf[4 * cb + 2, WINDOW:, :] = jnp.where(low, xs, zero).astype(BF16)
        buf[4 * cb + 3, WINDOW:, :] = jnp.where(low, zero, x).astype(BF16)

    scale = HEAD_DIM ** -0.5
    for cb in range(Q_WIDTH // LANES):
        sl = slice(cb * LANES, (cb + 1) * LANES)
        qbuf[:, sl] = (rope(q_ref[:, sl].astype(F32)) * scale).astype(BF16)
    for cb in range(KV_WIDTH // LANES):
        sl = slice(cb * LANES, (cb + 1) * LANES)
        split_heads(kbuf, cb, rope(k_ref[:, sl].astype(F32)))
        split_heads(vbuf, cb, v_ref[:, sl].astype(F32))

    kj = lax.broadcasted_iota(I32, (2 * WINDOW, WINDOW), 0)
    qi = lax.broadcasted_iota(I32, (2 * WINDOW, WINDOW), 1)
    dist = qi + WINDOW - kj
    in_band = (dist >= 0) & (dist < WINDOW)
    cur_only = kj >= WINDOW
    sinks = sink_ref[...]

    def window(w, carry):
        r0 = pl.multiple_of(w * WINDOW, WINDOW)
        rows = pl.ds(r0, 2 * WINDOW)
        not_first = (i * nw + w) > 0
        mask = in_band & (cur_only | not_first)

        for h in range(N_Q_HEADS):
            qp = qbuf[pl.ds(r0, WINDOW), (h // 2) * LANES:(h // 2 + 1) * LANES]
            sbuf[h] = _dot_nt(kbuf[2 * (h // Q_PER_KV) + h % 2, rows, :], qp)
        for h in range(N_Q_HEADS):
            s = jnp.where(mask, sbuf[h], -jnp.inf)
            sink = sinks[:, h:h + 1]
            m = jnp.maximum(jnp.max(s, axis=0, keepdims=True), sink)
            p = jnp.exp(s - m)
            denom = jnp.sum(p, axis=0, keepdims=True) + jnp.exp(sink - m)
            pbuf[h] = (p * (1.0 / denom)).astype(BF16)
        for a in range(N_Q_HEADS // 2):
            kv = (2 * a) // Q_PER_KV
            o = _dot_tn(pbuf[2 * a], vbuf[2 * kv, rows, :]) + _dot_tn(pbuf[2 * a + 1], vbuf[2 * kv + 1, rows, :])
            obuf[pl.ds(r0, WINDOW), a * LANES:(a + 1) * LANES] = o.astype(BF16)
        return carry

    lax.fori_loop(0, nw, window, 0)

    kbuf[:, 0:WINDOW, :] = kbuf[:, tq:tq + WINDOW, :]
    vbuf[:, 0:WINDOW, :] = vbuf[:, tq:tq + WINDOW, :]

    y = _dot(obuf[...], wbr_ref[...])
    o_ref[...] = (ysg_ref[...].astype(F32) + _sigmoid(g1_ref[...].astype(F32)) * y).astype(o_ref.dtype)


def _attn(proj, pos_row, sinks, w_br, ysg, tq=512):
    t = proj.shape[0]
    freq, pat = (jnp.asarray(a) for a in _rope_pattern())
    return pl.pallas_call(
        _attn_kernel,
        grid=(t // tq,),
        in_specs=[
            pl.BlockSpec((tq, Q_WIDTH), lambda i: (i, COL_Q // Q_WIDTH)),
            pl.BlockSpec((tq, KV_WIDTH), lambda i: (i, COL_K // KV_WIDTH)),
            pl.BlockSpec((tq, KV_WIDTH), lambda i: (i, COL_V // KV_WIDTH)),
            pl.BlockSpec((1, tq), lambda i: (0, i)),
            pl.BlockSpec((ROPE_DIM // 2, 1), lambda i: (0, 0)),
            pl.BlockSpec((16, LANES), lambda i: (0, 0)),
            pl.BlockSpec((1, N_Q_HEADS), lambda i: (0, 0)),
            pl.BlockSpec((Q_WIDTH, D_MODEL), lambda i: (0, 0)),
            pl.BlockSpec((tq, D_MODEL), lambda i: (i, 0)),
            pl.BlockSpec((tq, D_MODEL), lambda i: (i, COL_G1 // D_MODEL)),
        ],
        out_specs=pl.BlockSpec((tq, D_MODEL), lambda i: (i, 0)),
        out_shape=jax.ShapeDtypeStruct((t, D_MODEL), BF16),
        scratch_shapes=[
            pltpu.VMEM((tq, Q_WIDTH), BF16),
            pltpu.VMEM((2 * N_KV_HEADS, tq + WINDOW, LANES), BF16),
            pltpu.VMEM((2 * N_KV_HEADS, tq + WINDOW, LANES), BF16),
            pltpu.VMEM((tq, Q_WIDTH), BF16),
            pltpu.VMEM((N_Q_HEADS, 2 * WINDOW, WINDOW), F32),
            pltpu.VMEM((N_Q_HEADS, 2 * WINDOW, WINDOW), BF16),
        ],
        compiler_params=_params(("arbitrary",)),
        name="attn",
    )(proj, proj, proj, pos_row, freq, pat, sinks, w_br, ysg, proj)


def _oproj_kernel(x_ref, mix_ref, wo_ref, g_ref, wrh_ref, wrl_ref, br_ref, h_ref, hp_ref, lt_ref):
    h = x_ref[...] + _dot(mix_ref[...], wo_ref[...])
    h_ref[...] = h
    ms = jnp.mean(h * h, axis=-1, keepdims=True)
    hn = h * lax.rsqrt(ms + EPS) * g_ref[...]
    _store_token_tiles(hp_ref, hn)
    hn_hi = hn.astype(BF16)
    hn_lo = (hn - hn_hi.astype(F32)).astype(BF16)
    logits = (_dot(hn_hi, wrh_ref[...]) + _dot(hn_lo, wrh_ref[...]) + _dot(hn_hi, wrl_ref[...])
              + br_ref[...])
    lt_ref[...] = logits.T


def _oproj(x, mixed, w_o, g, w_router, b_router, tm=512):
    t = x.shape[0]
    w_router_hi = w_router.astype(BF16)
    w_router_hi_rest = (w_router - w_router_hi.astype(F32)).astype(BF16)
    return pl.pallas_call(
        _oproj_kernel,
        grid=(t // tm,),
        in_specs=[
            pl.BlockSpec((tm, D_MODEL), lambda i: (i, 0)),
            pl.BlockSpec((tm, D_MODEL), lambda i: (i, 0)),
            pl.BlockSpec((D_MODEL, D_MODEL), lambda i: (0, 0)),
            pl.BlockSpec((1, D_MODEL), lambda i: (0, 0)),
            pl.BlockSpec((D_MODEL, LANES), lambda i: (0, 0)),
            pl.BlockSpec((D_MODEL, LANES), lambda i: (0, 0)),
            pl.BlockSpec((1, LANES), lambda i: (0, 0)),
        ],
        out_specs=[
            pl.BlockSpec((tm, D_MODEL), lambda i: (i, 0)),
            pl.BlockSpec((tm * SUBLANES, LANES), lambda i: (i, 0)),
            pl.BlockSpec((LANES, tm), lambda i: (0, i)),
        ],
        out_shape=[
            jax.ShapeDtypeStruct((t, D_MODEL), F32),
            jax.ShapeDtypeStruct((t * SUBLANES, LANES), jnp.uint32),
            jax.ShapeDtypeStruct((LANES, t), F32),
        ],
        compiler_params=_params(("arbitrary",)),
        name="oproj",
    )(x, mixed, w_o, g, w_router_hi, w_router_hi_rest, b_router)


ROUTE_CHUNK = 256


def _route_kernel(lt_ref, dest_ref, wts_ref, bexp_ref, nvalid_ref, eid_ref, rank_ref):
    t = lt_ref.shape[1]
    nc = t // ROUTE_CHUNK
    r8 = lax.broadcasted_iota(I32, (N_GROUPS, ROUTE_CHUNK), 0)
    r64 = lax.broadcasted_iota(I32, (N_EXPERTS, ROUTE_CHUNK), 0)

    def pick(c, carry):
        cols = pl.ds(pl.multiple_of(c * ROUTE_CHUNK, ROUTE_CHUNK), ROUTE_CHUNK)
        lg = lt_ref[0:N_GROUPS, cols]
        m = jnp.max(lg, axis=0, keepdims=True)
        ssum = jnp.sum(jnp.exp(lg - m), axis=0, keepdims=True)
        p_grp = 1.0 / ssum
        grp = jnp.min(jnp.where(lg == m, r8, N_GROUPS), axis=0, keepdims=True)
        le = lt_ref[N_GROUPS:N_GROUPS + N_EXPERTS, cols]
        leg = jnp.where((r64 // EXPERTS_PER_GROUP) == grp, le, -jnp.inf)
        m1 = jnp.max(leg, axis=0, keepdims=True)
        i1 = jnp.min(jnp.where(leg == m1, r64, N_EXPERTS), axis=0, keepdims=True)
        leg2 = jnp.where(r64 == i1, -jnp.inf, leg)
        m2 = jnp.max(leg2, axis=0, keepdims=True)
        i2 = jnp.min(jnp.where(leg2 == m2, r64, N_EXPERTS), axis=0, keepdims=True)
        ex = jnp.exp(m2 - m1)
        eid_ref[0:1, cols] = i1
        eid_ref[1:2, cols] = i2
        wts_ref[0:1, cols] = p_grp / (1.0 + ex)
        wts_ref[1:2, cols] = p_grp * ex / (1.0 + ex)
        return carry

    lax.fori_loop(0, nc, pick, 0)

    a_row = lax.broadcasted_iota(I32, (ROUTE_CHUNK, ROUTE_CHUNK), 0)
    a_col = lax.broadcasted_iota(I32, (ROUTE_CHUNK, ROUTE_CHUNK), 1)
    before = (a_row < a_col).astype(BF16)

    def count(n, carry):
        j = n // nc
        c = n - j * nc
        cols = pl.ds(pl.multiple_of(c * ROUTE_CHUNK, ROUTE_CHUNK), ROUTE_CHUNK)
        oh = r64 == eid_ref[pl.ds(j, 1), cols]
        ohf = oh.astype(F32)
        pref = _dot(ohf.astype(BF16), before) + carry
        rank_ref[pl.ds(j, 1), cols] = jnp.sum(jnp.where(oh, pref, 0.0), axis=0, keepdims=True)
        return carry + jnp.sum(ohf, axis=1, keepdims=True)

    counts = lax.fori_loop(0, TOP_K * nc, count, jnp.zeros((N_EXPERTS, 1), F32))

    padded = jnp.floor((counts + (MOE_BLOCK - 1)) * (1.0 / MOE_BLOCK)) * MOE_BLOCK
    e_row = lax.broadcasted_iota(I32, (N_EXPERTS, N_EXPERTS), 0)
    e_col = lax.broadcasted_iota(I32, (N_EXPERTS, N_EXPERTS), 1)
    incl = (e_col <= e_row).astype(F32)
    pad_end = _dot(incl, jnp.broadcast_to(padded, (N_EXPERTS, LANES)), precision=HIGHEST)[:, 0:1]
    pad_start = pad_end - padded

    def place(n, carry):
        j = n // nc
        c = n - j * nc
        cols = pl.ds(pl.multiple_of(c * ROUTE_CHUNK, ROUTE_CHUNK), ROUTE_CHUNK)
        oh = r64 == eid_ref[pl.ds(j, 1), cols]
        start = jnp.sum(jnp.where(oh, pad_start, 0.0), axis=0, keepdims=True)
        dest_ref[pl.ds(j, 1), cols] = (start + rank_ref[pl.ds(j, 1), cols]).astype(I32)
        return carry

    lax.fori_loop(0, TOP_K * nc, place, 0)

    b0 = (lax.broadcasted_iota(I32, (N_EXPERTS, bexp_ref.shape[1]), 1) * MOE_BLOCK).astype(F32)
    n_done = jnp.sum((pad_end <= b0).astype(F32), axis=0, keepdims=True)
    bexp_ref[...] = jnp.minimum(n_done, float(N_EXPERTS - 1)).astype(I32)
    live = jnp.minimum(pad_start + counts, b0 + MOE_BLOCK) - jnp.maximum(pad_start, b0)
    nvalid_ref[...] = jnp.sum(jnp.maximum(live, 0.0), axis=0, keepdims=True).astype(I32)


def _route(logits_t, n_blocks_pad):
    t = logits_t.shape[1]
    return pl.pallas_call(
        _route_kernel,
        out_shape=[
            jax.ShapeDtypeStruct((TOP_K, t), I32),
            jax.ShapeDtypeStruct((TOP_K, t), F32),
            jax.ShapeDtypeStruct((1, n_blocks_pad), I32),
            jax.ShapeDtypeStruct((1, n_blocks_pad), I32),
        ],
        scratch_shapes=[pltpu.VMEM((TOP_K, t), I32), pltpu.VMEM((TOP_K, t), F32)],
        compiler_params=pltpu.CompilerParams(vmem_limit_bytes=VMEM_LIMIT),
        name="route",
    )(logits_t)


def _invmap_kernel(dest_ref, fill_ref, rt_ref, sem):
    t = dest_ref.shape[0] // TOP_K
    fill = pltpu.make_async_copy(fill_ref, rt_ref, sem)
    fill.start()
    fill.wait()

    def put(tok, c):
        rt_ref[dest_ref[tok]] = tok
        rt_ref[dest_ref[t + tok]] = tok
        return c

    lax.fori_loop(0, t, put, 0, unroll=8)


def _invmap(dest_flat, n_rows):
    t = dest_flat.shape[0] // TOP_K
    fill = jnp.asarray(np.arange(n_rows, dtype=np.int32) % t)
    return pl.pallas_call(
        _invmap_kernel,
        in_specs=[pl.BlockSpec(memory_space=pltpu.SMEM), pl.BlockSpec(memory_space=pltpu.VMEM)],
        out_specs=pl.BlockSpec(memory_space=pltpu.SMEM),
        out_shape=jax.ShapeDtypeStruct((n_rows,), I32),
        scratch_shapes=[pltpu.SemaphoreType.DMA(())],
        name="invmap",
    )(dest_flat, fill)


GATHER_SLOTS = 4
WEIGHT_SLOTS = 3


def _row_copy(src_hbm, src_row, dst_buf, slot, dst_row, sem):
    return pltpu.make_async_copy(src_hbm.at[pl.ds(pl.multiple_of(src_row * SUBLANES, SUBLANES), SUBLANES), :],
                                 dst_buf.at[slot, pl.ds(dst_row * SUBLANES, SUBLANES), :], sem.at[slot])


def _rows_copy(src_hbm, dst_buf, slot, rows, sem):
    n = rows * SUBLANES
    return pltpu.make_async_copy(src_hbm.at[pl.ds(0, n), :], dst_buf.at[slot, pl.ds(0, n), :], sem.at[slot])


def _expert_kernel(bexp_ref, nvalid_ref, rt_ref, hp_hbm, wg_hbm, wu_hbm, wd_hbm, ys_ref,
                   xbuf, wf_g, wf_u, wf_d, ord_ref, xsem, wsem):
    b = pl.program_id(0)
    nb = pl.num_programs(0) - 1

    def expert_of(blk):
        return bexp_ref[jnp.minimum(blk, nb - 1)]

    def next_owner(blk, e):
        return lax.while_loop(lambda j: (j < nb) & (expert_of(j) == e), lambda j: j + 1, blk)

    def weight_copies(e, slot):
        return (pltpu.make_async_copy(wg_hbm.at[e], wf_g.at[slot], wsem.at[slot, 0]),
                pltpu.make_async_copy(wu_hbm.at[e], wf_u.at[slot], wsem.at[slot, 1]),
                pltpu.make_async_copy(wd_hbm.at[e], wf_d.at[slot], wsem.at[slot, 2]))

    def start_weights(e, slot):
        for c in weight_copies(e, slot):
            c.start(priority=1)

    @pl.when(b == 0)
    def _():
        ord_ref[0] = 0
        e0 = bexp_ref[0]
        start_weights(e0, 0)
        n1 = next_owner(1, e0)

        @pl.when(n1 < nb)
        def _():
            start_weights(expert_of(n1), 1)

    def gathered_rows(blk):
        n = nvalid_ref[jnp.clip(blk, 0, nb - 1)]
        live_rows = ((n + (SUBLANES - 1)) >> 3) << 3
        return jnp.where(blk >= nb, 0, jnp.where((blk < GATHER_SLOTS) & (n > 0), MOE_BLOCK, live_rows))

    def gather(blk):
        base = blk * MOE_BLOCK
        slot = blk % GATHER_SLOTS

        def body(i, c):
            for s in range(SUBLANES):
                r = i * SUBLANES + s
                _row_copy(hp_hbm, rt_ref[base + r], xbuf, slot, r, xsem).start()
            return c

        lax.fori_loop(0, gathered_rows(blk) // SUBLANES, body, 0)

    @pl.when(b == 0)
    def _():
        for blk in range(GATHER_SLOTS - 2):
            gather(blk)

    gather(b + GATHER_SLOTS - 2)

    @pl.when(b > 0)
    def _():
        blk = b - 1
        e = bexp_ref[blk]
        first = (blk == 0) | (e != bexp_ref[jnp.maximum(blk - 1, 0)])

        @pl.when(first)
        def _():
            n = ord_ref[0]
            ord_ref[0] = n + 1
            for c in weight_copies(e, n % WEIGHT_SLOTS):
                c.wait()
            n1 = next_owner(blk + 1, e)
            n2 = next_owner(n1 + 1, expert_of(n1))

            @pl.when((n1 < nb) & (n2 < nb))
            def _():
                start_weights(expert_of(n2), (n + 2) % WEIGHT_SLOTS)

        slot = blk % GATHER_SLOTS
        rows = gathered_rows(blk)

        @pl.when(rows > 0)
        def _():
            _rows_copy(hp_hbm, xbuf, slot, rows, xsem).wait()
            lo, hi = (v.astype(BF16) for v in _load_token_tiles(xbuf, slot, 0, MOE_BLOCK))
            ws = (ord_ref[0] - 1) % WEIGHT_SLOTS
            g = _dot(lo, wf_g[ws, :HALF, :].astype(BF16)) + _dot(hi, wf_g[ws, HALF:, :].astype(BF16))
            u = _dot(lo, wf_u[ws, :HALF, :].astype(BF16)) + _dot(hi, wf_u[ws, HALF:, :].astype(BF16))
            h = (g * _sigmoid(g) * u).astype(BF16)
            y = _dot(h, wf_d[ws].astype(BF16))
            _store_token_tiles(ys_ref, y)

        @pl.when(rows == 0)
        def _():
            _store_token_tiles(ys_ref, jnp.zeros((MOE_BLOCK, D_MODEL), F32))


def _experts(bexp, nvalid, row_tok, hn_packed, w_gate, w_up, w_down, n_blocks):
    grid_spec = pltpu.PrefetchScalarGridSpec(
        num_scalar_prefetch=3,
        grid=(n_blocks + 1,),
        in_specs=[pl.BlockSpec(memory_space=pl.ANY)] * 4,
        out_specs=pl.BlockSpec((MOE_BLOCK * SUBLANES, LANES), lambda b, be, nv, rt: (jnp.maximum(b - 1, 0), 0)),
        scratch_shapes=[
            pltpu.VMEM((GATHER_SLOTS, MOE_BLOCK * SUBLANES, LANES), jnp.uint32),
            pltpu.VMEM((WEIGHT_SLOTS, D_MODEL, D_EXPERT), F32), pltpu.VMEM((WEIGHT_SLOTS, D_MODEL, D_EXPERT), F32),
            pltpu.VMEM((WEIGHT_SLOTS, D_EXPERT, D_MODEL), F32),
            pltpu.SMEM((1,), I32),
            pltpu.SemaphoreType.DMA((GATHER_SLOTS,)), pltpu.SemaphoreType.DMA((WEIGHT_SLOTS, 3)),
        ],
    )
    return pl.pallas_call(
        _expert_kernel,
        grid_spec=grid_spec,
        out_shape=jax.ShapeDtypeStruct((n_blocks * MOE_BLOCK * SUBLANES, LANES), jnp.uint32),
        compiler_params=_params(("arbitrary",)),
        name="experts",
    )(bexp, nvalid, row_tok, hn_packed, w_gate, w_up, w_down)


def _combine_kernel(dest_ref, ys_hbm, h_ref, w_ref, g_ref, o_ref, ybuf, sem):
    i = pl.program_id(0)
    n = pl.num_programs(0)
    tq = h_ref.shape[0]
    t = n * tq

    def start_gather(blk, slot):
        base = blk * tq

        def body(j, c):
            for s in range(SUBLANES):
                r = j * SUBLANES + s
                _row_copy(ys_hbm, dest_ref[base + r], ybuf, slot, r, sem).start(priority=0)
                _row_copy(ys_hbm, dest_ref[t + base + r], ybuf, slot, tq + r, sem).start(priority=1)
            return c

        lax.fori_loop(0, tq // SUBLANES, body, 0)

    @pl.when(i == 0)
    def _():
        start_gather(0, 0)

    @pl.when(i + 1 < n)
    def _():
        start_gather(i + 1, (i + 1) % 2)

    slot = i % 2
    _rows_copy(ys_hbm, ybuf, slot, TOP_K * tq, sem).wait()
    y0_lo, y0_hi = _load_token_tiles(ybuf, slot, 0, tq)
    y1_lo, y1_hi = _load_token_tiles(ybuf, slot, tq, tq)
    w = w_ref[...]
    w0 = w[:, 0:1]
    w1 = w[:, 1:2]
    h = h_ref[...]
    lo = h[:, :HALF] + w0 * y0_lo + w1 * y1_lo
    hi = h[:, HALF:] + w0 * y0_hi + w1 * y1_hi
    ms = (jnp.sum(lo * lo, axis=-1, keepdims=True) + jnp.sum(hi * hi, axis=-1, keepdims=True)) * (1.0 / D_MODEL)
    inv = lax.rsqrt(ms + EPS)
    g = g_ref[...]
    o_ref[:, :HALF] = lo * inv * g[:, :HALF]
    o_ref[:, HALF:] = hi * inv * g[:, HALF:]


def _combine(dest_flat, ys_packed, h1, wts_tok, g_final, tq=256):
    t = h1.shape[0]
    grid_spec = pltpu.PrefetchScalarGridSpec(
        num_scalar_prefetch=1,
        grid=(t // tq,),
        in_specs=[
            pl.BlockSpec(memory_space=pl.ANY),
            pl.BlockSpec((tq, D_MODEL), lambda i, d: (i, 0)),
            pl.BlockSpec((tq, TOP_K), lambda i, d: (i, 0)),
            pl.BlockSpec((1, D_MODEL), lambda i, d: (0, 0)),
        ],
        out_specs=pl.BlockSpec((tq, D_MODEL), lambda i, d: (i, 0)),
        scratch_shapes=[pltpu.VMEM((2, TOP_K * tq * SUBLANES, LANES), jnp.uint32),
                        pltpu.SemaphoreType.DMA((2,))],
    )
    return pl.pallas_call(
        _combine_kernel,
        grid_spec=grid_spec,
        out_shape=jax.ShapeDtypeStruct((t, D_MODEL), F32),
        compiler_params=_params(("arbitrary",)),
        name="combine",
    )(dest_flat, ys_packed, h1, wts_tok, g_final)


def _mixers(h, positions, norm_g, w_in, b_in, a_re, a_im, log_dt, b_re, b_im, c_re, c_im, d_skip,
            w_glu, b_glu, w_br_ssm, sinks, w_br_attn):
    t = h.shape[0]
    ngate = 2 * D_MODEL
    g = norm_g.reshape(1, D_MODEL)
    b_row = b_in.reshape(1, IN_WIDTH)
    proj, u3 = _inproj(h, g, w_in, b_row, u_segment=(0, SSM_WIDTH),
                       main_segments=((IN_WIDTH - ngate, ngate), (SSM_WIDTH, IN_WIDTH - ngate - SSM_WIDTH)))

    tmat, wsr, wsi, rxr, rxi, a16r, a16i = _ssm_prep(
        a_re, a_im, log_dt, b_re.transpose(0, 2, 1), b_im.transpose(0, 2, 1), c_re, c_im)
    d_tiled = jnp.tile(d_skip, (1, CHUNK)).reshape(SSM_GROUPS, 1, CW)
    z3 = _ssm(u3, tmat, wsr, wsi, rxr, rxi, a16r.reshape(SSM_GROUPS, SSM_STATE),
              a16i.reshape(SSM_GROUPS, SSM_STATE), d_tiled)

    ysg = _glu(z3, w_glu.astype(BF16), b_glu.reshape(1, -1), w_br_ssm.astype(BF16), proj)
    return _attn(proj, positions.reshape(1, t), sinks.reshape(1, N_Q_HEADS), w_br_attn.astype(BF16), ysg)


def _moe_tail(h, mixed, w_o, norm_ffn_g, w_rg, b_rg, w_re, b_re, w_gate, w_up, w_down, norm_final_g):
    t = h.shape[0]
    n_route = N_GROUPS + N_EXPERTS
    w_router = jnp.concatenate([w_rg, w_re, jnp.zeros((D_MODEL, LANES - n_route), F32)], axis=1)
    b_router = jnp.concatenate([b_rg, b_re, jnp.zeros((LANES - n_route,), F32)]).reshape(1, LANES)
    h1, hn_packed, logits_t = _oproj(h, mixed, w_o.astype(BF16), norm_ffn_g.reshape(1, D_MODEL),
                                     w_router, b_router)
    n_assign = t * TOP_K
    n_blocks = -(-(n_assign + N_EXPERTS * (MOE_BLOCK - 1)) // MOE_BLOCK)
    n_blocks_pad = -(-n_blocks // LANES) * LANES
    dest, wts, bexp, nvalid = _route(logits_t, n_blocks_pad)
    dest_flat = dest.reshape(n_assign)
    row_tok = _invmap(dest_flat, n_blocks * MOE_BLOCK)
    ys_packed = _experts(bexp[0, :n_blocks], nvalid[0, :n_blocks], row_tok, hn_packed, w_gate, w_up, w_down,
                         n_blocks)
    return _combine(dest_flat, ys_packed, h1, wts.T, norm_final_g.reshape(1, D_MODEL))


def kernel(x, positions, norm_mix_g, w_in, b_in, ssm_a_re, ssm_a_im, ssm_log_dt, ssm_b_re, ssm_b_im, ssm_c_re, ssm_c_im, ssm_d, w_glu, b_glu, w_br_ssm, attn_sinks, w_br_attn, w_o, norm_ffn_g, w_router_group, b_router_group, w_router_expert, b_router_expert, w_exp_gate, w_exp_up, w_exp_down, norm_final_g):
    bsz, seq, d = x.shape
    assert bsz == 1 and d == D_MODEL and norm_mix_g.shape[0] == 1
    h = x.reshape(seq, d)
    mixed = _mixers(h, positions, norm_mix_g[0], w_in[0], b_in[0], ssm_a_re[0], ssm_a_im[0], ssm_log_dt[0],
                    ssm_b_re[0], ssm_b_im[0], ssm_c_re[0], ssm_c_im[0], ssm_d[0], w_glu[0], b_glu[0],
                    w_br_ssm[0], attn_sinks[0], w_br_attn[0])
    out = _moe_tail(h, mixed, w_o[0], norm_ffn_g[0], w_router_group[0], b_router_group[0],
                    w_router_expert[0], b_router_expert[0], w_exp_gate[0], w_exp_up[0], w_exp_down[0],
                    norm_final_g)
    return out.reshape(bsz, seq, d)
```
